```python
import math
import jax, jax.numpy as jnp
from jax import lax
import numpy as np

D_MODEL = 1024
BATCH = 4
SEQ = 8192
DEPTH = 1

NSA_HEADS = 8
NSA_KV_GROUPS = 2
NSA_HPG = NSA_HEADS // NSA_KV_GROUPS
NSA_DK = 64
NSA_DV = 64
CMP_LEN = 32
CMP_STRIDE = 16
CMP_HIDDEN = 256
SLC_BLOCK = 64
SLC_TOPK = 16
WINDOW = 512
NSA_QCHUNK = 64
FORCED_SCORE = 1e4
ML_HEADS = 4
ML_DK = 128
ML_DV = 128
ML_CHUNK = 64
CONV_WIDTH = 4
D_FF = 4 * D_MODEL
REL_BUCKETS = 32
REL_MAX_DIST = 128
RMS_EPS = 1e-6

NSA_QW = NSA_HEADS * NSA_DK
NSA_KW = NSA_KV_GROUPS * NSA_DK
NSA_VW = NSA_KV_GROUPS * NSA_DV
NSA_OW = NSA_HEADS * NSA_DV
ML_QW = ML_HEADS * ML_DK
ML_VW = ML_HEADS * ML_DV
PROJ_WIDTHS = (NSA_QW, NSA_KW, NSA_VW, NSA_KW, NSA_VW, NSA_KW, NSA_VW, 3 * NSA_HEADS,
               ML_QW, ML_QW, ML_VW, ML_HEADS, ML_HEADS, ML_VW, 2 * D_MODEL)
D_PROJ = sum(PROJ_WIDTHS)

kernel_name = 'hybrid_nsa_mlstm_block'


def rms_norm(x, g):
    xf = x.astype(jnp.float32)
    y = xf * lax.rsqrt(jnp.mean(xf * xf, axis=-1, keepdims=True) + RMS_EPS)
    return (y * g.astype(jnp.float32)).astype(x.dtype)


def t5_bucket(dist):
    n = jnp.maximum(dist, 0)
    max_exact = REL_BUCKETS // 2
    nf = jnp.maximum(n, 1).astype(jnp.float32)
    large = max_exact + (jnp.log(nf / max_exact) / math.log(REL_MAX_DIST / max_exact)
                         * (REL_BUCKETS - max_exact)).astype(jnp.int32)
    return jnp.where(n < max_exact, n, jnp.minimum(large, REL_BUCKETS - 1))


def masked_softmax(s, mask):
    s = jnp.where(mask, s.astype(jnp.float32), -jnp.inf)
    m = jnp.max(s, axis=-1, keepdims=True)
    m = jnp.where(jnp.isfinite(m), m, 0.0)
    p = jnp.where(mask, jnp.exp(s - m), 0.0)
    return p / jnp.maximum(jnp.sum(p, axis=-1, keepdims=True), 1e-30)


def compress_blocks(kv, pos, w1, b1, w2, b2):
    bsz, s, g, d = kv.shape
    segs = kv.reshape(bsz, s // CMP_STRIDE, CMP_STRIDE, g, d)
    blocks = jnp.concatenate([segs[:, :-1], segs[:, 1:]], axis=2) + pos[:, None, :]
    flat = blocks.transpose(0, 1, 3, 2, 4).reshape(bsz, s // CMP_STRIDE - 1, g, CMP_LEN * d)
    return jax.nn.gelu(flat @ w1 + b1) @ w2 + b2


def nsa_mixer(q, kc, vc, ks, vs, kw, vw, gate_logits, q_gain, k_gain, cmp_k, cmp_v, rel_table):
    bsz, s = q.shape[:2]
    G, HPG = NSA_KV_GROUPS, NSA_HPG
    nc = s // CMP_STRIDE - 1
    ns = s // SLC_BLOCK
    topk = min(SLC_TOPK, ns)
    qg = (rms_norm(q, q_gain) * (NSA_DK ** -0.5)).reshape(bsz, s, G, HPG, NSA_DK)
    kcmp = rms_norm(compress_blocks(kc, *cmp_k), k_gain[0])
    vcmp = compress_blocks(vc, *cmp_v)
    ks_blk = rms_norm(ks, k_gain[1]).reshape(bsz, ns, SLC_BLOCK, G, NSA_DK)
    vs_blk = vs.reshape(bsz, ns, SLC_BLOCK, G, NSA_DV)
    pad = ((0, 0), (WINDOW, 0), (0, 0), (0, 0))
    kw_pad = jnp.pad(rms_norm(kw, k_gain[2]), pad)
    vw_pad = jnp.pad(vw, pad)
    gates = jax.nn.sigmoid(gate_logits.astype(jnp.float32)).astype(q.dtype).reshape(bsz, s, G, HPG, 3)
    table = rel_table.reshape(REL_BUCKETS, G, HPG)
    cmp_end = jnp.arange(nc) * CMP_STRIDE + (CMP_LEN - 1)
    ci = jnp.arange(nc)[:, None] * CMP_STRIDE
    sj = jnp.arange(ns)[None, :] * SLC_BLOCK
    cmp_to_slc = ((ci < sj + SLC_BLOCK) & (ci + CMP_LEN > sj)).astype(jnp.float32)
    blk_ids = jnp.arange(ns)
    b_i = jnp.arange(bsz)[:, None, None, None]
    g_i = jnp.arange(G)[None, None, :, None]
    g_i5 = jnp.arange(G)[None, None, :, None, None]

    def chunk(c):
        t0 = c * NSA_QCHUNK
        t = t0 + jnp.arange(NSA_QCHUNK)
        qc = lax.dynamic_slice_in_dim(qg, t0, NSA_QCHUNK, axis=1)
        dist_c = t[:, None] - cmp_end[None, :]
        bias_c = table[t5_bucket(dist_c)].transpose(0, 2, 3, 1)
        s_c = jnp.einsum('bqghd,bngd->bqghn', qc, kcmp).astype(jnp.float32) + bias_c
        p_c = masked_softmax(s_c, (dist_c >= 0)[:, None, None, :])
        o_c = jnp.einsum('bqghn,bngd->bqghd', p_c.astype(vcmp.dtype), vcmp)
        imp = jnp.sum(p_c, axis=3) @ cmp_to_slc
        cur = t // SLC_BLOCK
        valid = (blk_ids[None, :] * SLC_BLOCK) <= t[:, None]
        forced = (blk_ids[None, :] == 0) | (blk_ids[None, :] == cur[:, None]) | (blk_ids[None, :] == cur[:, None] - 1)
        score = jnp.where(forced[None, :, None, :], FORCED_SCORE,
                          jnp.where(valid[None, :, None, :], imp, -1.0))
        _, idx = lax.top_k(score, topk)
        k_sel = ks_blk[b_i, idx, :, g_i, :]
        v_sel = vs_blk[b_i, idx, :, g_i, :]
        pos_s = idx[..., None] * SLC_BLOCK + jnp.arange(SLC_BLOCK)
        dist_s = t[None, :, None, None, None] - pos_s
        bias_s = jnp.moveaxis(table[t5_bucket(dist_s), g_i5], -1, 3)
        s_s = (jnp.einsum('bqghd,bqgnkd->bqghnk', qc, k_sel).astype(jnp.float32) + bias_s)
        s_s = s_s.reshape(bsz, NSA_QCHUNK, G, HPG, topk * SLC_BLOCK)
        p_s = masked_softmax(s_s, (dist_s >= 0).reshape(bsz, NSA_QCHUNK, G, 1, topk * SLC_BLOCK))
        o_s = jnp.einsum('bqghm,bqgmd->bqghd', p_s.astype(v_sel.dtype),
                         v_sel.reshape(bsz, NSA_QCHUNK, G, topk * SLC_BLOCK, NSA_DV))
        kwc = lax.dynamic_slice_in_dim(kw_pad, t0, NSA_QCHUNK + WINDOW, axis=1)
        vwc = lax.dynamic_slice_in_dim(vw_pad, t0, NSA_QCHUNK + WINDOW, axis=1)
        pos_w = t0 - WINDOW + jnp.arange(NSA_QCHUNK + WINDOW)
        dist_w = t[:, None] - pos_w[None, :]
        mask_w = (dist_w >= 0) & (dist_w < WINDOW) & (pos_w >= 0)[None, :]
        bias_w = table[t5_bucket(dist_w)].transpose(0, 2, 3, 1)
        s_w = jnp.einsum('bqghd,bkgd->bqghk', qc, kwc).astype(jnp.float32) + bias_w
        p_w = masked_softmax(s_w, mask_w[:, None, None, :])
        o_w = jnp.einsum('bqghk,bkgd->bqghd', p_w.astype(vwc.dtype), vwc)
        gc = lax.dynamic_slice_in_dim(gates, t0, NSA_QCHUNK, axis=1)
        o = gc[..., 0:1] * o_c + gc[..., 1:2] * o_s + gc[..., 2:3] * o_w
        return o.reshape(bsz, NSA_QCHUNK, NSA_OW)

    out = lax.map(chunk, jnp.arange(s // NSA_QCHUNK))
    return out.transpose(1, 0, 2, 3).reshape(bsz, s, NSA_OW)


def causal_dwconv(x, w, b):
    s = x.shape[1]
    xp = jnp.pad(x, ((0, 0), (CONV_WIDTH - 1, 0), (0, 0)))
    y = b
    for j in range(CONV_WIDTH):
        y = y + xp[:, j:j + s] * w[j]
    return y


def mlstm_mixer(q, k, v, i_raw, f_raw, o_raw, conv_w, conv_b, i_bias, f_bias):
    bsz, s = q.shape[:2]
    nch = s // ML_CHUNK
    qk = jax.nn.silu(causal_dwconv(jnp.concatenate([q, k], axis=-1), conv_w, conv_b))
    q, k = jnp.split(qk, 2, axis=-1)

    def to_chunks(a, d):
        return a.astype(jnp.float32).reshape(bsz, nch, ML_CHUNK, ML_HEADS, d).transpose(1, 0, 3, 2, 4)

    def gate_chunks(a):
        return a.astype(jnp.float32).reshape(bsz, nch, ML_CHUNK, ML_HEADS).transpose(1, 0, 3, 2)

    qc = to_chunks(q * (ML_DK ** -0.5), ML_DK)
    kc = to_chunks(k, ML_DK)
    vc = to_chunks(v, ML_DV)
    li = gate_chunks(i_raw + i_bias)
    lf = jax.nn.log_sigmoid(gate_chunks(f_raw + f_bias))
    causal = jnp.tril(jnp.ones((ML_CHUNK, ML_CHUNK), dtype=bool))

    def step(carry, inp):
        c_mat, n_vec, m_prev = carry
        qb, kb, vb, lib, lfb = inp
        b = jnp.cumsum(lfb, axis=-1)
        g = b[..., -1]
        log_d = jnp.where(causal, b[..., :, None] - b[..., None, :] + lib[..., None, :], -jnp.inf)
        inter = b + m_prev[..., None]
        m_row = jnp.maximum(inter, jnp.max(log_d, axis=-1))
        w = jnp.einsum('bhid,bhjd->bhij', qb, kb) * jnp.exp(log_d - m_row[..., None])
        inter_scale = jnp.exp(inter - m_row)
        num = inter_scale[..., None] * jnp.einsum('bhid,bhde->bhie', qb, c_mat) + jnp.einsum('bhij,bhje->bhie', w, vb)
        den = inter_scale * jnp.einsum('bhid,bhd->bhi', qb, n_vec) + jnp.sum(w, axis=-1)
        h = num / jnp.maximum(jnp.abs(den), jnp.exp(-m_row))[..., None]
        log_src = g[..., None] - b + lib
        m_new = jnp.maximum(g + m_prev, jnp.max(log_src, axis=-1))
        decay = jnp.exp(g + m_prev - m_new)
        src = jnp.exp(log_src - m_new[..., None])
        c_new = decay[..., None, None] * c_mat + jnp.einsum('bhj,bhjd,bhje->bhde', src, kb, vb)
        n_new = decay[..., None] * n_vec + jnp.einsum('bhj,bhjd->bhd', src, kb)
        return (c_new, n_new, m_new), h

    init = (jnp.zeros((bsz, ML_HEADS, ML_DK, ML_DV), jnp.float32),
            jnp.zeros((bsz, ML_HEADS, ML_DK), jnp.float32),
            jnp.zeros((bsz, ML_HEADS), jnp.float32))
    _, h = lax.scan(step, init, (qc, kc, vc, li, lf))
    h = h.transpose(1, 0, 3, 2, 4).reshape(bsz, s, ML_VW)
    return (jax.nn.sigmoid(o_raw.astype(jnp.float32)) * h).astype(o_raw.dtype)


def setup_inputs(seed: int = 0) -> dict:
    key = jax.random.key(seed)
    keys = iter(jax.random.split(key, 40))

    def nrm(shape, scale):
        return jax.random.normal(next(keys), shape, jnp.float32) * scale

    L = DEPTH
    inp = {}
    inp['x'] = nrm((BATCH, SEQ, D_MODEL), 1.0)
    inp['norm1_g'] = 1.0 + nrm((L, D_MODEL), 0.01)
    inp['w_in'] = nrm((L, D_MODEL, D_PROJ), D_MODEL ** -0.5)
    inp['nsa_q_gain'] = 1.0 + nrm((L, NSA_DK), 0.01)
    inp['nsa_k_gain'] = 1.0 + nrm((L, 3, NSA_DK), 0.01)
    for name, d in (('k', NSA_DK), ('v', NSA_DV)):
        inp['cmp_' + name + '_pos'] = nrm((L, CMP_LEN, d), 0.1)
        inp['cmp_' + name + '_w1'] = nrm((L, CMP_LEN * d, CMP_HIDDEN), (CMP_LEN * d) ** -0.5)
        inp['cmp_' + name + '_b1'] = nrm((L, CMP_HIDDEN), 0.01)
        inp['cmp_' + name + '_w2'] = nrm((L, CMP_HIDDEN, d), CMP_HIDDEN ** -0.5)
        inp['cmp_' + name + '_b2'] = nrm((L, d), 0.01)
    inp['rel_table'] = nrm((REL_BUCKETS, NSA_HEADS), 0.5)
    inp['ml_conv_w'] = nrm((L, CONV_WIDTH, 2 * ML_QW), CONV_WIDTH ** -0.5)
    inp['ml_conv_b'] = nrm((L, 2 * ML_QW), 0.01)
    inp['ml_i_bias'] = nrm((L, ML_HEADS), 0.1)
    inp['ml_f_bias'] = jnp.linspace(3.0, 6.0, ML_HEADS, dtype=jnp.float32)[None, :] + nrm((L, ML_HEADS), 0.1)
    inp['w_branch_a'] = nrm((L, NSA_OW, D_MODEL), NSA_OW ** -0.5)
    inp['w_branch_b'] = nrm((L, ML_VW, D_MODEL), ML_VW ** -0.5)
    inp['w_out'] = nrm((L, D_MODEL, D_MODEL), D_MODEL ** -0.5)
    inp['norm2_g'] = 1.0 + nrm((L, D_MODEL), 0.01)
    inp['w_ff1'] = nrm((L, D_MODEL, D_FF), D_MODEL ** -0.5)
    inp['w_ff2'] = nrm((L, D_FF, D_MODEL), D_FF ** -0.5)
    return inp


def reference(x, norm1_g, w_in, nsa_q_gain, nsa_k_gain,
              cmp_k_pos, cmp_k_w1, cmp_k_b1, cmp_k_w2, cmp_k_b2,
              cmp_v_pos, cmp_v_w1, cmp_v_b1, cmp_v_w2, cmp_v_b2,
              rel_table, ml_conv_w, ml_conv_b, ml_i_bias, ml_f_bias,
              w_branch_a, w_branch_b, w_out, norm2_g, w_ff1, w_ff2):
    bsz, s = x.shape[:2]
    offsets = np.cumsum(PROJ_WIDTHS)[:-1].tolist()
    for l in range(DEPTH):
        h = rms_norm(x, norm1_g[l])
        proj = h @ w_in[l]
        (nq, nkc, nvc, nks, nvs, nkw, nvw, ngate,
         mq, mk, mv, mi, mf, mo, mgate) = jnp.split(proj, offsets, axis=-1)
        heads = lambda a, n: a.reshape(bsz, s, n, -1)
        y_a = nsa_mixer(heads(nq, NSA_HEADS),
                        heads(nkc, NSA_KV_GROUPS), heads(nvc, NSA_KV_GROUPS),
                        heads(nks, NSA_KV_GROUPS), heads(nvs, NSA_KV_GROUPS),
                        heads(nkw, NSA_KV_GROUPS), heads(nvw, NSA_KV_GROUPS),
                        ngate, nsa_q_gain[l], nsa_k_gain[l],
                        (cmp_k_pos[l], cmp_k_w1[l], cmp_k_b1[l], cmp_k_w2[l], cmp_k_b2[l]),
                        (cmp_v_pos[l], cmp_v_w1[l], cmp_v_b1[l], cmp_v_w2[l], cmp_v_b2[l]),
                        rel_table)
        y_b = mlstm_mixer(mq, mk, mv, mi, mf, mo, ml_conv_w[l], ml_conv_b[l], ml_i_bias[l], ml_f_bias[l])
        gate_a, gate_b = jnp.split(jax.nn.sigmoid(mgate), 2, axis=-1)
        mixed = gate_a * (y_a @ w_branch_a[l]) + gate_b * (y_b @ w_branch_b[l])
        x = x + mixed @ w_out[l]
        h2 = rms_norm(x, norm2_g[l])
        x = x + jnp.square(jax.nn.relu(h2 @ w_ff1[l])) @ w_ff2[l]
    return x
```

```python
import functools
import math

import numpy as np
import jax
import jax.numpy as jnp
from jax import lax
from jax.experimental import pallas as pl
from jax.experimental.pallas import tpu as pltpu

F32 = jnp.float32
BF16 = jnp.bfloat16

D_MODEL = 1024
NSA_HEADS = 8
NSA_GROUPS = 2
NSA_HPG = NSA_HEADS // NSA_GROUPS
NSA_D = 64
CMP_LEN = 32
CMP_STRIDE = 16
CMP_HIDDEN = 256
SLC_BLOCK = 64
SLC_TOPK = 16
WINDOW = 512
FORCED_SCORE = 1e4
ML_HEADS = 4
ML_D = 128
ML_CHUNK = 64
CONV_WIDTH = 4
D_FF = 4 * D_MODEL
REL_BUCKETS = 32
REL_MAX_DIST = 128
RMS_EPS = 1e-6

LANES = 128
NSA_TILE = 128
SLC_PAD = 128
FAR_CHUNK = 512
NEAR_KEYS = 2 * NSA_TILE
WIN_KEYS = WINDOW + NSA_TILE
MASK_BIG = -1e9
MASK_F32 = -1e30
VMEM_LIMIT = 56 * 1024 * 1024

_Q_OFF, _KS_OFF, _KW_OFF, _KC_OFF, _VC_OFF, _VS_OFF, _VW_OFF = 0, 1024, 1152, 1280, 1408, 1536, 1664
_GATE_OFF, _MQ_OFF, _MK_OFF, _MV_OFF, _MO_OFF, _MG_OFF, _W_COLS = 1792, 1920, 2432, 2944, 3456, 3968, 6016


def _nt(a, b, precision=None):
    return lax.dot_general(a, b, (((1,), (1,)), ((), ())), precision=precision,
                           preferred_element_type=F32)


def _dot(a, b, precision=None):
    return jnp.dot(a, b, precision=precision, preferred_element_type=F32)


def _const_spec(shape):
    nd = len(shape)
    return pl.BlockSpec(shape, lambda *_: (0,) * nd, pipeline_mode=pl.Buffered(1))


def _proj_kernel(x_ref, g1_ref, w_ref, wif_ref, qg_ref, kg_ref,
                 q_ref, ks_ref, kw_ref, kc_ref, vc_ref, vs_ref, vw_ref, gate_ref,
                 mq_ref, mk_ref, mv_ref, mo_ref, mg_ref, gif_ref):
    x = x_ref[...]
    h = x * lax.rsqrt(jnp.mean(x * x, axis=-1, keepdims=True) + RMS_EPS) * g1_ref[...]
    hb = h.astype(BF16)

    def proj(off, width):
        return _dot(hb, w_ref[:, off:off + width])

    lane = lax.broadcasted_iota(jnp.int32, (1, LANES), 1)
    low = lane < NSA_D

    for hh in range(NSA_HEADS):
        blk = proj(_Q_OFF + LANES * hh, LANES)
        ms = jnp.sum(blk * blk, axis=-1, keepdims=True) * (1.0 / NSA_D)
        qn = blk * lax.rsqrt(ms + RMS_EPS) * qg_ref[...] * (NSA_D ** -0.5)
        q_ref[:, LANES * hh:LANES * (hh + 1)] = qn.astype(q_ref.dtype)

    for off, ref, row in ((_KS_OFF, ks_ref, 0), (_KW_OFF, kw_ref, 1)):
        blk = proj(off, LANES)
        sq = blk * blk
        ms0 = jnp.sum(jnp.where(low, sq, 0.0), axis=-1, keepdims=True) * (1.0 / NSA_D)
        ms1 = jnp.sum(jnp.where(low, 0.0, sq), axis=-1, keepdims=True) * (1.0 / NSA_D)
        inv = jnp.where(low, lax.rsqrt(ms0 + RMS_EPS), lax.rsqrt(ms1 + RMS_EPS))
        ref[...] = (blk * inv * kg_ref[row:row + 1, :]).astype(ref.dtype)

    for off, ref in ((_KC_OFF, kc_ref), (_VC_OFF, vc_ref), (_VS_OFF, vs_ref), (_VW_OFF, vw_ref)):
        ref[...] = proj(off, LANES).astype(ref.dtype)

    gate_ref[...] = jax.nn.sigmoid(proj(_GATE_OFF, LANES))
    mq_ref[...] = proj(_MQ_OFF, 512)
    mk_ref[...] = proj(_MK_OFF, 512)
    mv_ref[...] = proj(_MV_OFF, 512)
    mo_ref[...] = jax.nn.sigmoid(proj(_MO_OFF, 512))
    for c in range(4):
        mg_ref[:, 512 * c:512 * (c + 1)] = jax.nn.sigmoid(proj(_MG_OFF + 512 * c, 512)).astype(mg_ref.dtype)
    gif_ref[...] = _nt(wif_ref[...], hb)


def _proj(x2d, g1, w_all, w_if, qg_pad, kg_pad, tm=256):
    n = x2d.shape[0]
    row = lambda w: pl.BlockSpec((tm, w), lambda i: (i, 0))
    out_shapes = (
        jax.ShapeDtypeStruct((n, 1024), BF16),
        jax.ShapeDtypeStruct((n, LANES), BF16),
        jax.ShapeDtypeStruct((n, LANES), BF16),
        jax.ShapeDtypeStruct((n, LANES), BF16),
        jax.ShapeDtypeStruct((n, LANES), BF16),
        jax.ShapeDtypeStruct((n, LANES), BF16),
        jax.ShapeDtypeStruct((n, LANES), BF16),
        jax.ShapeDtypeStruct((n, LANES), F32),
        jax.ShapeDtypeStruct((n, 512), F32),
        jax.ShapeDtypeStruct((n, 512), F32),
        jax.ShapeDtypeStruct((n, 512), F32),
        jax.ShapeDtypeStruct((n, 512), F32),
        jax.ShapeDtypeStruct((n, 2048), BF16),
        jax.ShapeDtypeStruct((8, n), F32),
    )
    out_specs = (row(1024),) + (row(LANES),) * 7 + (row(512),) * 4 + (row(2048),
                 pl.BlockSpec((8, tm), lambda i: (0, i)))
    return pl.pallas_call(
        _proj_kernel,
        grid=(n // tm,),
        in_specs=[row(D_MODEL), _const_spec((1, D_MODEL)), _const_spec((D_MODEL, _W_COLS)),
                  _const_spec((8, D_MODEL)), _const_spec((1, LANES)), _const_spec((2, LANES))],
        out_specs=out_specs,
        out_shape=out_shapes,
        compiler_params=pltpu.CompilerParams(dimension_semantics=("parallel",),
                                             vmem_limit_bytes=VMEM_LIMIT),
        name="proj",
    )(x2d, g1, w_all, w_if, qg_pad, kg_pad)


def _compress_kernel(seg_ref, w1_ref, pos_ref, b1_ref, w2_ref, b2_ref, gain_ref, out_ref, *, normalize):
    seg = seg_ref[0, 0]
    half = CMP_STRIDE * NSA_D
    a = _dot(seg, w1_ref[0:half, :])
    b = _dot(seg, w1_ref[half:2 * half, :])
    nseg = seg.shape[0]
    posb = _dot(jnp.broadcast_to(pos_ref[...], (8, 2 * half)), w1_ref[...])[0:1] + b1_ref[...]
    pre = a + pltpu.roll(b, nseg - 1, 0) + posb
    c = math.sqrt(2.0 / math.pi)
    hid = 0.5 * pre * (1.0 + jnp.tanh(c * (pre + 0.044715 * (pre * pre * pre))))
    out = _dot(hid.astype(BF16), w2_ref[...]) + b2_ref[...]
    if normalize:
        ms = jnp.mean(out * out, axis=-1, keepdims=True)
        out = out * lax.rsqrt(ms + RMS_EPS) * gain_ref[...]
    out_ref[0, 0] = out


def _compress(segs, w1, pos, b1, w2, b2, gain, normalize):
    bsz, g, nseg, width = segs.shape
    return pl.pallas_call(
        functools.partial(_compress_kernel, normalize=normalize),
        grid=(bsz, g),
        in_specs=[pl.BlockSpec((1, 1, nseg, width), lambda b, gi: (b, gi, 0, 0)),
                  _const_spec(w1.shape), _const_spec(pos.shape), _const_spec(b1.shape),
                  _const_spec(w2.shape), _const_spec(b2.shape), _const_spec(gain.shape)],
        out_specs=pl.BlockSpec((1, 1, nseg, NSA_D), lambda b, gi: (b, gi, 0, 0)),
        out_shape=jax.ShapeDtypeStruct((bsz, g, nseg, NSA_D), F32),
        compiler_params=pltpu.CompilerParams(dimension_semantics=("parallel", "parallel"),
                                             vmem_limit_bytes=VMEM_LIMIT),
        name="compress",
    )(segs, w1, pos, b1, w2, b2, gain)


def _nsa_kernel(q_ref, ks_ref, vs_ref, kw_ref, vw_ref, kc_ref, vc_ref, oh_ref, gate_ref,
                bcf_ref, bs_ref, bw_ref, c2s_ref, out_ref, m_sc, acc_sc):
    i = pl.program_id(1)
    t0 = pl.multiple_of(i * NSA_TILE, NSA_TILE)
    T = NSA_TILE
    rows = NSA_HEADS * T
    grows = NSA_HPG * T
    ncmp = kc_ref.shape[1]
    nblk = LANES

    q = q_ref[0]
    qpad = jnp.concatenate([q[:, LANES * h:LANES * (h + 1)] for h in range(NSA_HEADS)], axis=0)
    ones_v = jnp.ones((FAR_CHUNK, LANES), BF16)

    n_io = lax.broadcasted_iota(jnp.int32, (ncmp, LANES), 0)
    l_io = lax.broadcasted_iota(jnp.int32, (ncmp, LANES), 1)
    band = jnp.clip(n_io - 8 * i + 16, 0, 25)
    oh_c = jnp.where(band == (l_io & 63), 1.0, 0.0).astype(BF16)
    kc_aug = jnp.concatenate([kc_ref[0], oh_c], axis=1)
    qc_aug = jnp.concatenate([qpad, bcf_ref[...]], axis=1)
    s_c = _nt(qc_aug, kc_aug)
    m_c = jnp.max(s_c, axis=1, keepdims=True)
    m_c = jnp.where(m_c < 0.1 * MASK_BIG, 0.0, m_c)
    p_c = jnp.exp(s_c - m_c)
    l_c = jnp.sum(p_c, axis=1, keepdims=True)
    p_c = p_c * (1.0 / jnp.maximum(l_c, 1e-30))
    o_c = _dot(p_c.astype(BF16), vc_ref[0])

    j_io = lax.broadcasted_iota(jnp.int32, (nblk, T), 0)
    r_io = lax.broadcasted_iota(jnp.int32, (nblk, T), 1)
    j_f = j_io.astype(F32)
    cur = 2 * i + jnp.where(r_io >= SLC_BLOCK, 1, 0)
    forced = (j_io == 0) | (j_io == cur) | (j_io == cur - 1)
    valid = j_io <= cur
    jl_io = lax.broadcasted_iota(jnp.int32, (T, nblk), 1)
    near_blk = jl_io >= 2 * (i - 1)
    qn_parts, qf_parts = [], []
    for g in range(NSA_GROUPS):
        ps = p_c[g * grows:g * grows + T]
        for hh in range(1, NSA_HPG):
            ps = ps + p_c[g * grows + hh * T:g * grows + (hh + 1) * T]
        ps_hi = ps.astype(BF16)
        ps_lo = (ps - ps_hi.astype(F32)).astype(BF16)
        imp = _dot(ps_hi, c2s_ref[...]) + _dot(ps_lo, c2s_ref[...])
        score = jnp.where(forced, FORCED_SCORE, jnp.where(valid, imp.T, -1.0))
        selb = jnp.full((nblk, T), MASK_BIG, F32)
        for _ in range(SLC_TOPK):
            best = jnp.max(score, axis=0, keepdims=True)
            first = jnp.min(jnp.where(score == best, j_f, float(nblk)), axis=0, keepdims=True)
            hit = j_f == first
            selb = jnp.where(hit, 0.0, selb)
            score = jnp.where(hit, -jnp.inf, score)
        selb = selb.T
        sb_near = jnp.where(near_blk, selb, MASK_BIG).astype(BF16)
        sb_far = jnp.where(near_blk, MASK_BIG, selb).astype(BF16)
        qg = qpad[g * grows:(g + 1) * grows]
        qn_parts.append(jnp.concatenate([qg, jnp.concatenate([sb_near] * NSA_HPG, axis=0)], axis=1))
        qf_parts.append(jnp.concatenate([qg, jnp.concatenate([sb_far] * NSA_HPG, axis=0)], axis=1))
    q_near = jnp.concatenate(qn_parts, axis=0)
    q_far = jnp.concatenate(qf_parts, axis=0)

    k_near = jnp.concatenate([ks_ref[0, pl.ds(t0, NEAR_KEYS), :], oh_ref[pl.ds(t0, NEAR_KEYS), :]], axis=1)
    v_near = jnp.concatenate([vs_ref[0, pl.ds(t0, NEAR_KEYS), :], ones_v[:NEAR_KEYS]], axis=1)
    s_n = _nt(q_near, k_near) + bs_ref[...]
    m_n = jnp.max(s_n, axis=1, keepdims=True)
    p_n = jnp.exp(s_n - m_n)
    m_sc[...] = m_n
    acc_sc[...] = _dot(p_n.astype(BF16), v_near)

    def far_step(c, carry):
        start = pl.multiple_of(SLC_PAD + FAR_CHUNK * c, LANES)
        k_f = jnp.concatenate([ks_ref[0, pl.ds(start, FAR_CHUNK), :], oh_ref[pl.ds(start, FAR_CHUNK), :]], axis=1)
        v_f = jnp.concatenate([vs_ref[0, pl.ds(start, FAR_CHUNK), :], ones_v], axis=1)
        s_f = _nt(q_far, k_f)
        m_old = m_sc[...]
        m_new = jnp.maximum(m_old, jnp.max(s_f, axis=1, keepdims=True))
        alpha = jnp.exp(m_old - m_new)
        p_f = jnp.exp(s_f - m_new)
        acc_sc[...] = alpha * acc_sc[...] + _dot(p_f.astype(BF16), v_f)
        m_sc[...] = m_new
        return carry

    lax.fori_loop(0, (i + 2) // 4, far_step, 0)
    acc = acc_sc[...]
    o_s = acc[:, :LANES] * (1.0 / acc[:, LANES:LANES + 1])

    k_w = kw_ref[0, pl.ds(t0, WIN_KEYS), :]
    v_w = jnp.concatenate([vw_ref[0, pl.ds(t0, WIN_KEYS), :],
                           jnp.ones((WIN_KEYS, LANES), BF16)], axis=1)
    s_w = _nt(qpad, k_w) + bw_ref[...]
    c_io = lax.broadcasted_iota(jnp.int32, (rows, WIN_KEYS), 1)
    s_w = jnp.where(c_io + (t0 - WINDOW) >= 0, s_w, MASK_F32)
    m_w = jnp.max(s_w, axis=1, keepdims=True)
    p_w = jnp.exp(s_w - m_w)
    acc_w = _dot(p_w.astype(BF16), v_w)
    o_w = acc_w[:, :LANES] * (1.0 / acc_w[:, LANES:LANES + 1])

    gates = gate_ref[0]
    lane_hi = lax.broadcasted_iota(jnp.int32, (T, LANES), 1) >= NSA_D
    for h in range(NSA_HEADS):
        sl = slice(h * T, (h + 1) * T)
        o = (gates[:, h:h + 1] * o_c[sl] + gates[:, 8 + h:9 + h] * o_s[sl]
             + gates[:, 16 + h:17 + h] * o_w[sl])
        keep = lane_hi if h // NSA_HPG == 1 else jnp.logical_not(lane_hi)
        out_ref[0, :, LANES * h:LANES * (h + 1)] = jnp.where(keep, o, 0.0).astype(out_ref.dtype)


def _nsa(q, ks, vs, kw, vw, kc, vc, oh, gates, bcf, bs, bw, c2s):
    bsz, s = q.shape[:2]
    T = NSA_TILE
    rows = NSA_HEADS * T
    per_b = lambda a: pl.BlockSpec((1,) + a.shape[1:], lambda b, i: (b, 0, 0))
    return pl.pallas_call(
        _nsa_kernel,
        grid=(bsz, s // T),
        in_specs=[pl.BlockSpec((1, T, 1024), lambda b, i: (b, i, 0)),
                  per_b(ks), per_b(vs), per_b(kw), per_b(vw), per_b(kc), per_b(vc),
                  _const_spec(oh.shape),
                  pl.BlockSpec((1, T, LANES), lambda b, i: (b, i, 0)),
                  _const_spec(bcf.shape), _const_spec(bs.shape), _const_spec(bw.shape),
                  _const_spec(c2s.shape)],
        out_specs=pl.BlockSpec((1, T, 1024), lambda b, i: (b, i, 0)),
        out_shape=jax.ShapeDtypeStruct((bsz, s, 1024), BF16),
        scratch_shapes=[pltpu.VMEM((rows, 1), F32), pltpu.VMEM((rows, 2 * LANES), F32)],
        compiler_params=pltpu.CompilerParams(dimension_semantics=("parallel", "arbitrary"),
                                             vmem_limit_bytes=VMEM_LIMIT),
        name="nsa",
    )(q, ks, vs, kw, vw, kc, vc, oh, gates, bcf, bs, bw, c2s)


ML_ROWS = 256


def _mlstm_kernel(q_ref, k_ref, v_ref, o_ref, gif_ref, cw_ref, cb_ref, gb_ref, out_ref,
                  c_sc, m_sc, tail_sc):
    j = pl.program_id(1)
    R = ML_ROWS
    L = ML_CHUNK
    hp = lax.Precision.HIGHEST

    @pl.when(j == 0)
    def _():
        c_sc[...] = jnp.zeros_like(c_sc)
        m_sc[...] = jnp.zeros_like(m_sc)
        tail_sc[...] = jnp.zeros_like(tail_sc)

    raw = jnp.concatenate([q_ref[0], k_ref[0]], axis=1)
    ext = jnp.concatenate([tail_sc[...], raw], axis=0)
    tail_sc[...] = raw[R - 8:, :]
    conv = cb_ref[...]
    for t in range(CONV_WIDTH):
        conv = conv + ext[8 - (CONV_WIDTH - 1) + t:8 - (CONV_WIDTH - 1) + t + R, :] * cw_ref[t:t + 1, :]
    qk = conv * jax.nn.sigmoid(conv)
    qs = qk[:, :ML_HEADS * ML_D] * (ML_D ** -0.5)
    kk = qk[:, ML_HEADS * ML_D:]
    vv = v_ref[0]
    og = o_ref[0]

    pre = gif_ref[...] + gb_ref[...]
    row8 = lax.broadcasted_iota(jnp.int32, (8, R), 0)
    logf = jnp.minimum(pre, 0.0) - jnp.log(1.0 + jnp.exp(-jnp.abs(pre)))
    g8 = jnp.where(row8 < ML_HEADS, pre, logf)

    a_io = lax.broadcasted_iota(jnp.int32, (L, L), 0)
    b_io = lax.broadcasted_iota(jnp.int32, (L, L), 1)
    causal = b_io <= a_io
    tri_low = jnp.where(causal, 1.0, 0.0)
    tri_up = jnp.where(a_io <= b_io, 1.0, 0.0)
    eye = jnp.where(a_io == b_io, 1.0, 0.0)
    ones_l = jnp.ones((L, ML_D), F32)

    for c in range(R // L):
        g8c = g8[:, c * L:(c + 1) * L]
        cum_col = _nt(tri_low, g8c, hp)
        id_col = _nt(eye, g8c, hp)
        cum_row = _dot(g8c, tri_up, hp)
        rs = slice(c * L, (c + 1) * L)
        for h in range(ML_HEADS):
            cs = slice(h * ML_D, (h + 1) * ML_D)
            qh, kh, vh = qs[rs, cs], kk[rs, cs], vv[rs, cs]
            b_row = cum_row[ML_HEADS + h:ML_HEADS + h + 1, :]
            li_row = g8c[h:h + 1, :]
            b_col = cum_col[:, ML_HEADS + h:ML_HEADS + h + 1]
            li_col = id_col[:, h:h + 1]
            gsum = b_row[:, L - 1:L]
            m_prev = m_sc[h:h + 1, 0:1]
            c_aug = c_sc[h]
            log_d = jnp.where(causal, b_col - b_row + li_row, -jnp.inf)
            inter = b_col + m_prev
            m_row = jnp.maximum(inter, jnp.max(log_d, axis=1, keepdims=True))
            w = _nt(qh, kh, hp) * jnp.exp(log_d - m_row)
            inter_scale = jnp.exp(inter - m_row)
            v_aug = jnp.concatenate([vh, ones_l], axis=1)
            nd = inter_scale * _dot(qh, c_aug, hp) + _dot(w, v_aug, hp)
            den = nd[:, ML_D:ML_D + 1]
            hval = nd[:, :ML_D] / jnp.maximum(jnp.abs(den), jnp.exp(-m_row))
            out_ref[0, rs, cs] = (og[rs, cs] * hval).astype(out_ref.dtype)
            log_src_row = gsum - b_row + li_row
            m_new = jnp.maximum(gsum + m_prev, jnp.max(log_src_row, axis=1, keepdims=True))
            decay = jnp.exp(gsum + m_prev - m_new)
            src_col = jnp.exp(gsum - b_col + li_col - m_new)
            upd = lax.dot_general(kh * src_col, v_aug, (((0,), (0,)), ((), ())),
                                  precision=hp, preferred_element_type=F32)
            c_sc[h] = decay * c_aug + upd
            m_sc[h:h + 1, :] = jnp.broadcast_to(m_new, (1, LANES))


def _mlstm(mq, mk, mv, og, gif, conv_w, conv_b, gate_b):
    bsz, s, width = mq.shape
    R = ML_ROWS
    nblk = s // R
    seq = lambda: pl.BlockSpec((1, R, width), lambda b, j: (b, j, 0))
    return pl.pallas_call(
        _mlstm_kernel,
        grid=(bsz, nblk),
        in_specs=[seq(), seq(), seq(), seq(),
                  pl.BlockSpec((8, R), lambda b, j: (0, b * nblk + j)),
                  _const_spec(conv_w.shape), _const_spec(conv_b.shape), _const_spec(gate_b.shape)],
        out_specs=seq(),
        out_shape=jax.ShapeDtypeStruct((bsz, s, width), BF16),
        scratch_shapes=[pltpu.VMEM((ML_HEADS, ML_D, 2 * ML_D), F32),
                        pltpu.VMEM((8, LANES), F32),
                        pltpu.VMEM((8, 2 * width), F32)],
        compiler_params=pltpu.CompilerParams(dimension_semantics=("parallel", "arbitrary"),
                                             vmem_limit_bytes=VMEM_LIMIT),
        name="mlstm",
    )(mq, mk, mv, og, gif, conv_w, conv_b, gate_b)


FF_CHUNK = 512


def _merge_ffn_kernel(x_ref, ya_ref, yb_ref, mg_ref, wa_ref, wb_ref, wo_ref, g2_ref, w1_ref, w2_ref,
                      out_ref):
    mg = mg_ref[...]
    mixed = (mg[:, :D_MODEL].astype(F32) * _dot(ya_ref[...], wa_ref[...])
             + mg[:, D_MODEL:].astype(F32) * _dot(yb_ref[...], wb_ref[...]))
    x1 = x_ref[...] + _dot(mixed.astype(BF16), wo_ref[...])
    h2 = x1 * lax.rsqrt(jnp.mean(x1 * x1, axis=-1, keepdims=True) + RMS_EPS) * g2_ref[...]
    h2 = h2.astype(BF16)
    acc = x1
    for c in range(D_FF // FF_CHUNK):
        a = jnp.maximum(_dot(h2, w1_ref[:, c * FF_CHUNK:(c + 1) * FF_CHUNK]), 0.0)
        acc = acc + _dot((a * a).astype(BF16), w2_ref[c * FF_CHUNK:(c + 1) * FF_CHUNK, :])
    out_ref[...] = acc


def _merge_ffn(x2d, ya, yb, mg, wa, wb, wo, g2, w1, w2, tm=256):
    n = x2d.shape[0]
    row = lambda w: pl.BlockSpec((tm, w), lambda i: (i, 0))
    return pl.pallas_call(
        _merge_ffn_kernel,
        grid=(n // tm,),
        in_specs=[row(D_MODEL), row(1024), row(512), row(2048),
                  _const_spec(wa.shape), _const_spec(wb.shape), _const_spec(wo.shape),
                  _const_spec(g2.shape), _const_spec(w1.shape), _const_spec(w2.shape)],
        out_specs=row(D_MODEL),
        out_shape=jax.ShapeDtypeStruct((n, D_MODEL), F32),
        compiler_params=pltpu.CompilerParams(dimension_semantics=("parallel",),
                                             vmem_limit_bytes=VMEM_LIMIT),
        name="merge_ffn",
    )(x2d, ya, yb, mg, wa, wb, wo, g2, w1, w2)


def _proj_column_index():
    widths = (512, 128, 128, 128, 128, 128, 128, 24, 512, 512, 512, 4, 4, 512, 2048)
    off = np.concatenate([[0], np.cumsum(widths)])
    (nq, nkc, nvc, nks, nvs, nkw, nvw, ngate, mq, mk, mv, mi, mf, mo, mgate) = off[:-1]
    idx = np.full((_W_COLS,), -1, np.int64)
    for h in range(NSA_HEADS):
        g = h // NSA_HPG
        dst = _Q_OFF + LANES * h + NSA_D * g
        idx[dst:dst + NSA_D] = nq + NSA_D * h + np.arange(NSA_D)
    for dst, src in ((_KS_OFF, nks), (_KW_OFF, nkw), (_KC_OFF, nkc), (_VC_OFF, nvc),
                     (_VS_OFF, nvs), (_VW_OFF, nvw)):
        idx[dst:dst + LANES] = src + np.arange(LANES)
    for br in range(3):
        for h in range(NSA_HEADS):
            idx[_GATE_OFF + br * 8 + h] = ngate + h * 3 + br
    for dst, src in ((_MQ_OFF, mq), (_MK_OFF, mk), (_MV_OFF, mv), (_MO_OFF, mo)):
        idx[dst:dst + 512] = src + np.arange(512)
    idx[_MG_OFF:_MG_OFF + 2048] = mgate + np.arange(2048)
    if_idx = np.concatenate([mi + np.arange(4), mf + np.arange(4)])
    return idx, if_idx


def _t5_bucket(dist):
    n = jnp.maximum(dist, 0)
    max_exact = REL_BUCKETS // 2
    nf = jnp.maximum(n, 1).astype(F32)
    large = max_exact + (jnp.log(nf / max_exact) / math.log(REL_MAX_DIST / max_exact)
                         * (REL_BUCKETS - max_exact)).astype(jnp.int32)
    return jnp.where(n < max_exact, n, jnp.minimum(large, REL_BUCKETS - 1))


def _bias_tables(rel_table):
    T = NSA_TILE
    ndist = WIN_KEYS
    tab = rel_table[_t5_bucket(jnp.arange(ndist))] - rel_table[REL_BUCKETS - 1][None, :]
    tab = tab.T
    r = np.arange(T)[:, None]

    def toeplitz(dist, ok, masked):
        vals = jnp.take(tab, jnp.asarray(np.clip(dist, 0, ndist - 1)), axis=1)
        return jnp.where(jnp.asarray(ok)[None], vals, masked)

    c = np.arange(NEAR_KEYS)[None, :]
    d = r - c + NSA_TILE
    bs = toeplitz(d, d >= 0, MASK_F32).reshape(NSA_HEADS * T, NEAR_KEYS)
    c = np.arange(WIN_KEYS)[None, :]
    d = r - c + WINDOW
    bw = toeplitz(d, (d >= 0) & (d < WINDOW), MASK_F32).reshape(NSA_HEADS * T, WIN_KEYS)
    m = np.arange(64)[None, :]
    d = r - 16 * m + 225
    band = (m >= 1) & (m <= 24)
    vals = toeplitz(d, band & (d >= 0), 0.0)
    dead = (band & (d < 0)) | (m == 25)
    vals = jnp.where(jnp.asarray(dead)[None], MASK_BIG, vals)
    hi = vals.astype(BF16)
    lo = (vals - hi.astype(F32)).astype(BF16)
    bcf = jnp.concatenate([hi, lo], axis=-1).reshape(NSA_HEADS * T, LANES)
    return bcf, bs.astype(F32), bw.astype(F32)


def _segments(a, bsz, s):
    nseg = s // CMP_STRIDE
    a = a.reshape(bsz, nseg, CMP_STRIDE, NSA_GROUPS, NSA_D).transpose(0, 3, 1, 2, 4)
    return a.reshape(bsz, NSA_GROUPS, nseg, CMP_STRIDE * NSA_D)


def _layer(l, x2d, bsz, s, tables, norm1_g, w_in, nsa_q_gain, nsa_k_gain, cmp_k, cmp_v,
           ml_conv_w, ml_conv_b, ml_i_bias, ml_f_bias, w_branch_a, w_branch_b, w_out, norm2_g, w_ff1, w_ff2):
    n = bsz * s
    nseg = s // CMP_STRIDE
    oh, c2s, bcf, bs, bw = tables
    idx, if_idx = _proj_column_index()
    w_all = jnp.where(jnp.asarray(idx >= 0)[None, :],
                      jnp.take(w_in[l], jnp.asarray(np.maximum(idx, 0)), axis=1), 0.0).astype(BF16)
    w_if = jnp.take(w_in[l], jnp.asarray(if_idx), axis=1).T.astype(BF16)
    qg_pad = jnp.concatenate([nsa_q_gain[l]] * 2)[None, :]
    kg_pad = jnp.stack([jnp.concatenate([nsa_k_gain[l, 1]] * 2), jnp.concatenate([nsa_k_gain[l, 2]] * 2)])
    (q, ks, kw, kc, vc, vs, vw, gates, mq, mk, mv, og, mg, gif) = _proj(
        x2d, norm1_g[l][None, :], w_all, w_if, qg_pad, kg_pad)

    def compress(a, params, gain, normalize):
        pos, w1, b1, w2, b2 = (p[l] for p in params)
        out = _compress(_segments(a, bsz, s), w1.astype(BF16), pos.reshape(1, -1).astype(BF16),
                        b1[None, :], w2.astype(BF16), b2[None, :], gain[None, :], normalize)
        return out.transpose(0, 2, 1, 3).reshape(bsz, nseg, LANES).astype(BF16)

    kcmp = compress(kc, cmp_k, nsa_k_gain[l, 0], True)
    vcmp = compress(vc, cmp_v, jnp.ones((NSA_D,), F32), False)

    seq = lambda a: a.reshape(bsz, s, a.shape[-1])
    front = lambda a, p: jnp.pad(seq(a), ((0, 0), (p, 0), (0, 0)))
    y_a = _nsa(seq(q), front(ks, SLC_PAD), front(vs, SLC_PAD), front(kw, WINDOW), front(vw, WINDOW),
               kcmp, vcmp, oh, seq(gates), bcf, bs, bw, c2s)

    gate_b = jnp.concatenate([ml_i_bias[l], ml_f_bias[l]])[:, None]
    y_b = _mlstm(seq(mq), seq(mk), seq(mv), seq(og), gif, ml_conv_w[l], ml_conv_b[l][None, :], gate_b)

    rows = np.full((NSA_HEADS * LANES,), -1, np.int64)
    for h in range(NSA_HEADS):
        g = h // NSA_HPG
        rows[LANES * h + NSA_D * g:LANES * h + NSA_D * (g + 1)] = NSA_D * h + np.arange(NSA_D)
    wa = jnp.where(jnp.asarray(rows >= 0)[:, None],
                   jnp.take(w_branch_a[l], jnp.asarray(np.maximum(rows, 0)), axis=0), 0.0).astype(BF16)
    out = _merge_ffn(x2d, y_a.reshape(n, 1024), y_b.reshape(n, 512), mg, wa,
                     w_branch_b[l].astype(BF16), w_out[l].astype(BF16), norm2_g[l][None, :],
                     w_ff1[l].astype(BF16), w_ff2[l].astype(BF16))
    return out, y_a, y_b


def _tables(s, rel_table):
    nseg = s // CMP_STRIDE
    nsel = s // SLC_BLOCK
    blk_of_key = np.arange(s) // SLC_BLOCK
    oh = np.concatenate([np.ones((SLC_PAD, LANES), np.float32),
                         (blk_of_key[:, None] == np.arange(LANES)[None, :]).astype(np.float32)], axis=0)
    ci = np.arange(nseg)[:, None] * CMP_STRIDE
    sj = np.arange(LANES)[None, :] * SLC_BLOCK
    c2s = ((ci < sj + SLC_BLOCK) & (ci + CMP_LEN > sj) & (np.arange(LANES)[None, :] < nsel)
           & (np.arange(nseg)[:, None] < nseg - 1))
    return (jnp.asarray(oh, BF16), jnp.asarray(c2s.astype(np.float32), BF16)) + _bias_tables(rel_table)


def kernel(x, norm1_g, w_in, nsa_q_gain, nsa_k_gain, cmp_k_pos, cmp_k_w1, cmp_k_b1, cmp_k_w2, cmp_k_b2, cmp_v_pos, cmp_v_w1, cmp_v_b1, cmp_v_w2, cmp_v_b2, rel_table, ml_conv_w, ml_conv_b, ml_i_bias, ml_f_bias, w_branch_a, w_branch_b, w_out, norm2_g, w_ff1, w_ff2):
    bsz, s, _ = x.shape
    tables = _tables(s, rel_table)
    x2d = x.reshape(bsz * s, D_MODEL)
    for l in range(norm1_g.shape[0]):
        x2d, _, _ = _layer(l, x2d, bsz, s, tables, norm1_g, w_in, nsa_q_gain, nsa_k_gain,
                           (cmp_k_pos, cmp_k_w1, cmp_k_b1, cmp_k_w2, cmp_k_b2),
                           (cmp_v_pos, cmp_v_w1, cmp_v_b1, cmp_v_w2, cmp_v_b2),
                           ml_conv_w, ml_conv_b, ml_i_bias, ml_f_bias,
                           w_branch_a, w_branch_b, w_out, norm2_g, w_ff1, w_ff2)
    return x2d.reshape(bsz, s, D_MODEL)
```

```python
import functools
import math

import numpy as np
import jax
import jax.numpy as jnp
from jax import lax
from jax.experimental import pallas as pl
from jax.experimental.pallas import tpu as pltpu

F32 = jnp.float32
BF16 = jnp.bfloat16

D_MODEL = 1024
NSA_HEADS = 8
NSA_GROUPS = 2
NSA_HPG = NSA_HEADS // NSA_GROUPS
NSA_D = 64
CMP_LEN = 32
CMP_STRIDE = 16
CMP_HIDDEN = 256
SLC_BLOCK = 64
SLC_TOPK = 16
WINDOW = 512
FORCED_SCORE = 1e4
ML_HEADS = 4
ML_D = 128
ML_CHUNK = 64
CONV_WIDTH = 4
D_FF = 4 * D_MODEL
REL_BUCKETS = 32
REL_MAX_DIST = 128
RMS_EPS = 1e-6

LANES = 128
NSA_TILE = 128
SLC_PAD = 128
FAR_CHUNK = 512
NEAR_KEYS = 2 * NSA_TILE
WIN_KEYS = WINDOW + NSA_TILE
BAND_SLOTS = 26
MASK_BIG = -1e9
MASK_F32 = -1e30
LOG2E = math.log2(math.e)
SAFE_SHIFT_LOG2 = 50.0
VMEM_LIMIT = 56 * 1024 * 1024

_Q_OFF, _KS_OFF, _KW_OFF, _KC_OFF, _VC_OFF, _VS_OFF, _VW_OFF = 0, 1024, 1152, 1280, 1408, 1536, 1664
_GATE_OFF, _MQ_OFF, _MK_OFF, _MV_OFF, _MO_OFF, _MG_OFF, _W_COLS = 1792, 1920, 2432, 2944, 3456, 3968, 6016


def _nt(a, b, precision=None):
    return lax.dot_general(a, b, (((1,), (1,)), ((), ())), precision=precision,
                           preferred_element_type=F32)


def _dot(a, b, precision=None):
    return jnp.dot(a, b, precision=precision, preferred_element_type=F32)


def _const_spec(shape):
    nd = len(shape)
    return pl.BlockSpec(shape, lambda *_: (0,) * nd, pipeline_mode=pl.Buffered(1))


def _proj_kernel(x_ref, g1_ref, w_ref, wif_ref, qg_ref, kg_ref,
                 q_ref, ks_ref, kw_ref, kc_ref, vc_ref, vs_ref, vw_ref, gate_ref,
                 mq_ref, mk_ref, mv_ref, mo_ref, mg_ref, gif_ref):
    x = x_ref[...]
    h = x * lax.rsqrt(jnp.mean(x * x, axis=-1, keepdims=True) + RMS_EPS) * g1_ref[...]
    hb = h.astype(BF16)

    def proj(off, width):
        return _dot(hb, w_ref[:, off:off + width])

    lane = lax.broadcasted_iota(jnp.int32, (1, LANES), 1)
    low = lane < NSA_D

    for hh in range(NSA_HEADS):
        blk = proj(_Q_OFF + LANES * hh, LANES)
        ms = jnp.sum(blk * blk, axis=-1, keepdims=True) * (1.0 / NSA_D)
        qn = blk * lax.rsqrt(ms + RMS_EPS) * qg_ref[...] * (NSA_D ** -0.5 * LOG2E)
        q_ref[:, LANES * hh:LANES * (hh + 1)] = qn.astype(q_ref.dtype)

    for off, ref, row in ((_KS_OFF, ks_ref, 0), (_KW_OFF, kw_ref, 1)):
        blk = proj(off, LANES)
        sq = blk * blk
        ms0 = jnp.sum(jnp.where(low, sq, 0.0), axis=-1, keepdims=True) * (1.0 / NSA_D)
        ms1 = jnp.sum(jnp.where(low, 0.0, sq), axis=-1, keepdims=True) * (1.0 / NSA_D)
        inv = jnp.where(low, lax.rsqrt(ms0 + RMS_EPS), lax.rsqrt(ms1 + RMS_EPS))
        ref[...] = (blk * inv * kg_ref[row:row + 1, :]).astype(ref.dtype)

    for off, ref in ((_KC_OFF, kc_ref), (_VC_OFF, vc_ref), (_VS_OFF, vs_ref), (_VW_OFF, vw_ref)):
        ref[...] = proj(off, LANES).astype(ref.dtype)

    gate_ref[...] = jax.nn.sigmoid(proj(_GATE_OFF, LANES))
    mq_ref[...] = proj(_MQ_OFF, 512)
    mk_ref[...] = proj(_MK_OFF, 512)
    mv_ref[...] = proj(_MV_OFF, 512)
    mo_ref[...] = jax.nn.sigmoid(proj(_MO_OFF, 512))
    for c in range(4):
        mg_ref[:, 512 * c:512 * (c + 1)] = jax.nn.sigmoid(proj(_MG_OFF + 512 * c, 512)).astype(mg_ref.dtype)
    gif_ref[...] = _nt(wif_ref[...], hb)


def _proj(x2d, g1, w_all, w_if, qg_pad, kg_pad, tm=256):
    n = x2d.shape[0]
    row = lambda w: pl.BlockSpec((tm, w), lambda i: (i, 0))
    out_shapes = (
        jax.ShapeDtypeStruct((n, 1024), BF16),
        jax.ShapeDtypeStruct((n, LANES), BF16),
        jax.ShapeDtypeStruct((n, LANES), BF16),
        jax.ShapeDtypeStruct((n, LANES), BF16),
        jax.ShapeDtypeStruct((n, LANES), BF16),
        jax.ShapeDtypeStruct((n, LANES), BF16),
        jax.ShapeDtypeStruct((n, LANES), BF16),
        jax.ShapeDtypeStruct((n, LANES), F32),
        jax.ShapeDtypeStruct((n, 512), F32),
        jax.ShapeDtypeStruct((n, 512), F32),
        jax.ShapeDtypeStruct((n, 512), F32),
        jax.ShapeDtypeStruct((n, 512), F32),
        jax.ShapeDtypeStruct((n, 2048), BF16),
        jax.ShapeDtypeStruct((8, n), F32),
    )
    out_specs = (row(1024),) + (row(LANES),) * 7 + (row(512),) * 4 + (row(2048),
                 pl.BlockSpec((8, tm), lambda i: (0, i)))
    return pl.pallas_call(
        _proj_kernel,
        grid=(n // tm,),
        in_specs=[row(D_MODEL), _const_spec((1, D_MODEL)), _const_spec((D_MODEL, _W_COLS)),
                  _const_spec((8, D_MODEL)), _const_spec((1, LANES)), _const_spec((2, LANES))],
        out_specs=out_specs,
        out_shape=out_shapes,
        compiler_params=pltpu.CompilerParams(dimension_semantics=("parallel",),
                                             vmem_limit_bytes=VMEM_LIMIT),
        name="proj",
    )(x2d, g1, w_all, w_if, qg_pad, kg_pad)


def _compress_kernel(seg_ref, w1_ref, pos_ref, b1_ref, w2_ref, b2_ref, gain_ref, out_ref, *, normalize):
    seg = seg_ref[0]
    nseg, width = seg.shape
    c = math.sqrt(2.0 / math.pi)
    pos_lo = jnp.broadcast_to(pos_ref[0:1, :], (8, width))
    pos_hi = jnp.broadcast_to(pos_ref[1:2, :], (8, width))
    outs = []
    for g in range(NSA_GROUPS):
        a = _dot(seg, w1_ref[g, 0])
        b = _dot(seg, w1_ref[g, 1])
        posb = (_dot(pos_lo, w1_ref[g, 0]) + _dot(pos_hi, w1_ref[g, 1]))[0:1] + b1_ref[...]
        pre = a + pltpu.roll(b, nseg - 1, 0) + posb
        hid = 0.5 * pre * (1.0 + jnp.tanh(c * (pre + 0.044715 * (pre * pre * pre))))
        out = _dot(hid.astype(BF16), w2_ref[...]) + b2_ref[...]
        if normalize:
            ms = jnp.mean(out * out, axis=-1, keepdims=True)
            out = out * lax.rsqrt(ms + RMS_EPS) * gain_ref[...]
        outs.append(out)
    out_ref[0] = jnp.concatenate(outs, axis=1).astype(out_ref.dtype)


def _compress(segs, w1x, posx, b1, w2, b2, gain, normalize):
    bsz, nseg, width = segs.shape
    return pl.pallas_call(
        functools.partial(_compress_kernel, normalize=normalize),
        grid=(bsz,),
        in_specs=[pl.BlockSpec((1, nseg, width), lambda b: (b, 0, 0)),
                  _const_spec(w1x.shape), _const_spec(posx.shape), _const_spec(b1.shape),
                  _const_spec(w2.shape), _const_spec(b2.shape), _const_spec(gain.shape)],
        out_specs=pl.BlockSpec((1, nseg, LANES), lambda b: (b, 0, 0)),
        out_shape=jax.ShapeDtypeStruct((bsz, nseg, LANES), BF16),
        compiler_params=pltpu.CompilerParams(dimension_semantics=("parallel",),
                                             vmem_limit_bytes=VMEM_LIMIT),
        name="compress",
    )(segs, w1x, posx, b1, w2, b2, gain)


def _nsa_kernel(q_ref, ks_ref, vs_ref, kw_ref, vw_ref, kc_ref, vc_ref, oh_ref, gate_ref,
                bcf_ref, bs_ref, bw_ref, c2s_ref, shift_ref, out_ref, acc_sc, *maybe_m_sc, online):
    i = pl.program_id(1)
    t0 = pl.multiple_of(i * NSA_TILE, NSA_TILE)
    T = NSA_TILE
    rows = NSA_HEADS * T
    grows = NSA_HPG * T
    ncmp = kc_ref.shape[1]
    nblk = LANES

    q = q_ref[0]
    qpad = jnp.concatenate([q[:, LANES * h:LANES * (h + 1)] for h in range(NSA_HEADS)], axis=0)
    ones_v = jnp.ones((WIN_KEYS, LANES), BF16)

    def attend(s, v, n_keys):
        if online:
            m = jnp.max(s, axis=1, keepdims=True)
            m = jnp.where(m < 0.1 * MASK_BIG, 0.0, m)
            s = s - m
        p = jnp.exp2(s)
        acc = _dot(p.astype(BF16), jnp.concatenate([v, ones_v[:n_keys]], axis=1))
        return p, acc[:, :LANES], acc[:, LANES:LANES + 1]

    n_io = lax.broadcasted_iota(jnp.int32, (ncmp, LANES), 0)
    l_io = lax.broadcasted_iota(jnp.int32, (ncmp, LANES), 1)
    band = jnp.clip(n_io - 8 * i + 16, 0, BAND_SLOTS - 1)
    oh_c = jnp.where(band == (l_io & 63), 1.0, 0.0).astype(BF16)
    kc_aug = jnp.concatenate([kc_ref[0], oh_c], axis=1)
    qc_aug = jnp.concatenate([qpad, bcf_ref[...]], axis=1)
    p_c, num_c, l_c = attend(_nt(qc_aug, kc_aug), vc_ref[0], ncmp)
    inv_c = 1.0 / jnp.maximum(l_c, 1e-30)
    o_c = num_c * inv_c
    p_c = p_c * inv_c

    j_io = lax.broadcasted_iota(jnp.int32, (nblk, T), 0)
    r_io = lax.broadcasted_iota(jnp.int32, (nblk, T), 1)
    j_f = j_io.astype(F32)
    cur = 2 * i + jnp.where(r_io >= SLC_BLOCK, 1, 0)
    forced = (j_io == 0) | (j_io == cur) | (j_io == cur - 1)
    valid = j_io <= cur
    jl_io = lax.broadcasted_iota(jnp.int32, (T, nblk), 1)
    near_blk = jl_io >= 2 * (i - 1)
    far_pick = shift_ref[0:1, 0:1]
    qn_parts, qf_parts = [], []
    for g in range(NSA_GROUPS):
        ps = p_c[g * grows:g * grows + T]
        for hh in range(1, NSA_HPG):
            ps = ps + p_c[g * grows + hh * T:g * grows + (hh + 1) * T]
        ps_hi = ps.astype(BF16)
        ps_lo = (ps - ps_hi.astype(F32)).astype(BF16)
        imp = _dot(ps_hi, c2s_ref[...]) + _dot(ps_lo, c2s_ref[...])
        score = jnp.where(forced, FORCED_SCORE, jnp.where(valid, imp.T, -1.0))
        picked = jnp.zeros((nblk, T), F32)
        for _ in range(SLC_TOPK):
            best = jnp.max(score, axis=0, keepdims=True)
            first = jnp.min(jnp.where(score == best, j_f, float(nblk)), axis=0, keepdims=True)
            hit = j_f == first
            picked = jnp.where(hit, 1.0, picked)
            score = jnp.where(hit, -jnp.inf, score)
        picked = picked.T > 0.5
        sb_near = jnp.where(picked & near_blk, 0.0, MASK_BIG).astype(BF16)
        sb_far = jnp.where(picked & jnp.logical_not(near_blk), far_pick, MASK_BIG).astype(BF16)
        qg = qpad[g * grows:(g + 1) * grows]
        qn_parts.append(jnp.concatenate([qg, jnp.concatenate([sb_near] * NSA_HPG, axis=0)], axis=1))
        qf_parts.append(jnp.concatenate([qg, jnp.concatenate([sb_far] * NSA_HPG, axis=0)], axis=1))
    q_near = jnp.concatenate(qn_parts, axis=0)
    q_far = jnp.concatenate(qf_parts, axis=0)

    k_near = jnp.concatenate([ks_ref[0, pl.ds(t0, NEAR_KEYS), :], oh_ref[pl.ds(t0, NEAR_KEYS), :]], axis=1)
    v_near = jnp.concatenate([vs_ref[0, pl.ds(t0, NEAR_KEYS), :], ones_v[:NEAR_KEYS]], axis=1)
    s_n = _nt(q_near, k_near) + bs_ref[...]
    if online:
        m_sc, = maybe_m_sc
        m_n = jnp.max(s_n, axis=1, keepdims=True)
        m_sc[...] = m_n
        s_n = s_n - m_n
    acc_sc[...] = _dot(jnp.exp2(s_n).astype(BF16), v_near)

    def far_step(c, carry):
        start = pl.multiple_of(SLC_PAD + FAR_CHUNK * c, LANES)
        k_f = jnp.concatenate([ks_ref[0, pl.ds(start, FAR_CHUNK), :], oh_ref[pl.ds(start, FAR_CHUNK), :]], axis=1)
        v_f = jnp.concatenate([vs_ref[0, pl.ds(start, FAR_CHUNK), :], ones_v[:FAR_CHUNK]], axis=1)
        s_f = _nt(q_far, k_f)
        if online:
            m_old = m_sc[...]
            m_new = jnp.maximum(m_old, jnp.max(s_f, axis=1, keepdims=True))
            m_sc[...] = m_new
            acc_sc[...] = (jnp.exp2(m_old - m_new) * acc_sc[...]
                           + _dot(jnp.exp2(s_f - m_new).astype(BF16), v_f))
        else:
            acc_sc[...] += _dot(jnp.exp2(s_f).astype(BF16), v_f)
        return carry

    lax.fori_loop(0, (i + 2) // 4, far_step, 0)
    acc = acc_sc[...]
    o_s = acc[:, :LANES] * (1.0 / acc[:, LANES:LANES + 1])

    s_w = _nt(qpad, kw_ref[0, pl.ds(t0, WIN_KEYS), :]) + bw_ref[...]
    c_io = lax.broadcasted_iota(jnp.int32, (rows, WIN_KEYS), 1)
    s_w = jnp.where(c_io + (t0 - WINDOW) >= 0, s_w, MASK_F32)
    _, num_w, l_w = attend(s_w, vw_ref[0, pl.ds(t0, WIN_KEYS), :], WIN_KEYS)
    o_w = num_w * (1.0 / l_w)

    gates = gate_ref[0]
    lane_hi = lax.broadcasted_iota(jnp.int32, (T, LANES), 1) >= NSA_D
    for h in range(NSA_HEADS):
        sl = slice(h * T, (h + 1) * T)
        o = (gates[:, h:h + 1] * o_c[sl] + gates[:, 8 + h:9 + h] * o_s[sl]
             + gates[:, 16 + h:17 + h] * o_w[sl])
        keep = lane_hi if h // NSA_HPG == 1 else jnp.logical_not(lane_hi)
        out_ref[0, :, LANES * h:LANES * (h + 1)] = jnp.where(keep, o, 0.0).astype(out_ref.dtype)


def _nsa(online, q, ks, vs, kw, vw, kc, vc, oh, gates, bcf, bs, bw, c2s, shift):
    bsz, s = q.shape[:2]
    T = NSA_TILE
    rows = NSA_HEADS * T
    per_b = lambda a: pl.BlockSpec((1,) + a.shape[1:], lambda b, i: (b, 0, 0))
    scratch = [pltpu.VMEM((rows, 2 * LANES), F32)]
    if online:
        scratch.append(pltpu.VMEM((rows, 1), F32))
    return pl.pallas_call(
        functools.partial(_nsa_kernel, online=online),
        grid=(bsz, s // T),
        in_specs=[pl.BlockSpec((1, T, 1024), lambda b, i: (b, i, 0)),
                  per_b(ks), per_b(vs), per_b(kw), per_b(vw), per_b(kc), per_b(vc),
                  _const_spec(oh.shape),
                  pl.BlockSpec((1, T, LANES), lambda b, i: (b, i, 0)),
                  _const_spec(bcf.shape), _const_spec(bs.shape), _const_spec(bw.shape),
                  _const_spec(c2s.shape), _const_spec(shift.shape)],
        out_specs=pl.BlockSpec((1, T, 1024), lambda b, i: (b, i, 0)),
        out_shape=jax.ShapeDtypeStruct((bsz, s, 1024), BF16),
        scratch_shapes=scratch,
        compiler_params=pltpu.CompilerParams(dimension_semantics=("parallel", "arbitrary"),
                                             vmem_limit_bytes=VMEM_LIMIT),
        name="nsa_online" if online else "nsa",
    )(q, ks, vs, kw, vw, kc, vc, oh, gates, bcf, bs, bw, c2s, shift)


ML_ROWS = 256


def _mlstm_kernel(q_ref, k_ref, v_ref, o_ref, gif_ref, cw_ref, cb_ref, gb_ref, out_ref,
                  c_sc, m_sc, tail_sc):
    j = pl.program_id(1)
    R = ML_ROWS
    L = ML_CHUNK
    hp = lax.Precision.HIGHEST

    @pl.when(j == 0)
    def _():
        c_sc[...] = jnp.zeros_like(c_sc)
        m_sc[...] = jnp.zeros_like(m_sc)
        tail_sc[...] = jnp.zeros_like(tail_sc)

    raw = jnp.concatenate([q_ref[0], k_ref[0]], axis=1)
    ext = jnp.concatenate([tail_sc[...], raw], axis=0)
    tail_sc[...] = raw[R - 8:, :]
    conv = cb_ref[...]
    for t in range(CONV_WIDTH):
        conv = conv + ext[8 - (CONV_WIDTH - 1) + t:8 - (CONV_WIDTH - 1) + t + R, :] * cw_ref[t:t + 1, :]
    qk = conv * jax.nn.sigmoid(conv)
    qs = qk[:, :ML_HEADS * ML_D] * (ML_D ** -0.5)
    kk = qk[:, ML_HEADS * ML_D:]
    vv = v_ref[0]
    og = o_ref[0]

    pre = gif_ref[...] + gb_ref[...]
    row8 = lax.broadcasted_iota(jnp.int32, (8, R), 0)
    logf = jnp.minimum(pre, 0.0) - jnp.log(1.0 + jnp.exp(-jnp.abs(pre)))
    g8 = jnp.where(row8 < ML_HEADS, pre, logf)

    a_io = lax.broadcasted_iota(jnp.int32, (L, L), 0)
    b_io = lax.broadcasted_iota(jnp.int32, (L, L), 1)
    causal = b_io <= a_io
    tri_low = jnp.where(causal, 1.0, 0.0)
    tri_up = jnp.where(a_io <= b_io, 1.0, 0.0)
    eye = jnp.where(a_io == b_io, 1.0, 0.0)
    ones_l = jnp.ones((L, ML_D), F32)

    for c in range(R // L):
        g8c = g8[:, c * L:(c + 1) * L]
        cum_col = _nt(tri_low, g8c, hp)
        id_col = _nt(eye, g8c, hp)
        cum_row = _dot(g8c, tri_up, hp)
        rs = slice(c * L, (c + 1) * L)
        for h in range(ML_HEADS):
            cs = slice(h * ML_D, (h + 1) * ML_D)
            qh, kh, vh = qs[rs, cs], kk[rs, cs], vv[rs, cs]
            b_row = cum_row[ML_HEADS + h:ML_HEADS + h + 1, :]
            li_row = g8c[h:h + 1, :]
            b_col = cum_col[:, ML_HEADS + h:ML_HEADS + h + 1]
            li_col = id_col[:, h:h + 1]
            gsum = b_row[:, L - 1:L]
            m_prev = m_sc[h:h + 1, 0:1]
            c_aug = c_sc[h]
            log_d = jnp.where(causal, b_col - b_row + li_row, -jnp.inf)
            inter = b_col + m_prev
            m_row = jnp.maximum(inter, jnp.max(log_d, axis=1, keepdims=True))
            w = _nt(qh, kh, hp) * jnp.exp(log_d - m_row)
            inter_scale = jnp.exp(inter - m_row)
            v_aug = jnp.concatenate([vh, ones_l], axis=1)
            nd = inter_scale * _dot(qh, c_aug, hp) + _dot(w, v_aug, hp)
            den = nd[:, ML_D:ML_D + 1]
            hval = nd[:, :ML_D] / jnp.maximum(jnp.abs(den), jnp.exp(-m_row))
            out_ref[0, rs, cs] = (og[rs, cs] * hval).astype(out_ref.dtype)
            log_src_row = gsum - b_row + li_row
            m_new = jnp.maximum(gsum + m_prev, jnp.max(log_src_row, axis=1, keepdims=True))
            decay = jnp.exp(gsum + m_prev - m_new)
            src_col = jnp.exp(gsum - b_col + li_col - m_new)
            upd = lax.dot_general(kh * src_col, v_aug, (((0,), (0,)), ((), ())),
                                  precision=hp, preferred_element_type=F32)
            c_sc[h] = decay * c_aug + upd
            m_sc[h:h + 1, :] = jnp.broadcast_to(m_new, (1, LANES))


def _mlstm(mq, mk, mv, og, gif, conv_w, conv_b, gate_b):
    bsz, s, width = mq.shape
    R = ML_ROWS
    nblk = s // R
    seq = lambda: pl.BlockSpec((1, R, width), lambda b, j: (b, j, 0))
    return pl.pallas_call(
        _mlstm_kernel,
        grid=(bsz, nblk),
        in_specs=[seq(), seq(), seq(), seq(),
                  pl.BlockSpec((8, R), lambda b, j: (0, b * nblk + j)),
                  _const_spec(conv_w.shape), _const_spec(conv_b.shape), _const_spec(gate_b.shape)],
        out_specs=seq(),
        out_shape=jax.ShapeDtypeStruct((bsz, s, width), BF16),
        scratch_shapes=[pltpu.VMEM((ML_HEADS, ML_D, 2 * ML_D), F32),
                        pltpu.VMEM((8, LANES), F32),
                        pltpu.VMEM((8, 2 * width), F32)],
        compiler_params=pltpu.CompilerParams(dimension_semantics=("parallel", "arbitrary"),
                                             vmem_limit_bytes=VMEM_LIMIT),
        name="mlstm",
    )(mq, mk, mv, og, gif, conv_w, conv_b, gate_b)


FF_CHUNK = 512


def _merge_ffn_kernel(x_ref, ya_ref, yb_ref, mg_ref, wa_ref, wb_ref, wo_ref, g2_ref, w1_ref, w2_ref,
                      out_ref):
    mg = mg_ref[...]
    mixed = (mg[:, :D_MODEL].astype(F32) * _dot(ya_ref[...], wa_ref[...])
             + mg[:, D_MODEL:].astype(F32) * _dot(yb_ref[...], wb_ref[...]))
    x1 = x_ref[...] + _dot(mixed.astype(BF16), wo_ref[...])
    h2 = x1 * lax.rsqrt(jnp.mean(x1 * x1, axis=-1, keepdims=True) + RMS_EPS) * g2_ref[...]
    h2 = h2.astype(BF16)
    acc = x1
    for c in range(D_FF // FF_CHUNK):
        a = jnp.maximum(_dot(h2, w1_ref[:, c * FF_CHUNK:(c + 1) * FF_CHUNK]), 0.0)
        acc = acc + _dot((a * a).astype(BF16), w2_ref[c * FF_CHUNK:(c + 1) * FF_CHUNK, :])
    out_ref[...] = acc


def _merge_ffn(x2d, ya, yb, mg, wa, wb, wo, g2, w1, w2, tm=256):
    n = x2d.shape[0]
    row = lambda w: pl.BlockSpec((tm, w), lambda i: (i, 0))
    return pl.pallas_call(
        _merge_ffn_kernel,
        grid=(n // tm,),
        in_specs=[row(D_MODEL), row(1024), row(512), row(2048),
                  _const_spec(wa.shape), _const_spec(wb.shape), _const_spec(wo.shape),
                  _const_spec(g2.shape), _const_spec(w1.shape), _const_spec(w2.shape)],
        out_specs=row(D_MODEL),
        out_shape=jax.ShapeDtypeStruct((n, D_MODEL), F32),
        compiler_params=pltpu.CompilerParams(dimension_semantics=("parallel",),
                                             vmem_limit_bytes=VMEM_LIMIT),
        name="merge_ffn",
    )(x2d, ya, yb, mg, wa, wb, wo, g2, w1, w2)


def _proj_weights(w):
    widths = (512, 128, 128, 128, 128, 128, 128, 24, 512, 512, 512, 4, 4, 512, 2048)
    off = np.concatenate([[0], np.cumsum(widths)])
    (nq, nkc, nvc, nks, nvs, nkw, nvw, ngate, mq, mk, mv, mi, mf, mo, mgate) = (int(o) for o in off[:-1])
    col = lambda start, width: w[:, start:start + width]
    zero = jnp.zeros((w.shape[0], NSA_D), w.dtype)
    parts = []
    for h in range(NSA_HEADS):
        qh = col(nq + NSA_D * h, NSA_D)
        parts += [qh, zero] if h // NSA_HPG == 0 else [zero, qh]
    parts += [col(nks, 128), col(nkw, 128), col(nkc, 128), col(nvc, 128), col(nvs, 128), col(nvw, 128)]
    gate = col(ngate, 24).reshape(-1, NSA_HEADS, 3).transpose(0, 2, 1).reshape(-1, 24)
    parts += [gate, jnp.zeros((w.shape[0], LANES - 24), w.dtype)]
    parts += [col(mq, 512), col(mk, 512), col(mv, 512), col(mo, 512), col(mgate, 2048)]
    w_all = jnp.concatenate(parts, axis=1).astype(BF16)
    w_if = jnp.concatenate([col(mi, 4), col(mf, 4)], axis=1).T.astype(BF16)
    return w_all, w_if


def _branch_a_weights(w):
    zero = jnp.zeros((NSA_D, w.shape[1]), w.dtype)
    parts = []
    for h in range(NSA_HEADS):
        wh = w[NSA_D * h:NSA_D * (h + 1)]
        parts += [wh, zero] if h // NSA_HPG == 0 else [zero, wh]
    return jnp.concatenate(parts, axis=0).astype(BF16)


def _compress_weights(pos, w1):
    r = w1.reshape(2, CMP_STRIDE, 1, NSA_D, CMP_HIDDEN)
    z = jnp.zeros_like(r)
    w1x = jnp.stack([jnp.concatenate([r, z], axis=2), jnp.concatenate([z, r], axis=2)])
    w1x = w1x.reshape(NSA_GROUPS, 2, CMP_STRIDE * NSA_GROUPS * NSA_D, CMP_HIDDEN).astype(BF16)
    posx = jnp.broadcast_to(pos.reshape(2, CMP_STRIDE, 1, NSA_D), (2, CMP_STRIDE, NSA_GROUPS, NSA_D))
    return w1x, posx.reshape(2, CMP_STRIDE * NSA_GROUPS * NSA_D).astype(BF16)


def _t5_bucket(dist):
    n = jnp.maximum(dist, 0)
    max_exact = REL_BUCKETS // 2
    nf = jnp.maximum(n, 1).astype(F32)
    large = max_exact + (jnp.log(nf / max_exact) / math.log(REL_MAX_DIST / max_exact)
                         * (REL_BUCKETS - max_exact)).astype(jnp.int32)
    return jnp.where(n < max_exact, n, jnp.minimum(large, REL_BUCKETS - 1))


def _toeplitz(tab, n_rows, n_cols, off):
    nd = tab.shape[1]
    ext = lambda lo, hi: jnp.take(tab, jnp.asarray(np.clip(np.arange(lo, hi), 0, nd - 1)), axis=1)
    w = jnp.concatenate([ext(off - n_cols + 1, off + 1)[:, ::-1], jnp.zeros((tab.shape[0], 1), tab.dtype),
                         ext(off + 1, off + n_rows)[:, ::-1]], axis=1)
    lw = n_rows + n_cols
    flat = jnp.tile(w, (1, n_rows))[:, :n_rows * (lw - 1)]
    return flat.reshape(tab.shape[0], n_rows, lw - 1)[:, :, :n_cols]


def _bias_tables(rel_table, shifts):
    T = NSA_TILE
    sh_c, sh_s, sh_w = shifts
    tab = (rel_table[_t5_bucket(jnp.arange(WIN_KEYS))] - rel_table[REL_BUCKETS - 1][None, :]).T * LOG2E
    r = np.arange(T)[:, None]

    c = np.arange(NEAR_KEYS)[None, :]
    ok = jnp.asarray((r - c + NSA_TILE) >= 0)[None]
    bs = jnp.where(ok, _toeplitz(tab, T, NEAR_KEYS, NSA_TILE) - sh_s, MASK_F32)
    c = np.arange(WIN_KEYS)[None, :]
    d = r - c + WINDOW
    ok = jnp.asarray((d >= 0) & (d < WINDOW))[None]
    bw = jnp.where(ok, _toeplitz(tab, T, WIN_KEYS, WINDOW) - sh_w, MASK_F32)
    m = np.arange(64)[None, :]
    d = r - 16 * m + 225
    band = (m >= 1) & (m < BAND_SLOTS - 1)
    vals = _toeplitz(tab, T, 16 * BAND_SLOTS, 225)[:, :, ::16]
    vals = jnp.pad(vals, ((0, 0), (0, 0), (0, 64 - BAND_SLOTS)))
    vals = jnp.where(jnp.asarray(band & (d >= 0))[None], vals, 0.0) - sh_c
    dead = (band & (d < 0)) | (m == BAND_SLOTS - 1)
    vals = jnp.where(jnp.asarray(dead)[None], MASK_BIG, vals)
    hi = vals.astype(BF16)
    lo = (vals - hi.astype(F32)).astype(BF16)
    bcf = jnp.concatenate([hi, lo], axis=-1)
    flat = lambda a: a.reshape(NSA_HEADS * T, a.shape[-1])
    return flat(bcf), flat(bs).astype(F32), flat(bw).astype(F32), jnp.max(tab)


def _layer(l, x2d, bsz, s, consts, rel_table, norm1_g, w_in, nsa_q_gain, nsa_k_gain, cmp_k, cmp_v,
           ml_conv_w, ml_conv_b, ml_i_bias, ml_f_bias, w_branch_a, w_branch_b, w_out, norm2_g, w_ff1, w_ff2):
    n = bsz * s
    nseg = s // CMP_STRIDE
    oh, c2s = consts
    w_all, w_if = _proj_weights(w_in[l])
    qg_pad = jnp.concatenate([nsa_q_gain[l]] * 2)[None, :]
    kg_pad = jnp.stack([jnp.concatenate([nsa_k_gain[l, 1]] * 2), jnp.concatenate([nsa_k_gain[l, 2]] * 2)])
    (q, ks, kw, kc, vc, vs, vw, gates, mq, mk, mv, og, mg, gif) = _proj(
        x2d, norm1_g[l][None, :], w_all, w_if, qg_pad, kg_pad)

    def compress(a, params, gain, normalize):
        pos, w1, b1, w2, b2 = (p[l] for p in params)
        w1x, posx = _compress_weights(pos, w1)
        return _compress(a.reshape(bsz, nseg, CMP_STRIDE * LANES), w1x, posx, b1[None, :],
                         w2.astype(BF16), b2[None, :], gain[None, :], normalize)

    kcmp = compress(kc, cmp_k, nsa_k_gain[l, 0], True)
    vcmp = compress(vc, cmp_v, jnp.ones((NSA_D,), F32), False)

    qk_bound = lambda kg: 8.0 * LOG2E * jnp.max(jnp.abs(nsa_q_gain[l])) * jnp.max(jnp.abs(kg))
    _, _, _, tab_max = _bias_tables(rel_table, (0.0, 0.0, 0.0))
    snap = lambda v: v.astype(BF16).astype(F32)
    shifts = tuple(snap(qk_bound(nsa_k_gain[l, j]) + jnp.maximum(tab_max, 0.0)) for j in range(3))
    safe = 2.0 * jnp.max(jnp.stack(shifts)) < SAFE_SHIFT_LOG2

    seq = lambda a: a.reshape(bsz, s, a.shape[-1])
    front = lambda a, p: jnp.pad(seq(a), ((0, 0), (p, 0), (0, 0)))
    operands = (seq(q), front(ks, SLC_PAD), front(vs, SLC_PAD), front(kw, WINDOW), front(vw, WINDOW),
                kcmp, vcmp, oh, seq(gates))

    def run(online):
        sh = (0.0, 0.0, 0.0) if online else shifts
        bcf, bs, bw, _ = _bias_tables(rel_table, sh)
        shift = jnp.full((1, LANES), -sh[1], F32)
        return _nsa(online, *operands, bcf, bs, bw, c2s, shift)

    y_a = lax.cond(safe, lambda: run(False), lambda: run(True))

    gate_b = jnp.concatenate([ml_i_bias[l], ml_f_bias[l]])[:, None]
    y_b = _mlstm(seq(mq), seq(mk), seq(mv), seq(og), gif, ml_conv_w[l], ml_conv_b[l][None, :], gate_b)

    out = _merge_ffn(x2d, y_a.reshape(n, 1024), y_b.reshape(n, 512), mg, _branch_a_weights(w_branch_a[l]),
                     w_branch_b[l].astype(BF16), w_out[l].astype(BF16), norm2_g[l][None, :],
                     w_ff1[l].astype(BF16), w_ff2[l].astype(BF16))
    return out, y_a, y_b


def _consts(s):
    nseg = s // CMP_STRIDE
    nsel = s // SLC_BLOCK
    blk_of_key = np.arange(s) // SLC_BLOCK
    oh = np.concatenate([np.ones((SLC_PAD, LANES), np.float32),
                         (blk_of_key[:, None] == np.arange(LANES)[None, :]).astype(np.float32)], axis=0)
    ci = np.arange(nseg)[:, None] * CMP_STRIDE
    sj = np.arange(LANES)[None, :] * SLC_BLOCK
    c2s = ((ci < sj + SLC_BLOCK) & (ci + CMP_LEN > sj) & (np.arange(LANES)[None, :] < nsel)
           & (np.arange(nseg)[:, None] < nseg - 1))
    return jnp.asarray(oh, BF16), jnp.asarray(c2s.astype(np.float32), BF16)


def kernel(x, norm1_g, w_in, nsa_q_gain, nsa_k_gain, cmp_k_pos, cmp_k_w1, cmp_k_b1, cmp_k_w2, cmp_k_b2, cmp_v_pos, cmp_v_w1, cmp_v_b1, cmp_v_w2, cmp_v_b2, rel_table, ml_conv_w, ml_conv_b, ml_i_bias, ml_f_bias, w_branch_a, w_branch_b, w_out, norm2_g, w_ff1, w_ff2):
    bsz, s, _ = x.shape
    consts = _consts(s)
    x2d = x.reshape(bsz * s, D_MODEL)
    for l in range(norm1_g.shape[0]):
        x2d, _, _ = _layer(l, x2d, bsz, s, consts, rel_table, norm1_g, w_in, nsa_q_gain, nsa_k_gain,
                           (cmp_k_pos, cmp_k_w1, cmp_k_b1, cmp_k_w2, cmp_k_b2),
                           (cmp_v_pos, cmp_v_w1, cmp_v_b1, cmp_v_w2, cmp_v_b2),
                           ml_conv_w, ml_conv_b, ml_i_bias, ml_f_bias,
                           w_branch_a, w_branch_b, w_out, norm2_g, w_ff1, w_ff2)
    return x2d.reshape(bsz, s, D_MODEL)
```

```python
import functools
import math

import numpy as np
import jax
import jax.numpy as jnp
from jax import lax
from jax.experimental import pallas as pl
from jax.experimental.pallas import tpu as pltpu

F32 = jnp.float32
BF16 = jnp.bfloat16

D_MODEL = 1024
NSA_HEADS = 8
NSA_GROUPS = 2
NSA_HPG = NSA_HEADS // NSA_GROUPS
NSA_D = 64
CMP_LEN = 32
CMP_STRIDE = 16
CMP_HIDDEN = 256
SLC_BLOCK = 64
SLC_TOPK = 16
WINDOW = 512
FORCED_SCORE = 1e4
ML_HEADS = 4
ML_D = 128
ML_CHUNK = 64
CONV_WIDTH = 4
D_FF = 4 * D_MODEL
REL_BUCKETS = 32
REL_MAX_DIST = 128
RMS_EPS = 1e-6

LANES = 128
NSA_TILE = 128
SLC_PAD = 128
FAR_CHUNK = 512
NEAR_KEYS = 2 * NSA_TILE
WIN_KEYS = WINDOW + NSA_TILE
BAND_SLOTS = 26
MASK_BIG = -1e9
MASK_F32 = -1e30
LOG2E = math.log2(math.e)
SAFE_SHIFT_LOG2 = 50.0
VMEM_LIMIT = 56 * 1024 * 1024

_Q_OFF, _KS_OFF, _KW_OFF, _KC_OFF, _VC_OFF, _VS_OFF, _VW_OFF = 0, 1024, 1152, 1280, 1408, 1536, 1664
_GATE_OFF, _MQ_OFF, _MK_OFF, _MV_OFF, _MO_OFF, _MG_OFF, _W_COLS = 1792, 1920, 2432, 2944, 3456, 3968, 6016


def _nt(a, b, precision=None):
    return lax.dot_general(a, b, (((1,), (1,)), ((), ())), precision=precision,
                           preferred_element_type=F32)


def _dot(a, b, precision=None):
    return jnp.dot(a, b, precision=precision, preferred_element_type=F32)


def _const_spec(shape):
    nd = len(shape)
    return pl.BlockSpec(shape, lambda *_: (0,) * nd, pipeline_mode=pl.Buffered(1))


def _proj_kernel(x_ref, g1_ref, w_ref, wif_ref, qg_ref, kg_ref,
                 q_ref, ks_ref, kw_ref, kc_ref, vc_ref, vs_ref, vw_ref, gate_ref,
                 mq_ref, mk_ref, mv_ref, mo_ref, mg_ref, gif_ref):
    x = x_ref[...]
    h = x * lax.rsqrt(jnp.mean(x * x, axis=-1, keepdims=True) + RMS_EPS) * g1_ref[...]
    hb = h.astype(BF16)

    def proj(off, width):
        return _dot(hb, w_ref[:, off:off + width])

    lane = lax.broadcasted_iota(jnp.int32, (1, LANES), 1)
    low = lane < NSA_D

    for hh in range(NSA_HEADS):
        blk = proj(_Q_OFF + LANES * hh, LANES)
        ms = jnp.sum(blk * blk, axis=-1, keepdims=True) * (1.0 / NSA_D)
        qn = blk * lax.rsqrt(ms + RMS_EPS) * qg_ref[...] * (NSA_D ** -0.5 * LOG2E)
        q_ref[:, LANES * hh:LANES * (hh + 1)] = qn.astype(q_ref.dtype)

    for off, ref, row in ((_KS_OFF, ks_ref, 0), (_KW_OFF, kw_ref, 1)):
        blk = proj(off, LANES)
        sq = blk * blk
        ms0 = jnp.sum(jnp.where(low, sq, 0.0), axis=-1, keepdims=True) * (1.0 / NSA_D)
        ms1 = jnp.sum(jnp.where(low, 0.0, sq), axis=-1, keepdims=True) * (1.0 / NSA_D)
        inv = jnp.where(low, lax.rsqrt(ms0 + RMS_EPS), lax.rsqrt(ms1 + RMS_EPS))
        ref[...] = (blk * inv * kg_ref[row:row + 1, :]).astype(ref.dtype)

    for off, ref in ((_KC_OFF, kc_ref), (_VC_OFF, vc_ref), (_VS_OFF, vs_ref), (_VW_OFF, vw_ref)):
        ref[...] = proj(off, LANES).astype(ref.dtype)

    gate_ref[...] = jax.nn.sigmoid(proj(_GATE_OFF, LANES))
    mq_ref[...] = proj(_MQ_OFF, 512)
    mk_ref[...] = proj(_MK_OFF, 512)
    mv_ref[...] = proj(_MV_OFF, 512).astype(mv_ref.dtype)
    mo_ref[...] = jax.nn.sigmoid(proj(_MO_OFF, 512)).astype(mo_ref.dtype)
    for c in range(4):
        mg_ref[:, 512 * c:512 * (c + 1)] = jax.nn.sigmoid(proj(_MG_OFF + 512 * c, 512)).astype(mg_ref.dtype)
    gif_ref[...] = _nt(wif_ref[...], hb)


def _proj(x2d, g1, w_all, w_if, qg_pad, kg_pad, tm=256):
    n = x2d.shape[0]
    row = lambda w: pl.BlockSpec((tm, w), lambda i: (i, 0))
    out_shapes = (
        jax.ShapeDtypeStruct((n, 1024), BF16),
        jax.ShapeDtypeStruct((n, LANES), BF16),
        jax.ShapeDtypeStruct((n, LANES), BF16),
        jax.ShapeDtypeStruct((n, LANES), BF16),
        jax.ShapeDtypeStruct((n, LANES), BF16),
        jax.ShapeDtypeStruct((n, LANES), BF16),
        jax.ShapeDtypeStruct((n, LANES), BF16),
        jax.ShapeDtypeStruct((n, LANES), F32),
        jax.ShapeDtypeStruct((n, 512), F32),
        jax.ShapeDtypeStruct((n, 512), F32),
        jax.ShapeDtypeStruct((n, 512), BF16),
        jax.ShapeDtypeStruct((n, 512), BF16),
        jax.ShapeDtypeStruct((n, 2048), BF16),
        jax.ShapeDtypeStruct((8, n), F32),
    )
    out_specs = (row(1024),) + (row(LANES),) * 7 + (row(512),) * 4 + (row(2048),
                 pl.BlockSpec((8, tm), lambda i: (0, i)))
    return pl.pallas_call(
        _proj_kernel,
        grid=(n // tm,),
        in_specs=[row(D_MODEL), _const_spec((1, D_MODEL)), _const_spec((D_MODEL, _W_COLS)),
                  _const_spec((8, D_MODEL)), _const_spec((1, LANES)), _const_spec((2, LANES))],
        out_specs=out_specs,
        out_shape=out_shapes,
        compiler_params=pltpu.CompilerParams(dimension_semantics=("parallel",),
                                             vmem_limit_bytes=VMEM_LIMIT),
        name="proj",
    )(x2d, g1, w_all, w_if, qg_pad, kg_pad)


def _compress_kernel(seg_ref, w1_ref, pos_ref, b1_ref, w2_ref, b2_ref, gain_ref, out_ref, *, normalize):
    seg = seg_ref[0]
    nseg, width = seg.shape
    c = math.sqrt(2.0 / math.pi)
    pos_lo = jnp.broadcast_to(pos_ref[0:1, :], (8, width))
    pos_hi = jnp.broadcast_to(pos_ref[1:2, :], (8, width))
    outs = []
    for g in range(NSA_GROUPS):
        a = _dot(seg, w1_ref[g, 0])
        b = _dot(seg, w1_ref[g, 1])
        posb = (_dot(pos_lo, w1_ref[g, 0]) + _dot(pos_hi, w1_ref[g, 1]))[0:1] + b1_ref[...]
        pre = a + pltpu.roll(b, nseg - 1, 0) + posb
        hid = 0.5 * pre * (1.0 + jnp.tanh(c * (pre + 0.044715 * (pre * pre * pre))))
        out = _dot(hid.astype(BF16), w2_ref[...]) + b2_ref[...]
        if normalize:
            ms = jnp.mean(out * out, axis=-1, keepdims=True)
            out = out * lax.rsqrt(ms + RMS_EPS) * gain_ref[...]
        outs.append(out)
    out_ref[0] = jnp.concatenate(outs, axis=1).astype(out_ref.dtype)


def _compress(segs, w1x, posx, b1, w2, b2, gain, normalize):
    bsz, nseg, width = segs.shape
    return pl.pallas_call(
        functools.partial(_compress_kernel, normalize=normalize),
        grid=(bsz,),
        in_specs=[pl.BlockSpec((1, nseg, width), lambda b: (b, 0, 0)),
                  _const_spec(w1x.shape), _const_spec(posx.shape), _const_spec(b1.shape),
                  _const_spec(w2.shape), _const_spec(b2.shape), _const_spec(gain.shape)],
        out_specs=pl.BlockSpec((1, nseg, LANES), lambda b: (b, 0, 0)),
        out_shape=jax.ShapeDtypeStruct((bsz, nseg, LANES), BF16),
        compiler_params=pltpu.CompilerParams(dimension_semantics=("parallel",),
                                             vmem_limit_bytes=VMEM_LIMIT),
        name="compress",
    )(segs, w1x, posx, b1, w2, b2, gain)


def _nsa_kernel(q_ref, ks_ref, vs_ref, kw_ref, vw_ref, kc_ref, vc_ref, oh_ref, gate_ref,
                bcf_ref, bs_ref, bw_ref, c2s_ref, shift_ref, out_ref, acc_sc, *extra_sc, online):
    i = pl.program_id(1)
    t0 = pl.multiple_of(i * NSA_TILE, NSA_TILE)
    T = NSA_TILE
    rows = NSA_HEADS * T
    grows = NSA_HPG * T
    ncmp = kc_ref.shape[1]
    nblk = LANES

    q = q_ref[0]
    qpad = jnp.concatenate([q[:, LANES * h:LANES * (h + 1)] for h in range(NSA_HEADS)], axis=0)
    ones_v = jnp.ones((WIN_KEYS, LANES), BF16)

    def attend(s, v, n_keys):
        if online:
            m = jnp.max(s, axis=1, keepdims=True)
            m = jnp.where(m < 0.1 * MASK_BIG, 0.0, m)
            s = s - m
        p = jnp.exp2(s)
        acc = _dot(p.astype(BF16), jnp.concatenate([v, ones_v[:n_keys]], axis=1))
        return p, acc[:, :LANES], acc[:, LANES:]

    n_io = lax.broadcasted_iota(jnp.int32, (ncmp, LANES), 0)
    l_io = lax.broadcasted_iota(jnp.int32, (ncmp, LANES), 1)
    band = jnp.clip(n_io - 8 * i + 16, 0, BAND_SLOTS - 1)
    oh_c = jnp.where(band == (l_io & 63), 1.0, 0.0).astype(BF16)
    kc_aug = jnp.concatenate([kc_ref[0], oh_c], axis=1)
    qc_aug = jnp.concatenate([qpad, bcf_ref[...]], axis=1)
    p_c, num_c, l_c = attend(_nt(qc_aug, kc_aug), vc_ref[0], ncmp)
    inv_c = 1.0 / jnp.maximum(l_c, 1e-30)
    o_c = num_c * inv_c
    p_c = p_c * jnp.concatenate([inv_c] * (ncmp // LANES), axis=1)

    s_w = _nt(qpad, kw_ref[0, pl.ds(t0, WIN_KEYS), :]) + bw_ref[...]
    c_io = lax.broadcasted_iota(jnp.int32, (rows, WIN_KEYS), 1)
    s_w = jnp.where(c_io + (t0 - WINDOW) >= 0, s_w, MASK_F32)
    _, num_w, l_w = attend(s_w, vw_ref[0, pl.ds(t0, WIN_KEYS), :], WIN_KEYS)
    o_w = num_w * (1.0 / l_w)

    j_io = lax.broadcasted_iota(jnp.int32, (nblk, T), 0)
    r_io = lax.broadcasted_iota(jnp.int32, (nblk, T), 1)
    j_f = j_io.astype(F32)
    cur = 2 * i + jnp.where(r_io >= SLC_BLOCK, 1, 0)
    forced = (j_io == 0) | (j_io == cur) | (j_io == cur - 1)
    valid = j_io <= cur
    jl_io = lax.broadcasted_iota(jnp.int32, (T, nblk), 1)
    near_blk = jl_io >= 2 * (i - 1)
    far_pick = shift_ref[0:1, 0:1]
    qn_parts, qf_parts = [], []
    for g in range(NSA_GROUPS):
        ps = p_c[g * grows:g * grows + T]
        for hh in range(1, NSA_HPG):
            ps = ps + p_c[g * grows + hh * T:g * grows + (hh + 1) * T]
        ps_hi = ps.astype(BF16)
        ps_lo = (ps - ps_hi.astype(F32)).astype(BF16)
        imp = _dot(ps_hi, c2s_ref[...]) + _dot(ps_lo, c2s_ref[...])
        score = jnp.where(forced, -jnp.inf, jnp.where(valid, imp.T, -1.0))
        for _ in range(SLC_TOPK - 3):
            best = jnp.max(score, axis=0, keepdims=True)
            first = jnp.min(jnp.where(score == best, j_f, float(nblk)), axis=0, keepdims=True)
            score = jnp.where(j_f == first, -jnp.inf, score)
        picked = score.T == -jnp.inf
        sb_near = jnp.where(picked & near_blk, 0.0, MASK_BIG).astype(BF16)
        sb_far = jnp.where(picked & jnp.logical_not(near_blk), far_pick, MASK_BIG).astype(BF16)
        qg = qpad[g * grows:(g + 1) * grows]
        qn_parts.append(jnp.concatenate([qg, jnp.concatenate([sb_near] * NSA_HPG, axis=0)], axis=1))
        qf_parts.append(jnp.concatenate([qg, jnp.concatenate([sb_far] * NSA_HPG, axis=0)], axis=1))
    q_near = jnp.concatenate(qn_parts, axis=0)
    q_far = jnp.concatenate(qf_parts, axis=0)

    k_near = jnp.concatenate([ks_ref[0, pl.ds(t0, NEAR_KEYS), :], oh_ref[pl.ds(t0, NEAR_KEYS), :]], axis=1)
    v_near = jnp.concatenate([vs_ref[0, pl.ds(t0, NEAR_KEYS), :], ones_v[:NEAR_KEYS]], axis=1)
    s_n = _nt(q_near, k_near) + bs_ref[...]
    if online:
        m_sc, = extra_sc
        m_n = jnp.max(s_n, axis=1, keepdims=True)
        m_sc[...] = m_n
        s_n = s_n - m_n
    acc_sc[...] = _dot(jnp.exp2(s_n).astype(BF16), v_near)

    last_chunk = (ks_ref.shape[1] - SLC_PAD) // FAR_CHUNK - 1

    def far_start(c):
        return pl.multiple_of(SLC_PAD + FAR_CHUNK * jnp.minimum(c, last_chunk), LANES)

    def far_scores(c):
        start = far_start(c)
        k_f = jnp.concatenate([ks_ref[0, pl.ds(start, FAR_CHUNK), :], oh_ref[pl.ds(start, FAR_CHUNK), :]], axis=1)
        return _nt(q_far, k_f)

    def far_values(c):
        return jnp.concatenate([vs_ref[0, pl.ds(far_start(c), FAR_CHUNK), :], ones_v[:FAR_CHUNK]], axis=1)

    n_far = (i + 2) // 4
    if online:
        def far_online(c, carry):
            s_f = far_scores(c)
            m_old = m_sc[...]
            m_new = jnp.maximum(m_old, jnp.max(s_f, axis=1, keepdims=True))
            m_sc[...] = m_new
            acc_sc[...] = (jnp.exp2(m_old - m_new) * acc_sc[...]
                           + _dot(jnp.exp2(s_f - m_new).astype(BF16), far_values(c)))
            return carry

        lax.fori_loop(0, n_far, far_online, 0)
    else:
        pa_sc, pb_sc = extra_sc
        pa_sc[...] = jnp.exp2(far_scores(0)).astype(BF16)

        def far_pair(cc, carry):
            c0 = 2 * cc
            pv0 = _dot(pa_sc[...], far_values(c0))
            pb_sc[...] = jnp.exp2(far_scores(c0 + 1)).astype(BF16)
            pv1 = _dot(pb_sc[...], far_values(c0 + 1))
            pa_sc[...] = jnp.exp2(far_scores(c0 + 2)).astype(BF16)
            acc_sc[...] += pv0 + pv1
            return carry

        lax.fori_loop(0, (n_far + 1) // 2, far_pair, 0)

    acc = acc_sc[...]
    o_s = acc[:, :LANES] * (1.0 / acc[:, LANES:])

    gates = gate_ref[0]
    lane_hi = lax.broadcasted_iota(jnp.int32, (T, LANES), 1) >= NSA_D
    for h in range(NSA_HEADS):
        sl = slice(h * T, (h + 1) * T)
        o = (gates[:, h:h + 1] * o_c[sl] + gates[:, 8 + h:9 + h] * o_s[sl]
             + gates[:, 16 + h:17 + h] * o_w[sl])
        keep = lane_hi if h // NSA_HPG == 1 else jnp.logical_not(lane_hi)
        out_ref[0, :, LANES * h:LANES * (h + 1)] = jnp.where(keep, o, 0.0).astype(out_ref.dtype)


def _nsa(online, q, ks, vs, kw, vw, kc, vc, oh, gates, bcf, bs, bw, c2s, shift):
    bsz, s = q.shape[:2]
    T = NSA_TILE
    rows = NSA_HEADS * T
    per_b = lambda a: pl.BlockSpec((1,) + a.shape[1:], lambda b, i: (b, 0, 0))
    scratch = [pltpu.VMEM((rows, 2 * LANES), F32)]
    if online:
        scratch.append(pltpu.VMEM((rows, 1), F32))
    else:
        scratch += [pltpu.VMEM((rows, FAR_CHUNK), BF16)] * 2
    return pl.pallas_call(
        functools.partial(_nsa_kernel, online=online),
        grid=(bsz, s // T),
        in_specs=[pl.BlockSpec((1, T, 1024), lambda b, i: (b, i, 0)),
                  per_b(ks), per_b(vs), per_b(kw), per_b(vw), per_b(kc), per_b(vc),
                  _const_spec(oh.shape),
                  pl.BlockSpec((1, T, LANES), lambda b, i: (b, i, 0)),
                  _const_spec(bcf.shape), _const_spec(bs.shape), _const_spec(bw.shape),
                  _const_spec(c2s.shape), _const_spec(shift.shape)],
        out_specs=pl.BlockSpec((1, T, 1024), lambda b, i: (b, i, 0)),
        out_shape=jax.ShapeDtypeStruct((bsz, s, 1024), BF16),
        scratch_shapes=scratch,
        compiler_params=pltpu.CompilerParams(dimension_semantics=("parallel", "arbitrary"),
                                             vmem_limit_bytes=VMEM_LIMIT),
        name="nsa_online" if online else "nsa",
    )(q, ks, vs, kw, vw, kc, vc, oh, gates, bcf, bs, bw, c2s, shift)


ML_ROWS = 256


def _mlstm_kernel(q_ref, k_ref, v_ref, o_ref, gif_ref, cw_ref, cb_ref, gb_ref, tril_ref, triu_ref, eye_ref,
                  out_ref,
                  c_sc, m_sc, tail_sc):
    j = pl.program_id(1)
    R = ML_ROWS
    L = ML_CHUNK
    hp = lax.Precision.HIGHEST

    @pl.when(j == 0)
    def _():
        c_sc[...] = jnp.zeros_like(c_sc)
        m_sc[...] = jnp.zeros_like(m_sc)
        tail_sc[...] = jnp.zeros_like(tail_sc)

    raw = jnp.concatenate([q_ref[0], k_ref[0]], axis=1)
    ext = jnp.concatenate([tail_sc[...], raw], axis=0)
    tail_sc[...] = raw[R - 8:, :]
    conv = cb_ref[...]
    for t in range(CONV_WIDTH):
        conv = conv + ext[8 - (CONV_WIDTH - 1) + t:8 - (CONV_WIDTH - 1) + t + R, :] * cw_ref[t:t + 1, :]
    qk = conv * jax.nn.sigmoid(conv)
    qs = qk[:, :ML_HEADS * ML_D] * (ML_D ** -0.5)
    kk = qk[:, ML_HEADS * ML_D:]
    qb = qs.astype(BF16)
    vv = v_ref[0]
    og = o_ref[0]

    pre = gif_ref[...] + gb_ref[...]
    row8 = lax.broadcasted_iota(jnp.int32, (8, R), 0)
    logf = jnp.minimum(pre, 0.0) - jnp.log(1.0 + jnp.exp(-jnp.abs(pre)))
    g8 = jnp.where(row8 < ML_HEADS, pre, logf)
    cum_col = _nt(tril_ref[...], g8, hp)
    id_col = _nt(eye_ref[...], g8, hp)
    cum_row = _dot(g8, triu_ref[...], hp)

    a_io = lax.broadcasted_iota(jnp.int32, (L, L), 0)
    b_io = lax.broadcasted_iota(jnp.int32, (L, L), 1)
    causal = b_io <= a_io
    ones_l = jnp.ones((L, ML_D), BF16)

    for h in range(ML_HEADS):
        cs = slice(h * ML_D, (h + 1) * ML_D)
        local = []
        for c in range(R // L):
            rs = slice(c * L, (c + 1) * L)
            b_row = cum_row[ML_HEADS + h:ML_HEADS + h + 1, rs]
            li_row = g8[h:h + 1, rs]
            b_col = cum_col[rs, ML_HEADS + h:ML_HEADS + h + 1]
            li_col = id_col[rs, h:h + 1]
            gsum = b_row[:, L - 1:L]
            log_d = jnp.where(causal, b_col - b_row + li_row, -jnp.inf)
            a_row = jnp.max(log_d, axis=1, keepdims=True)
            w = _nt(qb[rs, cs], kk[rs, cs].astype(BF16)) * jnp.exp(log_d - a_row)
            v_aug = jnp.concatenate([vv[rs, cs], ones_l], axis=1)
            wv = _dot(w.astype(BF16), v_aug)
            s_max = jnp.max(gsum - b_row + li_row, axis=1, keepdims=True)
            src_col = jnp.exp(gsum - b_col + li_col - s_max)
            upd = lax.dot_general((kk[rs, cs] * src_col).astype(BF16), v_aug, (((0,), (0,)), ((), ())),
                                  preferred_element_type=F32)
            local.append((b_col, gsum, a_row, wv, s_max, upd))
        m_prev = m_sc[h:h + 1, 0:1]
        c_aug = c_sc[h]
        for c in range(R // L):
            rs = slice(c * L, (c + 1) * L)
            b_col, gsum, a_row, wv, s_max, upd = local[c]
            inter = b_col + m_prev
            m_row = jnp.maximum(inter, a_row)
            nd = (jnp.exp(inter - m_row) * _dot(qb[rs, cs], c_aug.astype(BF16))
                  + jnp.exp(a_row - m_row) * wv)
            den = nd[:, ML_D:ML_D + 1]
            hval = nd[:, :ML_D] / jnp.maximum(jnp.abs(den), jnp.exp(-m_row))
            out_ref[0, rs, cs] = (og[rs, cs].astype(F32) * hval).astype(out_ref.dtype)
            m_new = jnp.maximum(gsum + m_prev, s_max)
            c_aug = jnp.exp(gsum + m_prev - m_new) * c_aug + jnp.exp(s_max - m_new) * upd
            m_prev = m_new
        c_sc[h] = c_aug
        m_sc[h:h + 1, :] = jnp.broadcast_to(m_prev, (1, LANES))


def _mlstm(mq, mk, mv, og, gif, conv_w, conv_b, gate_b):
    bsz, s, width = mq.shape
    R = ML_ROWS
    nblk = s // R
    seq = lambda: pl.BlockSpec((1, R, width), lambda b, j: (b, j, 0))
    pos = np.arange(R)
    same_chunk = (pos[:, None] // ML_CHUNK) == (pos[None, :] // ML_CHUNK)
    tril = (same_chunk & (pos[None, :] <= pos[:, None])).astype(np.float32)
    eye = np.eye(R, dtype=np.float32)
    return pl.pallas_call(
        _mlstm_kernel,
        grid=(bsz, nblk),
        in_specs=[seq(), seq(), seq(), seq(),
                  pl.BlockSpec((8, R), lambda b, j: (0, b * nblk + j)),
                  _const_spec(conv_w.shape), _const_spec(conv_b.shape), _const_spec(gate_b.shape),
                  _const_spec((R, R)), _const_spec((R, R)), _const_spec((R, R))],
        out_specs=seq(),
        out_shape=jax.ShapeDtypeStruct((bsz, s, width), BF16),
        scratch_shapes=[pltpu.VMEM((ML_HEADS, ML_D, 2 * ML_D), F32),
                        pltpu.VMEM((8, LANES), F32),
                        pltpu.VMEM((8, 2 * width), F32)],
        compiler_params=pltpu.CompilerParams(dimension_semantics=("parallel", "arbitrary"),
                                             vmem_limit_bytes=VMEM_LIMIT),
        name="mlstm",
    )(mq, mk, mv, og, gif, conv_w, conv_b, gate_b, jnp.asarray(tril), jnp.asarray(tril.T), jnp.asarray(eye))


FF_CHUNK = 512


def _merge_ffn_kernel(x_ref, ya_ref, yb_ref, mg_ref, wa_ref, wb_ref, wo_ref, g2_ref, w1_ref, w2_ref,
                      out_ref):
    mg = mg_ref[...]
    mixed = (mg[:, :D_MODEL].astype(F32) * _dot(ya_ref[...], wa_ref[...])
             + mg[:, D_MODEL:].astype(F32) * _dot(yb_ref[...], wb_ref[...]))
    x1 = x_ref[...] + _dot(mixed.astype(BF16), wo_ref[...])
    h2 = x1 * lax.rsqrt(jnp.mean(x1 * x1, axis=-1, keepdims=True) + RMS_EPS) * g2_ref[...]
    h2 = h2.astype(BF16)
    acc = x1
    for c in range(D_FF // FF_CHUNK):
        a = jnp.maximum(_dot(h2, w1_ref[:, c * FF_CHUNK:(c + 1) * FF_CHUNK]), 0.0)
        acc = acc + _dot((a * a).astype(BF16), w2_ref[c * FF_CHUNK:(c + 1) * FF_CHUNK, :])
    out_ref[...] = acc


def _merge_ffn(x2d, ya, yb, mg, wa, wb, wo, g2, w1, w2, tm=256):
    n = x2d.shape[0]
    row = lambda w: pl.BlockSpec((tm, w), lambda i: (i, 0))
    return pl.pallas_call(
        _merge_ffn_kernel,
        grid=(n // tm,),
        in_specs=[row(D_MODEL), row(1024), row(512), row(2048),
                  _const_spec(wa.shape), _const_spec(wb.shape), _const_spec(wo.shape),
                  _const_spec(g2.shape), _const_spec(w1.shape), _const_spec(w2.shape)],
        out_specs=row(D_MODEL),
        out_shape=jax.ShapeDtypeStruct((n, D_MODEL), F32),
        compiler_params=pltpu.CompilerParams(dimension_semantics=("parallel",),
                                             vmem_limit_bytes=VMEM_LIMIT),
        name="merge_ffn",
    )(x2d, ya, yb, mg, wa, wb, wo, g2, w1, w2)


def _proj_weights(w):
    widths = (512, 128, 128, 128, 128, 128, 128, 24, 512, 512, 512, 4, 4, 512, 2048)
    off = np.concatenate([[0], np.cumsum(widths)])
    (nq, nkc, nvc, nks, nvs, nkw, nvw, ngate, mq, mk, mv, mi, mf, mo, mgate) = (int(o) for o in off[:-1])
    col = lambda start, width: w[:, start:start + width]
    zero = jnp.zeros((w.shape[0], NSA_D), w.dtype)
    parts = []
    for h in range(NSA_HEADS):
        qh = col(nq + NSA_D * h, NSA_D)
        parts += [qh, zero] if h // NSA_HPG == 0 else [zero, qh]
    parts += [col(nks, 128), col(nkw, 128), col(nkc, 128), col(nvc, 128), col(nvs, 128), col(nvw, 128)]
    gate = col(ngate, 24).reshape(-1, NSA_HEADS, 3).transpose(0, 2, 1).reshape(-1, 24)
    parts += [gate, jnp.zeros((w.shape[0], LANES - 24), w.dtype)]
    parts += [col(mq, 512), col(mk, 512), col(mv, 512), col(mo, 512), col(mgate, 2048)]
    w_all = jnp.concatenate(parts, axis=1).astype(BF16)
    w_if = jnp.concatenate([col(mi, 4), col(mf, 4)], axis=1).T.astype(BF16)
    return w_all, w_if


def _branch_a_weights(w):
    zero = jnp.zeros((NSA_D, w.shape[1]), w.dtype)
    parts = []
    for h in range(NSA_HEADS):
        wh = w[NSA_D * h:NSA_D * (h + 1)]
        parts += [wh, zero] if h // NSA_HPG == 0 else [zero, wh]
    return jnp.concatenate(parts, axis=0).astype(BF16)


def _compress_weights(pos, w1):
    r = w1.reshape(2, CMP_STRIDE, 1, NSA_D, CMP_HIDDEN)
    z = jnp.zeros_like(r)
    w1x = jnp.stack([jnp.concatenate([r, z], axis=2), jnp.concatenate([z, r], axis=2)])
    w1x = w1x.reshape(NSA_GROUPS, 2, CMP_STRIDE * NSA_GROUPS * NSA_D, CMP_HIDDEN).astype(BF16)
    posx = jnp.broadcast_to(pos.reshape(2, CMP_STRIDE, 1, NSA_D), (2, CMP_STRIDE, NSA_GROUPS, NSA_D))
    return w1x, posx.reshape(2, CMP_STRIDE * NSA_GROUPS * NSA_D).astype(BF16)


def _t5_bucket(dist):
    n = jnp.maximum(dist, 0)
    max_exact = REL_BUCKETS // 2
    nf = jnp.maximum(n, 1).astype(F32)
    large = max_exact + (jnp.log(nf / max_exact) / math.log(REL_MAX_DIST / max_exact)
                         * (REL_BUCKETS - max_exact)).astype(jnp.int32)
    return jnp.where(n < max_exact, n, jnp.minimum(large, REL_BUCKETS - 1))


def _toeplitz(tab, n_rows, n_cols, off):
    nd = tab.shape[1]
    ext = lambda lo, hi: jnp.take(tab, jnp.asarray(np.clip(np.arange(lo, hi), 0, nd - 1)), axis=1)
    w = jnp.concatenate([ext(off - n_cols + 1, off + 1)[:, ::-1], jnp.zeros((tab.shape[0], 1), tab.dtype),
                         ext(off + 1, off + n_rows)[:, ::-1]], axis=1)
    lw = n_rows + n_cols
    flat = jnp.tile(w, (1, n_rows))[:, :n_rows * (lw - 1)]
    return flat.reshape(tab.shape[0], n_rows, lw - 1)[:, :, :n_cols]


def _bias_tables(rel_table, shifts):
    T = NSA_TILE
    sh_c, sh_s, sh_w = shifts
    tab = (rel_table[_t5_bucket(jnp.arange(WIN_KEYS))] - rel_table[REL_BUCKETS - 1][None, :]).T * LOG2E
    r = np.arange(T)[:, None]

    c = np.arange(NEAR_KEYS)[None, :]
    ok = jnp.asarray((r - c + NSA_TILE) >= 0)[None]
    bs = jnp.where(ok, _toeplitz(tab, T, NEAR_KEYS, NSA_TILE) - sh_s, MASK_F32)
    c = np.arange(WIN_KEYS)[None, :]
    d = r - c + WINDOW
    ok = jnp.asarray((d >= 0) & (d < WINDOW))[None]
    bw = jnp.where(ok, _toeplitz(tab, T, WIN_KEYS, WINDOW) - sh_w, MASK_F32)
    m = np.arange(64)[None, :]
    d = r - 16 * m + 225
    band = (m >= 1) & (m < BAND_SLOTS - 1)
    vals = _toeplitz(tab, T, 16 * BAND_SLOTS, 225)[:, :, ::16]
    vals = jnp.pad(vals, ((0, 0), (0, 0), (0, 64 - BAND_SLOTS)))
    vals = jnp.where(jnp.asarray(band & (d >= 0))[None], vals, 0.0) - sh_c
    dead = (band & (d < 0)) | (m == BAND_SLOTS - 1)
    vals = jnp.where(jnp.asarray(dead)[None], MASK_BIG, vals)
    hi = vals.astype(BF16)
    lo = (vals - hi.astype(F32)).astype(BF16)
    bcf = jnp.concatenate([hi, lo], axis=-1)
    flat = lambda a: a.reshape(NSA_HEADS * T, a.shape[-1])
    return flat(bcf), flat(bs).astype(F32), flat(bw).astype(F32), jnp.max(tab)


def _layer(l, x2d, bsz, s, consts, rel_table, norm1_g, w_in, nsa_q_gain, nsa_k_gain, cmp_k, cmp_v,
           ml_conv_w, ml_conv_b, ml_i_bias, ml_f_bias, w_branch_a, w_branch_b, w_out, norm2_g, w_ff1, w_ff2):
    n = bsz * s
    nseg = s // CMP_STRIDE
    oh, c2s = consts
    w_all, w_if = _proj_weights(w_in[l])
    qg_pad = jnp.concatenate([nsa_q_gain[l]] * 2)[None, :]
    kg_pad = jnp.stack([jnp.concatenate([nsa_k_gain[l, 1]] * 2), jnp.concatenate([nsa_k_gain[l, 2]] * 2)])
    (q, ks, kw, kc, vc, vs, vw, gates, mq, mk, mv, og, mg, gif) = _proj(
        x2d, norm1_g[l][None, :], w_all, w_if, qg_pad, kg_pad)

    def compress(a, params, gain, normalize):
        pos, w1, b1, w2, b2 = (p[l] for p in params)
        w1x, posx = _compress_weights(pos, w1)
        return _compress(a.reshape(bsz, nseg, CMP_STRIDE * LANES), w1x, posx, b1[None, :],
                         w2.astype(BF16), b2[None, :], gain[None, :], normalize)

    kcmp = compress(kc, cmp_k, nsa_k_gain[l, 0], True)
    vcmp = compress(vc, cmp_v, jnp.ones((NSA_D,), F32), False)

    qk_bound = lambda kg: 8.0 * LOG2E * jnp.max(jnp.abs(nsa_q_gain[l])) * jnp.max(jnp.abs(kg))
    _, _, _, tab_max = _bias_tables(rel_table, (0.0, 0.0, 0.0))
    snap = lambda v: v.astype(BF16).astype(F32)
    shifts = tuple(snap(qk_bound(nsa_k_gain[l, j]) + jnp.maximum(tab_max, 0.0)) for j in range(3))
    safe = 2.0 * jnp.max(jnp.stack(shifts)) < SAFE_SHIFT_LOG2

    seq = lambda a: a.reshape(bsz, s, a.shape[-1])
    front = lambda a, p: jnp.pad(seq(a), ((0, 0), (p, 0), (0, 0)))
    operands = (seq(q), front(ks, SLC_PAD), front(vs, SLC_PAD), front(kw, WINDOW), front(vw, WINDOW),
                kcmp, vcmp, oh, seq(gates))

    def run(online):
        sh = (0.0, 0.0, 0.0) if online else shifts
        bcf, bs, bw, _ = _bias_tables(rel_table, sh)
        shift = jnp.full((1, LANES), -sh[1], F32)
        return _nsa(online, *operands, bcf, bs, bw, c2s, shift)

    y_a = lax.cond(safe, lambda: run(False), lambda: run(True))

    gate_b = jnp.concatenate([ml_i_bias[l], ml_f_bias[l]])[:, None]
    y_b = _mlstm(seq(mq), seq(mk), seq(mv), seq(og), gif, ml_conv_w[l], ml_conv_b[l][None, :], gate_b)

    out = _merge_ffn(x2d, y_a.reshape(n, 1024), y_b.reshape(n, 512), mg, _branch_a_weights(w_branch_a[l]),
                     w_branch_b[l].astype(BF16), w_out[l].astype(BF16), norm2_g[l][None, :],
                     w_ff1[l].astype(BF16), w_ff2[l].astype(BF16))
    return out, y_a, y_b


def _consts(s):
    nseg = s // CMP_STRIDE
    nsel = s // SLC_BLOCK
    blk_of_key = np.arange(s) // SLC_BLOCK
    oh = np.concatenate([np.ones((SLC_PAD, LANES), np.float32),
                         (blk_of_key[:, None] == np.arange(LANES)[None, :]).astype(np.float32)], axis=0)
    ci = np.arange(nseg)[:, None] * CMP_STRIDE
    sj = np.arange(LANES)[None, :] * SLC_BLOCK
    c2s = ((ci < sj + SLC_BLOCK) & (ci + CMP_LEN > sj) & (np.arange(LANES)[None, :] < nsel)
           & (np.arange(nseg)[:, None] < nseg - 1))
    return jnp.asarray(oh, BF16), jnp.asarray(c2s.astype(np.float32), BF16)


def kernel(x, norm1_g, w_in, nsa_q_gain, nsa_k_gain, cmp_k_pos, cmp_k_w1, cmp_k_b1, cmp_k_w2, cmp_k_b2, cmp_v_pos, cmp_v_w1, cmp_v_b1, cmp_v_w2, cmp_v_b2, rel_table, ml_conv_w, ml_conv_b, ml_i_bias, ml_f_bias, w_branch_a, w_branch_b, w_out, norm2_g, w_ff1, w_ff2):
    bsz, s, _ = x.shape
    consts = _consts(s)
    x2d = x.reshape(bsz * s, D_MODEL)
    for l in range(norm1_g.shape[0]):
        x2d, _, _ = _layer(l, x2d, bsz, s, consts, rel_table, norm1_g, w_in, nsa_q_gain, nsa_k_gain,
                           (cmp_k_pos, cmp_k_w1, cmp_k_b1, cmp_k_w2, cmp_k_b2),
                           (cmp_v_pos, cmp_v_w1, cmp_v_b1, cmp_v_w2, cmp_v_b2),
                           ml_conv_w, ml_conv_b, ml_i_bias, ml_f_bias,
                           w_branch_a, w_branch_b, w_out, norm2_g, w_ff1, w_ff2)
    return x2d.reshape(bsz, s, D_MODEL)
```

```python
import functools
import math

import numpy as np
import jax
import jax.numpy as jnp
from jax import lax
from jax.experimental import pallas as pl
from jax.experimental.pallas import tpu as pltpu

F32 = jnp.float32
BF16 = jnp.bfloat16

D_MODEL = 1024
NSA_HEADS = 8
NSA_GROUPS = 2
NSA_HPG = NSA_HEADS // NSA_GROUPS
NSA_D = 64
CMP_LEN = 32
CMP_STRIDE = 16
CMP_HIDDEN = 256
SLC_BLOCK = 64
SLC_TOPK = 16
WINDOW = 512
FORCED_SCORE = 1e4
ML_HEADS = 4
ML_D = 128
ML_CHUNK = 64
CONV_WIDTH = 4
D_FF = 4 * D_MODEL
REL_BUCKETS = 32
REL_MAX_DIST = 128
RMS_EPS = 1e-6

LANES = 128
NSA_TILE = 128
SLC_PAD = 128
FAR_CHUNK = 512
NEAR_KEYS = 2 * NSA_TILE
WIN_KEYS = WINDOW + NSA_TILE
BAND_SLOTS = 26
MASK_BIG = -1e9
MASK_F32 = -1e30
LOG2E = math.log2(math.e)
SAFE_SHIFT_LOG2 = 50.0
VMEM_LIMIT = 56 * 1024 * 1024

_Q_OFF, _KS_OFF, _KW_OFF, _KC_OFF, _VC_OFF, _VS_OFF, _VW_OFF = 0, 1024, 1152, 1280, 1408, 1536, 1664
_GATE_OFF, _MQ_OFF, _MK_OFF, _MV_OFF, _MO_OFF, _MG_OFF, _W_COLS = 1792, 1920, 2432, 2944, 3456, 3968, 6016
_IF_LANE = 24


def _nt(a, b, precision=None):
    return lax.dot_general(a, b, (((1,), (1,)), ((), ())), precision=precision,
                           preferred_element_type=F32)


def _dot(a, b, precision=None):
    return jnp.dot(a, b, precision=precision, preferred_element_type=F32)


def _split3(x, dot_part):
    hi = x.astype(BF16)
    r1 = x - hi.astype(F32)
    mid = r1.astype(BF16)
    lo = (r1 - mid.astype(F32)).astype(BF16)
    return dot_part(hi) + dot_part(mid) + dot_part(lo)


def _const_spec(shape):
    nd = len(shape)
    return pl.BlockSpec(shape, lambda *_: (0,) * nd, pipeline_mode=pl.Buffered(1))


def _proj_kernel(x_ref, g1_ref, w_ref, wif_ref, qg_ref, kg_ref, cw_ref, cb_ref,
                 q_ref, ks_ref, kw_ref, kc_ref, vc_ref, vs_ref, vw_ref, gate_ref, gcol_ref,
                 mq_ref, mk_ref, mv_ref, mo_ref, mg_ref, gif_ref, ext_sc, *, tiles_per_seq):
    tm = x_ref.shape[0]
    x = x_ref[...]
    h = x * lax.rsqrt(jnp.mean(x * x, axis=-1, keepdims=True) + RMS_EPS) * g1_ref[...]
    hb = h.astype(BF16)

    def proj(off, width):
        return _dot(hb, w_ref[:, off:off + width])

    lane = lax.broadcasted_iota(jnp.int32, (1, LANES), 1)
    low = lane < NSA_D

    for pair in range(NSA_HEADS // 2):
        both = proj(_Q_OFF + 2 * LANES * pair, 2 * LANES)
        for hh in range(2):
            blk = both[:, LANES * hh:LANES * (hh + 1)]
            ms = jnp.sum(blk * blk, axis=-1, keepdims=True) * (1.0 / NSA_D)
            qn = blk * lax.rsqrt(ms + RMS_EPS) * qg_ref[...] * (NSA_D ** -0.5 * LOG2E)
            col = LANES * (2 * pair + hh)
            q_ref[:, col:col + LANES] = qn.astype(q_ref.dtype)

    both = proj(_KS_OFF, 2 * LANES)
    for hh, ref in enumerate((ks_ref, kw_ref)):
        blk = both[:, LANES * hh:LANES * (hh + 1)]
        sq = blk * blk
        ms0 = jnp.sum(jnp.where(low, sq, 0.0), axis=-1, keepdims=True) * (1.0 / NSA_D)
        ms1 = jnp.sum(jnp.where(low, 0.0, sq), axis=-1, keepdims=True) * (1.0 / NSA_D)
        inv = jnp.where(low, lax.rsqrt(ms0 + RMS_EPS), lax.rsqrt(ms1 + RMS_EPS))
        ref[...] = (blk * inv * kg_ref[hh:hh + 1, :]).astype(ref.dtype)

    for off, refs in ((_KC_OFF, (kc_ref, vc_ref)), (_VS_OFF, (vs_ref, vw_ref))):
        both = proj(off, 2 * LANES)
        for hh, ref in enumerate(refs):
            ref[...] = both[:, LANES * hh:LANES * (hh + 1)].astype(ref.dtype)

    slab = proj(_GATE_OFF, LANES)
    gate_ref[...] = jax.nn.sigmoid(slab)
    gcol_ref[...] = slab

    @pl.when(pl.program_id(0) % tiles_per_seq == 0)
    def _():
        ext_sc[0:8, :] = jnp.zeros((8, ext_sc.shape[1]), F32)

    ext_sc[8:, 0:512] = proj(_MQ_OFF, 512)
    ext_sc[8:, 512:1024] = proj(_MK_OFF, 512)
    conv = cb_ref[...]
    for t in range(CONV_WIDTH):
        lo = 8 - (CONV_WIDTH - 1) + t
        conv = conv + ext_sc[lo:lo + tm, :] * cw_ref[t:t + 1, :]
    ext_sc[0:8, :] = ext_sc[tm:tm + 8, :]
    qk = conv * jax.nn.sigmoid(conv)
    mq_ref[...] = (qk[:, :512] * (ML_D ** -0.5)).astype(mq_ref.dtype)
    mk_ref[...] = qk[:, 512:].astype(mk_ref.dtype)

    mv_ref[...] = proj(_MV_OFF, 512).astype(mv_ref.dtype)
    mo_ref[...] = jax.nn.sigmoid(proj(_MO_OFF, 512)).astype(mo_ref.dtype)
    for c in range(4):
        mg_ref[:, 512 * c:512 * (c + 1)] = jax.nn.sigmoid(proj(_MG_OFF + 512 * c, 512)).astype(mg_ref.dtype)
    gif_ref[...] = _nt(wif_ref[...], hb)


def _proj(x2d, seq_len, g1, w_all, w_if, qg_pad, kg_pad, conv_w, conv_b, tm=512):
    n = x2d.shape[0]
    row = lambda w: pl.BlockSpec((tm, w), lambda i: (i, 0))
    out_shapes = (
        jax.ShapeDtypeStruct((n, 1024), BF16),
        jax.ShapeDtypeStruct((n, LANES), BF16),
        jax.ShapeDtypeStruct((n, LANES), BF16),
        jax.ShapeDtypeStruct((n, LANES), BF16),
        jax.ShapeDtypeStruct((n, LANES), BF16),
        jax.ShapeDtypeStruct((n, LANES), BF16),
        jax.ShapeDtypeStruct((n, LANES), BF16),
        jax.ShapeDtypeStruct((n, LANES), F32),
        jax.ShapeDtypeStruct((n, LANES), F32),
        jax.ShapeDtypeStruct((n, 512), BF16),
        jax.ShapeDtypeStruct((n, 512), BF16),
        jax.ShapeDtypeStruct((n, 512), BF16),
        jax.ShapeDtypeStruct((n, 512), BF16),
        jax.ShapeDtypeStruct((n, 2048), BF16),
        jax.ShapeDtypeStruct((8, n), F32),
    )
    out_specs = (row(1024),) + (row(LANES),) * 8 + (row(512),) * 4 + (row(2048),
                 pl.BlockSpec((8, tm), lambda i: (0, i)))
    return pl.pallas_call(
        functools.partial(_proj_kernel, tiles_per_seq=seq_len // tm),
        grid=(n // tm,),
        in_specs=[row(D_MODEL), _const_spec((1, D_MODEL)), _const_spec((D_MODEL, _W_COLS)),
                  _const_spec((8, D_MODEL)), _const_spec((1, LANES)), _const_spec((2, LANES)),
                  _const_spec(conv_w.shape), _const_spec(conv_b.shape)],
        out_specs=out_specs,
        out_shape=out_shapes,
        scratch_shapes=[pltpu.VMEM((tm + 8, 1024), F32)],
        compiler_params=pltpu.CompilerParams(dimension_semantics=("arbitrary",),
                                             vmem_limit_bytes=VMEM_LIMIT),
        name="proj",
    )(x2d, g1, w_all, w_if, qg_pad, kg_pad, conv_w, conv_b)


def _compress_kernel(seg_ref, w1_ref, pos_ref, b1_ref, w2_ref, b2_ref, gain_ref, out_ref, *, normalize):
    seg = seg_ref[0]
    nseg, width = seg.shape
    c = math.sqrt(2.0 / math.pi)
    pos_lo = jnp.broadcast_to(pos_ref[0:1, :], (8, width))
    pos_hi = jnp.broadcast_to(pos_ref[1:2, :], (8, width))
    outs = []
    for g in range(NSA_GROUPS):
        a = _dot(seg, w1_ref[g, 0])
        b = _dot(seg, w1_ref[g, 1])
        posb = (_dot(pos_lo, w1_ref[g, 0]) + _dot(pos_hi, w1_ref[g, 1]))[0:1] + b1_ref[...]
        pre = a + pltpu.roll(b, nseg - 1, 0) + posb
        hid = 0.5 * pre * (1.0 + jnp.tanh(c * (pre + 0.044715 * (pre * pre * pre))))
        out = _dot(hid.astype(BF16), w2_ref[...]) + b2_ref[...]
        if normalize:
            ms = jnp.mean(out * out, axis=-1, keepdims=True)
            out = out * lax.rsqrt(ms + RMS_EPS) * gain_ref[...]
        outs.append(out)
    out_ref[0] = jnp.concatenate(outs, axis=1).astype(out_ref.dtype)


def _compress(segs, w1x, posx, b1, w2, b2, gain, normalize):
    bsz, nseg, width = segs.shape
    return pl.pallas_call(
        functools.partial(_compress_kernel, normalize=normalize),
        grid=(bsz,),
        in_specs=[pl.BlockSpec((1, nseg, width), lambda b: (b, 0, 0)),
                  _const_spec(w1x.shape), _const_spec(posx.shape), _const_spec(b1.shape),
                  _const_spec(w2.shape), _const_spec(b2.shape), _const_spec(gain.shape)],
        out_specs=pl.BlockSpec((1, nseg, LANES), lambda b: (b, 0, 0)),
        out_shape=jax.ShapeDtypeStruct((bsz, nseg, LANES), BF16),
        compiler_params=pltpu.CompilerParams(dimension_semantics=("parallel",),
                                             vmem_limit_bytes=VMEM_LIMIT),
        name="compress",
    )(segs, w1x, posx, b1, w2, b2, gain)


def _nsa_kernel(q_ref, ks_ref, vs_ref, kw_ref, vw_ref, kc_ref, vc_ref, oh_ref, gate_ref,
                bcf_ref, bs_ref, bw_ref, c2s_ref, shift_ref, out_ref, acc_sc, *extra_sc, online):
    i = pl.program_id(1)
    t0 = pl.multiple_of(i * NSA_TILE, NSA_TILE)
    T = NSA_TILE
    rows = NSA_HEADS * T
    grows = NSA_HPG * T
    ncmp = kc_ref.shape[1]
    nblk = LANES

    q = q_ref[0]
    qpad = jnp.concatenate([q[:, LANES * h:LANES * (h + 1)] for h in range(NSA_HEADS)], axis=0)
    ones_v = jnp.ones((WIN_KEYS, LANES), BF16)

    def attend(s, v, n_keys):
        if online:
            m = jnp.max(s, axis=1, keepdims=True)
            m = jnp.where(m < 0.1 * MASK_BIG, 0.0, m)
            s = s - m
        p = jnp.exp2(s)
        acc = _dot(p.astype(BF16), jnp.concatenate([v, ones_v[:n_keys]], axis=1))
        return p, acc[:, :LANES], acc[:, LANES:]

    n_io = lax.broadcasted_iota(jnp.int32, (ncmp, LANES), 0)
    l_io = lax.broadcasted_iota(jnp.int32, (ncmp, LANES), 1)
    band = jnp.clip(n_io - 8 * i + 16, 0, BAND_SLOTS - 1)
    oh_c = jnp.where(band == (l_io & 63), 1.0, 0.0).astype(BF16)
    kc_aug = jnp.concatenate([kc_ref[0], oh_c], axis=1)
    qc_aug = jnp.concatenate([qpad, bcf_ref[...]], axis=1)
    p_c, num_c, l_c = attend(_nt(qc_aug, kc_aug), vc_ref[0], ncmp)
    inv_c = 1.0 / jnp.maximum(l_c, 1e-30)
    o_c = num_c * inv_c
    p_c = p_c * jnp.concatenate([inv_c] * (ncmp // LANES), axis=1)

    s_w = _nt(qpad, kw_ref[0, pl.ds(t0, WIN_KEYS), :]) + bw_ref[...]
    c_io = lax.broadcasted_iota(jnp.int32, (rows, WIN_KEYS), 1)
    s_w = jnp.where(c_io + (t0 - WINDOW) >= 0, s_w, MASK_F32)
    _, num_w, l_w = attend(s_w, vw_ref[0, pl.ds(t0, WIN_KEYS), :], WIN_KEYS)
    o_w = num_w * (1.0 / l_w)

    j_io = lax.broadcasted_iota(jnp.int32, (nblk, T), 0)
    r_io = lax.broadcasted_iota(jnp.int32, (nblk, T), 1)
    j_f = j_io.astype(F32)
    cur = 2 * i + jnp.where(r_io >= SLC_BLOCK, 1, 0)
    forced = (j_io == 0) | (j_io == cur) | (j_io == cur - 1)
    valid = j_io <= cur
    jl_io = lax.broadcasted_iota(jnp.int32, (T, nblk), 1)
    near_blk = jl_io >= 2 * (i - 1)
    far_pick = shift_ref[0:1, 0:1]
    qn_parts, qf_parts = [], []
    for g in range(NSA_GROUPS):
        ps = p_c[g * grows:g * grows + T]
        for hh in range(1, NSA_HPG):
            ps = ps + p_c[g * grows + hh * T:g * grows + (hh + 1) * T]
        ps_hi = ps.astype(BF16)
        ps_lo = (ps - ps_hi.astype(F32)).astype(BF16)
        imp = _dot(ps_hi, c2s_ref[...]) + _dot(ps_lo, c2s_ref[...])
        score = jnp.where(forced, -jnp.inf, jnp.where(valid, imp.T, -1.0))
        for _ in range(SLC_TOPK - 3):
            best = jnp.max(score, axis=0, keepdims=True)
            first = jnp.min(jnp.where(score == best, j_f, float(nblk)), axis=0, keepdims=True)
            score = jnp.where(j_f == first, -jnp.inf, score)
        picked = score.T == -jnp.inf
        sb_near = jnp.where(picked & near_blk, 0.0, MASK_BIG).astype(BF16)
        sb_far = jnp.where(picked & jnp.logical_not(near_blk), far_pick, MASK_BIG).astype(BF16)
        qg = qpad[g * grows:(g + 1) * grows]
        qn_parts.append(jnp.concatenate([qg, jnp.concatenate([sb_near] * NSA_HPG, axis=0)], axis=1))
        qf_parts.append(jnp.concatenate([qg, jnp.concatenate([sb_far] * NSA_HPG, axis=0)], axis=1))
    q_near = jnp.concatenate(qn_parts, axis=0)
    q_far = jnp.concatenate(qf_parts, axis=0)

    k_near = jnp.concatenate([ks_ref[0, pl.ds(t0, NEAR_KEYS), :], oh_ref[pl.ds(t0, NEAR_KEYS), :]], axis=1)
    v_near = jnp.concatenate([vs_ref[0, pl.ds(t0, NEAR_KEYS), :], ones_v[:NEAR_KEYS]], axis=1)
    s_n = _nt(q_near, k_near) + bs_ref[...]
    if online:
        m_sc, = extra_sc
        m_n = jnp.max(s_n, axis=1, keepdims=True)
        m_sc[...] = m_n
        s_n = s_n - m_n
    acc_sc[...] = _dot(jnp.exp2(s_n).astype(BF16), v_near)

    last_chunk = (ks_ref.shape[1] - SLC_PAD) // FAR_CHUNK - 1

    def far_start(c):
        return pl.multiple_of(SLC_PAD + FAR_CHUNK * jnp.minimum(c, last_chunk), LANES)

    def far_scores(c):
        start = far_start(c)
        k_f = jnp.concatenate([ks_ref[0, pl.ds(start, FAR_CHUNK), :], oh_ref[pl.ds(start, FAR_CHUNK), :]], axis=1)
        return _nt(q_far, k_f)

    def far_values(c):
        return jnp.concatenate([vs_ref[0, pl.ds(far_start(c), FAR_CHUNK), :], ones_v[:FAR_CHUNK]], axis=1)

    n_far = (i + 2) // 4
    if online:
        def far_online(c, carry):
            s_f = far_scores(c)
            m_old = m_sc[...]
            m_new = jnp.maximum(m_old, jnp.max(s_f, axis=1, keepdims=True))
            m_sc[...] = m_new
            acc_sc[...] = (jnp.exp2(m_old - m_new) * acc_sc[...]
                           + _dot(jnp.exp2(s_f - m_new).astype(BF16), far_values(c)))
            return carry

        lax.fori_loop(0, n_far, far_online, 0)
    else:
        pa_sc, pb_sc = extra_sc
        pa_sc[...] = jnp.exp2(far_scores(0)).astype(BF16)

        def far_pair(cc, carry):
            c0 = 2 * cc
            pv0 = _dot(pa_sc[...], far_values(c0))
            pb_sc[...] = jnp.exp2(far_scores(c0 + 1)).astype(BF16)
            pv1 = _dot(pb_sc[...], far_values(c0 + 1))
            pa_sc[...] = jnp.exp2(far_scores(c0 + 2)).astype(BF16)
            acc_sc[...] += pv0 + pv1
            return carry

        lax.fori_loop(0, (n_far + 1) // 2, far_pair, 0)

    acc = acc_sc[...]
    o_s = acc[:, :LANES] * (1.0 / acc[:, LANES:])

    gates = gate_ref[0]
    lane_hi = lax.broadcasted_iota(jnp.int32, (T, LANES), 1) >= NSA_D
    for h in range(NSA_HEADS):
        sl = slice(h * T, (h + 1) * T)
        o = (gates[:, h:h + 1] * o_c[sl] + gates[:, 8 + h:9 + h] * o_s[sl]
             + gates[:, 16 + h:17 + h] * o_w[sl])
        keep = lane_hi if h // NSA_HPG == 1 else jnp.logical_not(lane_hi)
        out_ref[0, :, LANES * h:LANES * (h + 1)] = jnp.where(keep, o, 0.0).astype(out_ref.dtype)


def _nsa(online, q, ks, vs, kw, vw, kc, vc, oh, gates, bcf, bs, bw, c2s, shift):
    bsz, s = q.shape[:2]
    T = NSA_TILE
    rows = NSA_HEADS * T
    per_b = lambda a: pl.BlockSpec((1,) + a.shape[1:], lambda b, i: (b, 0, 0))
    scratch = [pltpu.VMEM((rows, 2 * LANES), F32)]
    if online:
        scratch.append(pltpu.VMEM((rows, 1), F32))
    else:
        scratch += [pltpu.VMEM((rows, FAR_CHUNK), BF16)] * 2
    return pl.pallas_call(
        functools.partial(_nsa_kernel, online=online),
        grid=(bsz, s // T),
        in_specs=[pl.BlockSpec((1, T, 1024), lambda b, i: (b, i, 0)),
                  per_b(ks), per_b(vs), per_b(kw), per_b(vw), per_b(kc), per_b(vc),
                  _const_spec(oh.shape),
                  pl.BlockSpec((1, T, LANES), lambda b, i: (b, i, 0)),
                  _const_spec(bcf.shape), _const_spec(bs.shape), _const_spec(bw.shape),
                  _const_spec(c2s.shape), _const_spec(shift.shape)],
        out_specs=pl.BlockSpec((1, T, 1024), lambda b, i: (b, i, 0)),
        out_shape=jax.ShapeDtypeStruct((bsz, s, 1024), BF16),
        scratch_shapes=scratch,
        compiler_params=pltpu.CompilerParams(dimension_semantics=("parallel", "arbitrary"),
                                             vmem_limit_bytes=VMEM_LIMIT),
        name="nsa_online" if online else "nsa",
    )(q, ks, vs, kw, vw, kc, vc, oh, gates, bcf, bs, bw, c2s, shift)


ML_ROWS = 256


def _mlstm_kernel(q_ref, k_ref, v_ref, o_ref, gif_ref, gcol_ref, gb_ref, gbrow_ref, tril_ref, triu_ref,
                  out_ref, c_sc, m_sc):
    j = pl.program_id(1)
    R = ML_ROWS
    L = ML_CHUNK

    @pl.when(j == 0)
    def _():
        c_sc[...] = jnp.zeros_like(c_sc)
        m_sc[...] = jnp.zeros_like(m_sc)

    og = o_ref[0]

    pre = gif_ref[...] + gb_ref[...]
    row8 = lax.broadcasted_iota(jnp.int32, (8, R), 0)
    logf = jnp.minimum(pre, 0.0) - jnp.log(1.0 + jnp.exp(-jnp.abs(pre)))
    g8 = jnp.where(row8 < ML_HEADS, pre, logf)
    pre_c = gcol_ref[0] + gbrow_ref[...]
    lane_c = lax.broadcasted_iota(jnp.int32, pre_c.shape, 1)
    logf_c = jnp.minimum(pre_c, 0.0) - jnp.log(1.0 + jnp.exp(-jnp.abs(pre_c)))
    g_c = jnp.where(lane_c < _IF_LANE + ML_HEADS, pre_c, logf_c)
    cum_c = _split3(g_c, lambda part: _dot(tril_ref[...], part))
    cum_row = _split3(g8, lambda part: _dot(part, triu_ref[...]))

    a_io = lax.broadcasted_iota(jnp.int32, (L, L), 0)
    b_io = lax.broadcasted_iota(jnp.int32, (L, L), 1)
    causal = (b_io <= a_io)[None]
    nc = R // L
    pairs = [(c, h) for c in range(nc) for h in range(ML_HEADS)]

    def blocks(a):
        return jnp.stack([a[c * L:(c + 1) * L, h * ML_D:(h + 1) * ML_D] for c, h in pairs])

    def cols(a, k0):
        return jnp.stack([jnp.broadcast_to(a[c * L:(c + 1) * L, k0 + h:k0 + h + 1], (L, LANES)) for c, h in pairs])

    def rows(a, k0):
        return jnp.stack([a[k0 + h:k0 + h + 1, c * L:(c + 1) * L] for c, h in pairs])

    bdot = lambda eq, x, y: jnp.einsum(eq, x, y, preferred_element_type=F32)
    qb, kb, vb = blocks(q_ref[0]), blocks(k_ref[0]), blocks(v_ref[0])
    b_col, li_col = cols(cum_c, _IF_LANE + ML_HEADS), cols(g_c, _IF_LANE)
    b_row, li_row = rows(cum_row, ML_HEADS), rows(g8, 0)
    gsum = b_row[:, :, L - 1:L]

    s_max = jnp.max(gsum - b_row + li_row, axis=2, keepdims=True)
    m_run = m_sc[:, 0:1, 0:1]
    m_ins, m_outs = [], []
    for c in range(nc):
        hs = slice(c * ML_HEADS, (c + 1) * ML_HEADS)
        m_ins.append(m_run)
        m_run = jnp.maximum(gsum[hs] + m_run, s_max[hs])
        m_outs.append(m_run)
    m_sc[...] = jnp.broadcast_to(m_run, m_sc.shape)
    m_in, m_out = jnp.concatenate(m_ins, axis=0), jnp.concatenate(m_outs, axis=0)

    log_d = jnp.where(causal, b_col[:, :, :L] - b_row + li_row, -jnp.inf)
    inter = b_col + m_in
    m_row = jnp.maximum(inter, jnp.broadcast_to(jnp.max(log_d, axis=2, keepdims=True), inter.shape))
    w = bdot('bik,bjk->bij', qb, kb) * jnp.exp(log_d - m_row[:, :, :L])
    v_aug = jnp.concatenate([vb, jnp.ones(vb.shape, BF16)], axis=2)
    wv = bdot('bij,bjd->bid', w.astype(BF16), v_aug)
    inter_scale = jnp.exp(inter - m_row)
    inter_scale = jnp.concatenate([inter_scale, inter_scale], axis=2)
    floor = jnp.exp(-m_row)
    k_src = (kb.astype(F32) * jnp.exp(gsum - b_col + li_col - m_out)).astype(BF16)
    upd = bdot('bjk,bjd->bkd', k_src, v_aug)
    decay = jnp.exp(gsum + m_in - m_out)

    c_aug = c_sc[...]
    for c in range(nc):
        hs = slice(c * ML_HEADS, (c + 1) * ML_HEADS)
        nd = inter_scale[hs] * bdot('hik,hkd->hid', qb[hs], c_aug.astype(BF16)) + wv[hs]
        hval = nd[:, :, :ML_D] / jnp.maximum(jnp.abs(nd[:, :, ML_D:]), floor[hs])
        for h in range(ML_HEADS):
            rs, cs = slice(c * L, (c + 1) * L), slice(h * ML_D, (h + 1) * ML_D)
            out_ref[0, rs, cs] = (og[rs, cs].astype(F32) * hval[h]).astype(out_ref.dtype)
        c_aug = decay[hs] * c_aug + upd[hs]
    c_sc[...] = c_aug


def _mlstm(mq, mk, mv, og, gif, gcol, gate_b, gate_b_row):
    bsz, s, width = mq.shape
    R = ML_ROWS
    nblk = s // R
    seq = lambda: pl.BlockSpec((1, R, width), lambda b, j: (b, j, 0))
    pos = np.arange(R)
    same_chunk = (pos[:, None] // ML_CHUNK) == (pos[None, :] // ML_CHUNK)
    tril = (same_chunk & (pos[None, :] <= pos[:, None])).astype(np.float32)
    return pl.pallas_call(
        _mlstm_kernel,
        grid=(bsz, nblk),
        in_specs=[seq(), seq(), seq(), seq(),
                  pl.BlockSpec((8, R), lambda b, j: (0, b * nblk + j)),
                  pl.BlockSpec((1, R, LANES), lambda b, j: (b, j, 0)),
                  _const_spec(gate_b.shape), _const_spec(gate_b_row.shape),
                  _const_spec((R, R)), _const_spec((R, R))],
        out_specs=seq(),
        out_shape=jax.ShapeDtypeStruct((bsz, s, width), BF16),
        scratch_shapes=[pltpu.VMEM((ML_HEADS, ML_D, 2 * ML_D), F32),
                        pltpu.VMEM((ML_HEADS, 8, LANES), F32)],
        compiler_params=pltpu.CompilerParams(dimension_semantics=("parallel", "arbitrary"),
                                             vmem_limit_bytes=VMEM_LIMIT),
        name="mlstm",
    )(mq, mk, mv, og, gif, gcol, gate_b, gate_b_row, jnp.asarray(tril, BF16), jnp.asarray(tril.T, BF16))


FF_CHUNK = 512


def _merge_ffn_kernel(x_ref, ya_ref, yb_ref, mg_ref, wa_ref, wb_ref, wo_ref, g2_ref, w1_ref, w2_ref,
                      out_ref):
    mg = mg_ref[...]
    mixed = (mg[:, :D_MODEL].astype(F32) * _dot(ya_ref[...], wa_ref[...])
             + mg[:, D_MODEL:].astype(F32) * _dot(yb_ref[...], wb_ref[...]))
    x1 = x_ref[...] + _dot(mixed.astype(BF16), wo_ref[...])
    h2 = x1 * lax.rsqrt(jnp.mean(x1 * x1, axis=-1, keepdims=True) + RMS_EPS) * g2_ref[...]
    h2 = h2.astype(BF16)
    acc = x1
    for c in range(D_FF // FF_CHUNK):
        a = jnp.maximum(_dot(h2, w1_ref[:, c * FF_CHUNK:(c + 1) * FF_CHUNK]), 0.0)
        acc = acc + _dot((a * a).astype(BF16), w2_ref[c * FF_CHUNK:(c + 1) * FF_CHUNK, :])
    out_ref[...] = acc


def _merge_ffn(x2d, ya, yb, mg, wa, wb, wo, g2, w1, w2, tm=512):
    n = x2d.shape[0]
    row = lambda w: pl.BlockSpec((tm, w), lambda i: (i, 0))
    return pl.pallas_call(
        _merge_ffn_kernel,
        grid=(n // tm,),
        in_specs=[row(D_MODEL), row(1024), row(512), row(2048),
                  _const_spec(wa.shape), _const_spec(wb.shape), _const_spec(wo.shape),
                  _const_spec(g2.shape), _const_spec(w1.shape), _const_spec(w2.shape)],
        out_specs=row(D_MODEL),
        out_shape=jax.ShapeDtypeStruct((n, D_MODEL), F32),
        compiler_params=pltpu.CompilerParams(dimension_semantics=("parallel",),
                                             vmem_limit_bytes=VMEM_LIMIT),
        name="merge_ffn",
    )(x2d, ya, yb, mg, wa, wb, wo, g2, w1, w2)


def _proj_weights(w):
    widths = (512, 128, 128, 128, 128, 128, 128, 24, 512, 512, 512, 4, 4, 512, 2048)
    off = np.concatenate([[0], np.cumsum(widths)])
    (nq, nkc, nvc, nks, nvs, nkw, nvw, ngate, mq, mk, mv, mi, mf, mo, mgate) = (int(o) for o in off[:-1])
    col = lambda start, width: w[:, start:start + width]
    zero = jnp.zeros((w.shape[0], NSA_D), w.dtype)
    parts = []
    for h in range(NSA_HEADS):
        qh = col(nq + NSA_D * h, NSA_D)
        parts += [qh, zero] if h // NSA_HPG == 0 else [zero, qh]
    parts += [col(nks, 128), col(nkw, 128), col(nkc, 128), col(nvc, 128), col(nvs, 128), col(nvw, 128)]
    gate = col(ngate, 24).reshape(-1, NSA_HEADS, 3).transpose(0, 2, 1).reshape(-1, 24)
    parts += [gate, col(mi, 4), col(mf, 4), jnp.zeros((w.shape[0], LANES - _IF_LANE - 8), w.dtype)]
    parts += [col(mq, 512), col(mk, 512), col(mv, 512), col(mo, 512), col(mgate, 2048)]
    w_all = jnp.concatenate(parts, axis=1).astype(BF16)
    w_if = jnp.concatenate([col(mi, 4), col(mf, 4)], axis=1).T.astype(BF16)
    return w_all, w_if


def _branch_a_weights(w):
    zero = jnp.zeros((NSA_D, w.shape[1]), w.dtype)
    parts = []
    for h in range(NSA_HEADS):
        wh = w[NSA_D * h:NSA_D * (h + 1)]
        parts += [wh, zero] if h // NSA_HPG == 0 else [zero, wh]
    return jnp.concatenate(parts, axis=0).astype(BF16)


def _compress_weights(pos, w1):
    r = w1.reshape(2, CMP_STRIDE, 1, NSA_D, CMP_HIDDEN)
    z = jnp.zeros_like(r)
    w1x = jnp.stack([jnp.concatenate([r, z], axis=2), jnp.concatenate([z, r], axis=2)])
    w1x = w1x.reshape(NSA_GROUPS, 2, CMP_STRIDE * NSA_GROUPS * NSA_D, CMP_HIDDEN).astype(BF16)
    posx = jnp.broadcast_to(pos.reshape(2, CMP_STRIDE, 1, NSA_D), (2, CMP_STRIDE, NSA_GROUPS, NSA_D))
    return w1x, posx.reshape(2, CMP_STRIDE * NSA_GROUPS * NSA_D).astype(BF16)


def _t5_bucket(dist):
    n = jnp.maximum(dist, 0)
    max_exact = REL_BUCKETS // 2
    nf = jnp.maximum(n, 1).astype(F32)
    large = max_exact + (jnp.log(nf / max_exact) / math.log(REL_MAX_DIST / max_exact)
                         * (REL_BUCKETS - max_exact)).astype(jnp.int32)
    return jnp.where(n < max_exact, n, jnp.minimum(large, REL_BUCKETS - 1))


def _toeplitz(tab, n_rows, n_cols, off):
    nd = tab.shape[1]
    ext = lambda lo, hi: jnp.take(tab, jnp.asarray(np.clip(np.arange(lo, hi), 0, nd - 1)), axis=1)
    w = jnp.concatenate([ext(off - n_cols + 1, off + 1)[:, ::-1], jnp.zeros((tab.shape[0], 1), tab.dtype),
                         ext(off + 1, off + n_rows)[:, ::-1]], axis=1)
    lw = n_rows + n_cols
    flat = jnp.tile(w, (1, n_rows))[:, :n_rows * (lw - 1)]
    return flat.reshape(tab.shape[0], n_rows, lw - 1)[:, :, :n_cols]


def _bias_tables(rel_table, shifts):
    T = NSA_TILE
    sh_c, sh_s, sh_w = shifts
    tab = (rel_table[_t5_bucket(jnp.arange(WIN_KEYS))] - rel_table[REL_BUCKETS - 1][None, :]).T * LOG2E
    r = np.arange(T)[:, None]

    c = np.arange(NEAR_KEYS)[None, :]
    ok = jnp.asarray((r - c + NSA_TILE) >= 0)[None]
    bs = jnp.where(ok, _toeplitz(tab, T, NEAR_KEYS, NSA_TILE) - sh_s, MASK_F32)
    c = np.arange(WIN_KEYS)[None, :]
    d = r - c + WINDOW
    ok = jnp.asarray((d >= 0) & (d < WINDOW))[None]
    bw = jnp.where(ok, _toeplitz(tab, T, WIN_KEYS, WINDOW) - sh_w, MASK_F32)
    m = np.arange(64)[None, :]
    d = r - 16 * m + 225
    band = (m >= 1) & (m < BAND_SLOTS - 1)
    vals = _toeplitz(tab, T, 16 * BAND_SLOTS, 225)[:, :, ::16]
    vals = jnp.pad(vals, ((0, 0), (0, 0), (0, 64 - BAND_SLOTS)))
    vals = jnp.where(jnp.asarray(band & (d >= 0))[None], vals, 0.0) - sh_c
    dead = (band & (d < 0)) | (m == BAND_SLOTS - 1)
    vals = jnp.where(jnp.asarray(dead)[None], MASK_BIG, vals)
    hi = vals.astype(BF16)
    lo = (vals - hi.astype(F32)).astype(BF16)
    bcf = jnp.concatenate([hi, lo], axis=-1)
    flat = lambda a: a.reshape(NSA_HEADS * T, a.shape[-1])
    return flat(bcf), flat(bs).astype(F32), flat(bw).astype(F32), jnp.max(tab)


def _layer(l, x2d, bsz, s, consts, rel_table, norm1_g, w_in, nsa_q_gain, nsa_k_gain, cmp_k, cmp_v,
           ml_conv_w, ml_conv_b, ml_i_bias, ml_f_bias, w_branch_a, w_branch_b, w_out, norm2_g, w_ff1, w_ff2):
    n = bsz * s
    nseg = s // CMP_STRIDE
    oh, c2s = consts
    w_all, w_if = _proj_weights(w_in[l])
    qg_pad = jnp.concatenate([nsa_q_gain[l]] * 2)[None, :]
    kg_pad = jnp.stack([jnp.concatenate([nsa_k_gain[l, 1]] * 2), jnp.concatenate([nsa_k_gain[l, 2]] * 2)])
    (q, ks, kw, kc, vc, vs, vw, gates, gcol, mq, mk, mv, og, mg, gif) = _proj(
        x2d, s, norm1_g[l][None, :], w_all, w_if, qg_pad, kg_pad, ml_conv_w[l], ml_conv_b[l][None, :])

    def compress(a, params, gain, normalize):
        pos, w1, b1, w2, b2 = (p[l] for p in params)
        w1x, posx = _compress_weights(pos, w1)
        return _compress(a.reshape(bsz, nseg, CMP_STRIDE * LANES), w1x, posx, b1[None, :],
                         w2.astype(BF16), b2[None, :], gain[None, :], normalize)

    kcmp = compress(kc, cmp_k, nsa_k_gain[l, 0], True)
    vcmp = compress(vc, cmp_v, jnp.ones((NSA_D,), F32), False)

    qk_bound = lambda kg: 8.0 * LOG2E * jnp.max(jnp.abs(nsa_q_gain[l])) * jnp.max(jnp.abs(kg))
    _, _, _, tab_max = _bias_tables(rel_table, (0.0, 0.0, 0.0))
    snap = lambda v: v.astype(BF16).astype(F32)
    shifts = tuple(snap(qk_bound(nsa_k_gain[l, j]) + jnp.maximum(tab_max, 0.0)) for j in range(3))
    safe = 2.0 * jnp.max(jnp.stack(shifts)) < SAFE_SHIFT_LOG2

    seq = lambda a: a.reshape(bsz, s, a.shape[-1])
    front = lambda a, p: jnp.pad(seq(a), ((0, 0), (p, 0), (0, 0)))
    operands = (seq(q), front(ks, SLC_PAD), front(vs, SLC_PAD), front(kw, WINDOW), front(vw, WINDOW),
                kcmp, vcmp, oh, seq(gates))

    def run(online):
        sh = (0.0, 0.0, 0.0) if online else shifts
        bcf, bs, bw, _ = _bias_tables(rel_table, sh)
        shift = jnp.full((1, LANES), -sh[1], F32)
        return _nsa(online, *operands, bcf, bs, bw, c2s, shift)

    y_a = lax.cond(safe, lambda: run(False), lambda: run(True))

    gate_b = jnp.concatenate([ml_i_bias[l], ml_f_bias[l]])
    gate_b_row = jnp.pad(gate_b, (_IF_LANE, LANES - _IF_LANE - 8))[None, :]
    y_b = _mlstm(seq(mq), seq(mk), seq(mv), seq(og), gif, seq(gcol), gate_b[:, None], gate_b_row)

    out = _merge_ffn(x2d, y_a.reshape(n, 1024), y_b.reshape(n, 512), mg, _branch_a_weights(w_branch_a[l]),
                     w_branch_b[l].astype(BF16), w_out[l].astype(BF16), norm2_g[l][None, :],
                     w_ff1[l].astype(BF16), w_ff2[l].astype(BF16))
    return out, y_a, y_b


def _consts(s):
    nseg = s // CMP_STRIDE
    nsel = s // SLC_BLOCK
    blk_of_key = np.arange(s) // SLC_BLOCK
    oh = np.concatenate([np.ones((SLC_PAD, LANES), np.float32),
                         (blk_of_key[:, None] == np.arange(LANES)[None, :]).astype(np.float32)], axis=0)
    ci = np.arange(nseg)[:, None] * CMP_STRIDE
    sj = np.arange(LANES)[None, :] * SLC_BLOCK
    c2s = ((ci < sj + SLC_BLOCK) & (ci + CMP_LEN > sj) & (np.arange(LANES)[None, :] < nsel)
           & (np.arange(nseg)[:, None] < nseg - 1))
    return jnp.asarray(oh, BF16), jnp.asarray(c2s.astype(np.float32), BF16)


def kernel(x, norm1_g, w_in, nsa_q_gain, nsa_k_gain, cmp_k_pos, cmp_k_w1, cmp_k_b1, cmp_k_w2, cmp_k_b2, cmp_v_pos, cmp_v_w1, cmp_v_b1, cmp_v_w2, cmp_v_b2, rel_table, ml_conv_w, ml_conv_b, ml_i_bias, ml_f_bias, w_branch_a, w_branch_b, w_out, norm2_g, w_ff1, w_ff2):
    bsz, s, _ = x.shape
    consts = _consts(s)
    x2d = x.reshape(bsz * s, D_MODEL)
    for l in range(norm1_g.shape[0]):
        x2d, _, _ = _layer(l, x2d, bsz, s, consts, rel_table, norm1_g, w_in, nsa_q_gain, nsa_k_gain,
                           (cmp_k_pos, cmp_k_w1, cmp_k_b1, cmp_k_w2, cmp_k_b2),
                           (cmp_v_pos, cmp_v_w1, cmp_v_b1, cmp_v_w2, cmp_v_b2),
                           ml_conv_w, ml_conv_b, ml_i_bias, ml_f_bias,
                           w_branch_a, w_branch_b, w_out, norm2_g, w_ff1, w_ff2)
    return x2d.reshape(bsz, s, D_MODEL)
```

```python
import functools
import math

import numpy as np
import jax
import jax.numpy as jnp
from jax import lax
from jax.experimental import pallas as pl
from jax.experimental.pallas import tpu as pltpu

F32 = jnp.float32
BF16 = jnp.bfloat16

D_MODEL = 1024
NSA_HEADS = 8
NSA_GROUPS = 2
NSA_HPG = NSA_HEADS // NSA_GROUPS
NSA_D = 64
CMP_LEN = 32
CMP_STRIDE = 16
CMP_HIDDEN = 256
SLC_BLOCK = 64
SLC_TOPK = 16
WINDOW = 512
FORCED_SCORE = 1e4
ML_HEADS = 4
ML_D = 128
ML_CHUNK = 64
CONV_WIDTH = 4
D_FF = 4 * D_MODEL
REL_BUCKETS = 32
REL_MAX_DIST = 128
RMS_EPS = 1e-6

LANES = 128
NSA_TILE = 256
SLC_PAD = 128
FAR_CHUNK = 512
NEAR_KEYS = SLC_PAD + NSA_TILE
WIN_KEYS = WINDOW + NSA_TILE
BAND_SLOTS = 26
BAND_OFF = 10
BAND_DIST0 = CMP_STRIDE * BAND_OFF - (CMP_LEN - 1)
MASK_BIG = -1e9
MASK_F32 = -1e30
LOG2E = math.log2(math.e)
SAFE_SHIFT_LOG2 = 50.0
VMEM_LIMIT = 56 * 1024 * 1024

_Q_OFF, _KS_OFF, _KW_OFF, _KC_OFF, _VC_OFF, _VS_OFF, _VW_OFF = 0, 512, 640, 768, 896, 1024, 1152
_GATE_OFF, _MQ_OFF, _MK_OFF, _MV_OFF, _MO_OFF, _MG_OFF, _W_COLS = 1280, 1408, 1920, 2432, 2944, 3456, 5504
_IF_LANE = 24


def _nt(a, b, precision=None):
    return lax.dot_general(a, b, (((1,), (1,)), ((), ())), precision=precision,
                           preferred_element_type=F32)


def _dot(a, b, precision=None):
    return jnp.dot(a, b, precision=precision, preferred_element_type=F32)


def _split3(x, dot_part):
    hi = x.astype(BF16)
    r1 = x - hi.astype(F32)
    mid = r1.astype(BF16)
    lo = (r1 - mid.astype(F32)).astype(BF16)
    return dot_part(hi) + dot_part(mid) + dot_part(lo)


def _const_spec(shape):
    nd = len(shape)
    return pl.BlockSpec(shape, lambda *_: (0,) * nd, pipeline_mode=pl.Buffered(1))


def _proj_kernel(x_ref, g1_ref, w_ref, wif_ref, qg_ref, kg_ref, cw_ref, cb_ref,
                 q_ref, ks_ref, kw_ref, kc_ref, vc_ref, vs_ref, vw_ref, gate_ref, gcol_ref,
                 mq_ref, mk_ref, mv_ref, mo_ref, mg_ref, gif_ref, ext_sc, *, tiles_per_seq):
    tm = x_ref.shape[0]
    x = x_ref[...]
    h = x * lax.rsqrt(jnp.mean(x * x, axis=-1, keepdims=True) + RMS_EPS) * g1_ref[...]
    hb = h.astype(BF16)

    def proj(off, width):
        return _dot(hb, w_ref[:, off:off + width])

    lane = lax.broadcasted_iota(jnp.int32, (1, LANES), 1)
    low = lane < NSA_D

    def half_norm(blk):
        sq = blk * blk
        ms0 = jnp.sum(jnp.where(low, sq, 0.0), axis=-1, keepdims=True) * (1.0 / NSA_D)
        ms1 = jnp.sum(jnp.where(low, 0.0, sq), axis=-1, keepdims=True) * (1.0 / NSA_D)
        return blk * jnp.where(low, lax.rsqrt(ms0 + RMS_EPS), lax.rsqrt(ms1 + RMS_EPS))

    for pair in range(NSA_HPG // 2):
        both = proj(_Q_OFF + 2 * LANES * pair, 2 * LANES)
        for hh in range(2):
            qn = half_norm(both[:, LANES * hh:LANES * (hh + 1)]) * qg_ref[...] * (NSA_D ** -0.5 * LOG2E)
            col = LANES * (2 * pair + hh)
            q_ref[:, col:col + LANES] = qn.astype(q_ref.dtype)

    both = proj(_KS_OFF, 2 * LANES)
    for hh, ref in enumerate((ks_ref, kw_ref)):
        ref[...] = (half_norm(both[:, LANES * hh:LANES * (hh + 1)]) * kg_ref[hh:hh + 1, :]).astype(ref.dtype)

    for off, refs in ((_KC_OFF, (kc_ref, vc_ref)), (_VS_OFF, (vs_ref, vw_ref))):
        both = proj(off, 2 * LANES)
        for hh, ref in enumerate(refs):
            ref[...] = both[:, LANES * hh:LANES * (hh + 1)].astype(ref.dtype)

    slab = proj(_GATE_OFF, LANES)
    gate_ref[...] = jax.nn.sigmoid(slab)
    gcol_ref[...] = slab

    @pl.when(pl.program_id(0) % tiles_per_seq == 0)
    def _():
        ext_sc[0:8, :] = jnp.zeros((8, ext_sc.shape[1]), F32)

    ext_sc[8:, 0:512] = proj(_MQ_OFF, 512)
    ext_sc[8:, 512:1024] = proj(_MK_OFF, 512)
    conv = cb_ref[...]
    for t in range(CONV_WIDTH):
        lo = 8 - (CONV_WIDTH - 1) + t
        conv = conv + ext_sc[lo:lo + tm, :] * cw_ref[t:t + 1, :]
    ext_sc[0:8, :] = ext_sc[tm:tm + 8, :]
    qk = conv * jax.nn.sigmoid(conv)
    mq_ref[...] = (qk[:, :512] * (ML_D ** -0.5)).astype(mq_ref.dtype)
    mk_ref[...] = qk[:, 512:].astype(mk_ref.dtype)

    mv_ref[...] = proj(_MV_OFF, 512).astype(mv_ref.dtype)
    mo_ref[...] = jax.nn.sigmoid(proj(_MO_OFF, 512)).astype(mo_ref.dtype)
    for c in range(4):
        mg_ref[:, 512 * c:512 * (c + 1)] = jax.nn.sigmoid(proj(_MG_OFF + 512 * c, 512)).astype(mg_ref.dtype)
    gif_ref[...] = _nt(wif_ref[...], hb)


def _proj(x2d, seq_len, g1, w_all, w_if, qg_pad, kg_pad, conv_w, conv_b, tm=512):
    n = x2d.shape[0]
    row = lambda w: pl.BlockSpec((tm, w), lambda i: (i, 0))
    out_shapes = (
        jax.ShapeDtypeStruct((n, 512), BF16),
        jax.ShapeDtypeStruct((n, LANES), BF16),
        jax.ShapeDtypeStruct((n, LANES), BF16),
        jax.ShapeDtypeStruct((n, LANES), BF16),
        jax.ShapeDtypeStruct((n, LANES), BF16),
        jax.ShapeDtypeStruct((n, LANES), BF16),
        jax.ShapeDtypeStruct((n, LANES), BF16),
        jax.ShapeDtypeStruct((n, LANES), F32),
        jax.ShapeDtypeStruct((n, LANES), F32),
        jax.ShapeDtypeStruct((n, 512), BF16),
        jax.ShapeDtypeStruct((n, 512), BF16),
        jax.ShapeDtypeStruct((n, 512), BF16),
        jax.ShapeDtypeStruct((n, 512), BF16),
        jax.ShapeDtypeStruct((n, 2048), BF16),
        jax.ShapeDtypeStruct((8, n), F32),
    )
    out_specs = (row(512),) + (row(LANES),) * 8 + (row(512),) * 4 + (row(2048),
                 pl.BlockSpec((8, tm), lambda i: (0, i)))
    return pl.pallas_call(
        functools.partial(_proj_kernel, tiles_per_seq=seq_len // tm),
        grid=(n // tm,),
        in_specs=[row(D_MODEL), _const_spec((1, D_MODEL)), _const_spec((D_MODEL, _W_COLS)),
                  _const_spec((8, D_MODEL)), _const_spec((1, LANES)), _const_spec((2, LANES)),
                  _const_spec(conv_w.shape), _const_spec(conv_b.shape)],
        out_specs=out_specs,
        out_shape=out_shapes,
        scratch_shapes=[pltpu.VMEM((tm + 8, 1024), F32)],
        compiler_params=pltpu.CompilerParams(dimension_semantics=("arbitrary",),
                                             vmem_limit_bytes=VMEM_LIMIT),
        name="proj",
    )(x2d, g1, w_all, w_if, qg_pad, kg_pad, conv_w, conv_b)


def _compress_kernel(seg_ref, w1_ref, pos_ref, b1_ref, w2_ref, b2_ref, gain_ref, out_ref, *, normalize):
    seg = seg_ref[0]
    nseg, width = seg.shape
    c = math.sqrt(2.0 / math.pi)
    pos_lo = jnp.broadcast_to(pos_ref[0:1, :], (8, width))
    pos_hi = jnp.broadcast_to(pos_ref[1:2, :], (8, width))
    outs = []
    for g in range(NSA_GROUPS):
        a = _dot(seg, w1_ref[g, 0])
        b = _dot(seg, w1_ref[g, 1])
        posb = (_dot(pos_lo, w1_ref[g, 0]) + _dot(pos_hi, w1_ref[g, 1]))[0:1] + b1_ref[...]
        pre = a + pltpu.roll(b, nseg - 1, 0) + posb
        hid = 0.5 * pre * (1.0 + jnp.tanh(c * (pre + 0.044715 * (pre * pre * pre))))
        out = _dot(hid.astype(BF16), w2_ref[...]) + b2_ref[...]
        if normalize:
            ms = jnp.mean(out * out, axis=-1, keepdims=True)
            out = out * lax.rsqrt(ms + RMS_EPS) * gain_ref[...]
        outs.append(out)
    out_ref[0] = jnp.concatenate(outs, axis=1).astype(out_ref.dtype)


def _compress(segs, w1x, posx, b1, w2, b2, gain, normalize):
    bsz, nseg, width = segs.shape
    return pl.pallas_call(
        functools.partial(_compress_kernel, normalize=normalize),
        grid=(bsz,),
        in_specs=[pl.BlockSpec((1, nseg, width), lambda b: (b, 0, 0)),
                  _const_spec(w1x.shape), _const_spec(posx.shape), _const_spec(b1.shape),
                  _const_spec(w2.shape), _const_spec(b2.shape), _const_spec(gain.shape)],
        out_specs=pl.BlockSpec((1, nseg, LANES), lambda b: (b, 0, 0)),
        out_shape=jax.ShapeDtypeStruct((bsz, nseg, LANES), BF16),
        compiler_params=pltpu.CompilerParams(dimension_semantics=("parallel",),
                                             vmem_limit_bytes=VMEM_LIMIT),
        name="compress",
    )(segs, w1x, posx, b1, w2, b2, gain)


def _nsa_kernel(q_ref, ks_ref, vs_ref, kw_ref, vw_ref, kc_ref, vc_ref, oh_ref, gate_ref,
                bcf_ref, bs_ref, bw_ref, c2s_ref, shift_ref, out_ref, acc_sc, *extra_sc, online):
    i = pl.program_id(1)
    t0 = pl.multiple_of(i * NSA_TILE, NSA_TILE)
    T = NSA_TILE
    rows = NSA_HEADS * T
    grows = NSA_HPG * T
    ncmp = kc_ref.shape[1]
    nblk = LANES

    q = q_ref[0]
    lane_q = lax.broadcasted_iota(jnp.int32, (T, LANES), 1)
    qpad = jnp.concatenate(
        [jnp.where((lane_q >= NSA_D) == (g == 1), q[:, LANES * j:LANES * (j + 1)], jnp.zeros((), BF16))
         for g in range(NSA_GROUPS) for j in range(NSA_HPG)], axis=0)
    ones_v = jnp.ones((WIN_KEYS, LANES), BF16)

    def attend(s, v, n_keys):
        if online:
            m = jnp.max(s, axis=1, keepdims=True)
            m = jnp.where(m < 0.1 * MASK_BIG, 0.0, m)
            s = s - m
        p = jnp.exp2(s)
        acc = _dot(p.astype(BF16), jnp.concatenate([v, ones_v[:n_keys]], axis=1))
        return p, acc[:, :LANES], acc[:, LANES:]

    n_io = lax.broadcasted_iota(jnp.int32, (ncmp, LANES), 0)
    l_io = lax.broadcasted_iota(jnp.int32, (ncmp, LANES), 1)
    band = jnp.clip(n_io - i * (T // CMP_STRIDE) + BAND_OFF, 0, BAND_SLOTS - 1)
    oh_c = jnp.where(band == (l_io & 63), 1.0, 0.0).astype(BF16)
    kc_aug = jnp.concatenate([kc_ref[0], oh_c], axis=1)
    qc_aug = jnp.concatenate([qpad, bcf_ref[...]], axis=1)
    p_c, num_c, l_c = attend(_nt(qc_aug, kc_aug), vc_ref[0], ncmp)
    inv_c = 1.0 / jnp.maximum(l_c, 1e-30)
    o_c = num_c * inv_c
    p_c = p_c * jnp.concatenate([inv_c] * (ncmp // LANES), axis=1)

    j_io = lax.broadcasted_iota(jnp.int32, (nblk, T), 0)
    r_io = lax.broadcasted_iota(jnp.int32, (nblk, T), 1)
    j_f = j_io.astype(F32)
    cur = i * (T // SLC_BLOCK) + jnp.right_shift(r_io, SLC_BLOCK.bit_length() - 1)
    forced = (j_io == 0) | (j_io == cur) | (j_io == cur - 1)
    valid = j_io <= cur
    jl_io = lax.broadcasted_iota(jnp.int32, (T, nblk), 1)
    near_blk = jl_io >= i * (T // SLC_BLOCK) - SLC_PAD // SLC_BLOCK
    far_pick = shift_ref[0:1, 0:1]
    qn_parts, qf_parts = [], []
    for g in range(NSA_GROUPS):
        ps = p_c[g * grows:g * grows + T]
        for hh in range(1, NSA_HPG):
            ps = ps + p_c[g * grows + hh * T:g * grows + (hh + 1) * T]
        ps_hi = ps.astype(BF16)
        ps_lo = (ps - ps_hi.astype(F32)).astype(BF16)
        imp = _dot(ps_hi, c2s_ref[...]) + _dot(ps_lo, c2s_ref[...])
        score = jnp.where(forced, -jnp.inf, jnp.where(valid, imp.T, -1.0))
        for _ in range(SLC_TOPK - 3):
            best = jnp.max(score, axis=0, keepdims=True)
            first = jnp.min(jnp.where(score == best, j_f, float(nblk)), axis=0, keepdims=True)
            score = jnp.where(j_f == first, -jnp.inf, score)
        picked = score.T == -jnp.inf
        sb_near = jnp.where(picked & near_blk, 0.0, MASK_BIG).astype(BF16)
        sb_far = jnp.where(picked & jnp.logical_not(near_blk), far_pick, MASK_BIG).astype(BF16)
        qg = qpad[g * grows:(g + 1) * grows]
        qn_parts.append(jnp.concatenate([qg, jnp.concatenate([sb_near] * NSA_HPG, axis=0)], axis=1))
        qf_parts.append(jnp.concatenate([qg, jnp.concatenate([sb_far] * NSA_HPG, axis=0)], axis=1))
    q_near = jnp.concatenate(qn_parts, axis=0)
    q_far = jnp.concatenate(qf_parts, axis=0)

    r_w = lax.broadcasted_iota(jnp.int32, (WIN_KEYS, LANES), 0)
    pad_flag = jnp.where(r_w + (t0 - WINDOW) < 0, 1.0, 0.0).astype(BF16)
    kw_aug = jnp.concatenate([kw_ref[0, pl.ds(t0, WIN_KEYS), :], pad_flag], axis=1)
    qw_aug = jnp.concatenate([qpad, jnp.full((rows, LANES), MASK_BIG / LANES, BF16)], axis=1)
    s_w = _nt(qw_aug, kw_aug) + bw_ref[...]
    _, num_w, l_w = attend(s_w, vw_ref[0, pl.ds(t0, WIN_KEYS), :], WIN_KEYS)
    o_w = num_w * (1.0 / l_w)

    gates = gate_ref[0]
    gate_tile = lambda col: jnp.broadcast_to(gates[:, col:col + 1], (T, LANES))
    head_rows = [slice(h * T, (h + 1) * T) for h in range(NSA_HEADS)]
    o_cw = [gate_tile(h) * o_c[head_rows[h]] + gate_tile(16 + h) * o_w[head_rows[h]] for h in range(NSA_HEADS)]
    g_slc = [gate_tile(8 + h) for h in range(NSA_HEADS)]

    k_near = jnp.concatenate([ks_ref[0, pl.ds(t0, NEAR_KEYS), :], oh_ref[pl.ds(t0, NEAR_KEYS), :]], axis=1)
    v_near = jnp.concatenate([vs_ref[0, pl.ds(t0, NEAR_KEYS), :], ones_v[:NEAR_KEYS]], axis=1)
    s_n = _nt(q_near, k_near) + bs_ref[...]
    if online:
        m_sc, = extra_sc
        m_n = jnp.max(s_n, axis=1, keepdims=True)
        m_sc[...] = m_n
        s_n = s_n - m_n
    acc_sc[...] = _dot(jnp.exp2(s_n).astype(BF16), v_near)

    last_chunk = (ks_ref.shape[1] - SLC_PAD) // FAR_CHUNK - 1

    def far_start(c):
        return pl.multiple_of(SLC_PAD + FAR_CHUNK * jnp.minimum(c, last_chunk), LANES)

    def far_scores(c, by_group=False):
        start = far_start(c)
        k_f = jnp.concatenate([ks_ref[0, pl.ds(start, FAR_CHUNK), :], oh_ref[pl.ds(start, FAR_CHUNK), :]], axis=1)
        if by_group:
            return jnp.concatenate([_nt(q_far[:grows], k_f), _nt(q_far[grows:], k_f)], axis=0)
        return _nt(q_far, k_f)

    def far_values(c):
        return jnp.concatenate([vs_ref[0, pl.ds(far_start(c), FAR_CHUNK), :], ones_v[:FAR_CHUNK]], axis=1)

    n_far = (t0 + (FAR_CHUNK - 1 - SLC_PAD)) // FAR_CHUNK
    if online:
        def far_online(c, carry):
            s_f = far_scores(c)
            m_old = m_sc[...]
            m_new = jnp.maximum(m_old, jnp.max(s_f, axis=1, keepdims=True))
            m_sc[...] = m_new
            acc_sc[...] = (jnp.exp2(m_old - m_new) * acc_sc[...]
                           + _dot(jnp.exp2(s_f - m_new).astype(BF16), far_values(c)))
            return carry

        lax.fori_loop(0, n_far, far_online, 0)
    else:
        pa_sc, pb_sc = extra_sc
        pa_sc[...] = jnp.exp2(far_scores(0, by_group=True)).astype(BF16)

        def far_pair(cc, carry):
            c0 = 2 * cc
            pv0 = _dot(pa_sc[...], far_values(c0))
            pb_sc[...] = jnp.exp2(far_scores(c0 + 1)).astype(BF16)
            pv1 = _dot(pb_sc[...], far_values(c0 + 1))
            pa_sc[...] = jnp.exp2(far_scores(c0 + 2)).astype(BF16)
            acc_sc[...] += pv0 + pv1
            return carry

        lax.fori_loop(0, (n_far + 1) // 2, far_pair, 0)

    acc = acc_sc[...]
    o_s = acc[:, :LANES] * (1.0 / acc[:, LANES:])

    head_out = [o_cw[h] + g_slc[h] * o_s[head_rows[h]] for h in range(NSA_HEADS)]
    for j in range(NSA_HPG):
        out_ref[0, :, LANES * j:LANES * (j + 1)] = jnp.where(
            lane_q < NSA_D, head_out[j], head_out[NSA_HPG + j]).astype(out_ref.dtype)


def _nsa(online, q, ks, vs, kw, vw, kc, vc, oh, gates, bcf, bs, bw, c2s, shift):
    bsz, s = q.shape[:2]
    T = NSA_TILE
    rows = NSA_HEADS * T
    per_b = lambda a: pl.BlockSpec((1,) + a.shape[1:], lambda b, i: (b, 0, 0), pipeline_mode=pl.Buffered(1))
    scratch = [pltpu.VMEM((rows, 2 * LANES), F32)]
    if online:
        scratch.append(pltpu.VMEM((rows, 1), F32))
    else:
        scratch += [pltpu.VMEM((rows, FAR_CHUNK), BF16)] * 2
    return pl.pallas_call(
        functools.partial(_nsa_kernel, online=online),
        grid=(bsz, s // T),
        in_specs=[pl.BlockSpec((1, T, NSA_HPG * LANES), lambda b, i: (b, i, 0)),
                  per_b(ks), per_b(vs), per_b(kw), per_b(vw), per_b(kc), per_b(vc),
                  _const_spec(oh.shape),
                  pl.BlockSpec((1, T, LANES), lambda b, i: (b, i, 0)),
                  _const_spec(bcf.shape), _const_spec(bs.shape), _const_spec(bw.shape),
                  _const_spec(c2s.shape), _const_spec(shift.shape)],
        out_specs=pl.BlockSpec((1, T, NSA_HPG * LANES), lambda b, i: (b, i, 0)),
        out_shape=jax.ShapeDtypeStruct((bsz, s, NSA_HPG * LANES), BF16),
        scratch_shapes=scratch,
        compiler_params=pltpu.CompilerParams(dimension_semantics=("parallel", "arbitrary"),
                                             vmem_limit_bytes=VMEM_LIMIT),
        name="nsa_online" if online else "nsa",
    )(q, ks, vs, kw, vw, kc, vc, oh, gates, bcf, bs, bw, c2s, shift)


ML_ROWS = 256


def _mlstm_kernel(q_ref, k_ref, v_ref, o_ref, gif_ref, gcol_ref, gb_ref, gbrow_ref, tril_ref, triu_ref,
                  out_ref, c_sc, m_sc):
    j = pl.program_id(1)
    R = ML_ROWS
    L = ML_CHUNK

    @pl.when(j == 0)
    def _():
        c_sc[...] = jnp.zeros_like(c_sc)
        m_sc[...] = jnp.zeros_like(m_sc)

    og = o_ref[0]

    pre = gif_ref[...] + gb_ref[...]
    row8 = lax.broadcasted_iota(jnp.int32, (8, R), 0)
    logf = jnp.minimum(pre, 0.0) - jnp.log(1.0 + jnp.exp(-jnp.abs(pre)))
    g8 = jnp.where(row8 < ML_HEADS, pre, logf)
    pre_c = gcol_ref[0] + gbrow_ref[...]
    lane_c = lax.broadcasted_iota(jnp.int32, pre_c.shape, 1)
    logf_c = jnp.minimum(pre_c, 0.0) - jnp.log(1.0 + jnp.exp(-jnp.abs(pre_c)))
    g_c = jnp.where(lane_c < _IF_LANE + ML_HEADS, pre_c, logf_c)
    cum_c = _split3(g_c, lambda part: _dot(tril_ref[...], part))
    cum_row = _split3(g8, lambda part: _dot(part, triu_ref[...]))

    a_io = lax.broadcasted_iota(jnp.int32, (L, L), 0)
    b_io = lax.broadcasted_iota(jnp.int32, (L, L), 1)
    causal = (b_io <= a_io)[None]
    nc = R // L
    pairs = [(c, h) for c in range(nc) for h in range(ML_HEADS)]

    def blocks(a):
        return jnp.stack([a[c * L:(c + 1) * L, h * ML_D:(h + 1) * ML_D] for c, h in pairs])

    def cols(a, k0):
        return jnp.stack([jnp.broadcast_to(a[c * L:(c + 1) * L, k0 + h:k0 + h + 1], (L, LANES)) for c, h in pairs])

    def rows(a, k0):
        return jnp.stack([a[k0 + h:k0 + h + 1, c * L:(c + 1) * L] for c, h in pairs])

    bdot = lambda eq, x, y: jnp.einsum(eq, x, y, preferred_element_type=F32)
    qb, kb, vb = blocks(q_ref[0]), blocks(k_ref[0]), blocks(v_ref[0])
    b_col, li_col = cols(cum_c, _IF_LANE + ML_HEADS), cols(g_c, _IF_LANE)
    b_row, li_row = rows(cum_row, ML_HEADS), rows(g8, 0)
    gsum = b_row[:, :, L - 1:L]

    s_max = jnp.max(gsum - b_row + li_row, axis=2, keepdims=True)
    m_run = m_sc[:, 0:1, 0:1]
    m_ins, m_outs = [], []
    for c in range(nc):
        hs = slice(c * ML_HEADS, (c + 1) * ML_HEADS)
        m_ins.append(m_run)
        m_run = jnp.maximum(gsum[hs] + m_run, s_max[hs])
        m_outs.append(m_run)
    m_sc[...] = jnp.broadcast_to(m_run, m_sc.shape)
    m_in, m_out = jnp.concatenate(m_ins, axis=0), jnp.concatenate(m_outs, axis=0)

    log_d = jnp.where(causal, b_col[:, :, :L] - b_row + li_row, -jnp.inf)
    inter = b_col + m_in
    m_row = jnp.maximum(inter, jnp.broadcast_to(jnp.max(log_d, axis=2, keepdims=True), inter.shape))
    w = bdot('bik,bjk->bij', qb, kb) * jnp.exp(log_d - m_row[:, :, :L])
    v_aug = jnp.concatenate([vb, jnp.ones(vb.shape, BF16)], axis=2)
    wv = bdot('bij,bjd->bid', w.astype(BF16), v_aug)
    inter_scale = jnp.exp(inter - m_row)
    inter_scale = jnp.concatenate([inter_scale, inter_scale], axis=2)
    floor = jnp.exp(-m_row)
    k_src = (kb.astype(F32) * jnp.exp(gsum - b_col + li_col - m_out)).astype(BF16)
    upd = bdot('bjk,bjd->bkd', k_src, v_aug)
    decay = jnp.exp(gsum + m_in - m_out)

    c_aug = c_sc[...]
    for c in range(nc):
        hs = slice(c * ML_HEADS, (c + 1) * ML_HEADS)
        nd = inter_scale[hs] * bdot('hik,hkd->hid', qb[hs], c_aug.astype(BF16)) + wv[hs]
        hval = nd[:, :, :ML_D] / jnp.maximum(jnp.abs(nd[:, :, ML_D:]), floor[hs])
        for h in range(ML_HEADS):
            rs, cs = slice(c * L, (c + 1) * L), slice(h * ML_D, (h + 1) * ML_D)
            out_ref[0, rs, cs] = (og[rs, cs].astype(F32) * hval[h]).astype(out_ref.dtype)
        c_aug = decay[hs] * c_aug + upd[hs]
    c_sc[...] = c_aug


def _mlstm(mq, mk, mv, og, gif, gcol, gate_b, gate_b_row):
    bsz, s, width = mq.shape
    R = ML_ROWS
    nblk = s // R
    seq = lambda: pl.BlockSpec((1, R, width), lambda b, j: (b, j, 0))
    pos = np.arange(R)
    same_chunk = (pos[:, None] // ML_CHUNK) == (pos[None, :] // ML_CHUNK)
    tril = (same_chunk & (pos[None, :] <= pos[:, None])).astype(np.float32)
    return pl.pallas_call(
        _mlstm_kernel,
        grid=(bsz, nblk),
        in_specs=[seq(), seq(), seq(), seq(),
                  pl.BlockSpec((8, R), lambda b, j: (0, b * nblk + j)),
                  pl.BlockSpec((1, R, LANES), lambda b, j: (b, j, 0)),
                  _const_spec(gate_b.shape), _const_spec(gate_b_row.shape),
                  _const_spec((R, R)), _const_spec((R, R))],
        out_specs=seq(),
        out_shape=jax.ShapeDtypeStruct((bsz, s, width), BF16),
        scratch_shapes=[pltpu.VMEM((ML_HEADS, ML_D, 2 * ML_D), F32),
                        pltpu.VMEM((ML_HEADS, 8, LANES), F32)],
        compiler_params=pltpu.CompilerParams(dimension_semantics=("parallel", "arbitrary"),
                                             vmem_limit_bytes=VMEM_LIMIT),
        name="mlstm",
    )(mq, mk, mv, og, gif, gcol, gate_b, gate_b_row, jnp.asarray(tril, BF16), jnp.asarray(tril.T, BF16))


FF_CHUNK = 512


def _merge_ffn_kernel(x_ref, ya_ref, yb_ref, mg_ref, wa_ref, wb_ref, wo_ref, g2_ref, w1_ref, w2_ref,
                      out_ref):
    mg = mg_ref[...]
    mixed = (mg[:, :D_MODEL].astype(F32) * _dot(ya_ref[...], wa_ref[...])
             + mg[:, D_MODEL:].astype(F32) * _dot(yb_ref[...], wb_ref[...]))
    x1 = x_ref[...] + _dot(mixed.astype(BF16), wo_ref[...])
    h2 = x1 * lax.rsqrt(jnp.mean(x1 * x1, axis=-1, keepdims=True) + RMS_EPS) * g2_ref[...]
    h2 = h2.astype(BF16)
    acc = x1
    for c in range(D_FF // FF_CHUNK):
        a = jnp.maximum(_dot(h2, w1_ref[:, c * FF_CHUNK:(c + 1) * FF_CHUNK]), 0.0)
        acc = acc + _dot((a * a).astype(BF16), w2_ref[c * FF_CHUNK:(c + 1) * FF_CHUNK, :])
    out_ref[...] = acc


def _merge_ffn(x2d, ya, yb, mg, wa, wb, wo, g2, w1, w2, tm=512):
    n = x2d.shape[0]
    row = lambda w: pl.BlockSpec((tm, w), lambda i: (i, 0))
    return pl.pallas_call(
        _merge_ffn_kernel,
        grid=(n // tm,),
        in_specs=[row(D_MODEL), row(512), row(512), row(2048),
                  _const_spec(wa.shape), _const_spec(wb.shape), _const_spec(wo.shape),
                  _const_spec(g2.shape), _const_spec(w1.shape), _const_spec(w2.shape)],
        out_specs=row(D_MODEL),
        out_shape=jax.ShapeDtypeStruct((n, D_MODEL), F32),
        compiler_params=pltpu.CompilerParams(dimension_semantics=("parallel",),
                                             vmem_limit_bytes=VMEM_LIMIT),
        name="merge_ffn",
    )(x2d, ya, yb, mg, wa, wb, wo, g2, w1, w2)


def _proj_weights(w):
    widths = (512, 128, 128, 128, 128, 128, 128, 24, 512, 512, 512, 4, 4, 512, 2048)
    off = np.concatenate([[0], np.cumsum(widths)])
    (nq, nkc, nvc, nks, nvs, nkw, nvw, ngate, mq, mk, mv, mi, mf, mo, mgate) = (int(o) for o in off[:-1])
    col = lambda start, width: w[:, start:start + width]
    parts = [col(nq + NSA_D * (g * NSA_HPG + j), NSA_D) for j in range(NSA_HPG) for g in range(NSA_GROUPS)]
    parts += [col(nks, 128), col(nkw, 128), col(nkc, 128), col(nvc, 128), col(nvs, 128), col(nvw, 128)]
    gate = col(ngate, 24).reshape(-1, NSA_HEADS, 3).transpose(0, 2, 1).reshape(-1, 24)
    parts += [gate, col(mi, 4), col(mf, 4), jnp.zeros((w.shape[0], LANES - _IF_LANE - 8), w.dtype)]
    parts += [col(mq, 512), col(mk, 512), col(mv, 512), col(mo, 512), col(mgate, 2048)]
    w_all = jnp.concatenate(parts, axis=1).astype(BF16)
    w_if = jnp.concatenate([col(mi, 4), col(mf, 4)], axis=1).T.astype(BF16)
    return w_all, w_if


def _branch_a_weights(w):
    head = lambda h: w[NSA_D * h:NSA_D * (h + 1)]
    return jnp.concatenate([head(g * NSA_HPG + j) for j in range(NSA_HPG) for g in range(NSA_GROUPS)],
                           axis=0).astype(BF16)


def _compress_weights(pos, w1):
    r = w1.reshape(2, CMP_STRIDE, 1, NSA_D, CMP_HIDDEN)
    z = jnp.zeros_like(r)
    w1x = jnp.stack([jnp.concatenate([r, z], axis=2), jnp.concatenate([z, r], axis=2)])
    w1x = w1x.reshape(NSA_GROUPS, 2, CMP_STRIDE * NSA_GROUPS * NSA_D, CMP_HIDDEN).astype(BF16)
    posx = jnp.broadcast_to(pos.reshape(2, CMP_STRIDE, 1, NSA_D), (2, CMP_STRIDE, NSA_GROUPS, NSA_D))
    return w1x, posx.reshape(2, CMP_STRIDE * NSA_GROUPS * NSA_D).astype(BF16)


def _t5_bucket(dist):
    n = jnp.maximum(dist, 0)
    max_exact = REL_BUCKETS // 2
    nf = jnp.maximum(n, 1).astype(F32)
    large = max_exact + (jnp.log(nf / max_exact) / math.log(REL_MAX_DIST / max_exact)
                         * (REL_BUCKETS - max_exact)).astype(jnp.int32)
    return jnp.where(n < max_exact, n, jnp.minimum(large, REL_BUCKETS - 1))


def _toeplitz(tab, n_rows, n_cols, off):
    nd = tab.shape[1]
    ext = lambda lo, hi: jnp.take(tab, jnp.asarray(np.clip(np.arange(lo, hi), 0, nd - 1)), axis=1)
    w = jnp.concatenate([ext(off - n_cols + 1, off + 1)[:, ::-1], jnp.zeros((tab.shape[0], 1), tab.dtype),
                         ext(off + 1, off + n_rows)[:, ::-1]], axis=1)
    lw = n_rows + n_cols
    flat = jnp.tile(w, (1, n_rows))[:, :n_rows * (lw - 1)]
    return flat.reshape(tab.shape[0], n_rows, lw - 1)[:, :, :n_cols]


def _bias_tables(rel_table, shifts):
    T = NSA_TILE
    sh_c, sh_s, sh_w = shifts
    tab = (rel_table[_t5_bucket(jnp.arange(WIN_KEYS))] - rel_table[REL_BUCKETS - 1][None, :]).T * LOG2E
    r = np.arange(T)[:, None]

    c = np.arange(NEAR_KEYS)[None, :]
    ok = jnp.asarray((r - c + SLC_PAD) >= 0)[None]
    bs = jnp.where(ok, _toeplitz(tab, T, NEAR_KEYS, SLC_PAD) - sh_s, MASK_F32)
    c = np.arange(WIN_KEYS)[None, :]
    d = r - c + WINDOW
    ok = jnp.asarray((d >= 0) & (d < WINDOW))[None]
    bw = jnp.where(ok, _toeplitz(tab, T, WIN_KEYS, WINDOW) - sh_w, MASK_F32)
    m = np.arange(64)[None, :]
    d = r - CMP_STRIDE * m + BAND_DIST0
    band = (m >= 1) & (m < BAND_SLOTS - 1)
    vals = _toeplitz(tab, T, CMP_STRIDE * BAND_SLOTS, BAND_DIST0)[:, :, ::CMP_STRIDE]
    vals = jnp.pad(vals, ((0, 0), (0, 0), (0, 64 - BAND_SLOTS)))
    vals = jnp.where(jnp.asarray(band & (d >= 0))[None], vals, 0.0) - sh_c
    dead = (band & (d < 0)) | (m == BAND_SLOTS - 1)
    vals = jnp.where(jnp.asarray(dead)[None], MASK_BIG, vals)
    hi = vals.astype(BF16)
    lo = (vals - hi.astype(F32)).astype(BF16)
    bcf = jnp.concatenate([hi, lo], axis=-1)
    flat = lambda a: a.reshape(NSA_HEADS * T, a.shape[-1])
    return flat(bcf), flat(bs).astype(F32), flat(bw).astype(F32), jnp.max(tab)


def _layer(l, x2d, bsz, s, consts, rel_table, norm1_g, w_in, nsa_q_gain, nsa_k_gain, cmp_k, cmp_v,
           ml_conv_w, ml_conv_b, ml_i_bias, ml_f_bias, w_branch_a, w_branch_b, w_out, norm2_g, w_ff1, w_ff2):
    n = bsz * s
    nseg = s // CMP_STRIDE
    oh, c2s = consts
    w_all, w_if = _proj_weights(w_in[l])
    qg_pad = jnp.concatenate([nsa_q_gain[l]] * 2)[None, :]
    kg_pad = jnp.stack([jnp.concatenate([nsa_k_gain[l, 1]] * 2), jnp.concatenate([nsa_k_gain[l, 2]] * 2)])
    (q, ks, kw, kc, vc, vs, vw, gates, gcol, mq, mk, mv, og, mg, gif) = _proj(
        x2d, s, norm1_g[l][None, :], w_all, w_if, qg_pad, kg_pad, ml_conv_w[l], ml_conv_b[l][None, :])

    def compress(a, params, gain, normalize):
        pos, w1, b1, w2, b2 = (p[l] for p in params)
        w1x, posx = _compress_weights(pos, w1)
        return _compress(a.reshape(bsz, nseg, CMP_STRIDE * LANES), w1x, posx, b1[None, :],
                         w2.astype(BF16), b2[None, :], gain[None, :], normalize)

    kcmp = compress(kc, cmp_k, nsa_k_gain[l, 0], True)
    vcmp = compress(vc, cmp_v, jnp.ones((NSA_D,), F32), False)

    qk_bound = lambda kg: 8.0 * LOG2E * jnp.max(jnp.abs(nsa_q_gain[l])) * jnp.max(jnp.abs(kg))
    _, _, _, tab_max = _bias_tables(rel_table, (0.0, 0.0, 0.0))
    snap = lambda v: v.astype(BF16).astype(F32)
    shifts = tuple(snap(qk_bound(nsa_k_gain[l, j]) + jnp.maximum(tab_max, 0.0)) for j in range(3))
    safe = 2.0 * jnp.max(jnp.stack(shifts)) < SAFE_SHIFT_LOG2

    seq = lambda a: a.reshape(bsz, s, a.shape[-1])
    front = lambda a, p: jnp.pad(seq(a), ((0, 0), (p, 0), (0, 0)))
    operands = (seq(q), front(ks, SLC_PAD), front(vs, SLC_PAD), front(kw, WINDOW), front(vw, WINDOW),
                kcmp, vcmp, oh, seq(gates))

    def run(online):
        sh = (0.0, 0.0, 0.0) if online else shifts
        bcf, bs, bw, _ = _bias_tables(rel_table, sh)
        shift = jnp.full((1, LANES), -sh[1], F32)
        return _nsa(online, *operands, bcf, bs, bw, c2s, shift)

    y_a = lax.cond(safe, lambda: run(False), lambda: run(True))

    gate_b = jnp.concatenate([ml_i_bias[l], ml_f_bias[l]])
    gate_b_row = jnp.pad(gate_b, (_IF_LANE, LANES - _IF_LANE - 8))[None, :]
    y_b = _mlstm(seq(mq), seq(mk), seq(mv), seq(og), gif, seq(gcol), gate_b[:, None], gate_b_row)

    out = _merge_ffn(x2d, y_a.reshape(n, 512), y_b.reshape(n, 512), mg, _branch_a_weights(w_branch_a[l]),
                     w_branch_b[l].astype(BF16), w_out[l].astype(BF16), norm2_g[l][None, :],
                     w_ff1[l].astype(BF16), w_ff2[l].astype(BF16))
    return out, y_a, y_b


def _consts(s):
    nseg = s // CMP_STRIDE
    nsel = s // SLC_BLOCK
    blk_of_key = np.arange(s) // SLC_BLOCK
    oh = np.concatenate([np.ones((SLC_PAD, LANES), np.float32),
                         (blk_of_key[:, None] == np.arange(LANES)[None, :]).astype(np.float32)], axis=0)
    ci = np.arange(nseg)[:, None] * CMP_STRIDE
    sj = np.arange(LANES)[None, :] * SLC_BLOCK
    c2s = ((ci < sj + SLC_BLOCK) & (ci + CMP_LEN > sj) & (np.arange(LANES)[None, :] < nsel)
           & (np.arange(nseg)[:, None] < nseg - 1))
    return jnp.asarray(oh, BF16), jnp.asarray(c2s.astype(np.float32), BF16)


def kernel(x, norm1_g, w_in, nsa_q_gain, nsa_k_gain, cmp_k_pos, cmp_k_w1, cmp_k_b1, cmp_k_w2, cmp_k_b2, cmp_v_pos, cmp_v_w1, cmp_v_b1, cmp_v_w2, cmp_v_b2, rel_table, ml_conv_w, ml_conv_b, ml_i_bias, ml_f_bias, w_branch_a, w_branch_b, w_out, norm2_g, w_ff1, w_ff2):
    bsz, s, _ = x.shape
    consts = _consts(s)
    x2d = x.reshape(bsz * s, D_MODEL)
    for l in range(norm1_g.shape[0]):
        x2d, _, _ = _layer(l, x2d, bsz, s, consts, rel_table, norm1_g, w_in, nsa_q_gain, nsa_k_gain,
                           (cmp_k_pos, cmp_k_w1, cmp_k_b1, cmp_k_w2, cmp_k_b2),
                           (cmp_v_pos, cmp_v_w1, cmp_v_b1, cmp_v_w2, cmp_v_b2),
                           ml_conv_w, ml_conv_b, ml_i_bias, ml_f_bias,
                           w_branch_a, w_branch_b, w_out, norm2_g, w_ff1, w_ff2)
    return x2d.reshape(bsz, s, D_MODEL)
```

```python
import functools
import math

import numpy as np
import jax
import jax.numpy as jnp
from jax import lax
from jax.experimental import pallas as pl
from jax.experimental.pallas import tpu as pltpu

F32 = jnp.float32
BF16 = jnp.bfloat16

D_MODEL = 1024
NSA_HEADS = 8
NSA_GROUPS = 2
NSA_HPG = NSA_HEADS // NSA_GROUPS
NSA_D = 64
CMP_LEN = 32
CMP_STRIDE = 16
CMP_HIDDEN = 256
SLC_BLOCK = 64
SLC_TOPK = 16
WINDOW = 512
FORCED_SCORE = 1e4
ML_HEADS = 4
ML_D = 128
ML_CHUNK = 64
CONV_WIDTH = 4
D_FF = 4 * D_MODEL
REL_BUCKETS = 32
REL_MAX_DIST = 128
RMS_EPS = 1e-6

LANES = 128
NSA_TILE = 256
SEL_TILE = 512
SLC_PAD = 128
FAR_CHUNK = 512
NEAR_KEYS = SLC_PAD + NSA_TILE
WIN_KEYS = WINDOW + NSA_TILE
BAND_OFF = 10
BAND_SLOTS = 2 + (SEL_TILE + 128) // CMP_STRIDE
BAND_DIST0 = CMP_STRIDE * BAND_OFF - (CMP_LEN - 1)
MASK_BIG = -1e9
MASK_F32 = -1e30
LOG2E = math.log2(math.e)
SAFE_SHIFT_LOG2 = 50.0
VMEM_LIMIT = 56 * 1024 * 1024

_Q_OFF, _KS_OFF, _KW_OFF, _KC_OFF, _VC_OFF, _VS_OFF, _VW_OFF = 0, 512, 640, 768, 896, 1024, 1152
_GATE_OFF, _MQ_OFF, _MK_OFF, _MV_OFF, _MO_OFF, _MG_OFF, _W_COLS = 1280, 1408, 1920, 2432, 2944, 3456, 5504
_IF_LANE = 24


def _nt(a, b, precision=None):
    return lax.dot_general(a, b, (((1,), (1,)), ((), ())), precision=precision,
                           preferred_element_type=F32)


def _dot(a, b, precision=None):
    return jnp.dot(a, b, precision=precision, preferred_element_type=F32)


def _split3(x, dot_part):
    hi = x.astype(BF16)
    r1 = x - hi.astype(F32)
    mid = r1.astype(BF16)
    lo = (r1 - mid.astype(F32)).astype(BF16)
    return dot_part(hi) + dot_part(mid) + dot_part(lo)


def _const_spec(shape):
    nd = len(shape)
    return pl.BlockSpec(shape, lambda *_: (0,) * nd, pipeline_mode=pl.Buffered(1))


def _proj_kernel(x_ref, g1_ref, w_ref, wif_ref, qg_ref, kg_ref, cw_ref, cb_ref,
                 q_ref, ks_ref, kw_ref, kc_ref, vc_ref, vs_ref, vw_ref, gate_ref, gcol_ref,
                 mq_ref, mk_ref, mv_ref, mo_ref, mg_ref, gif_ref, ext_sc, *, tiles_per_seq):
    tm = x_ref.shape[0]
    x = x_ref[...]
    h = x * lax.rsqrt(jnp.mean(x * x, axis=-1, keepdims=True) + RMS_EPS) * g1_ref[...]
    hb = h.astype(BF16)

    def proj(off, width):
        return _dot(hb, w_ref[:, off:off + width])

    lane = lax.broadcasted_iota(jnp.int32, (1, LANES), 1)
    low = lane < NSA_D

    def half_norm(blk):
        sq = blk * blk
        ms0 = jnp.sum(jnp.where(low, sq, 0.0), axis=-1, keepdims=True) * (1.0 / NSA_D)
        ms1 = jnp.sum(jnp.where(low, 0.0, sq), axis=-1, keepdims=True) * (1.0 / NSA_D)
        return blk * jnp.where(low, lax.rsqrt(ms0 + RMS_EPS), lax.rsqrt(ms1 + RMS_EPS))

    for pair in range(NSA_HPG // 2):
        both = proj(_Q_OFF + 2 * LANES * pair, 2 * LANES)
        for hh in range(2):
            qn = half_norm(both[:, LANES * hh:LANES * (hh + 1)]) * qg_ref[...] * (NSA_D ** -0.5 * LOG2E)
            col = LANES * (2 * pair + hh)
            q_ref[:, col:col + LANES] = qn.astype(q_ref.dtype)

    both = proj(_KS_OFF, 2 * LANES)
    for hh, ref in enumerate((ks_ref, kw_ref)):
        ref[...] = (half_norm(both[:, LANES * hh:LANES * (hh + 1)]) * kg_ref[hh:hh + 1, :]).astype(ref.dtype)

    for off, refs in ((_KC_OFF, (kc_ref, vc_ref)), (_VS_OFF, (vs_ref, vw_ref))):
        both = proj(off, 2 * LANES)
        for hh, ref in enumerate(refs):
            ref[...] = both[:, LANES * hh:LANES * (hh + 1)].astype(ref.dtype)

    slab = proj(_GATE_OFF, LANES)
    gate_ref[...] = jax.nn.sigmoid(slab)
    gcol_ref[...] = slab

    @pl.when(pl.program_id(0) % tiles_per_seq == 0)
    def _():
        ext_sc[0:8, :] = jnp.zeros((8, ext_sc.shape[1]), F32)

    ext_sc[8:, 0:512] = proj(_MQ_OFF, 512)
    ext_sc[8:, 512:1024] = proj(_MK_OFF, 512)
    conv = cb_ref[...]
    for t in range(CONV_WIDTH):
        lo = 8 - (CONV_WIDTH - 1) + t
        conv = conv + ext_sc[lo:lo + tm, :] * cw_ref[t:t + 1, :]
    ext_sc[0:8, :] = ext_sc[tm:tm + 8, :]
    qk = conv * jax.nn.sigmoid(conv)
    mq_ref[...] = (qk[:, :512] * (ML_D ** -0.5)).astype(mq_ref.dtype)
    mk_ref[...] = qk[:, 512:].astype(mk_ref.dtype)

    mv_ref[...] = proj(_MV_OFF, 512).astype(mv_ref.dtype)
    mo_ref[...] = jax.nn.sigmoid(proj(_MO_OFF, 512)).astype(mo_ref.dtype)
    for c in range(4):
        mg_ref[:, 512 * c:512 * (c + 1)] = jax.nn.sigmoid(proj(_MG_OFF + 512 * c, 512)).astype(mg_ref.dtype)
    gif_ref[...] = _nt(wif_ref[...], hb)


def _proj(x2d, seq_len, g1, w_all, w_if, qg_pad, kg_pad, conv_w, conv_b, tm=512):
    n = x2d.shape[0]
    row = lambda w: pl.BlockSpec((tm, w), lambda i: (i, 0))
    out_shapes = (
        jax.ShapeDtypeStruct((n, 512), BF16),
        jax.ShapeDtypeStruct((n, LANES), BF16),
        jax.ShapeDtypeStruct((n, LANES), BF16),
        jax.ShapeDtypeStruct((n, LANES), BF16),
        jax.ShapeDtypeStruct((n, LANES), BF16),
        jax.ShapeDtypeStruct((n, LANES), BF16),
        jax.ShapeDtypeStruct((n, LANES), BF16),
        jax.ShapeDtypeStruct((n, LANES), F32),
        jax.ShapeDtypeStruct((n, LANES), F32),
        jax.ShapeDtypeStruct((n, 512), BF16),
        jax.ShapeDtypeStruct((n, 512), BF16),
        jax.ShapeDtypeStruct((n, 512), BF16),
        jax.ShapeDtypeStruct((n, 512), BF16),
        jax.ShapeDtypeStruct((n, 2048), BF16),
        jax.ShapeDtypeStruct((8, n), F32),
    )
    out_specs = (row(512),) + (row(LANES),) * 8 + (row(512),) * 4 + (row(2048),
                 pl.BlockSpec((8, tm), lambda i: (0, i)))
    return pl.pallas_call(
        functools.partial(_proj_kernel, tiles_per_seq=seq_len // tm),
        grid=(n // tm,),
        in_specs=[row(D_MODEL), _const_spec((1, D_MODEL)), _const_spec((D_MODEL, _W_COLS)),
                  _const_spec((8, D_MODEL)), _const_spec((1, LANES)), _const_spec((2, LANES)),
                  _const_spec(conv_w.shape), _const_spec(conv_b.shape)],
        out_specs=out_specs,
        out_shape=out_shapes,
        scratch_shapes=[pltpu.VMEM((tm + 8, 1024), F32)],
        compiler_params=pltpu.CompilerParams(dimension_semantics=("arbitrary",),
                                             vmem_limit_bytes=VMEM_LIMIT),
        name="proj",
    )(x2d, g1, w_all, w_if, qg_pad, kg_pad, conv_w, conv_b)


def _compress_kernel(seg_ref, w1_ref, pos_ref, b1_ref, w2_ref, b2_ref, gain_ref, out_ref, *, normalize):
    seg = seg_ref[0]
    nseg, width = seg.shape
    c = math.sqrt(2.0 / math.pi)
    pos_lo = jnp.broadcast_to(pos_ref[0:1, :], (8, width))
    pos_hi = jnp.broadcast_to(pos_ref[1:2, :], (8, width))
    outs = []
    for g in range(NSA_GROUPS):
        a = _dot(seg, w1_ref[g, 0])
        b = _dot(seg, w1_ref[g, 1])
        posb = (_dot(pos_lo, w1_ref[g, 0]) + _dot(pos_hi, w1_ref[g, 1]))[0:1] + b1_ref[...]
        pre = a + pltpu.roll(b, nseg - 1, 0) + posb
        hid = 0.5 * pre * (1.0 + jnp.tanh(c * (pre + 0.044715 * (pre * pre * pre))))
        out = _dot(hid.astype(BF16), w2_ref[...]) + b2_ref[...]
        if normalize:
            ms = jnp.mean(out * out, axis=-1, keepdims=True)
            out = out * lax.rsqrt(ms + RMS_EPS) * gain_ref[...]
        outs.append(out)
    out_ref[0] = jnp.concatenate(outs, axis=1).astype(out_ref.dtype)


def _compress(segs, w1x, posx, b1, w2, b2, gain, normalize):
    bsz, nseg, width = segs.shape
    return pl.pallas_call(
        functools.partial(_compress_kernel, normalize=normalize),
        grid=(bsz,),
        in_specs=[pl.BlockSpec((1, nseg, width), lambda b: (b, 0, 0)),
                  _const_spec(w1x.shape), _const_spec(posx.shape), _const_spec(b1.shape),
                  _const_spec(w2.shape), _const_spec(b2.shape), _const_spec(gain.shape)],
        out_specs=pl.BlockSpec((1, nseg, LANES), lambda b: (b, 0, 0)),
        out_shape=jax.ShapeDtypeStruct((bsz, nseg, LANES), BF16),
        compiler_params=pltpu.CompilerParams(dimension_semantics=("parallel",),
                                             vmem_limit_bytes=VMEM_LIMIT),
        name="compress",
    )(segs, w1x, posx, b1, w2, b2, gain)


def _stack_heads(q_ref, lane_q):
    q = q_ref[0]
    return jnp.concatenate(
        [jnp.where((lane_q >= NSA_D) == (g == 1), q[:, LANES * j:LANES * (j + 1)], jnp.zeros((), BF16))
         for g in range(NSA_GROUPS) for j in range(NSA_HPG)], axis=0)


def _softmax_pv(s, v, online):
    if online:
        m = jnp.max(s, axis=1, keepdims=True)
        s = s - jnp.where(m < 0.1 * MASK_BIG, 0.0, m)
    p = jnp.exp2(s)
    acc = _dot(p.astype(BF16), jnp.concatenate([v, jnp.ones(v.shape, BF16)], axis=1))
    return p, acc[:, :LANES], acc[:, LANES:]


def _nsa_select_kernel(q_ref, kc_ref, vc_ref, gate_ref, bcf_ref, c2s_ref, shift_ref,
                       oc_ref, sbn_ref, sbf_ref, *, online):
    i = pl.program_id(1)
    T = SEL_TILE
    grows = NSA_HPG * T
    ncmp = kc_ref.shape[1]
    nblk = LANES
    lane_q = lax.broadcasted_iota(jnp.int32, (T, LANES), 1)
    qpad = _stack_heads(q_ref, lane_q)

    n_io = lax.broadcasted_iota(jnp.int32, (ncmp, LANES), 0)
    l_io = lax.broadcasted_iota(jnp.int32, (ncmp, LANES), 1)
    band = jnp.clip(n_io - i * (T // CMP_STRIDE) + BAND_OFF, 0, BAND_SLOTS - 1)
    oh_c = jnp.where(band == (l_io & 63), 1.0, 0.0).astype(BF16)
    kc_aug = jnp.concatenate([kc_ref[0], oh_c], axis=1)
    qc_aug = jnp.concatenate([qpad, bcf_ref[...]], axis=1)
    p_c, num_c, l_c = _softmax_pv(_nt(qc_aug, kc_aug), vc_ref[0], online)
    inv_c = 1.0 / jnp.maximum(l_c, 1e-30)
    o_c = num_c * inv_c
    p_c = p_c * jnp.concatenate([inv_c] * (ncmp // LANES), axis=1)

    gates = gate_ref[0]
    gated = [jnp.broadcast_to(gates[:, h:h + 1], (T, LANES)) * o_c[h * T:(h + 1) * T] for h in range(NSA_HEADS)]
    for j in range(NSA_HPG):
        oc_ref[0, :, LANES * j:LANES * (j + 1)] = jnp.where(
            lane_q < NSA_D, gated[j], gated[NSA_HPG + j]).astype(oc_ref.dtype)

    imp_t = []
    for g in range(NSA_GROUPS):
        ps = p_c[g * grows:g * grows + T]
        for hh in range(1, NSA_HPG):
            ps = ps + p_c[g * grows + hh * T:g * grows + (hh + 1) * T]
        ps_hi = ps.astype(BF16)
        ps_lo = (ps - ps_hi.astype(F32)).astype(BF16)
        imp_t.append((_dot(ps_hi, c2s_ref[...]) + _dot(ps_lo, c2s_ref[...])).T)
    width = NSA_GROUPS * T
    j_io = lax.broadcasted_iota(jnp.int32, (nblk, width), 0)
    r_io = lax.broadcasted_iota(jnp.int32, (nblk, width), 1) & (T - 1)
    j_f = j_io.astype(F32)
    cur = i * (T // SLC_BLOCK) + jnp.right_shift(r_io, SLC_BLOCK.bit_length() - 1)
    forced = (j_io == 0) | (j_io == cur) | (j_io == cur - 1)
    score = jnp.where(forced, -jnp.inf, jnp.where(j_io <= cur, jnp.concatenate(imp_t, axis=1), -1.0))
    for _ in range(SLC_TOPK - 3):
        best = jnp.max(score, axis=0, keepdims=True)
        first = jnp.min(jnp.where(score == best, j_f, float(nblk)), axis=0, keepdims=True)
        score = jnp.where(j_f == first, -jnp.inf, score)
    tok = i * T + lax.broadcasted_iota(jnp.int32, (T, nblk), 0)
    near_start = (tok // NSA_TILE) * (NSA_TILE // SLC_BLOCK) - SLC_PAD // SLC_BLOCK
    near_blk = lax.broadcasted_iota(jnp.int32, (T, nblk), 1) >= near_start
    far_pick = shift_ref[0:1, 0:1]
    for g in range(NSA_GROUPS):
        picked = score[:, g * T:(g + 1) * T].T == -jnp.inf
        sbn_ref[0, :, LANES * g:LANES * (g + 1)] = jnp.where(picked & near_blk, 0.0, MASK_BIG).astype(BF16)
        sbf_ref[0, :, LANES * g:LANES * (g + 1)] = jnp.where(
            picked & jnp.logical_not(near_blk), far_pick, MASK_BIG).astype(BF16)


def _nsa_attend_kernel(q_ref, ks_ref, vs_ref, kw_ref, vw_ref, oh_ref, gate_ref, oc_ref, sbn_ref, sbf_ref,
                       bs_ref, bw_ref, out_ref, acc_sc, *extra_sc, online):
    i = pl.program_id(1)
    t0 = pl.multiple_of(i * NSA_TILE, NSA_TILE)
    T = NSA_TILE
    rows = NSA_HEADS * T
    grows = NSA_HPG * T
    lane_q = lax.broadcasted_iota(jnp.int32, (T, LANES), 1)
    qpad = _stack_heads(q_ref, lane_q)

    def with_features(feat_ref):
        feats = [jnp.concatenate([feat_ref[0, :, LANES * g:LANES * (g + 1)]] * NSA_HPG, axis=0)
                 for g in range(NSA_GROUPS)]
        return jnp.concatenate([qpad, jnp.concatenate(feats, axis=0)], axis=1)

    q_near, q_far = with_features(sbn_ref), with_features(sbf_ref)

    r_w = lax.broadcasted_iota(jnp.int32, (WIN_KEYS, LANES), 0)
    pad_flag = jnp.where(r_w + (t0 - WINDOW) < 0, 1.0, 0.0).astype(BF16)
    kw_aug = jnp.concatenate([kw_ref[0, pl.ds(t0, WIN_KEYS), :], pad_flag], axis=1)
    qw_aug = jnp.concatenate([qpad, jnp.full((rows, LANES), MASK_BIG / LANES, BF16)], axis=1)
    s_w = _nt(qw_aug, kw_aug) + bw_ref[...]
    _, num_w, l_w = _softmax_pv(s_w, vw_ref[0, pl.ds(t0, WIN_KEYS), :], online)
    o_w = num_w * (1.0 / l_w)

    gates = gate_ref[0]
    gate_tile = lambda col: jnp.broadcast_to(gates[:, col:col + 1], (T, LANES))
    head_rows = [slice(h * T, (h + 1) * T) for h in range(NSA_HEADS)]
    o_gw = [gate_tile(16 + h) * o_w[head_rows[h]] for h in range(NSA_HEADS)]
    g_slc = [gate_tile(8 + h) for h in range(NSA_HEADS)]

    ones_f = jnp.ones((FAR_CHUNK, LANES), BF16)
    k_near = jnp.concatenate([ks_ref[0, pl.ds(t0, NEAR_KEYS), :], oh_ref[pl.ds(t0, NEAR_KEYS), :]], axis=1)
    v_near = jnp.concatenate([vs_ref[0, pl.ds(t0, NEAR_KEYS), :], jnp.ones((NEAR_KEYS, LANES), BF16)], axis=1)
    s_n = _nt(q_near, k_near) + bs_ref[...]
    if online:
        m_sc, = extra_sc
        m_n = jnp.max(s_n, axis=1, keepdims=True)
        m_sc[...] = m_n
        s_n = s_n - m_n
    acc_sc[...] = _dot(jnp.exp2(s_n).astype(BF16), v_near)

    last_chunk = (ks_ref.shape[1] - SLC_PAD) // FAR_CHUNK - 1

    def far_start(c):
        return pl.multiple_of(SLC_PAD + FAR_CHUNK * jnp.minimum(c, last_chunk), LANES)

    def far_scores(c, by_group=False):
        start = far_start(c)
        k_f = jnp.concatenate([ks_ref[0, pl.ds(start, FAR_CHUNK), :], oh_ref[pl.ds(start, FAR_CHUNK), :]], axis=1)
        if by_group:
            return jnp.concatenate([_nt(q_far[:grows], k_f), _nt(q_far[grows:], k_f)], axis=0)
        return _nt(q_far, k_f)

    def far_values(c):
        return jnp.concatenate([vs_ref[0, pl.ds(far_start(c), FAR_CHUNK), :], ones_f], axis=1)

    n_far = (t0 + (FAR_CHUNK - 1 - SLC_PAD)) // FAR_CHUNK
    if online:
        def far_online(c, carry):
            s_f = far_scores(c)
            m_old = m_sc[...]
            m_new = jnp.maximum(m_old, jnp.max(s_f, axis=1, keepdims=True))
            m_sc[...] = m_new
            acc_sc[...] = (jnp.exp2(m_old - m_new) * acc_sc[...]
                           + _dot(jnp.exp2(s_f - m_new).astype(BF16), far_values(c)))
            return carry

        lax.fori_loop(0, n_far, far_online, 0)
    else:
        pa_sc, pb_sc = extra_sc
        pa_sc[...] = jnp.exp2(far_scores(0, by_group=True)).astype(BF16)

        def far_pair(cc, carry):
            c0 = 2 * cc
            pv0 = _dot(pa_sc[...], far_values(c0))
            pb_sc[...] = jnp.exp2(far_scores(c0 + 1)).astype(BF16)
            pv1 = _dot(pb_sc[...], far_values(c0 + 1))
            pa_sc[...] = jnp.exp2(far_scores(c0 + 2)).astype(BF16)
            acc_sc[...] += pv0 + pv1
            return carry

        lax.fori_loop(0, (n_far + 1) // 2, far_pair, 0)

    acc = acc_sc[...]
    o_s = acc[:, :LANES] * (1.0 / acc[:, LANES:])

    head_out = [o_gw[h] + g_slc[h] * o_s[head_rows[h]] for h in range(NSA_HEADS)]
    for j in range(NSA_HPG):
        cols = slice(LANES * j, LANES * (j + 1))
        out_ref[0, :, cols] = (oc_ref[0, :, cols].astype(F32) + jnp.where(
            lane_q < NSA_D, head_out[j], head_out[NSA_HPG + j])).astype(out_ref.dtype)


def _nsa(online, q, ks, vs, kw, vw, kc, vc, oh, gates, bcf, bs, bw, c2s, shift):
    bsz, s = q.shape[:2]
    T = NSA_TILE
    rows = NSA_HEADS * T
    grid = (bsz, s // T)
    params = pltpu.CompilerParams(dimension_semantics=("parallel", "arbitrary"), vmem_limit_bytes=VMEM_LIMIT)
    tile = lambda width: pl.BlockSpec((1, T, width), lambda b, i: (b, i, 0))
    per_b = lambda a: pl.BlockSpec((1,) + a.shape[1:], lambda b, i: (b, 0, 0), pipeline_mode=pl.Buffered(1))
    sel_tile = lambda width: pl.BlockSpec((1, SEL_TILE, width), lambda b, i: (b, i, 0))
    o_cmp, sb_near, sb_far = pl.pallas_call(
        functools.partial(_nsa_select_kernel, online=online),
        grid=(bsz, s // SEL_TILE),
        in_specs=[sel_tile(NSA_HPG * LANES), per_b(kc), per_b(vc), sel_tile(LANES),
                  _const_spec(bcf.shape), _const_spec(c2s.shape), _const_spec(shift.shape)],
        out_specs=(sel_tile(NSA_HPG * LANES), sel_tile(NSA_GROUPS * LANES), sel_tile(NSA_GROUPS * LANES)),
        out_shape=(jax.ShapeDtypeStruct((bsz, s, NSA_HPG * LANES), BF16),
                   jax.ShapeDtypeStruct((bsz, s, NSA_GROUPS * LANES), BF16),
                   jax.ShapeDtypeStruct((bsz, s, NSA_GROUPS * LANES), BF16)),
        compiler_params=params,
        name="nsa_select_online" if online else "nsa_select",
    )(q, kc, vc, gates, bcf, c2s, shift)
    scratch = [pltpu.VMEM((rows, 2 * LANES), F32)]
    if online:
        scratch.append(pltpu.VMEM((rows, 1), F32))
    else:
        scratch += [pltpu.VMEM((rows, FAR_CHUNK), BF16)] * 2
    return pl.pallas_call(
        functools.partial(_nsa_attend_kernel, online=online),
        grid=grid,
        in_specs=[tile(NSA_HPG * LANES), per_b(ks), per_b(vs), per_b(kw), per_b(vw), _const_spec(oh.shape),
                  tile(LANES), tile(NSA_HPG * LANES), tile(NSA_GROUPS * LANES), tile(NSA_GROUPS * LANES),
                  _const_spec(bs.shape), _const_spec(bw.shape)],
        out_specs=tile(NSA_HPG * LANES),
        out_shape=jax.ShapeDtypeStruct((bsz, s, NSA_HPG * LANES), BF16),
        scratch_shapes=scratch,
        compiler_params=params,
        name="nsa_attend_online" if online else "nsa_attend",
    )(q, ks, vs, kw, vw, oh, gates, o_cmp, sb_near, sb_far, bs, bw)


ML_ROWS = 256


def _mlstm_kernel(q_ref, k_ref, v_ref, o_ref, gif_ref, gcol_ref, gb_ref, gbrow_ref, tril_ref, triu_ref,
                  out_ref, c_sc, m_sc):
    j = pl.program_id(1)
    R = ML_ROWS
    L = ML_CHUNK

    @pl.when(j == 0)
    def _():
        c_sc[...] = jnp.zeros_like(c_sc)
        m_sc[...] = jnp.zeros_like(m_sc)

    og = o_ref[0]

    pre = gif_ref[...] + gb_ref[...]
    row8 = lax.broadcasted_iota(jnp.int32, (8, R), 0)
    logf = jnp.minimum(pre, 0.0) - jnp.log(1.0 + jnp.exp(-jnp.abs(pre)))
    g8 = jnp.where(row8 < ML_HEADS, pre, logf)
    pre_c = gcol_ref[0] + gbrow_ref[...]
    lane_c = lax.broadcasted_iota(jnp.int32, pre_c.shape, 1)
    logf_c = jnp.minimum(pre_c, 0.0) - jnp.log(1.0 + jnp.exp(-jnp.abs(pre_c)))
    g_c = jnp.where(lane_c < _IF_LANE + ML_HEADS, pre_c, logf_c)
    cum_c = _split3(g_c, lambda part: _dot(tril_ref[...], part))
    cum_row = _split3(g8, lambda part: _dot(part, triu_ref[...]))

    a_io = lax.broadcasted_iota(jnp.int32, (L, L), 0)
    b_io = lax.broadcasted_iota(jnp.int32, (L, L), 1)
    causal = (b_io <= a_io)[None]
    nc = R // L
    pairs = [(c, h) for c in range(nc) for h in range(ML_HEADS)]

    def blocks(a):
        return jnp.stack([a[c * L:(c + 1) * L, h * ML_D:(h + 1) * ML_D] for c, h in pairs])

    def cols(a, k0):
        return jnp.stack([jnp.broadcast_to(a[c * L:(c + 1) * L, k0 + h:k0 + h + 1], (L, LANES)) for c, h in pairs])

    def rows(a, k0):
        return jnp.stack([a[k0 + h:k0 + h + 1, c * L:(c + 1) * L] for c, h in pairs])

    bdot = lambda eq, x, y: jnp.einsum(eq, x, y, preferred_element_type=F32)
    qb, kb, vb = blocks(q_ref[0]), blocks(k_ref[0]), blocks(v_ref[0])
    b_col, li_col = cols(cum_c, _IF_LANE + ML_HEADS), cols(g_c, _IF_LANE)
    b_row, li_row = rows(cum_row, ML_HEADS), rows(g8, 0)
    gsum = b_row[:, :, L - 1:L]

    s_max = jnp.max(gsum - b_row + li_row, axis=2, keepdims=True)
    m_run = m_sc[:, 0:1, 0:1]
    m_ins, m_outs = [], []
    for c in range(nc):
        hs = slice(c * ML_HEADS, (c + 1) * ML_HEADS)
        m_ins.append(m_run)
        m_run = jnp.maximum(gsum[hs] + m_run, s_max[hs])
        m_outs.append(m_run)
    m_sc[...] = jnp.broadcast_to(m_run, m_sc.shape)
    m_in, m_out = jnp.concatenate(m_ins, axis=0), jnp.concatenate(m_outs, axis=0)

    log_d = jnp.where(causal, b_col[:, :, :L] - b_row + li_row, -jnp.inf)
    inter = b_col + m_in
    m_row = jnp.maximum(inter, jnp.broadcast_to(jnp.max(log_d, axis=2, keepdims=True), inter.shape))
    w = bdot('bik,bjk->bij', qb, kb) * jnp.exp(log_d - m_row[:, :, :L])
    v_aug = jnp.concatenate([vb, jnp.ones(vb.shape, BF16)], axis=2)
    wv = bdot('bij,bjd->bid', w.astype(BF16), v_aug)
    inter_scale = jnp.exp(inter - m_row)
    inter_scale = jnp.concatenate([inter_scale, inter_scale], axis=2)
    floor = jnp.exp(-m_row)
    k_src = (kb.astype(F32) * jnp.exp(gsum - b_col + li_col - m_out)).astype(BF16)
    upd = bdot('bjk,bjd->bkd', k_src, v_aug)
    decay = jnp.exp(gsum + m_in - m_out)

    c_aug = c_sc[...]
    for c in range(nc):
        hs = slice(c * ML_HEADS, (c + 1) * ML_HEADS)
        nd = inter_scale[hs] * bdot('hik,hkd->hid', qb[hs], c_aug.astype(BF16)) + wv[hs]
        hval = nd[:, :, :ML_D] / jnp.maximum(jnp.abs(nd[:, :, ML_D:]), floor[hs])
        for h in range(ML_HEADS):
            rs, cs = slice(c * L, (c + 1) * L), slice(h * ML_D, (h + 1) * ML_D)
            out_ref[0, rs, cs] = (og[rs, cs].astype(F32) * hval[h]).astype(out_ref.dtype)
        c_aug = decay[hs] * c_aug + upd[hs]
    c_sc[...] = c_aug


def _mlstm(mq, mk, mv, og, gif, gcol, gate_b, gate_b_row):
    bsz, s, width = mq.shape
    R = ML_ROWS
    nblk = s // R
    seq = lambda: pl.BlockSpec((1, R, width), lambda b, j: (b, j, 0))
    pos = np.arange(R)
    same_chunk = (pos[:, None] // ML_CHUNK) == (pos[None, :] // ML_CHUNK)
    tril = (same_chunk & (pos[None, :] <= pos[:, None])).astype(np.float32)
    return pl.pallas_call(
        _mlstm_kernel,
        grid=(bsz, nblk),
        in_specs=[seq(), seq(), seq(), seq(),
                  pl.BlockSpec((8, R), lambda b, j: (0, b * nblk + j)),
                  pl.BlockSpec((1, R, LANES), lambda b, j: (b, j, 0)),
                  _const_spec(gate_b.shape), _const_spec(gate_b_row.shape),
                  _const_spec((R, R)), _const_spec((R, R))],
        out_specs=seq(),
        out_shape=jax.ShapeDtypeStruct((bsz, s, width), BF16),
        scratch_shapes=[pltpu.VMEM((ML_HEADS, ML_D, 2 * ML_D), F32),
                        pltpu.VMEM((ML_HEADS, 8, LANES), F32)],
        compiler_params=pltpu.CompilerParams(dimension_semantics=("parallel", "arbitrary"),
                                             vmem_limit_bytes=VMEM_LIMIT),
        name="mlstm",
    )(mq, mk, mv, og, gif, gcol, gate_b, gate_b_row, jnp.asarray(tril, BF16), jnp.asarray(tril.T, BF16))


FF_CHUNK = 512


def _merge_ffn_kernel(x_ref, ya_ref, yb_ref, mg_ref, wa_ref, wb_ref, wo_ref, g2_ref, w1_ref, w2_ref,
                      out_ref):
    mg = mg_ref[...]
    mixed = (mg[:, :D_MODEL].astype(F32) * _dot(ya_ref[...], wa_ref[...])
             + mg[:, D_MODEL:].astype(F32) * _dot(yb_ref[...], wb_ref[...]))
    x1 = x_ref[...] + _dot(mixed.astype(BF16), wo_ref[...])
    h2 = x1 * lax.rsqrt(jnp.mean(x1 * x1, axis=-1, keepdims=True) + RMS_EPS) * g2_ref[...]
    h2 = h2.astype(BF16)
    acc = x1
    for c in range(D_FF // FF_CHUNK):
        a = jnp.maximum(_dot(h2, w1_ref[:, c * FF_CHUNK:(c + 1) * FF_CHUNK]), 0.0)
        acc = acc + _dot((a * a).astype(BF16), w2_ref[c * FF_CHUNK:(c + 1) * FF_CHUNK, :])
    out_ref[...] = acc


def _merge_ffn(x2d, ya, yb, mg, wa, wb, wo, g2, w1, w2, tm=512):
    n = x2d.shape[0]
    row = lambda w: pl.BlockSpec((tm, w), lambda i: (i, 0))
    return pl.pallas_call(
        _merge_ffn_kernel,
        grid=(n // tm,),
        in_specs=[row(D_MODEL), row(512), row(512), row(2048),
                  _const_spec(wa.shape), _const_spec(wb.shape), _const_spec(wo.shape),
                  _const_spec(g2.shape), _const_spec(w1.shape), _const_spec(w2.shape)],
        out_specs=row(D_MODEL),
        out_shape=jax.ShapeDtypeStruct((n, D_MODEL), F32),
        compiler_params=pltpu.CompilerParams(dimension_semantics=("parallel",),
                                             vmem_limit_bytes=VMEM_LIMIT),
        name="merge_ffn",
    )(x2d, ya, yb, mg, wa, wb, wo, g2, w1, w2)


def _proj_weights(w):
    widths = (512, 128, 128, 128, 128, 128, 128, 24, 512, 512, 512, 4, 4, 512, 2048)
    off = np.concatenate([[0], np.cumsum(widths)])
    (nq, nkc, nvc, nks, nvs, nkw, nvw, ngate, mq, mk, mv, mi, mf, mo, mgate) = (int(o) for o in off[:-1])
    col = lambda start, width: w[:, start:start + width]
    parts = [col(nq + NSA_D * (g * NSA_HPG + j), NSA_D) for j in range(NSA_HPG) for g in range(NSA_GROUPS)]
    parts += [col(nks, 128), col(nkw, 128), col(nkc, 128), col(nvc, 128), col(nvs, 128), col(nvw, 128)]
    gate = col(ngate, 24).reshape(-1, NSA_HEADS, 3).transpose(0, 2, 1).reshape(-1, 24)
    parts += [gate, col(mi, 4), col(mf, 4), jnp.zeros((w.shape[0], LANES - _IF_LANE - 8), w.dtype)]
    parts += [col(mq, 512), col(mk, 512), col(mv, 512), col(mo, 512), col(mgate, 2048)]
    w_all = jnp.concatenate(parts, axis=1).astype(BF16)
    w_if = jnp.concatenate([col(mi, 4), col(mf, 4)], axis=1).T.astype(BF16)
    return w_all, w_if


def _branch_a_weights(w):
    head = lambda h: w[NSA_D * h:NSA_D * (h + 1)]
    return jnp.concatenate([head(g * NSA_HPG + j) for j in range(NSA_HPG) for g in range(NSA_GROUPS)],
                           axis=0).astype(BF16)


def _compress_weights(pos, w1):
    r = w1.reshape(2, CMP_STRIDE, 1, NSA_D, CMP_HIDDEN)
    z = jnp.zeros_like(r)
    w1x = jnp.stack([jnp.concatenate([r, z], axis=2), jnp.concatenate([z, r], axis=2)])
    w1x = w1x.reshape(NSA_GROUPS, 2, CMP_STRIDE * NSA_GROUPS * NSA_D, CMP_HIDDEN).astype(BF16)
    posx = jnp.broadcast_to(pos.reshape(2, CMP_STRIDE, 1, NSA_D), (2, CMP_STRIDE, NSA_GROUPS, NSA_D))
    return w1x, posx.reshape(2, CMP_STRIDE * NSA_GROUPS * NSA_D).astype(BF16)


def _t5_bucket(dist):
    n = jnp.maximum(dist, 0)
    max_exact = REL_BUCKETS // 2
    nf = jnp.maximum(n, 1).astype(F32)
    large = max_exact + (jnp.log(nf / max_exact) / math.log(REL_MAX_DIST / max_exact)
                         * (REL_BUCKETS - max_exact)).astype(jnp.int32)
    return jnp.where(n < max_exact, n, jnp.minimum(large, REL_BUCKETS - 1))


def _toeplitz(tab, n_rows, n_cols, off):
    nd = tab.shape[1]
    ext = lambda lo, hi: jnp.take(tab, jnp.asarray(np.clip(np.arange(lo, hi), 0, nd - 1)), axis=1)
    w = jnp.concatenate([ext(off - n_cols + 1, off + 1)[:, ::-1], jnp.zeros((tab.shape[0], 1), tab.dtype),
                         ext(off + 1, off + n_rows)[:, ::-1]], axis=1)
    lw = n_rows + n_cols
    flat = jnp.tile(w, (1, n_rows))[:, :n_rows * (lw - 1)]
    return flat.reshape(tab.shape[0], n_rows, lw - 1)[:, :, :n_cols]


def _rel_bias(rel_table):
    return (rel_table[_t5_bucket(jnp.arange(WIN_KEYS))] - rel_table[REL_BUCKETS - 1][None, :]).T * LOG2E


def _bias_tables(tab, shifts):
    T = NSA_TILE
    sh_c, sh_s, sh_w = shifts
    r = np.arange(T)[:, None]

    c = np.arange(NEAR_KEYS)[None, :]
    ok = jnp.asarray((r - c + SLC_PAD) >= 0)[None]
    bs = jnp.where(ok, _toeplitz(tab, T, NEAR_KEYS, SLC_PAD) - sh_s, MASK_F32)
    c = np.arange(WIN_KEYS)[None, :]
    d = r - c + WINDOW
    ok = jnp.asarray((d >= 0) & (d < WINDOW))[None]
    bw = jnp.where(ok, _toeplitz(tab, T, WIN_KEYS, WINDOW) - sh_w, MASK_F32)
    r = np.arange(SEL_TILE)[:, None]
    m = np.arange(64)[None, :]
    d = r - CMP_STRIDE * m + BAND_DIST0
    band = (m >= 1) & (m < BAND_SLOTS - 1)
    vals = _toeplitz(tab, SEL_TILE, CMP_STRIDE * BAND_SLOTS, BAND_DIST0)[:, :, ::CMP_STRIDE]
    vals = jnp.pad(vals, ((0, 0), (0, 0), (0, 64 - BAND_SLOTS)))
    vals = jnp.where(jnp.asarray(band & (d >= 0))[None], vals, 0.0) - sh_c
    dead = (band & (d < 0)) | (m == BAND_SLOTS - 1)
    vals = jnp.where(jnp.asarray(dead)[None], MASK_BIG, vals)
    hi = vals.astype(BF16)
    lo = (vals - hi.astype(F32)).astype(BF16)
    bcf = jnp.concatenate([hi, lo], axis=-1)
    flat = lambda a: a.reshape(-1, a.shape[-1])
    return flat(bcf), flat(bs).astype(F32), flat(bw).astype(F32)


def _layer(l, x2d, bsz, s, consts, rel_table, norm1_g, w_in, nsa_q_gain, nsa_k_gain, cmp_k, cmp_v,
           ml_conv_w, ml_conv_b, ml_i_bias, ml_f_bias, w_branch_a, w_branch_b, w_out, norm2_g, w_ff1, w_ff2):
    n = bsz * s
    nseg = s // CMP_STRIDE
    oh, c2s = consts
    w_all, w_if = _proj_weights(w_in[l])
    qg_pad = jnp.concatenate([nsa_q_gain[l]] * 2)[None, :]
    kg_pad = jnp.stack([jnp.concatenate([nsa_k_gain[l, 1]] * 2), jnp.concatenate([nsa_k_gain[l, 2]] * 2)])
    (q, ks, kw, kc, vc, vs, vw, gates, gcol, mq, mk, mv, og, mg, gif) = _proj(
        x2d, s, norm1_g[l][None, :], w_all, w_if, qg_pad, kg_pad, ml_conv_w[l], ml_conv_b[l][None, :])

    def compress(a, params, gain, normalize):
        pos, w1, b1, w2, b2 = (p[l] for p in params)
        w1x, posx = _compress_weights(pos, w1)
        return _compress(a.reshape(bsz, nseg, CMP_STRIDE * LANES), w1x, posx, b1[None, :],
                         w2.astype(BF16), b2[None, :], gain[None, :], normalize)

    kcmp = compress(kc, cmp_k, nsa_k_gain[l, 0], True)
    vcmp = compress(vc, cmp_v, jnp.ones((NSA_D,), F32), False)

    qk_bound = lambda kg: 8.0 * LOG2E * jnp.max(jnp.abs(nsa_q_gain[l])) * jnp.max(jnp.abs(kg))
    tab = _rel_bias(rel_table)
    snap = lambda v: v.astype(BF16).astype(F32)
    shifts = [snap(qk_bound(nsa_k_gain[l, j]) + jnp.maximum(jnp.max(tab), 0.0)) for j in range(3)]
    safe = 2.0 * jnp.max(jnp.stack(shifts)) < SAFE_SHIFT_LOG2
    shifts = [jnp.where(safe, sh, 0.0) for sh in shifts]
    bcf, bs, bw = _bias_tables(tab, shifts)

    seq = lambda a: a.reshape(bsz, s, a.shape[-1])
    front = lambda a, p: jnp.pad(seq(a), ((0, 0), (p, 0), (0, 0)))
    operands = (seq(q), front(ks, SLC_PAD), front(vs, SLC_PAD), front(kw, WINDOW), front(vw, WINDOW),
                kcmp, vcmp, oh, seq(gates), bcf, bs, bw, c2s, jnp.full((1, LANES), -shifts[1], F32))
    y_a = lax.cond(safe, lambda: _nsa(False, *operands), lambda: _nsa(True, *operands))

    gate_b = jnp.concatenate([ml_i_bias[l], ml_f_bias[l]])
    gate_b_row = jnp.pad(gate_b, (_IF_LANE, LANES - _IF_LANE - 8))[None, :]
    y_b = _mlstm(seq(mq), seq(mk), seq(mv), seq(og), gif, seq(gcol), gate_b[:, None], gate_b_row)

    out = _merge_ffn(x2d, y_a.reshape(n, 512), y_b.reshape(n, 512), mg, _branch_a_weights(w_branch_a[l]),
                     w_branch_b[l].astype(BF16), w_out[l].astype(BF16), norm2_g[l][None, :],
                     w_ff1[l].astype(BF16), w_ff2[l].astype(BF16))
    return out, y_a, y_b


def _consts(s):
    nseg = s // CMP_STRIDE
    nsel = s // SLC_BLOCK
    blk_of_key = np.arange(s) // SLC_BLOCK
    oh = np.concatenate([np.ones((SLC_PAD, LANES), np.float32),
                         (blk_of_key[:, None] == np.arange(LANES)[None, :]).astype(np.float32)], axis=0)
    ci = np.arange(nseg)[:, None] * CMP_STRIDE
    sj = np.arange(LANES)[None, :] * SLC_BLOCK
    c2s = ((ci < sj + SLC_BLOCK) & (ci + CMP_LEN > sj) & (np.arange(LANES)[None, :] < nsel)
           & (np.arange(nseg)[:, None] < nseg - 1))
    return jnp.asarray(oh, BF16), jnp.asarray(c2s.astype(np.float32), BF16)


def kernel(x, norm1_g, w_in, nsa_q_gain, nsa_k_gain, cmp_k_pos, cmp_k_w1, cmp_k_b1, cmp_k_w2, cmp_k_b2, cmp_v_pos, cmp_v_w1, cmp_v_b1, cmp_v_w2, cmp_v_b2, rel_table, ml_conv_w, ml_conv_b, ml_i_bias, ml_f_bias, w_branch_a, w_branch_b, w_out, norm2_g, w_ff1, w_ff2):
    bsz, s, _ = x.shape
    consts = _consts(s)
    x2d = x.reshape(bsz * s, D_MODEL)
    for l in range(norm1_g.shape[0]):
        x2d, _, _ = _layer(l, x2d, bsz, s, consts, rel_table, norm1_g, w_in, nsa_q_gain, nsa_k_gain,
                           (cmp_k_pos, cmp_k_w1, cmp_k_b1, cmp_k_w2, cmp_k_b2),
                           (cmp_v_pos, cmp_v_w1, cmp_v_b1, cmp_v_w2, cmp_v_b2),
                           ml_conv_w, ml_conv_b, ml_i_bias, ml_f_bias,
                           w_branch_a, w_branch_b, w_out, norm2_g, w_ff1, w_ff2)
    return x2d.reshape(bsz, s, D_MODEL)
```

```python
import functools
import math

import numpy as np
import jax
import jax.numpy as jnp
from jax import lax
from jax.experimental import pallas as pl
from jax.experimental.pallas import tpu as pltpu

F32 = jnp.float32
BF16 = jnp.bfloat16

D_MODEL = 1024
NSA_HEADS = 8
NSA_GROUPS = 2
NSA_HPG = NSA_HEADS // NSA_GROUPS
NSA_D = 64
CMP_LEN = 32
CMP_STRIDE = 16
CMP_HIDDEN = 256
SLC_BLOCK = 64
SLC_TOPK = 16
WINDOW = 512
FORCED_SCORE = 1e4
ML_HEADS = 4
ML_D = 128
ML_CHUNK = 64
CONV_WIDTH = 4
D_FF = 4 * D_MODEL
REL_BUCKETS = 32
REL_MAX_DIST = 128
RMS_EPS = 1e-6

LANES = 128
NSA_TILE = 256
SEL_TILE = 512
SLC_PAD = 128
FAR_CHUNK = 512
NEAR_KEYS = SLC_PAD + NSA_TILE
WIN_KEYS = WINDOW + NSA_TILE
BAND_OFF = 10
BAND_SLOTS = 2 + (SEL_TILE + 128) // CMP_STRIDE
BAND_DIST0 = CMP_STRIDE * BAND_OFF - (CMP_LEN - 1)
MASK_BIG = -1e9
MASK_F32 = -1e30
LOG2E = math.log2(math.e)
SAFE_SHIFT_LOG2 = 50.0
VMEM_LIMIT = 56 * 1024 * 1024

_Q_OFF, _KS_OFF, _KW_OFF, _KC_OFF, _VC_OFF, _VS_OFF, _VW_OFF = 0, 512, 640, 768, 896, 1024, 1152
_GATE_OFF, _MQ_OFF, _MK_OFF, _MV_OFF, _MO_OFF, _MG_OFF, _W_COLS = 1280, 1408, 1920, 2432, 2944, 3456, 5504
_IF_LANE = 24


def _nt(a, b, precision=None):
    return lax.dot_general(a, b, (((1,), (1,)), ((), ())), precision=precision,
                           preferred_element_type=F32)


def _dot(a, b, precision=None):
    return jnp.dot(a, b, precision=precision, preferred_element_type=F32)


def _split3(x, dot_part):
    hi = x.astype(BF16)
    r1 = x - hi.astype(F32)
    mid = r1.astype(BF16)
    lo = (r1 - mid.astype(F32)).astype(BF16)
    return dot_part(hi) + dot_part(mid) + dot_part(lo)


def _const_spec(shape):
    nd = len(shape)
    return pl.BlockSpec(shape, lambda *_: (0,) * nd, pipeline_mode=pl.Buffered(1))


def _proj_kernel(x_ref, g1_ref, w_ref, wif_ref, qg_ref, kg_ref, cw_ref, cb_ref,
                 q_ref, ks_ref, kw_ref, kc_ref, vc_ref, vs_ref, vw_ref, gate_ref, gcol_ref,
                 mq_ref, mk_ref, mv_ref, mo_ref, mg_ref, gif_ref, ext_sc, *, tiles_per_seq):
    tm = x_ref.shape[0]
    x = x_ref[...]
    h = x * lax.rsqrt(jnp.mean(x * x, axis=-1, keepdims=True) + RMS_EPS) * g1_ref[...]
    hb = h.astype(BF16)

    def proj(off, width):
        return _dot(hb, w_ref[:, off:off + width])

    lane = lax.broadcasted_iota(jnp.int32, (1, LANES), 1)
    low = lane < NSA_D

    def half_norm(blk):
        sq = blk * blk
        ms0 = jnp.sum(jnp.where(low, sq, 0.0), axis=-1, keepdims=True) * (1.0 / NSA_D)
        ms1 = jnp.sum(jnp.where(low, 0.0, sq), axis=-1, keepdims=True) * (1.0 / NSA_D)
        return blk * jnp.where(low, lax.rsqrt(ms0 + RMS_EPS), lax.rsqrt(ms1 + RMS_EPS))

    for pair in range(NSA_HPG // 2):
        both = proj(_Q_OFF + 2 * LANES * pair, 2 * LANES)
        for hh in range(2):
            qn = half_norm(both[:, LANES * hh:LANES * (hh + 1)]) * qg_ref[...] * (NSA_D ** -0.5 * LOG2E)
            col = LANES * (2 * pair + hh)
            q_ref[:, col:col + LANES] = qn.astype(q_ref.dtype)

    both = proj(_KS_OFF, 2 * LANES)
    for hh, ref in enumerate((ks_ref, kw_ref)):
        ref[...] = (half_norm(both[:, LANES * hh:LANES * (hh + 1)]) * kg_ref[hh:hh + 1, :]).astype(ref.dtype)

    for off, refs in ((_KC_OFF, (kc_ref, vc_ref)), (_VS_OFF, (vs_ref, vw_ref))):
        both = proj(off, 2 * LANES)
        for hh, ref in enumerate(refs):
            ref[...] = both[:, LANES * hh:LANES * (hh + 1)].astype(ref.dtype)

    slab = proj(_GATE_OFF, LANES)
    gate_ref[...] = jax.nn.sigmoid(slab)
    gcol_ref[...] = slab

    @pl.when(pl.program_id(0) % tiles_per_seq == 0)
    def _():
        ext_sc[0:8, :] = jnp.zeros((8, ext_sc.shape[1]), F32)

    ext_sc[8:, 0:512] = proj(_MQ_OFF, 512)
    ext_sc[8:, 512:1024] = proj(_MK_OFF, 512)
    conv = cb_ref[...]
    for t in range(CONV_WIDTH):
        lo = 8 - (CONV_WIDTH - 1) + t
        conv = conv + ext_sc[lo:lo + tm, :] * cw_ref[t:t + 1, :]
    ext_sc[0:8, :] = ext_sc[tm:tm + 8, :]
    qk = conv * jax.nn.sigmoid(conv)
    mq_ref[...] = (qk[:, :512] * (ML_D ** -0.5)).astype(mq_ref.dtype)
    mk_ref[...] = qk[:, 512:].astype(mk_ref.dtype)

    mv_ref[...] = proj(_MV_OFF, 512).astype(mv_ref.dtype)
    mo_ref[...] = jax.nn.sigmoid(proj(_MO_OFF, 512)).astype(mo_ref.dtype)
    for c in range(4):
        mg_ref[:, 512 * c:512 * (c + 1)] = jax.nn.sigmoid(proj(_MG_OFF + 512 * c, 512)).astype(mg_ref.dtype)
    gif_ref[...] = _nt(wif_ref[...], hb)


def _proj(x2d, seq_len, g1, w_all, w_if, qg_pad, kg_pad, conv_w, conv_b, tm=512):
    n = x2d.shape[0]
    row = lambda w: pl.BlockSpec((tm, w), lambda i: (i, 0))
    out_shapes = (
        jax.ShapeDtypeStruct((n, 512), BF16),
        jax.ShapeDtypeStruct((n, LANES), BF16),
        jax.ShapeDtypeStruct((n, LANES), BF16),
        jax.ShapeDtypeStruct((n, LANES), BF16),
        jax.ShapeDtypeStruct((n, LANES), BF16),
        jax.ShapeDtypeStruct((n, LANES), BF16),
        jax.ShapeDtypeStruct((n, LANES), BF16),
        jax.ShapeDtypeStruct((n, LANES), F32),
        jax.ShapeDtypeStruct((n, LANES), F32),
        jax.ShapeDtypeStruct((n, 512), BF16),
        jax.ShapeDtypeStruct((n, 512), BF16),
        jax.ShapeDtypeStruct((n, 512), BF16),
        jax.ShapeDtypeStruct((n, 512), BF16),
        jax.ShapeDtypeStruct((n, 2048), BF16),
        jax.ShapeDtypeStruct((8, n), F32),
    )
    out_specs = (row(512),) + (row(LANES),) * 8 + (row(512),) * 4 + (row(2048),
                 pl.BlockSpec((8, tm), lambda i: (0, i)))
    return pl.pallas_call(
        functools.partial(_proj_kernel, tiles_per_seq=seq_len // tm),
        grid=(n // tm,),
        in_specs=[row(D_MODEL), _const_spec((1, D_MODEL)), _const_spec((D_MODEL, _W_COLS)),
                  _const_spec((8, D_MODEL)), _const_spec((1, LANES)), _const_spec((2, LANES)),
                  _const_spec(conv_w.shape), _const_spec(conv_b.shape)],
        out_specs=out_specs,
        out_shape=out_shapes,
        scratch_shapes=[pltpu.VMEM((tm + 8, 1024), F32)],
        compiler_params=pltpu.CompilerParams(dimension_semantics=("arbitrary",),
                                             vmem_limit_bytes=VMEM_LIMIT),
        name="proj",
    )(x2d, g1, w_all, w_if, qg_pad, kg_pad, conv_w, conv_b)


def _compress_kernel(seg_ref, w1_ref, pos_ref, b1_ref, w2_ref, b2_ref, gain_ref, out_ref, *, normalize):
    seg = seg_ref[0]
    nseg, width = seg.shape
    c = math.sqrt(2.0 / math.pi)
    pos_lo = jnp.broadcast_to(pos_ref[0:1, :], (8, width))
    pos_hi = jnp.broadcast_to(pos_ref[1:2, :], (8, width))
    outs = []
    for g in range(NSA_GROUPS):
        a = _dot(seg, w1_ref[g, 0])
        b = _dot(seg, w1_ref[g, 1])
        posb = (_dot(pos_lo, w1_ref[g, 0]) + _dot(pos_hi, w1_ref[g, 1]))[0:1] + b1_ref[...]
        pre = a + pltpu.roll(b, nseg - 1, 0) + posb
        hid = 0.5 * pre * (1.0 + jnp.tanh(c * (pre + 0.044715 * (pre * pre * pre))))
        out = _dot(hid.astype(BF16), w2_ref[...]) + b2_ref[...]
        if normalize:
            ms = jnp.mean(out * out, axis=-1, keepdims=True)
            out = out * lax.rsqrt(ms + RMS_EPS) * gain_ref[...]
        outs.append(out)
    out_ref[0] = jnp.concatenate(outs, axis=1).astype(out_ref.dtype)


def _compress(segs, w1x, posx, b1, w2, b2, gain, normalize):
    bsz, nseg, width = segs.shape
    return pl.pallas_call(
        functools.partial(_compress_kernel, normalize=normalize),
        grid=(bsz,),
        in_specs=[pl.BlockSpec((1, nseg, width), lambda b: (b, 0, 0)),
                  _const_spec(w1x.shape), _const_spec(posx.shape), _const_spec(b1.shape),
                  _const_spec(w2.shape), _const_spec(b2.shape), _const_spec(gain.shape)],
        out_specs=pl.BlockSpec((1, nseg, LANES), lambda b: (b, 0, 0)),
        out_shape=jax.ShapeDtypeStruct((bsz, nseg, LANES), BF16),
        compiler_params=pltpu.CompilerParams(dimension_semantics=("parallel",),
                                             vmem_limit_bytes=VMEM_LIMIT),
        name="compress",
    )(segs, w1x, posx, b1, w2, b2, gain)


def _stack_heads(q_ref, lane_q):
    q = q_ref[0]
    return jnp.concatenate(
        [jnp.where((lane_q >= NSA_D) == (g == 1), q[:, LANES * j:LANES * (j + 1)], jnp.zeros((), BF16))
         for g in range(NSA_GROUPS) for j in range(NSA_HPG)], axis=0)


def _softmax_pv(s, v, online):
    if online:
        m = jnp.max(s, axis=1, keepdims=True)
        s = s - jnp.where(m < 0.1 * MASK_BIG, 0.0, m)
    p = jnp.exp2(s)
    acc = _dot(p.astype(BF16), jnp.concatenate([v, jnp.ones(v.shape, BF16)], axis=1))
    return p, acc[:, :LANES], acc[:, LANES:]


def _nsa_select_kernel(q_ref, kc_ref, vc_ref, gate_ref, bcf_ref, c2s_ref, shift_ref,
                       oc_ref, sbn_ref, sbf_ref, *, online):
    i = pl.program_id(1)
    T = SEL_TILE
    grows = NSA_HPG * T
    ncmp = kc_ref.shape[1]
    nblk = LANES
    lane_q = lax.broadcasted_iota(jnp.int32, (T, LANES), 1)
    qpad = _stack_heads(q_ref, lane_q)

    n_io = lax.broadcasted_iota(jnp.int32, (ncmp, LANES), 0)
    l_io = lax.broadcasted_iota(jnp.int32, (ncmp, LANES), 1)
    band = jnp.clip(n_io - i * (T // CMP_STRIDE) + BAND_OFF, 0, BAND_SLOTS - 1)
    oh_c = jnp.where(band == (l_io & 63), 1.0, 0.0).astype(BF16)
    kc_aug = jnp.concatenate([kc_ref[0], oh_c], axis=1)
    qc_aug = jnp.concatenate([qpad, bcf_ref[...]], axis=1)
    p_c, num_c, l_c = _softmax_pv(_nt(qc_aug, kc_aug), vc_ref[0], online)
    inv_c = 1.0 / jnp.maximum(l_c, 1e-30)
    o_c = num_c * inv_c
    p_c = p_c * jnp.concatenate([inv_c] * (ncmp // LANES), axis=1)

    gates = gate_ref[0]
    gated = [jnp.broadcast_to(gates[:, h:h + 1], (T, LANES)) * o_c[h * T:(h + 1) * T] for h in range(NSA_HEADS)]
    for j in range(NSA_HPG):
        oc_ref[0, :, LANES * j:LANES * (j + 1)] = jnp.where(
            lane_q < NSA_D, gated[j], gated[NSA_HPG + j]).astype(oc_ref.dtype)

    imp_t = []
    for g in range(NSA_GROUPS):
        ps = p_c[g * grows:g * grows + T]
        for hh in range(1, NSA_HPG):
            ps = ps + p_c[g * grows + hh * T:g * grows + (hh + 1) * T]
        ps_hi = ps.astype(BF16)
        ps_lo = (ps - ps_hi.astype(F32)).astype(BF16)
        imp_t.append((_dot(ps_hi, c2s_ref[...]) + _dot(ps_lo, c2s_ref[...])).T)
    width = NSA_GROUPS * T
    j_io = lax.broadcasted_iota(jnp.int32, (nblk, width), 0)
    r_io = lax.broadcasted_iota(jnp.int32, (nblk, width), 1) & (T - 1)
    j_f = j_io.astype(F32)
    cur = i * (T // SLC_BLOCK) + jnp.right_shift(r_io, SLC_BLOCK.bit_length() - 1)
    forced = (j_io == 0) | (j_io == cur) | (j_io == cur - 1)
    score = jnp.where(forced, -jnp.inf, jnp.where(j_io <= cur, jnp.concatenate(imp_t, axis=1), -1.0))
    for _ in range(SLC_TOPK - 3):
        best = jnp.max(score, axis=0, keepdims=True)
        first = jnp.min(jnp.where(score == best, j_f, float(nblk)), axis=0, keepdims=True)
        score = jnp.where(j_f == first, -jnp.inf, score)
    tok = i * T + lax.broadcasted_iota(jnp.int32, (T, nblk), 0)
    near_start = (tok // NSA_TILE) * (NSA_TILE // SLC_BLOCK) - SLC_PAD // SLC_BLOCK
    near_blk = lax.broadcasted_iota(jnp.int32, (T, nblk), 1) >= near_start
    far_pick = shift_ref[0:1, 0:1]
    for g in range(NSA_GROUPS):
        picked = score[:, g * T:(g + 1) * T].T == -jnp.inf
        sbn_ref[0, :, LANES * g:LANES * (g + 1)] = jnp.where(picked & near_blk, 0.0, MASK_BIG).astype(BF16)
        sbf_ref[0, :, LANES * g:LANES * (g + 1)] = jnp.where(
            picked & jnp.logical_not(near_blk), far_pick, MASK_BIG).astype(BF16)


def _nsa_attend_kernel(q_ref, ks_ref, vs_ref, kw_ref, vw_ref, oh_ref, gate_ref, oc_ref, sbn_ref, sbf_ref,
                       bs_ref, bw_ref, out_ref, acc_sc, *extra_sc, online):
    i = pl.program_id(1)
    t0 = pl.multiple_of(i * NSA_TILE, NSA_TILE)
    T = NSA_TILE
    rows = NSA_HEADS * T
    grows = NSA_HPG * T
    lane_q = lax.broadcasted_iota(jnp.int32, (T, LANES), 1)
    qpad = _stack_heads(q_ref, lane_q)

    def with_features(feat_ref):
        feats = [jnp.concatenate([feat_ref[0, :, LANES * g:LANES * (g + 1)]] * NSA_HPG, axis=0)
                 for g in range(NSA_GROUPS)]
        return jnp.concatenate([qpad, jnp.concatenate(feats, axis=0)], axis=1)

    q_near, q_far = with_features(sbn_ref), with_features(sbf_ref)

    r_w = lax.broadcasted_iota(jnp.int32, (WIN_KEYS, LANES), 0)
    pad_flag = jnp.where(r_w + (t0 - WINDOW) < 0, 1.0, 0.0).astype(BF16)
    kw_aug = jnp.concatenate([kw_ref[0, pl.ds(t0, WIN_KEYS), :], pad_flag], axis=1)
    qw_aug = jnp.concatenate([qpad, jnp.full((rows, LANES), MASK_BIG / LANES, BF16)], axis=1)
    s_w = _nt(qw_aug, kw_aug) + bw_ref[...]
    _, num_w, l_w = _softmax_pv(s_w, vw_ref[0, pl.ds(t0, WIN_KEYS), :], online)
    o_w = num_w * (1.0 / l_w)

    gates = gate_ref[0]
    gate_tile = lambda col: jnp.broadcast_to(gates[:, col:col + 1], (T, LANES))
    head_rows = [slice(h * T, (h + 1) * T) for h in range(NSA_HEADS)]
    o_gw = [gate_tile(16 + h) * o_w[head_rows[h]] for h in range(NSA_HEADS)]
    g_slc = [gate_tile(8 + h) for h in range(NSA_HEADS)]

    ones_f = jnp.ones((FAR_CHUNK, LANES), BF16)
    k_near = jnp.concatenate([ks_ref[0, pl.ds(t0, NEAR_KEYS), :], oh_ref[pl.ds(t0, NEAR_KEYS), :]], axis=1)
    v_near = jnp.concatenate([vs_ref[0, pl.ds(t0, NEAR_KEYS), :], jnp.ones((NEAR_KEYS, LANES), BF16)], axis=1)
    s_n = _nt(q_near, k_near) + bs_ref[...]
    if online:
        m_sc, = extra_sc
        m_n = jnp.max(s_n, axis=1, keepdims=True)
        m_sc[...] = m_n
        s_n = s_n - m_n
    acc_sc[...] = _dot(jnp.exp2(s_n).astype(BF16), v_near)

    last_chunk = (ks_ref.shape[1] - SLC_PAD) // FAR_CHUNK - 1

    def far_start(c):
        return pl.multiple_of(SLC_PAD + FAR_CHUNK * jnp.minimum(c, last_chunk), LANES)

    def far_scores(c, by_group=False):
        start = far_start(c)
        k_f = jnp.concatenate([ks_ref[0, pl.ds(start, FAR_CHUNK), :], oh_ref[pl.ds(start, FAR_CHUNK), :]], axis=1)
        if by_group:
            return jnp.concatenate([_nt(q_far[:grows], k_f), _nt(q_far[grows:], k_f)], axis=0)
        return _nt(q_far, k_f)

    def far_values(c):
        return jnp.concatenate([vs_ref[0, pl.ds(far_start(c), FAR_CHUNK), :], ones_f], axis=1)

    n_far = (t0 + (FAR_CHUNK - 1 - SLC_PAD)) // FAR_CHUNK
    if online:
        def far_online(c, carry):
            s_f = far_scores(c)
            m_old = m_sc[...]
            m_new = jnp.maximum(m_old, jnp.max(s_f, axis=1, keepdims=True))
            m_sc[...] = m_new
            acc_sc[...] = (jnp.exp2(m_old - m_new) * acc_sc[...]
                           + _dot(jnp.exp2(s_f - m_new).astype(BF16), far_values(c)))
            return carry

        lax.fori_loop(0, n_far, far_online, 0)
    else:
        pa_sc, pb_sc = extra_sc
        pa_sc[...] = jnp.exp2(far_scores(0, by_group=True)).astype(BF16)

        def far_pair(cc, carry):
            c0 = 2 * cc
            pv0 = _dot(pa_sc[...], far_values(c0))
            pb_sc[...] = jnp.exp2(far_scores(c0 + 1)).astype(BF16)
            pv1 = _dot(pb_sc[...], far_values(c0 + 1))
            pa_sc[...] = jnp.exp2(far_scores(c0 + 2)).astype(BF16)
            acc_sc[...] += pv0 + pv1
            return carry

        lax.fori_loop(0, (n_far + 1) // 2, far_pair, 0)

    acc = acc_sc[...]
    o_s = acc[:, :LANES] * (1.0 / acc[:, LANES:])

    head_out = [o_gw[h] + g_slc[h] * o_s[head_rows[h]] for h in range(NSA_HEADS)]
    for j in range(NSA_HPG):
        cols = slice(LANES * j, LANES * (j + 1))
        out_ref[0, :, cols] = (oc_ref[0, :, cols].astype(F32) + jnp.where(
            lane_q < NSA_D, head_out[j], head_out[NSA_HPG + j])).astype(out_ref.dtype)


def _nsa(online, q, ks, vs, kw, vw, kc, vc, oh, gates, bcf, bs, bw, c2s, shift):
    bsz, s = q.shape[:2]
    T = NSA_TILE
    rows = NSA_HEADS * T
    grid = (bsz, s // T)
    params = pltpu.CompilerParams(dimension_semantics=("parallel", "arbitrary"), vmem_limit_bytes=VMEM_LIMIT)
    tile = lambda width: pl.BlockSpec((1, T, width), lambda b, i: (b, i, 0))
    per_b = lambda a: pl.BlockSpec((1,) + a.shape[1:], lambda b, i: (b, 0, 0), pipeline_mode=pl.Buffered(1))
    sel_tile = lambda width: pl.BlockSpec((1, SEL_TILE, width), lambda b, i: (b, i, 0))
    o_cmp, sb_near, sb_far = pl.pallas_call(
        functools.partial(_nsa_select_kernel, online=online),
        grid=(bsz, s // SEL_TILE),
        in_specs=[sel_tile(NSA_HPG * LANES), per_b(kc), per_b(vc), sel_tile(LANES),
                  _const_spec(bcf.shape), _const_spec(c2s.shape), _const_spec(shift.shape)],
        out_specs=(sel_tile(NSA_HPG * LANES), sel_tile(NSA_GROUPS * LANES), sel_tile(NSA_GROUPS * LANES)),
        out_shape=(jax.ShapeDtypeStruct((bsz, s, NSA_HPG * LANES), BF16),
                   jax.ShapeDtypeStruct((bsz, s, NSA_GROUPS * LANES), BF16),
                   jax.ShapeDtypeStruct((bsz, s, NSA_GROUPS * LANES), BF16)),
        compiler_params=params,
        name="nsa_select_online" if online else "nsa_select",
    )(q, kc, vc, gates, bcf, c2s, shift)
    scratch = [pltpu.VMEM((rows, 2 * LANES), F32)]
    if online:
        scratch.append(pltpu.VMEM((rows, 1), F32))
    else:
        scratch += [pltpu.VMEM((rows, FAR_CHUNK), BF16)] * 2
    return pl.pallas_call(
        functools.partial(_nsa_attend_kernel, online=online),
        grid=grid,
        in_specs=[tile(NSA_HPG * LANES), per_b(ks), per_b(vs), per_b(kw), per_b(vw), _const_spec(oh.shape),
                  tile(LANES), tile(NSA_HPG * LANES), tile(NSA_GROUPS * LANES), tile(NSA_GROUPS * LANES),
                  _const_spec(bs.shape), _const_spec(bw.shape)],
        out_specs=tile(NSA_HPG * LANES),
        out_shape=jax.ShapeDtypeStruct((bsz, s, NSA_HPG * LANES), BF16),
        scratch_shapes=scratch,
        compiler_params=params,
        name="nsa_attend_online" if online else "nsa_attend",
    )(q, ks, vs, kw, vw, oh, gates, o_cmp, sb_near, sb_far, bs, bw)


ML_ROWS = 256


def _mlstm_kernel(q_ref, k_ref, v_ref, o_ref, gif_ref, gcol_ref, gb_ref, gbrow_ref, tril_ref, triu_ref,
                  out_ref, c_sc, m_sc):
    j = pl.program_id(1)
    R = ML_ROWS
    L = ML_CHUNK

    @pl.when(j == 0)
    def _():
        c_sc[...] = jnp.zeros_like(c_sc)
        m_sc[...] = jnp.zeros_like(m_sc)

    og = o_ref[0]

    pre = gif_ref[...] + gb_ref[...]
    row8 = lax.broadcasted_iota(jnp.int32, (8, R), 0)
    logf = jnp.minimum(pre, 0.0) - jnp.log(1.0 + jnp.exp(-jnp.abs(pre)))
    g8 = jnp.where(row8 < ML_HEADS, pre, logf)
    pre_c = gcol_ref[0] + gbrow_ref[...]
    lane_c = lax.broadcasted_iota(jnp.int32, pre_c.shape, 1)
    logf_c = jnp.minimum(pre_c, 0.0) - jnp.log(1.0 + jnp.exp(-jnp.abs(pre_c)))
    g_c = jnp.where(lane_c < _IF_LANE + ML_HEADS, pre_c, logf_c)
    cum_c = _split3(g_c, lambda part: _dot(tril_ref[...], part))
    cum_row = _split3(g8, lambda part: _dot(part, triu_ref[...]))

    a_io = lax.broadcasted_iota(jnp.int32, (L, L), 0)
    b_io = lax.broadcasted_iota(jnp.int32, (L, L), 1)
    causal = (b_io <= a_io)[None]
    nc = R // L
    pairs = [(c, h) for c in range(nc) for h in range(ML_HEADS)]

    def blocks(a):
        return jnp.stack([a[c * L:(c + 1) * L, h * ML_D:(h + 1) * ML_D] for c, h in pairs])

    def cols(a, k0):
        return jnp.stack([jnp.broadcast_to(a[c * L:(c + 1) * L, k0 + h:k0 + h + 1], (L, LANES)) for c, h in pairs])

    def rows(a, k0):
        return jnp.stack([a[k0 + h:k0 + h + 1, c * L:(c + 1) * L] for c, h in pairs])

    bdot = lambda eq, x, y: jnp.einsum(eq, x, y, preferred_element_type=F32)
    qb, kb, vb = blocks(q_ref[0]), blocks(k_ref[0]), blocks(v_ref[0])
    b_col, li_col = cols(cum_c, _IF_LANE + ML_HEADS), cols(g_c, _IF_LANE)
    b_row, li_row = rows(cum_row, ML_HEADS), rows(g8, 0)
    gsum = b_row[:, :, L - 1:L]

    s_max = jnp.max(gsum - b_row + li_row, axis=2, keepdims=True)
    m_run = m_sc[:, 0:1, 0:1]
    m_ins, m_outs = [], []
    for c in range(nc):
        hs = slice(c * ML_HEADS, (c + 1) * ML_HEADS)
        m_ins.append(m_run)
        m_run = jnp.maximum(gsum[hs] + m_run, s_max[hs])
        m_outs.append(m_run)
    m_sc[...] = jnp.broadcast_to(m_run, m_sc.shape)
    m_in, m_out = jnp.concatenate(m_ins, axis=0), jnp.concatenate(m_outs, axis=0)

    log_d = jnp.where(causal, b_col[:, :, :L] - b_row + li_row, -jnp.inf)
    inter = b_col + m_in
    m_row = jnp.maximum(inter, jnp.broadcast_to(jnp.max(log_d, axis=2, keepdims=True), inter.shape))
    w = bdot('bik,bjk->bij', qb, kb) * jnp.exp(log_d - m_row[:, :, :L])
    v_aug = jnp.concatenate([vb, jnp.ones(vb.shape, BF16)], axis=2)
    wv = bdot('bij,bjd->bid', w.astype(BF16), v_aug)
    inter_scale = jnp.exp(inter - m_row)
    inter_scale = jnp.concatenate([inter_scale, inter_scale], axis=2)
    floor = jnp.exp(-m_row)
    k_src = (kb.astype(F32) * jnp.exp(gsum - b_col + li_col - m_out)).astype(BF16)
    upd = bdot('bjk,bjd->bkd', k_src, v_aug)
    decay = jnp.exp(gsum + m_in - m_out)

    c_aug = c_sc[...]
    for c in range(nc):
        hs = slice(c * ML_HEADS, (c + 1) * ML_HEADS)
        nd = inter_scale[hs] * bdot('hik,hkd->hid', qb[hs], c_aug.astype(BF16)) + wv[hs]
        hval = nd[:, :, :ML_D] / jnp.maximum(jnp.abs(nd[:, :, ML_D:]), floor[hs])
        for h in range(ML_HEADS):
            rs, cs = slice(c * L, (c + 1) * L), slice(h * ML_D, (h + 1) * ML_D)
            out_ref[0, rs, cs] = (og[rs, cs].astype(F32) * hval[h]).astype(out_ref.dtype)
        c_aug = decay[hs] * c_aug + upd[hs]
    c_sc[...] = c_aug


def _mlstm(mq, mk, mv, og, gif, gcol, gate_b, gate_b_row):
    bsz, s, width = mq.shape
    R = ML_ROWS
    nblk = s // R
    seq = lambda: pl.BlockSpec((1, R, width), lambda b, j: (b, j, 0))
    pos = np.arange(R)
    same_chunk = (pos[:, None] // ML_CHUNK) == (pos[None, :] // ML_CHUNK)
    tril = (same_chunk & (pos[None, :] <= pos[:, None])).astype(np.float32)
    return pl.pallas_call(
        _mlstm_kernel,
        grid=(bsz, nblk),
        in_specs=[seq(), seq(), seq(), seq(),
                  pl.BlockSpec((8, R), lambda b, j: (0, b * nblk + j)),
                  pl.BlockSpec((1, R, LANES), lambda b, j: (b, j, 0)),
                  _const_spec(gate_b.shape), _const_spec(gate_b_row.shape),
                  _const_spec((R, R)), _const_spec((R, R))],
        out_specs=seq(),
        out_shape=jax.ShapeDtypeStruct((bsz, s, width), BF16),
        scratch_shapes=[pltpu.VMEM((ML_HEADS, ML_D, 2 * ML_D), F32),
                        pltpu.VMEM((ML_HEADS, 8, LANES), F32)],
        compiler_params=pltpu.CompilerParams(dimension_semantics=("parallel", "arbitrary"),
                                             vmem_limit_bytes=VMEM_LIMIT),
        name="mlstm",
    )(mq, mk, mv, og, gif, gcol, gate_b, gate_b_row, jnp.asarray(tril, BF16), jnp.asarray(tril.T, BF16))


FF_CHUNK = 512


def _merge_ffn_kernel(x_ref, ya_ref, yb_ref, mg_ref, wa_ref, wb_ref, wo_ref, g2_ref, w1_ref, w2_ref,
                      out_ref):
    mg = mg_ref[...]
    mixed = (mg[:, :D_MODEL].astype(F32) * _dot(ya_ref[...], wa_ref[...])
             + mg[:, D_MODEL:].astype(F32) * _dot(yb_ref[...], wb_ref[...]))
    x1 = x_ref[...] + _dot(mixed.astype(BF16), wo_ref[...])
    h2 = x1 * lax.rsqrt(jnp.mean(x1 * x1, axis=-1, keepdims=True) + RMS_EPS) * g2_ref[...]
    h2 = h2.astype(BF16)
    acc = x1
    for c in range(D_FF // FF_CHUNK):
        a = jnp.maximum(_dot(h2, w1_ref[:, c * FF_CHUNK:(c + 1) * FF_CHUNK]), 0.0)
        acc = acc + _dot((a * a).astype(BF16), w2_ref[c * FF_CHUNK:(c + 1) * FF_CHUNK, :])
    out_ref[...] = acc


def _merge_ffn(x2d, ya, yb, mg, wa, wb, wo, g2, w1, w2, tm=512):
    n = x2d.shape[0]
    row = lambda w: pl.BlockSpec((tm, w), lambda i: (i, 0))
    return pl.pallas_call(
        _merge_ffn_kernel,
        grid=(n // tm,),
        in_specs=[row(D_MODEL), row(512), row(512), row(2048),
                  _const_spec(wa.shape), _const_spec(wb.shape), _const_spec(wo.shape),
                  _const_spec(g2.shape), _const_spec(w1.shape), _const_spec(w2.shape)],
        out_specs=row(D_MODEL),
        out_shape=jax.ShapeDtypeStruct((n, D_MODEL), F32),
        compiler_params=pltpu.CompilerParams(dimension_semantics=("parallel",),
                                             vmem_limit_bytes=VMEM_LIMIT),
        name="merge_ffn",
    )(x2d, ya, yb, mg, wa, wb, wo, g2, w1, w2)


def _proj_weights(w):
    widths = (512, 128, 128, 128, 128, 128, 128, 24, 512, 512, 512, 4, 4, 512, 2048)
    off = np.concatenate([[0], np.cumsum(widths)])
    (nq, nkc, nvc, nks, nvs, nkw, nvw, ngate, mq, mk, mv, mi, mf, mo, mgate) = (int(o) for o in off[:-1])
    col = lambda start, width: w[:, start:start + width]
    parts = [col(nq + NSA_D * (g * NSA_HPG + j), NSA_D) for j in range(NSA_HPG) for g in range(NSA_GROUPS)]
    parts += [col(nks, 128), col(nkw, 128), col(nkc, 128), col(nvc, 128), col(nvs, 128), col(nvw, 128)]
    gate = col(ngate, 24).reshape(-1, NSA_HEADS, 3).transpose(0, 2, 1).reshape(-1, 24)
    parts += [gate, col(mi, 4), col(mf, 4), jnp.zeros((w.shape[0], LANES - _IF_LANE - 8), w.dtype)]
    parts += [col(mq, 512), col(mk, 512), col(mv, 512), col(mo, 512), col(mgate, 2048)]
    w_all = jnp.concatenate(parts, axis=1).astype(BF16)
    w_if = jnp.concatenate([col(mi, 4), col(mf, 4)], axis=1).T.astype(BF16)
    return w_all, w_if


def _branch_a_weights(w):
    head = lambda h: w[NSA_D * h:NSA_D * (h + 1)]
    return jnp.concatenate([head(g * NSA_HPG + j) for j in range(NSA_HPG) for g in range(NSA_GROUPS)],
                           axis=0).astype(BF16)


def _compress_weights(pos, w1):
    r = w1.reshape(2, CMP_STRIDE, 1, NSA_D, CMP_HIDDEN)
    z = jnp.zeros_like(r)
    w1x = jnp.stack([jnp.concatenate([r, z], axis=2), jnp.concatenate([z, r], axis=2)])
    w1x = w1x.reshape(NSA_GROUPS, 2, CMP_STRIDE * NSA_GROUPS * NSA_D, CMP_HIDDEN).astype(BF16)
    posx = jnp.broadcast_to(pos.reshape(2, CMP_STRIDE, 1, NSA_D), (2, CMP_STRIDE, NSA_GROUPS, NSA_D))
    return w1x, posx.reshape(2, CMP_STRIDE * NSA_GROUPS * NSA_D).astype(BF16)


def _t5_bucket(dist):
    n = np.maximum(dist, 0)
    max_exact = REL_BUCKETS // 2
    nf = np.maximum(n, 1).astype(np.float32)
    large = max_exact + (np.log(nf / np.float32(max_exact)) / np.float32(math.log(REL_MAX_DIST / max_exact))
                         * np.float32(REL_BUCKETS - max_exact)).astype(np.int32)
    return np.where(n < max_exact, n, np.minimum(large, REL_BUCKETS - 1))


def _toeplitz(rel, n_rows, n_cols, stride, off, inner=1):
    a_rows = n_rows // inner
    lw = a_rows + n_cols
    k = np.arange(lw)[:, None]
    dist = stride * np.where(k < n_cols, -k, lw - k) + np.arange(inner)[None, :] + off
    live = (dist >= 0) & (k != n_cols)
    onehot = live[..., None] & (_t5_bucket(dist)[..., None] == np.arange(REL_BUCKETS))
    w = jnp.dot(jnp.asarray(onehot.reshape(lw * inner, REL_BUCKETS), F32), rel.T,
                precision=lax.Precision.HIGHEST)
    w = w.T.reshape(-1, lw, inner)
    flat = jnp.tile(w, (1, a_rows, 1))[:, :a_rows * (lw - 1)]
    out = flat.reshape(-1, a_rows, lw - 1, inner)[:, :, :n_cols]
    return out.transpose(0, 1, 3, 2).reshape(-1, n_rows, n_cols)


def _rel_bias(rel_table):
    return (rel_table - rel_table[REL_BUCKETS - 1][None, :]).T * LOG2E


def _bias_tables(rel, shifts):
    T = NSA_TILE
    sh_c, sh_s, sh_w = shifts
    r = np.arange(T)[:, None]

    c = np.arange(NEAR_KEYS)[None, :]
    ok = jnp.asarray((r - c + SLC_PAD) >= 0)[None]
    bs = jnp.where(ok, _toeplitz(rel, T, NEAR_KEYS, 1, SLC_PAD) - sh_s, MASK_F32)
    c = np.arange(WIN_KEYS)[None, :]
    d = r - c + WINDOW
    ok = jnp.asarray((d >= 0) & (d < WINDOW))[None]
    bw = jnp.where(ok, _toeplitz(rel, T, WIN_KEYS, 1, WINDOW) - sh_w, MASK_F32)
    r = np.arange(SEL_TILE)[:, None]
    m = np.arange(64)[None, :]
    d = r - CMP_STRIDE * m + BAND_DIST0
    band = (m >= 1) & (m < BAND_SLOTS - 1)
    vals = _toeplitz(rel, SEL_TILE, BAND_SLOTS, CMP_STRIDE, BAND_DIST0, inner=CMP_STRIDE)
    vals = jnp.pad(vals, ((0, 0), (0, 0), (0, 64 - BAND_SLOTS)))
    vals = jnp.where(jnp.asarray(band & (d >= 0))[None], vals, 0.0) - sh_c
    dead = (band & (d < 0)) | (m == BAND_SLOTS - 1)
    vals = jnp.where(jnp.asarray(dead)[None], MASK_BIG, vals)
    hi = vals.astype(BF16)
    lo = (vals - hi.astype(F32)).astype(BF16)
    bcf = jnp.concatenate([hi, lo], axis=-1)
    flat = lambda a: a.reshape(-1, a.shape[-1])
    return flat(bcf), flat(bs).astype(F32), flat(bw).astype(F32)


def _layer(l, x2d, bsz, s, consts, rel_table, norm1_g, w_in, nsa_q_gain, nsa_k_gain, cmp_k, cmp_v,
           ml_conv_w, ml_conv_b, ml_i_bias, ml_f_bias, w_branch_a, w_branch_b, w_out, norm2_g, w_ff1, w_ff2):
    n = bsz * s
    nseg = s // CMP_STRIDE
    oh, c2s = consts
    w_all, w_if = _proj_weights(w_in[l])
    qg_pad = jnp.concatenate([nsa_q_gain[l]] * 2)[None, :]
    kg_pad = jnp.stack([jnp.concatenate([nsa_k_gain[l, 1]] * 2), jnp.concatenate([nsa_k_gain[l, 2]] * 2)])
    (q, ks, kw, kc, vc, vs, vw, gates, gcol, mq, mk, mv, og, mg, gif) = _proj(
        x2d, s, norm1_g[l][None, :], w_all, w_if, qg_pad, kg_pad, ml_conv_w[l], ml_conv_b[l][None, :])

    def compress(a, params, gain, normalize):
        pos, w1, b1, w2, b2 = (p[l] for p in params)
        w1x, posx = _compress_weights(pos, w1)
        return _compress(a.reshape(bsz, nseg, CMP_STRIDE * LANES), w1x, posx, b1[None, :],
                         w2.astype(BF16), b2[None, :], gain[None, :], normalize)

    kcmp = compress(kc, cmp_k, nsa_k_gain[l, 0], True)
    vcmp = compress(vc, cmp_v, jnp.ones((NSA_D,), F32), False)

    qk_bound = lambda kg: 8.0 * LOG2E * jnp.max(jnp.abs(nsa_q_gain[l])) * jnp.max(jnp.abs(kg))
    tab = _rel_bias(rel_table)
    snap = lambda v: v.astype(BF16).astype(F32)
    shifts = [snap(qk_bound(nsa_k_gain[l, j]) + jnp.maximum(jnp.max(tab), 0.0)) for j in range(3)]
    safe = 2.0 * jnp.max(jnp.stack(shifts)) < SAFE_SHIFT_LOG2
    shifts = [jnp.where(safe, sh, 0.0) for sh in shifts]
    bcf, bs, bw = _bias_tables(tab, shifts)

    seq = lambda a: a.reshape(bsz, s, a.shape[-1])
    front = lambda a, p: jnp.pad(seq(a), ((0, 0), (p, 0), (0, 0)))
    operands = (seq(q), front(ks, SLC_PAD), front(vs, SLC_PAD), front(kw, WINDOW), front(vw, WINDOW),
                kcmp, vcmp, oh, seq(gates), bcf, bs, bw, c2s, jnp.full((1, LANES), -shifts[1], F32))
    y_a = lax.cond(safe, lambda: _nsa(False, *operands), lambda: _nsa(True, *operands))

    gate_b = jnp.concatenate([ml_i_bias[l], ml_f_bias[l]])
    gate_b_row = jnp.pad(gate_b, (_IF_LANE, LANES - _IF_LANE - 8))[None, :]
    y_b = _mlstm(seq(mq), seq(mk), seq(mv), seq(og), gif, seq(gcol), gate_b[:, None], gate_b_row)

    out = _merge_ffn(x2d, y_a.reshape(n, 512), y_b.reshape(n, 512), mg, _branch_a_weights(w_branch_a[l]),
                     w_branch_b[l].astype(BF16), w_out[l].astype(BF16), norm2_g[l][None, :],
                     w_ff1[l].astype(BF16), w_ff2[l].astype(BF16))
    return out, y_a, y_b


def _consts(s):
    nseg = s // CMP_STRIDE
    nsel = s // SLC_BLOCK
    blk_of_key = np.arange(s) // SLC_BLOCK
    oh = np.concatenate([np.ones((SLC_PAD, LANES), np.float32),
                         (blk_of_key[:, None] == np.arange(LANES)[None, :]).astype(np.float32)], axis=0)
    ci = np.arange(nseg)[:, None] * CMP_STRIDE
    sj = np.arange(LANES)[None, :] * SLC_BLOCK
    c2s = ((ci < sj + SLC_BLOCK) & (ci + CMP_LEN > sj) & (np.arange(LANES)[None, :] < nsel)
           & (np.arange(nseg)[:, None] < nseg - 1))
    return jnp.asarray(oh, BF16), jnp.asarray(c2s.astype(np.float32), BF16)


def kernel(x, norm1_g, w_in, nsa_q_gain, nsa_k_gain, cmp_k_pos, cmp_k_w1, cmp_k_b1, cmp_k_w2, cmp_k_b2, cmp_v_pos, cmp_v_w1, cmp_v_b1, cmp_v_w2, cmp_v_b2, rel_table, ml_conv_w, ml_conv_b, ml_i_bias, ml_f_bias, w_branch_a, w_branch_b, w_out, norm2_g, w_ff1, w_ff2):
    bsz, s, _ = x.shape
    consts = _consts(s)
    x2d = x.reshape(bsz * s, D_MODEL)
    for l in range(norm1_g.shape[0]):
        x2d, _, _ = _layer(l, x2d, bsz, s, consts, rel_table, norm1_g, w_in, nsa_q_gain, nsa_k_gain,
                           (cmp_k_pos, cmp_k_w1, cmp_k_b1, cmp_k_w2, cmp_k_b2),
                           (cmp_v_pos, cmp_v_w1, cmp_v_b1, cmp_v_w2, cmp_v_b2),
                           ml_conv_w, ml_conv_b, ml_i_bias, ml_f_bias,
                           w_branch_a, w_branch_b, w_out, norm2_g, w_ff1, w_ff2)
    return x2d.reshape(bsz, s, D_MODEL)
```

```python
import functools
import math

import numpy as np
import jax
import jax.numpy as jnp
from jax import lax
from jax.experimental import pallas as pl
from jax.experimental.pallas import tpu as pltpu

F32 = jnp.float32
BF16 = jnp.bfloat16

D_MODEL = 1024
NSA_HEADS = 8
NSA_GROUPS = 2
NSA_HPG = NSA_HEADS // NSA_GROUPS
NSA_D = 64
CMP_LEN = 32
CMP_STRIDE = 16
CMP_HIDDEN = 256
SLC_BLOCK = 64
SLC_TOPK = 16
WINDOW = 512
FORCED_SCORE = 1e4
ML_HEADS = 4
ML_D = 128
ML_CHUNK = 64
CONV_WIDTH = 4
D_FF = 4 * D_MODEL
REL_BUCKETS = 32
REL_MAX_DIST = 128
RMS_EPS = 1e-6

LANES = 128
NSA_TILE = 256
SEL_TILE = 512
SLC_PAD = 128
FAR_CHUNK = 512
NEAR_KEYS = SLC_PAD + NSA_TILE
WIN_KEYS = WINDOW + NSA_TILE
BAND_OFF = 10
BAND_SLOTS = 2 + (SEL_TILE + 128) // CMP_STRIDE
BAND_DIST0 = CMP_STRIDE * BAND_OFF - (CMP_LEN - 1)
MASK_BIG = -1e9
MASK_F32 = -1e30
LOG2E = math.log2(math.e)
SAFE_SHIFT_LOG2 = 50.0
VMEM_LIMIT = 56 * 1024 * 1024

_Q_OFF, _KS_OFF, _KW_OFF, _KC_OFF, _VC_OFF, _VS_OFF, _VW_OFF = 0, 512, 640, 768, 896, 1024, 1152
_GATE_OFF, _MQ_OFF, _MK_OFF, _MV_OFF, _MO_OFF, _MG_OFF, _W_COLS = 1280, 1408, 1920, 2432, 2944, 3456, 5504
_IF_LANE = 24


def _nt(a, b, precision=None):
    return lax.dot_general(a, b, (((1,), (1,)), ((), ())), precision=precision,
                           preferred_element_type=F32)


def _dot(a, b, precision=None):
    return jnp.dot(a, b, precision=precision, preferred_element_type=F32)


def _split3(x, dot_part):
    hi = x.astype(BF16)
    r1 = x - hi.astype(F32)
    mid = r1.astype(BF16)
    lo = (r1 - mid.astype(F32)).astype(BF16)
    return dot_part(hi) + dot_part(mid) + dot_part(lo)


def _const_spec(shape):
    nd = len(shape)
    return pl.BlockSpec(shape, lambda *_: (0,) * nd, pipeline_mode=pl.Buffered(1))


def _proj_kernel(x_ref, g1_ref, w_ref, wif_ref, qg_ref, kg_ref, cw_ref, cb_ref,
                 q_ref, ks_ref, kw_ref, kc_ref, vc_ref, vs_ref, vw_ref, gate_ref, gcol_ref,
                 mq_ref, mk_ref, mv_ref, mo_ref, mg_ref, gif_ref, ext_sc, *, tiles_per_seq):
    tm = x_ref.shape[0]
    x = x_ref[...]
    h = x * lax.rsqrt(jnp.mean(x * x, axis=-1, keepdims=True) + RMS_EPS) * g1_ref[...]
    hb = h.astype(BF16)

    def proj(off, width):
        return _dot(hb, w_ref[:, off:off + width])

    lane = lax.broadcasted_iota(jnp.int32, (1, LANES), 1)
    low = lane < NSA_D

    def half_norm(blk):
        sq = blk * blk
        ms0 = jnp.sum(jnp.where(low, sq, 0.0), axis=-1, keepdims=True) * (1.0 / NSA_D)
        ms1 = jnp.sum(jnp.where(low, 0.0, sq), axis=-1, keepdims=True) * (1.0 / NSA_D)
        return blk * jnp.where(low, lax.rsqrt(ms0 + RMS_EPS), lax.rsqrt(ms1 + RMS_EPS))

    for pair in range(NSA_HPG // 2):
        both = proj(_Q_OFF + 2 * LANES * pair, 2 * LANES)
        for hh in range(2):
            qn = half_norm(both[:, LANES * hh:LANES * (hh + 1)]) * qg_ref[...] * (NSA_D ** -0.5 * LOG2E)
            col = LANES * (2 * pair + hh)
            q_ref[:, col:col + LANES] = qn.astype(q_ref.dtype)

    both = proj(_KS_OFF, 2 * LANES)
    for hh, ref in enumerate((ks_ref, kw_ref)):
        ref[...] = (half_norm(both[:, LANES * hh:LANES * (hh + 1)]) * kg_ref[hh:hh + 1, :]).astype(ref.dtype)

    for off, refs in ((_KC_OFF, (kc_ref, vc_ref)), (_VS_OFF, (vs_ref, vw_ref))):
        both = proj(off, 2 * LANES)
        for hh, ref in enumerate(refs):
            ref[...] = both[:, LANES * hh:LANES * (hh + 1)].astype(ref.dtype)

    slab = proj(_GATE_OFF, LANES)
    gate_ref[...] = jax.nn.sigmoid(slab)
    gcol_ref[...] = slab

    @pl.when(pl.program_id(0) % tiles_per_seq == 0)
    def _():
        ext_sc[0:8, :] = jnp.zeros((8, ext_sc.shape[1]), F32)

    ext_sc[8:, 0:512] = proj(_MQ_OFF, 512)
    ext_sc[8:, 512:1024] = proj(_MK_OFF, 512)
    conv = cb_ref[...]
    for t in range(CONV_WIDTH):
        lo = 8 - (CONV_WIDTH - 1) + t
        conv = conv + ext_sc[lo:lo + tm, :] * cw_ref[t:t + 1, :]
    ext_sc[0:8, :] = ext_sc[tm:tm + 8, :]
    qk = conv * jax.nn.sigmoid(conv)
    mq_ref[...] = (qk[:, :512] * (ML_D ** -0.5)).astype(mq_ref.dtype)
    mk_ref[...] = qk[:, 512:].astype(mk_ref.dtype)

    mv_ref[...] = proj(_MV_OFF, 512).astype(mv_ref.dtype)
    mo_ref[...] = jax.nn.sigmoid(proj(_MO_OFF, 512)).astype(mo_ref.dtype)
    for c in range(4):
        mg_ref[:, 512 * c:512 * (c + 1)] = jax.nn.sigmoid(proj(_MG_OFF + 512 * c, 512)).astype(mg_ref.dtype)
    gif_ref[...] = _nt(wif_ref[...], hb)


def _proj(x2d, seq_len, g1, w_all, w_if, qg_pad, kg_pad, conv_w, conv_b, tm=512):
    n = x2d.shape[0]
    row = lambda w: pl.BlockSpec((tm, w), lambda i: (i, 0))
    out_shapes = (
        jax.ShapeDtypeStruct((n, 512), BF16),
        jax.ShapeDtypeStruct((n, LANES), BF16),
        jax.ShapeDtypeStruct((n, LANES), BF16),
        jax.ShapeDtypeStruct((n, LANES), BF16),
        jax.ShapeDtypeStruct((n, LANES), BF16),
        jax.ShapeDtypeStruct((n, LANES), BF16),
        jax.ShapeDtypeStruct((n, LANES), BF16),
        jax.ShapeDtypeStruct((n, LANES), F32),
        jax.ShapeDtypeStruct((n, LANES), F32),
        jax.ShapeDtypeStruct((n, 512), BF16),
        jax.ShapeDtypeStruct((n, 512), BF16),
        jax.ShapeDtypeStruct((n, 512), BF16),
        jax.ShapeDtypeStruct((n, 512), BF16),
        jax.ShapeDtypeStruct((n, 2048), BF16),
        jax.ShapeDtypeStruct((8, n), F32),
    )
    out_specs = (row(512),) + (row(LANES),) * 8 + (row(512),) * 4 + (row(2048),
                 pl.BlockSpec((8, tm), lambda i: (0, i)))
    return pl.pallas_call(
        functools.partial(_proj_kernel, tiles_per_seq=seq_len // tm),
        grid=(n // tm,),
        in_specs=[row(D_MODEL), _const_spec((1, D_MODEL)), _const_spec((D_MODEL, _W_COLS)),
                  _const_spec((8, D_MODEL)), _const_spec((1, LANES)), _const_spec((2, LANES)),
                  _const_spec(conv_w.shape), _const_spec(conv_b.shape)],
        out_specs=out_specs,
        out_shape=out_shapes,
        scratch_shapes=[pltpu.VMEM((tm + 8, 1024), F32)],
        compiler_params=pltpu.CompilerParams(dimension_semantics=("arbitrary",),
                                             vmem_limit_bytes=VMEM_LIMIT),
        name="proj",
    )(x2d, g1, w_all, w_if, qg_pad, kg_pad, conv_w, conv_b)


def _compress_kernel(seg_ref, w1_ref, pos_ref, b1_ref, w2_ref, b2_ref, gain_ref, out_ref, *, normalize):
    seg = seg_ref[0]
    nseg, width = seg.shape
    c = math.sqrt(2.0 / math.pi)
    pos_lo = jnp.broadcast_to(pos_ref[0:1, :], (8, width))
    pos_hi = jnp.broadcast_to(pos_ref[1:2, :], (8, width))
    outs = []
    for g in range(NSA_GROUPS):
        a = _dot(seg, w1_ref[g, 0])
        b = _dot(seg, w1_ref[g, 1])
        posb = (_dot(pos_lo, w1_ref[g, 0]) + _dot(pos_hi, w1_ref[g, 1]))[0:1] + b1_ref[...]
        pre = a + pltpu.roll(b, nseg - 1, 0) + posb
        hid = 0.5 * pre * (1.0 + jnp.tanh(c * (pre + 0.044715 * (pre * pre * pre))))
        out = _dot(hid.astype(BF16), w2_ref[...]) + b2_ref[...]
        if normalize:
            ms = jnp.mean(out * out, axis=-1, keepdims=True)
            out = out * lax.rsqrt(ms + RMS_EPS) * gain_ref[...]
        outs.append(out)
    out_ref[0] = jnp.concatenate(outs, axis=1).astype(out_ref.dtype)


def _compress(segs, w1x, posx, b1, w2, b2, gain, normalize):
    bsz, nseg, width = segs.shape
    return pl.pallas_call(
        functools.partial(_compress_kernel, normalize=normalize),
        grid=(bsz,),
        in_specs=[pl.BlockSpec((1, nseg, width), lambda b: (b, 0, 0)),
                  _const_spec(w1x.shape), _const_spec(posx.shape), _const_spec(b1.shape),
                  _const_spec(w2.shape), _const_spec(b2.shape), _const_spec(gain.shape)],
        out_specs=pl.BlockSpec((1, nseg, LANES), lambda b: (b, 0, 0)),
        out_shape=jax.ShapeDtypeStruct((bsz, nseg, LANES), BF16),
        compiler_params=pltpu.CompilerParams(dimension_semantics=("parallel",),
                                             vmem_limit_bytes=VMEM_LIMIT),
        name="compress",
    )(segs, w1x, posx, b1, w2, b2, gain)


def _stack_heads(q_ref, lane_q):
    q = q_ref[0]
    return jnp.concatenate(
        [jnp.where((lane_q >= NSA_D) == (g == 1), q[:, LANES * j:LANES * (j + 1)], jnp.zeros((), BF16))
         for g in range(NSA_GROUPS) for j in range(NSA_HPG)], axis=0)


def _softmax_pv(s, v, online):
    if online:
        m = jnp.max(s, axis=1, keepdims=True)
        s = s - jnp.where(m < 0.1 * MASK_BIG, 0.0, m)
    p = jnp.exp2(s)
    acc = _dot(p.astype(BF16), jnp.concatenate([v, jnp.ones(v.shape, BF16)], axis=1))
    return p, acc[:, :LANES], acc[:, LANES:]


def _nsa_select_kernel(q_ref, kc_ref, vc_ref, gate_ref, bcf_ref, c2s_ref, shift_ref,
                       oc_ref, sbn_ref, sbf_ref, *, online):
    i = pl.program_id(1)
    ncmp = kc_ref.shape[1]
    n_tiles = ncmp * CMP_STRIDE // SEL_TILE
    refs = (q_ref, kc_ref, vc_ref, gate_ref, bcf_ref, c2s_ref, shift_ref, oc_ref, sbn_ref, sbf_ref)
    n_cls = max(c for c in (1, 2, 4) if n_tiles % c == 0 and (ncmp // c) % LANES == 0)
    for c in range(n_cls):
        @pl.when(i // (n_tiles // n_cls) == c)
        def _():
            _nsa_select_tile(i, ncmp * (c + 1) // n_cls, LANES * (c + 1) // n_cls, *refs, online)


def _nsa_select_tile(i, ncmp, nblk, q_ref, kc_ref, vc_ref, gate_ref, bcf_ref, c2s_ref, shift_ref,
                     oc_ref, sbn_ref, sbf_ref, online):
    T = SEL_TILE
    grows = NSA_HPG * T
    lane_q = lax.broadcasted_iota(jnp.int32, (T, LANES), 1)
    qpad = _stack_heads(q_ref, lane_q)

    n_io = lax.broadcasted_iota(jnp.int32, (ncmp, LANES), 0)
    l_io = lax.broadcasted_iota(jnp.int32, (ncmp, LANES), 1)
    band = jnp.clip(n_io - i * (T // CMP_STRIDE) + BAND_OFF, 0, BAND_SLOTS - 1)
    oh_c = jnp.where(band == (l_io & 63), 1.0, 0.0).astype(BF16)
    kc_aug = jnp.concatenate([kc_ref[0, :ncmp, :], oh_c], axis=1)
    qc_aug = jnp.concatenate([qpad, bcf_ref[...]], axis=1)
    p_c, num_c, l_c = _softmax_pv(_nt(qc_aug, kc_aug), vc_ref[0, :ncmp, :], online)
    inv_c = 1.0 / jnp.maximum(l_c, 1e-30)
    o_c = num_c * inv_c
    p_c = p_c * jnp.concatenate([inv_c] * (ncmp // LANES), axis=1)

    gates = gate_ref[0]
    gated = [jnp.broadcast_to(gates[:, h:h + 1], (T, LANES)) * o_c[h * T:(h + 1) * T] for h in range(NSA_HEADS)]
    for j in range(NSA_HPG):
        oc_ref[0, :, LANES * j:LANES * (j + 1)] = jnp.where(
            lane_q < NSA_D, gated[j], gated[NSA_HPG + j]).astype(oc_ref.dtype)

    imp_t = []
    for g in range(NSA_GROUPS):
        ps = p_c[g * grows:g * grows + T]
        for hh in range(1, NSA_HPG):
            ps = ps + p_c[g * grows + hh * T:g * grows + (hh + 1) * T]
        ps_hi = ps.astype(BF16)
        ps_lo = (ps - ps_hi.astype(F32)).astype(BF16)
        c2s = c2s_ref[:ncmp, :]
        imp_t.append((_dot(ps_hi, c2s) + _dot(ps_lo, c2s)).T[:nblk])
    width = NSA_GROUPS * T
    j_io = lax.broadcasted_iota(jnp.int32, (nblk, width), 0)
    r_io = lax.broadcasted_iota(jnp.int32, (nblk, width), 1) & (T - 1)
    j_f = j_io.astype(F32)
    cur = i * (T // SLC_BLOCK) + jnp.right_shift(r_io, SLC_BLOCK.bit_length() - 1)
    forced = (j_io == 0) | (j_io == cur) | (j_io == cur - 1)
    score = jnp.where(forced, -jnp.inf, jnp.where(j_io <= cur, jnp.concatenate(imp_t, axis=1), -1.0))
    for _ in range(SLC_TOPK - 3):
        best = jnp.max(score, axis=0, keepdims=True)
        first = jnp.min(jnp.where(score == best, j_f, float(nblk)), axis=0, keepdims=True)
        score = jnp.where(j_f == first, -jnp.inf, score)
    if nblk < LANES:
        score = jnp.concatenate([score, jnp.zeros((LANES - nblk, width), F32)], axis=0)
    near_start = ((i * T + lax.broadcasted_iota(jnp.int32, (T, LANES), 0)) // NSA_TILE) * (NSA_TILE // SLC_BLOCK)
    near_blk = lane_q >= near_start - SLC_PAD // SLC_BLOCK
    far_pick = shift_ref[0:1, 0:1]
    for g in range(NSA_GROUPS):
        picked = score[:, g * T:(g + 1) * T].T == -jnp.inf
        sbn_ref[0, :, LANES * g:LANES * (g + 1)] = jnp.where(picked & near_blk, 0.0, MASK_BIG).astype(BF16)
        sbf_ref[0, :, LANES * g:LANES * (g + 1)] = jnp.where(
            picked & jnp.logical_not(near_blk), far_pick, MASK_BIG).astype(BF16)


def _nsa_attend_kernel(q_ref, ks_ref, vs_ref, kw_ref, vw_ref, oh_ref, gate_ref, oc_ref, sbn_ref, sbf_ref,
                       bs_ref, bw_ref, out_ref, acc_sc, *extra_sc, online):
    i = pl.program_id(1)
    t0 = pl.multiple_of(i * NSA_TILE, NSA_TILE)
    T = NSA_TILE
    rows = NSA_HEADS * T
    grows = NSA_HPG * T
    lane_q = lax.broadcasted_iota(jnp.int32, (T, LANES), 1)
    qpad = _stack_heads(q_ref, lane_q)

    def with_features(feat_ref):
        feats = [jnp.concatenate([feat_ref[0, :, LANES * g:LANES * (g + 1)]] * NSA_HPG, axis=0)
                 for g in range(NSA_GROUPS)]
        return jnp.concatenate([qpad, jnp.concatenate(feats, axis=0)], axis=1)

    q_near, q_far = with_features(sbn_ref), with_features(sbf_ref)

    r_w = lax.broadcasted_iota(jnp.int32, (WIN_KEYS, LANES), 0)
    pad_flag = jnp.where(r_w + (t0 - WINDOW) < 0, 1.0, 0.0).astype(BF16)
    kw_aug = jnp.concatenate([kw_ref[0, pl.ds(t0, WIN_KEYS), :], pad_flag], axis=1)
    qw_aug = jnp.concatenate([qpad, jnp.full((rows, LANES), MASK_BIG / LANES, BF16)], axis=1)
    s_w = _nt(qw_aug, kw_aug) + bw_ref[...]
    _, num_w, l_w = _softmax_pv(s_w, vw_ref[0, pl.ds(t0, WIN_KEYS), :], online)
    o_w = num_w * (1.0 / l_w)

    gates = gate_ref[0]
    gate_tile = lambda col: jnp.broadcast_to(gates[:, col:col + 1], (T, LANES))
    head_rows = [slice(h * T, (h + 1) * T) for h in range(NSA_HEADS)]
    o_gw = [gate_tile(16 + h) * o_w[head_rows[h]] for h in range(NSA_HEADS)]
    g_slc = [gate_tile(8 + h) for h in range(NSA_HEADS)]

    ones_f = jnp.ones((FAR_CHUNK, LANES), BF16)
    k_near = jnp.concatenate([ks_ref[0, pl.ds(t0, NEAR_KEYS), :], oh_ref[pl.ds(t0, NEAR_KEYS), :]], axis=1)
    v_near = jnp.concatenate([vs_ref[0, pl.ds(t0, NEAR_KEYS), :], jnp.ones((NEAR_KEYS, LANES), BF16)], axis=1)
    s_n = _nt(q_near, k_near) + bs_ref[...]
    if online:
        m_sc, = extra_sc
        m_n = jnp.max(s_n, axis=1, keepdims=True)
        m_sc[...] = m_n
        s_n = s_n - m_n
    acc_sc[...] = _dot(jnp.exp2(s_n).astype(BF16), v_near)

    last_chunk = (ks_ref.shape[1] - SLC_PAD) // FAR_CHUNK - 1

    def far_start(c):
        return pl.multiple_of(SLC_PAD + FAR_CHUNK * jnp.minimum(c, last_chunk), LANES)

    def far_scores(c, by_group=False):
        start = far_start(c)
        k_f = jnp.concatenate([ks_ref[0, pl.ds(start, FAR_CHUNK), :], oh_ref[pl.ds(start, FAR_CHUNK), :]], axis=1)
        if by_group:
            return jnp.concatenate([_nt(q_far[:grows], k_f), _nt(q_far[grows:], k_f)], axis=0)
        return _nt(q_far, k_f)

    def far_values(c):
        return jnp.concatenate([vs_ref[0, pl.ds(far_start(c), FAR_CHUNK), :], ones_f], axis=1)

    n_far = (t0 + (FAR_CHUNK - 1 - SLC_PAD)) // FAR_CHUNK
    if online:
        def far_online(c, carry):
            s_f = far_scores(c)
            m_old = m_sc[...]
            m_new = jnp.maximum(m_old, jnp.max(s_f, axis=1, keepdims=True))
            m_sc[...] = m_new
            acc_sc[...] = (jnp.exp2(m_old - m_new) * acc_sc[...]
                           + _dot(jnp.exp2(s_f - m_new).astype(BF16), far_values(c)))
            return carry

        lax.fori_loop(0, n_far, far_online, 0)
    else:
        pa_sc, pb_sc = extra_sc
        pa_sc[...] = jnp.exp2(far_scores(0, by_group=True)).astype(BF16)

        def far_pair(cc, carry):
            c0 = 2 * cc
            pv0 = _dot(pa_sc[...], far_values(c0))
            pb_sc[...] = jnp.exp2(far_scores(c0 + 1)).astype(BF16)
            pv1 = _dot(pb_sc[...], far_values(c0 + 1))
            pa_sc[...] = jnp.exp2(far_scores(c0 + 2)).astype(BF16)
            acc_sc[...] += pv0 + pv1
            return carry

        lax.fori_loop(0, (n_far + 1) // 2, far_pair, 0)

    acc = acc_sc[...]
    o_s = acc[:, :LANES] * (1.0 / acc[:, LANES:])

    head_out = [o_gw[h] + g_slc[h] * o_s[head_rows[h]] for h in range(NSA_HEADS)]
    for j in range(NSA_HPG):
        cols = slice(LANES * j, LANES * (j + 1))
        out_ref[0, :, cols] = (oc_ref[0, :, cols].astype(F32) + jnp.where(
            lane_q < NSA_D, head_out[j], head_out[NSA_HPG + j])).astype(out_ref.dtype)


def _nsa(online, q, ks, vs, kw, vw, kc, vc, oh, gates, bcf, bs, bw, c2s, shift):
    bsz, s = q.shape[:2]
    T = NSA_TILE
    rows = NSA_HEADS * T
    grid = (bsz, s // T)
    params = pltpu.CompilerParams(dimension_semantics=("parallel", "arbitrary"), vmem_limit_bytes=VMEM_LIMIT)
    tile = lambda width: pl.BlockSpec((1, T, width), lambda b, i: (b, i, 0))
    per_b = lambda a: pl.BlockSpec((1,) + a.shape[1:], lambda b, i: (b, 0, 0), pipeline_mode=pl.Buffered(1))
    sel_tile = lambda width: pl.BlockSpec((1, SEL_TILE, width), lambda b, i: (b, i, 0))
    o_cmp, sb_near, sb_far = pl.pallas_call(
        functools.partial(_nsa_select_kernel, online=online),
        grid=(bsz, s // SEL_TILE),
        in_specs=[sel_tile(NSA_HPG * LANES), per_b(kc), per_b(vc), sel_tile(LANES),
                  _const_spec(bcf.shape), _const_spec(c2s.shape), _const_spec(shift.shape)],
        out_specs=(sel_tile(NSA_HPG * LANES), sel_tile(NSA_GROUPS * LANES), sel_tile(NSA_GROUPS * LANES)),
        out_shape=(jax.ShapeDtypeStruct((bsz, s, NSA_HPG * LANES), BF16),
                   jax.ShapeDtypeStruct((bsz, s, NSA_GROUPS * LANES), BF16),
                   jax.ShapeDtypeStruct((bsz, s, NSA_GROUPS * LANES), BF16)),
        compiler_params=params,
        name="nsa_select_online" if online else "nsa_select",
    )(q, kc, vc, gates, bcf, c2s, shift)
    scratch = [pltpu.VMEM((rows, 2 * LANES), F32)]
    if online:
        scratch.append(pltpu.VMEM((rows, 1), F32))
    else:
        scratch += [pltpu.VMEM((rows, FAR_CHUNK), BF16)] * 2
    return pl.pallas_call(
        functools.partial(_nsa_attend_kernel, online=online),
        grid=grid,
        in_specs=[tile(NSA_HPG * LANES), per_b(ks), per_b(vs), per_b(kw), per_b(vw), _const_spec(oh.shape),
                  tile(LANES), tile(NSA_HPG * LANES), tile(NSA_GROUPS * LANES), tile(NSA_GROUPS * LANES),
                  _const_spec(bs.shape), _const_spec(bw.shape)],
        out_specs=tile(NSA_HPG * LANES),
        out_shape=jax.ShapeDtypeStruct((bsz, s, NSA_HPG * LANES), BF16),
        scratch_shapes=scratch,
        compiler_params=params,
        name="nsa_attend_online" if online else "nsa_attend",
    )(q, ks, vs, kw, vw, oh, gates, o_cmp, sb_near, sb_far, bs, bw)


ML_ROWS = 256


def _mlstm_kernel(q_ref, k_ref, v_ref, o_ref, gif_ref, gcol_ref, gb_ref, gbrow_ref, tril_ref, triu_ref,
                  out_ref, c_sc, m_sc):
    j = pl.program_id(1)
    R = ML_ROWS
    L = ML_CHUNK

    @pl.when(j == 0)
    def _():
        c_sc[...] = jnp.zeros_like(c_sc)
        m_sc[...] = jnp.zeros_like(m_sc)

    og = o_ref[0]

    pre = gif_ref[...] + gb_ref[...]
    row8 = lax.broadcasted_iota(jnp.int32, (8, R), 0)
    logf = jnp.minimum(pre, 0.0) - jnp.log(1.0 + jnp.exp(-jnp.abs(pre)))
    g8 = jnp.where(row8 < ML_HEADS, pre, logf)
    pre_c = gcol_ref[0] + gbrow_ref[...]
    lane_c = lax.broadcasted_iota(jnp.int32, pre_c.shape, 1)
    logf_c = jnp.minimum(pre_c, 0.0) - jnp.log(1.0 + jnp.exp(-jnp.abs(pre_c)))
    g_c = jnp.where(lane_c < _IF_LANE + ML_HEADS, pre_c, logf_c)
    cum_c = _split3(g_c, lambda part: _dot(tril_ref[...], part))
    cum_row = _split3(g8, lambda part: _dot(part, triu_ref[...]))

    a_io = lax.broadcasted_iota(jnp.int32, (L, L), 0)
    b_io = lax.broadcasted_iota(jnp.int32, (L, L), 1)
    causal = (b_io <= a_io)[None]
    nc = R // L
    pairs = [(c, h) for c in range(nc) for h in range(ML_HEADS)]

    def blocks(a):
        return jnp.stack([a[c * L:(c + 1) * L, h * ML_D:(h + 1) * ML_D] for c, h in pairs])

    def cols(a, k0):
        return jnp.stack([jnp.broadcast_to(a[c * L:(c + 1) * L, k0 + h:k0 + h + 1], (L, LANES)) for c, h in pairs])

    def rows(a, k0):
        return jnp.stack([a[k0 + h:k0 + h + 1, c * L:(c + 1) * L] for c, h in pairs])

    bdot = lambda eq, x, y: jnp.einsum(eq, x, y, preferred_element_type=F32)
    qb, kb, vb = blocks(q_ref[0]), blocks(k_ref[0]), blocks(v_ref[0])
    b_col, li_col = cols(cum_c, _IF_LANE + ML_HEADS), cols(g_c, _IF_LANE)
    b_row, li_row = rows(cum_row, ML_HEADS), rows(g8, 0)
    gsum = b_row[:, :, L - 1:L]

    s_max = jnp.max(gsum - b_row + li_row, axis=2, keepdims=True)
    m_run = m_sc[:, 0:1, 0:1]
    m_ins, m_outs = [], []
    for c in range(nc):
        hs = slice(c * ML_HEADS, (c + 1) * ML_HEADS)
        m_ins.append(m_run)
        m_run = jnp.maximum(gsum[hs] + m_run, s_max[hs])
        m_outs.append(m_run)
    m_sc[...] = jnp.broadcast_to(m_run, m_sc.shape)
    m_in, m_out = jnp.concatenate(m_ins, axis=0), jnp.concatenate(m_outs, axis=0)

    log_d = jnp.where(causal, b_col[:, :, :L] - b_row + li_row, -jnp.inf)
    inter = b_col + m_in
    m_row = jnp.maximum(inter, jnp.broadcast_to(jnp.max(log_d, axis=2, keepdims=True), inter.shape))
    w = bdot('bik,bjk->bij', qb, kb) * jnp.exp(log_d - m_row[:, :, :L])
    v_aug = jnp.concatenate([vb, jnp.ones(vb.shape, BF16)], axis=2)
    wv = bdot('bij,bjd->bid', w.astype(BF16), v_aug)
    inter_scale = jnp.exp(inter - m_row)
    inter_scale = jnp.concatenate([inter_scale, inter_scale], axis=2)
    floor = jnp.exp(-m_row)
    k_src = (kb.astype(F32) * jnp.exp(gsum - b_col + li_col - m_out)).astype(BF16)
    upd = bdot('bjk,bjd->bkd', k_src, v_aug)
    decay = jnp.exp(gsum + m_in - m_out)

    c_aug = c_sc[...]
    for c in range(nc):
        hs = slice(c * ML_HEADS, (c + 1) * ML_HEADS)
        nd = inter_scale[hs] * bdot('hik,hkd->hid', qb[hs], c_aug.astype(BF16)) + wv[hs]
        hval = nd[:, :, :ML_D] / jnp.maximum(jnp.abs(nd[:, :, ML_D:]), floor[hs])
        for h in range(ML_HEADS):
            rs, cs = slice(c * L, (c + 1) * L), slice(h * ML_D, (h + 1) * ML_D)
            out_ref[0, rs, cs] = (og[rs, cs].astype(F32) * hval[h]).astype(out_ref.dtype)
        c_aug = decay[hs] * c_aug + upd[hs]
    c_sc[...] = c_aug


def _mlstm(mq, mk, mv, og, gif, gcol, gate_b, gate_b_row):
    bsz, s, width = mq.shape
    R = ML_ROWS
    nblk = s // R
    seq = lambda: pl.BlockSpec((1, R, width), lambda b, j: (b, j, 0))
    pos = np.arange(R)
    same_chunk = (pos[:, None] // ML_CHUNK) == (pos[None, :] // ML_CHUNK)
    tril = (same_chunk & (pos[None, :] <= pos[:, None])).astype(np.float32)
    return pl.pallas_call(
        _mlstm_kernel,
        grid=(bsz, nblk),
        in_specs=[seq(), seq(), seq(), seq(),
                  pl.BlockSpec((8, R), lambda b, j: (0, b * nblk + j)),
                  pl.BlockSpec((1, R, LANES), lambda b, j: (b, j, 0)),
                  _const_spec(gate_b.shape), _const_spec(gate_b_row.shape),
                  _const_spec((R, R)), _const_spec((R, R))],
        out_specs=seq(),
        out_shape=jax.ShapeDtypeStruct((bsz, s, width), BF16),
        scratch_shapes=[pltpu.VMEM((ML_HEADS, ML_D, 2 * ML_D), F32),
                        pltpu.VMEM((ML_HEADS, 8, LANES), F32)],
        compiler_params=pltpu.CompilerParams(dimension_semantics=("parallel", "arbitrary"),
                                             vmem_limit_bytes=VMEM_LIMIT),
        name="mlstm",
    )(mq, mk, mv, og, gif, gcol, gate_b, gate_b_row, jnp.asarray(tril, BF16), jnp.asarray(tril.T, BF16))


FF_CHUNK = 512


def _merge_ffn_kernel(x_ref, ya_ref, yb_ref, mg_ref, wa_ref, wb_ref, wo_ref, g2_ref, w1_ref, w2_ref,
                      out_ref):
    mg = mg_ref[...]
    mixed = (mg[:, :D_MODEL].astype(F32) * _dot(ya_ref[...], wa_ref[...])
             + mg[:, D_MODEL:].astype(F32) * _dot(yb_ref[...], wb_ref[...]))
    x1 = x_ref[...] + _dot(mixed.astype(BF16), wo_ref[...])
    h2 = x1 * lax.rsqrt(jnp.mean(x1 * x1, axis=-1, keepdims=True) + RMS_EPS) * g2_ref[...]
    h2 = h2.astype(BF16)
    acc = x1
    for c in range(D_FF // FF_CHUNK):
        a = jnp.maximum(_dot(h2, w1_ref[:, c * FF_CHUNK:(c + 1) * FF_CHUNK]), 0.0)
        acc = acc + _dot((a * a).astype(BF16), w2_ref[c * FF_CHUNK:(c + 1) * FF_CHUNK, :])
    out_ref[...] = acc


def _merge_ffn(x2d, ya, yb, mg, wa, wb, wo, g2, w1, w2, tm=512):
    n = x2d.shape[0]
    row = lambda w: pl.BlockSpec((tm, w), lambda i: (i, 0))
    return pl.pallas_call(
        _merge_ffn_kernel,
        grid=(n // tm,),
        in_specs=[row(D_MODEL), row(512), row(512), row(2048),
                  _const_spec(wa.shape), _const_spec(wb.shape), _const_spec(wo.shape),
                  _const_spec(g2.shape), _const_spec(w1.shape), _const_spec(w2.shape)],
        out_specs=row(D_MODEL),
        out_shape=jax.ShapeDtypeStruct((n, D_MODEL), F32),
        compiler_params=pltpu.CompilerParams(dimension_semantics=("parallel",),
                                             vmem_limit_bytes=VMEM_LIMIT),
        name="merge_ffn",
    )(x2d, ya, yb, mg, wa, wb, wo, g2, w1, w2)


def _proj_weights(w):
    widths = (512, 128, 128, 128, 128, 128, 128, 24, 512, 512, 512, 4, 4, 512, 2048)
    off = np.concatenate([[0], np.cumsum(widths)])
    (nq, nkc, nvc, nks, nvs, nkw, nvw, ngate, mq, mk, mv, mi, mf, mo, mgate) = (int(o) for o in off[:-1])
    col = lambda start, width: w[:, start:start + width]
    parts = [col(nq + NSA_D * (g * NSA_HPG + j), NSA_D) for j in range(NSA_HPG) for g in range(NSA_GROUPS)]
    parts += [col(nks, 128), col(nkw, 128), col(nkc, 128), col(nvc, 128), col(nvs, 128), col(nvw, 128)]
    gate = col(ngate, 24).reshape(-1, NSA_HEADS, 3).transpose(0, 2, 1).reshape(-1, 24)
    parts += [gate, col(mi, 4), col(mf, 4), jnp.zeros((w.shape[0], LANES - _IF_LANE - 8), w.dtype)]
    parts += [col(mq, 512), col(mk, 512), col(mv, 512), col(mo, 512), col(mgate, 2048)]
    w_all = jnp.concatenate(parts, axis=1).astype(BF16)
    w_if = jnp.concatenate([col(mi, 4), col(mf, 4)], axis=1).T.astype(BF16)
    return w_all, w_if


def _branch_a_weights(w):
    head = lambda h: w[NSA_D * h:NSA_D * (h + 1)]
    return jnp.concatenate([head(g * NSA_HPG + j) for j in range(NSA_HPG) for g in range(NSA_GROUPS)],
                           axis=0).astype(BF16)


def _compress_weights(pos, w1):
    r = w1.reshape(2, CMP_STRIDE, 1, NSA_D, CMP_HIDDEN)
    z = jnp.zeros_like(r)
    w1x = jnp.stack([jnp.concatenate([r, z], axis=2), jnp.concatenate([z, r], axis=2)])
    w1x = w1x.reshape(NSA_GROUPS, 2, CMP_STRIDE * NSA_GROUPS * NSA_D, CMP_HIDDEN).astype(BF16)
    posx = jnp.broadcast_to(pos.reshape(2, CMP_STRIDE, 1, NSA_D), (2, CMP_STRIDE, NSA_GROUPS, NSA_D))
    return w1x, posx.reshape(2, CMP_STRIDE * NSA_GROUPS * NSA_D).astype(BF16)


def _t5_bucket(dist):
    n = np.maximum(dist, 0)
    max_exact = REL_BUCKETS // 2
    nf = np.maximum(n, 1).astype(np.float32)
    large = max_exact + (np.log(nf / np.float32(max_exact)) / np.float32(math.log(REL_MAX_DIST / max_exact))
                         * np.float32(REL_BUCKETS - max_exact)).astype(np.int32)
    return np.where(n < max_exact, n, np.minimum(large, REL_BUCKETS - 1))


def _toeplitz(rel, n_rows, n_cols, stride, off, inner=1):
    a_rows = n_rows // inner
    lw = a_rows + n_cols
    k = np.arange(lw)[:, None]
    dist = stride * np.where(k < n_cols, -k, lw - k) + np.arange(inner)[None, :] + off
    live = (dist >= 0) & (k != n_cols)
    onehot = live[..., None] & (_t5_bucket(dist)[..., None] == np.arange(REL_BUCKETS))
    w = jnp.dot(jnp.asarray(onehot.reshape(lw * inner, REL_BUCKETS), F32), rel.T,
                precision=lax.Precision.HIGHEST)
    w = w.T.reshape(-1, lw, inner)
    flat = jnp.tile(w, (1, a_rows, 1))[:, :a_rows * (lw - 1)]
    out = flat.reshape(-1, a_rows, lw - 1, inner)[:, :, :n_cols]
    return out.transpose(0, 1, 3, 2).reshape(-1, n_rows, n_cols)


def _rel_bias(rel_table):
    return (rel_table - rel_table[REL_BUCKETS - 1][None, :]).T * LOG2E


def _bias_tables(rel, shifts):
    T = NSA_TILE
    sh_c, sh_s, sh_w = shifts
    r = np.arange(T)[:, None]

    c = np.arange(NEAR_KEYS)[None, :]
    ok = jnp.asarray((r - c + SLC_PAD) >= 0)[None]
    bs = jnp.where(ok, _toeplitz(rel, T, NEAR_KEYS, 1, SLC_PAD) - sh_s, MASK_F32)
    c = np.arange(WIN_KEYS)[None, :]
    d = r - c + WINDOW
    ok = jnp.asarray((d >= 0) & (d < WINDOW))[None]
    bw = jnp.where(ok, _toeplitz(rel, T, WIN_KEYS, 1, WINDOW) - sh_w, MASK_F32)
    r = np.arange(SEL_TILE)[:, None]
    m = np.arange(64)[None, :]
    d = r - CMP_STRIDE * m + BAND_DIST0
    band = (m >= 1) & (m < BAND_SLOTS - 1)
    vals = _toeplitz(rel, SEL_TILE, BAND_SLOTS, CMP_STRIDE, BAND_DIST0, inner=CMP_STRIDE)
    vals = jnp.pad(vals, ((0, 0), (0, 0), (0, 64 - BAND_SLOTS)))
    vals = jnp.where(jnp.asarray(band & (d >= 0))[None], vals, 0.0) - sh_c
    dead = (band & (d < 0)) | (m == BAND_SLOTS - 1)
    vals = jnp.where(jnp.asarray(dead)[None], MASK_BIG, vals)
    hi = vals.astype(BF16)
    lo = (vals - hi.astype(F32)).astype(BF16)
    bcf = jnp.concatenate([hi, lo], axis=-1)
    flat = lambda a: a.reshape(-1, a.shape[-1])
    return flat(bcf), flat(bs).astype(F32), flat(bw).astype(F32)


def _layer(l, x2d, bsz, s, consts, rel_table, norm1_g, w_in, nsa_q_gain, nsa_k_gain, cmp_k, cmp_v,
           ml_conv_w, ml_conv_b, ml_i_bias, ml_f_bias, w_branch_a, w_branch_b, w_out, norm2_g, w_ff1, w_ff2):
    n = bsz * s
    nseg = s // CMP_STRIDE
    oh, c2s = consts
    w_all, w_if = _proj_weights(w_in[l])
    qg_pad = jnp.concatenate([nsa_q_gain[l]] * 2)[None, :]
    kg_pad = jnp.stack([jnp.concatenate([nsa_k_gain[l, 1]] * 2), jnp.concatenate([nsa_k_gain[l, 2]] * 2)])
    (q, ks, kw, kc, vc, vs, vw, gates, gcol, mq, mk, mv, og, mg, gif) = _proj(
        x2d, s, norm1_g[l][None, :], w_all, w_if, qg_pad, kg_pad, ml_conv_w[l], ml_conv_b[l][None, :])

    def compress(a, params, gain, normalize):
        pos, w1, b1, w2, b2 = (p[l] for p in params)
        w1x, posx = _compress_weights(pos, w1)
        return _compress(a.reshape(bsz, nseg, CMP_STRIDE * LANES), w1x, posx, b1[None, :],
                         w2.astype(BF16), b2[None, :], gain[None, :], normalize)

    kcmp = compress(kc, cmp_k, nsa_k_gain[l, 0], True)
    vcmp = compress(vc, cmp_v, jnp.ones((NSA_D,), F32), False)

    qk_bound = lambda kg: 8.0 * LOG2E * jnp.max(jnp.abs(nsa_q_gain[l])) * jnp.max(jnp.abs(kg))
    tab = _rel_bias(rel_table)
    snap = lambda v: v.astype(BF16).astype(F32)
    shifts = [snap(qk_bound(nsa_k_gain[l, j]) + jnp.maximum(jnp.max(tab), 0.0)) for j in range(3)]
    safe = 2.0 * jnp.max(jnp.stack(shifts)) < SAFE_SHIFT_LOG2
    shifts = [jnp.where(safe, sh, 0.0) for sh in shifts]
    bcf, bs, bw = _bias_tables(tab, shifts)

    seq = lambda a: a.reshape(bsz, s, a.shape[-1])
    front = lambda a, p: jnp.pad(seq(a), ((0, 0), (p, 0), (0, 0)))
    operands = (seq(q), front(ks, SLC_PAD), front(vs, SLC_PAD), front(kw, WINDOW), front(vw, WINDOW),
                kcmp, vcmp, oh, seq(gates), bcf, bs, bw, c2s, jnp.full((1, LANES), -shifts[1], F32))
    y_a = lax.cond(safe, lambda: _nsa(False, *operands), lambda: _nsa(True, *operands))

    gate_b = jnp.concatenate([ml_i_bias[l], ml_f_bias[l]])
    gate_b_row = jnp.pad(gate_b, (_IF_LANE, LANES - _IF_LANE - 8))[None, :]
    y_b = _mlstm(seq(mq), seq(mk), seq(mv), seq(og), gif, seq(gcol), gate_b[:, None], gate_b_row)

    out = _merge_ffn(x2d, y_a.reshape(n, 512), y_b.reshape(n, 512), mg, _branch_a_weights(w_branch_a[l]),
                     w_branch_b[l].astype(BF16), w_out[l].astype(BF16), norm2_g[l][None, :],
                     w_ff1[l].astype(BF16), w_ff2[l].astype(BF16))
    return out, y_a, y_b


def _consts(s):
    nseg = s // CMP_STRIDE
    nsel = s // SLC_BLOCK
    blk_of_key = np.arange(s) // SLC_BLOCK
    oh = np.concatenate([np.ones((SLC_PAD, LANES), np.float32),
                         (blk_of_key[:, None] == np.arange(LANES)[None, :]).astype(np.float32)], axis=0)
    ci = np.arange(nseg)[:, None] * CMP_STRIDE
    sj = np.arange(LANES)[None, :] * SLC_BLOCK
    c2s = ((ci < sj + SLC_BLOCK) & (ci + CMP_LEN > sj) & (np.arange(LANES)[None, :] < nsel)
           & (np.arange(nseg)[:, None] < nseg - 1))
    return jnp.asarray(oh, BF16), jnp.asarray(c2s.astype(np.float32), BF16)


def kernel(x, norm1_g, w_in, nsa_q_gain, nsa_k_gain, cmp_k_pos, cmp_k_w1, cmp_k_b1, cmp_k_w2, cmp_k_b2, cmp_v_pos, cmp_v_w1, cmp_v_b1, cmp_v_w2, cmp_v_b2, rel_table, ml_conv_w, ml_conv_b, ml_i_bias, ml_f_bias, w_branch_a, w_branch_b, w_out, norm2_g, w_ff1, w_ff2):
    bsz, s, _ = x.shape
    consts = _consts(s)
    x2d = x.reshape(bsz * s, D_MODEL)
    for l in range(norm1_g.shape[0]):
        x2d, _, _ = _layer(l, x2d, bsz, s, consts, rel_table, norm1_g, w_in, nsa_q_gain, nsa_k_gain,
                           (cmp_k_pos, cmp_k_w1, cmp_k_b1, cmp_k_w2, cmp_k_b2),
                           (cmp_v_pos, cmp_v_w1, cmp_v_b1, cmp_v_w2, cmp_v_b2),
                           ml_conv_w, ml_conv_b, ml_i_bias, ml_f_bias,
                           w_branch_a, w_branch_b, w_out, norm2_g, w_ff1, w_ff2)
    return x2d.reshape(bsz, s, D_MODEL)
```

```python
import functools
import math

import numpy as np
import jax
import jax.numpy as jnp
from jax import lax
from jax.experimental import pallas as pl
from jax.experimental.pallas import tpu as pltpu

F32 = jnp.float32
BF16 = jnp.bfloat16

D_MODEL = 1024
NSA_HEADS = 8
NSA_GROUPS = 2
NSA_HPG = NSA_HEADS // NSA_GROUPS
NSA_D = 64
CMP_LEN = 32
CMP_STRIDE = 16
CMP_HIDDEN = 256
SLC_BLOCK = 64
SLC_TOPK = 16
WINDOW = 512
FORCED_SCORE = 1e4
ML_HEADS = 4
ML_D = 128
ML_CHUNK = 64
CONV_WIDTH = 4
D_FF = 4 * D_MODEL
REL_BUCKETS = 32
REL_MAX_DIST = 128
RMS_EPS = 1e-6

LANES = 128
NSA_TILE = 256
SEL_TILE = 512
SLC_PAD = 128
FAR_CHUNK = 512
NEAR_KEYS = SLC_PAD + NSA_TILE
WIN_KEYS = WINDOW + NSA_TILE
BAND_OFF = 10
BAND_SLOTS = 2 + (SEL_TILE + 128) // CMP_STRIDE
BAND_DIST0 = CMP_STRIDE * BAND_OFF - (CMP_LEN - 1)
MASK_BIG = -1e9
MASK_F32 = -1e30
LOG2E = math.log2(math.e)
SAFE_SHIFT_LOG2 = 50.0
VMEM_LIMIT = 56 * 1024 * 1024

_Q_OFF, _KS_OFF, _KW_OFF, _KC_OFF, _VC_OFF, _VS_OFF, _VW_OFF = 0, 512, 640, 768, 896, 1024, 1152
_GATE_OFF, _MQ_OFF, _MK_OFF, _MV_OFF, _MO_OFF, _MG_OFF, _W_COLS = 1280, 1408, 1920, 2432, 2944, 3456, 5504
_IF_LANE = 24


def _nt(a, b, precision=None):
    return lax.dot_general(a, b, (((1,), (1,)), ((), ())), precision=precision,
                           preferred_element_type=F32)


def _dot(a, b, precision=None):
    return jnp.dot(a, b, precision=precision, preferred_element_type=F32)


def _split3(x, dot_part):
    hi = x.astype(BF16)
    r1 = x - hi.astype(F32)
    mid = r1.astype(BF16)
    lo = (r1 - mid.astype(F32)).astype(BF16)
    return dot_part(hi) + dot_part(mid) + dot_part(lo)


def _const_spec(shape):
    nd = len(shape)
    return pl.BlockSpec(shape, lambda *_: (0,) * nd, pipeline_mode=pl.Buffered(1))


def _proj_kernel(x_ref, g1_ref, w_ref, wif_ref, qg_ref, kg_ref, cw_ref, cb_ref,
                 q_ref, ks_ref, kw_ref, kc_ref, vc_ref, vs_ref, vw_ref, gate_ref, gcol_ref,
                 mq_ref, mk_ref, mv_ref, mo_ref, mg_ref, gif_ref, ext_sc, *, tiles_per_seq):
    tm = x_ref.shape[0]
    x = x_ref[...]
    h = x * lax.rsqrt(jnp.mean(x * x, axis=-1, keepdims=True) + RMS_EPS) * g1_ref[...]
    hb = h.astype(BF16)

    def proj(off, width):
        return _dot(hb, w_ref[:, off:off + width])

    lane = lax.broadcasted_iota(jnp.int32, (1, LANES), 1)
    low = lane < NSA_D

    def half_norm(blk):
        sq = blk * blk
        ms0 = jnp.sum(jnp.where(low, sq, 0.0), axis=-1, keepdims=True) * (1.0 / NSA_D)
        ms1 = jnp.sum(jnp.where(low, 0.0, sq), axis=-1, keepdims=True) * (1.0 / NSA_D)
        return blk * jnp.where(low, lax.rsqrt(ms0 + RMS_EPS), lax.rsqrt(ms1 + RMS_EPS))

    for pair in range(NSA_HPG // 2):
        both = proj(_Q_OFF + 2 * LANES * pair, 2 * LANES)
        for hh in range(2):
            qn = half_norm(both[:, LANES * hh:LANES * (hh + 1)]) * qg_ref[...] * (NSA_D ** -0.5 * LOG2E)
            col = LANES * (2 * pair + hh)
            q_ref[:, col:col + LANES] = qn.astype(q_ref.dtype)

    both = proj(_KS_OFF, 2 * LANES)
    for hh, ref in enumerate((ks_ref, kw_ref)):
        ref[...] = (half_norm(both[:, LANES * hh:LANES * (hh + 1)]) * kg_ref[hh:hh + 1, :]).astype(ref.dtype)

    for off, refs in ((_KC_OFF, (kc_ref, vc_ref)), (_VS_OFF, (vs_ref, vw_ref))):
        both = proj(off, 2 * LANES)
        for hh, ref in enumerate(refs):
            ref[...] = both[:, LANES * hh:LANES * (hh + 1)].astype(ref.dtype)

    slab = proj(_GATE_OFF, LANES)
    gate_ref[...] = jax.nn.sigmoid(slab)
    gcol_ref[...] = slab

    @pl.when(pl.program_id(0) % tiles_per_seq == 0)
    def _():
        ext_sc[0:8, :] = jnp.zeros((8, ext_sc.shape[1]), F32)

    ext_sc[8:, 0:512] = proj(_MQ_OFF, 512)
    ext_sc[8:, 512:1024] = proj(_MK_OFF, 512)
    conv = cb_ref[...]
    for t in range(CONV_WIDTH):
        lo = 8 - (CONV_WIDTH - 1) + t
        conv = conv + ext_sc[lo:lo + tm, :] * cw_ref[t:t + 1, :]
    ext_sc[0:8, :] = ext_sc[tm:tm + 8, :]
    qk = conv * jax.nn.sigmoid(conv)
    mq_ref[...] = (qk[:, :512] * (ML_D ** -0.5)).astype(mq_ref.dtype)
    mk_ref[...] = qk[:, 512:].astype(mk_ref.dtype)

    mv_ref[...] = proj(_MV_OFF, 512).astype(mv_ref.dtype)
    mo_ref[...] = jax.nn.sigmoid(proj(_MO_OFF, 512)).astype(mo_ref.dtype)
    for c in range(4):
        mg_ref[:, 512 * c:512 * (c + 1)] = jax.nn.sigmoid(proj(_MG_OFF + 512 * c, 512)).astype(mg_ref.dtype)
    gif_ref[...] = _nt(wif_ref[...], hb)


def _proj(x2d, seq_len, g1, w_all, w_if, qg_pad, kg_pad, conv_w, conv_b, tm=512):
    n = x2d.shape[0]
    row = lambda w: pl.BlockSpec((tm, w), lambda i: (i, 0))
    out_shapes = (
        jax.ShapeDtypeStruct((n, 512), BF16),
        jax.ShapeDtypeStruct((n, LANES), BF16),
        jax.ShapeDtypeStruct((n, LANES), BF16),
        jax.ShapeDtypeStruct((n, LANES), BF16),
        jax.ShapeDtypeStruct((n, LANES), BF16),
        jax.ShapeDtypeStruct((n, LANES), BF16),
        jax.ShapeDtypeStruct((n, LANES), BF16),
        jax.ShapeDtypeStruct((n, LANES), F32),
        jax.ShapeDtypeStruct((n, LANES), F32),
        jax.ShapeDtypeStruct((n, 512), BF16),
        jax.ShapeDtypeStruct((n, 512), BF16),
        jax.ShapeDtypeStruct((n, 512), BF16),
        jax.ShapeDtypeStruct((n, 512), BF16),
        jax.ShapeDtypeStruct((n, 2048), BF16),
        jax.ShapeDtypeStruct((8, n), F32),
    )
    out_specs = (row(512),) + (row(LANES),) * 8 + (row(512),) * 4 + (row(2048),
                 pl.BlockSpec((8, tm), lambda i: (0, i)))
    return pl.pallas_call(
        functools.partial(_proj_kernel, tiles_per_seq=seq_len // tm),
        grid=(n // tm,),
        in_specs=[row(D_MODEL), _const_spec((1, D_MODEL)), _const_spec((D_MODEL, _W_COLS)),
                  _const_spec((8, D_MODEL)), _const_spec((1, LANES)), _const_spec((2, LANES)),
                  _const_spec(conv_w.shape), _const_spec(conv_b.shape)],
        out_specs=out_specs,
        out_shape=out_shapes,
        scratch_shapes=[pltpu.VMEM((tm + 8, 1024), F32)],
        compiler_params=pltpu.CompilerParams(dimension_semantics=("arbitrary",),
                                             vmem_limit_bytes=VMEM_LIMIT),
        name="proj",
    )(x2d, g1, w_all, w_if, qg_pad, kg_pad, conv_w, conv_b)


def _compress_kernel(seg_ref, w1_ref, pos_ref, b1_ref, w2_ref, b2_ref, gain_ref, out_ref, *, normalize):
    seg = seg_ref[0]
    nseg, width = seg.shape
    c = math.sqrt(2.0 / math.pi)
    pos_lo = jnp.broadcast_to(pos_ref[0:1, :], (8, width))
    pos_hi = jnp.broadcast_to(pos_ref[1:2, :], (8, width))
    outs = []
    for g in range(NSA_GROUPS):
        a = _dot(seg, w1_ref[g, 0])
        b = _dot(seg, w1_ref[g, 1])
        posb = (_dot(pos_lo, w1_ref[g, 0]) + _dot(pos_hi, w1_ref[g, 1]))[0:1] + b1_ref[...]
        pre = a + pltpu.roll(b, nseg - 1, 0) + posb
        hid = 0.5 * pre * (1.0 + jnp.tanh(c * (pre + 0.044715 * (pre * pre * pre))))
        out = _dot(hid.astype(BF16), w2_ref[...]) + b2_ref[...]
        if normalize:
            ms = jnp.mean(out * out, axis=-1, keepdims=True)
            out = out * lax.rsqrt(ms + RMS_EPS) * gain_ref[...]
        outs.append(out)
    out_ref[0] = jnp.concatenate(outs, axis=1).astype(out_ref.dtype)


def _compress(segs, w1x, posx, b1, w2, b2, gain, normalize):
    bsz, nseg, width = segs.shape
    return pl.pallas_call(
        functools.partial(_compress_kernel, normalize=normalize),
        grid=(bsz,),
        in_specs=[pl.BlockSpec((1, nseg, width), lambda b: (b, 0, 0)),
                  _const_spec(w1x.shape), _const_spec(posx.shape), _const_spec(b1.shape),
                  _const_spec(w2.shape), _const_spec(b2.shape), _const_spec(gain.shape)],
        out_specs=pl.BlockSpec((1, nseg, LANES), lambda b: (b, 0, 0)),
        out_shape=jax.ShapeDtypeStruct((bsz, nseg, LANES), BF16),
        compiler_params=pltpu.CompilerParams(dimension_semantics=("parallel",),
                                             vmem_limit_bytes=VMEM_LIMIT),
        name="compress",
    )(segs, w1x, posx, b1, w2, b2, gain)


def _stack_heads(q_ref, lane_q):
    q = q_ref[0]
    return jnp.concatenate(
        [jnp.where((lane_q >= NSA_D) == (g == 1), q[:, LANES * j:LANES * (j + 1)], jnp.zeros((), BF16))
         for g in range(NSA_GROUPS) for j in range(NSA_HPG)], axis=0)


def _softmax_pv(s, v, online):
    if online:
        m = jnp.max(s, axis=1, keepdims=True)
        s = s - jnp.where(m < 0.1 * MASK_BIG, 0.0, m)
    p = jnp.exp2(s)
    acc = _dot(p.astype(BF16), jnp.concatenate([v, jnp.ones(v.shape, BF16)], axis=1))
    return p, acc[:, :LANES], acc[:, LANES:]


def _nsa_select_kernel(safe_ref, q_ref, kc_ref, vc_ref, gate_ref, bcf_ref, c2s_ref, shift_ref,
                       oc_ref, sbn_ref, sbf_ref):
    i = pl.program_id(1)
    safe = safe_ref[0] != 0
    ncmp = kc_ref.shape[1]
    n_tiles = ncmp * CMP_STRIDE // SEL_TILE
    refs = (q_ref, kc_ref, vc_ref, gate_ref, bcf_ref, c2s_ref, shift_ref, oc_ref, sbn_ref, sbf_ref)
    n_cls = max(c for c in (1, 2, 4) if n_tiles % c == 0 and (ncmp // c) % LANES == 0)
    for c in range(n_cls):
        @pl.when(safe & (i // (n_tiles // n_cls) == c))
        def _():
            _nsa_select_tile(i, ncmp * (c + 1) // n_cls, LANES * (c + 1) // n_cls, *refs, False)

    @pl.when(jnp.logical_not(safe))
    def _():
        _nsa_select_tile(i, ncmp, LANES, *refs, True)


def _nsa_select_tile(i, ncmp, nblk, q_ref, kc_ref, vc_ref, gate_ref, bcf_ref, c2s_ref, shift_ref,
                     oc_ref, sbn_ref, sbf_ref, online):
    T = SEL_TILE
    grows = NSA_HPG * T
    lane_q = lax.broadcasted_iota(jnp.int32, (T, LANES), 1)
    qpad = _stack_heads(q_ref, lane_q)

    n_io = lax.broadcasted_iota(jnp.int32, (ncmp, LANES), 0)
    l_io = lax.broadcasted_iota(jnp.int32, (ncmp, LANES), 1)
    band = jnp.clip(n_io - i * (T // CMP_STRIDE) + BAND_OFF, 0, BAND_SLOTS - 1)
    oh_c = jnp.where(band == (l_io & 63), 1.0, 0.0).astype(BF16)
    kc_aug = jnp.concatenate([kc_ref[0, :ncmp, :], oh_c], axis=1)
    qc_aug = jnp.concatenate([qpad, bcf_ref[...]], axis=1)
    p_c, num_c, l_c = _softmax_pv(_nt(qc_aug, kc_aug), vc_ref[0, :ncmp, :], online)
    inv_c = 1.0 / jnp.maximum(l_c, 1e-30)
    o_c = num_c * inv_c
    p_c = p_c * jnp.concatenate([inv_c] * (ncmp // LANES), axis=1)

    gates = gate_ref[0]
    gated = [jnp.broadcast_to(gates[:, h:h + 1], (T, LANES)) * o_c[h * T:(h + 1) * T] for h in range(NSA_HEADS)]
    for j in range(NSA_HPG):
        oc_ref[0, :, LANES * j:LANES * (j + 1)] = jnp.where(
            lane_q < NSA_D, gated[j], gated[NSA_HPG + j]).astype(oc_ref.dtype)

    imp_t = []
    for g in range(NSA_GROUPS):
        ps = p_c[g * grows:g * grows + T]
        for hh in range(1, NSA_HPG):
            ps = ps + p_c[g * grows + hh * T:g * grows + (hh + 1) * T]
        ps_hi = ps.astype(BF16)
        ps_lo = (ps - ps_hi.astype(F32)).astype(BF16)
        c2s = c2s_ref[:ncmp, :]
        imp_t.append((_dot(ps_hi, c2s) + _dot(ps_lo, c2s)).T[:nblk])
    width = NSA_GROUPS * T
    j_io = lax.broadcasted_iota(jnp.int32, (nblk, width), 0)
    r_io = lax.broadcasted_iota(jnp.int32, (nblk, width), 1) & (T - 1)
    j_f = j_io.astype(F32)
    cur = i * (T // SLC_BLOCK) + jnp.right_shift(r_io, SLC_BLOCK.bit_length() - 1)
    forced = (j_io == 0) | (j_io == cur) | (j_io == cur - 1)
    score = jnp.where(forced, -jnp.inf, jnp.where(j_io <= cur, jnp.concatenate(imp_t, axis=1), -1.0))
    for _ in range(SLC_TOPK - 3):
        best = jnp.max(score, axis=0, keepdims=True)
        first = jnp.min(jnp.where(score == best, j_f, float(nblk)), axis=0, keepdims=True)
        score = jnp.where(j_f == first, -jnp.inf, score)
    if nblk < LANES:
        score = jnp.concatenate([score, jnp.zeros((LANES - nblk, width), F32)], axis=0)
    near_start = ((i * T + lax.broadcasted_iota(jnp.int32, (T, LANES), 0)) // NSA_TILE) * (NSA_TILE // SLC_BLOCK)
    near_blk = lane_q >= near_start - SLC_PAD // SLC_BLOCK
    far_pick = shift_ref[0:1, 0:1]
    for g in range(NSA_GROUPS):
        picked = score[:, g * T:(g + 1) * T].T == -jnp.inf
        sbn_ref[0, :, LANES * g:LANES * (g + 1)] = jnp.where(picked & near_blk, 0.0, MASK_BIG).astype(BF16)
        sbf_ref[0, :, LANES * g:LANES * (g + 1)] = jnp.where(
            picked & jnp.logical_not(near_blk), far_pick, MASK_BIG).astype(BF16)


def _nsa_attend_kernel(safe_ref, *refs):
    *io_refs, acc_sc, m_sc, pa_sc, pb_sc = refs

    @pl.when(safe_ref[0] != 0)
    def _():
        _nsa_attend_tile(*io_refs, acc_sc, pa_sc, pb_sc, online=False)

    @pl.when(safe_ref[0] == 0)
    def _():
        _nsa_attend_tile(*io_refs, acc_sc, m_sc, online=True)


def _nsa_attend_tile(q_ref, ks_ref, vs_ref, kw_ref, vw_ref, oh_ref, gate_ref, oc_ref, sbn_ref, sbf_ref,
                     bs_ref, bw_ref, out_ref, acc_sc, *extra_sc, online):
    i = pl.program_id(1)
    t0 = pl.multiple_of(i * NSA_TILE, NSA_TILE)
    T = NSA_TILE
    rows = NSA_HEADS * T
    grows = NSA_HPG * T
    lane_q = lax.broadcasted_iota(jnp.int32, (T, LANES), 1)
    qpad = _stack_heads(q_ref, lane_q)

    def with_features(feat_ref):
        feats = [jnp.concatenate([feat_ref[0, :, LANES * g:LANES * (g + 1)]] * NSA_HPG, axis=0)
                 for g in range(NSA_GROUPS)]
        return jnp.concatenate([qpad, jnp.concatenate(feats, axis=0)], axis=1)

    q_near, q_far = with_features(sbn_ref), with_features(sbf_ref)

    r_w = lax.broadcasted_iota(jnp.int32, (WIN_KEYS, LANES), 0)
    pad_flag = jnp.where(r_w + (t0 - WINDOW) < 0, 1.0, 0.0).astype(BF16)
    kw_aug = jnp.concatenate([kw_ref[0, pl.ds(t0, WIN_KEYS), :], pad_flag], axis=1)
    qw_aug = jnp.concatenate([qpad, jnp.full((rows, LANES), MASK_BIG / LANES, BF16)], axis=1)
    s_w = _nt(qw_aug, kw_aug) + bw_ref[...]
    _, num_w, l_w = _softmax_pv(s_w, vw_ref[0, pl.ds(t0, WIN_KEYS), :], online)
    o_w = num_w * (1.0 / l_w)

    gates = gate_ref[0]
    gate_tile = lambda col: jnp.broadcast_to(gates[:, col:col + 1], (T, LANES))
    head_rows = [slice(h * T, (h + 1) * T) for h in range(NSA_HEADS)]
    o_gw = [gate_tile(16 + h) * o_w[head_rows[h]] for h in range(NSA_HEADS)]
    g_slc = [gate_tile(8 + h) for h in range(NSA_HEADS)]

    ones_f = jnp.ones((FAR_CHUNK, LANES), BF16)
    k_near = jnp.concatenate([ks_ref[0, pl.ds(t0, NEAR_KEYS), :], oh_ref[pl.ds(t0, NEAR_KEYS), :]], axis=1)
    v_near = jnp.concatenate([vs_ref[0, pl.ds(t0, NEAR_KEYS), :], jnp.ones((NEAR_KEYS, LANES), BF16)], axis=1)
    s_n = _nt(q_near, k_near) + bs_ref[...]
    if online:
        m_sc, = extra_sc
        m_n = jnp.max(s_n, axis=1, keepdims=True)
        m_sc[...] = m_n
        s_n = s_n - m_n
    acc_sc[...] = _dot(jnp.exp2(s_n).astype(BF16), v_near)

    last_chunk = (ks_ref.shape[1] - SLC_PAD) // FAR_CHUNK - 1

    def far_start(c):
        return pl.multiple_of(SLC_PAD + FAR_CHUNK * jnp.minimum(c, last_chunk), LANES)

    def far_scores(c, by_group=False):
        start = far_start(c)
        k_f = jnp.concatenate([ks_ref[0, pl.ds(start, FAR_CHUNK), :], oh_ref[pl.ds(start, FAR_CHUNK), :]], axis=1)
        if by_group:
            return jnp.concatenate([_nt(q_far[:grows], k_f), _nt(q_far[grows:], k_f)], axis=0)
        return _nt(q_far, k_f)

    def far_values(c):
        return jnp.concatenate([vs_ref[0, pl.ds(far_start(c), FAR_CHUNK), :], ones_f], axis=1)

    n_far = (t0 + (FAR_CHUNK - 1 - SLC_PAD)) // FAR_CHUNK
    if online:
        def far_online(c, carry):
            s_f = far_scores(c)
            m_old = m_sc[...]
            m_new = jnp.maximum(m_old, jnp.max(s_f, axis=1, keepdims=True))
            m_sc[...] = m_new
            acc_sc[...] = (jnp.exp2(m_old - m_new) * acc_sc[...]
                           + _dot(jnp.exp2(s_f - m_new).astype(BF16), far_values(c)))
            return carry

        lax.fori_loop(0, n_far, far_online, 0)
    else:
        pa_sc, pb_sc = extra_sc
        pa_sc[...] = jnp.exp2(far_scores(0, by_group=True)).astype(BF16)

        def far_pair(cc, carry):
            c0 = 2 * cc
            pv0 = _dot(pa_sc[...], far_values(c0))
            pb_sc[...] = jnp.exp2(far_scores(c0 + 1)).astype(BF16)
            pv1 = _dot(pb_sc[...], far_values(c0 + 1))
            pa_sc[...] = jnp.exp2(far_scores(c0 + 2)).astype(BF16)
            acc_sc[...] += pv0 + pv1
            return carry

        lax.fori_loop(0, (n_far + 1) // 2, far_pair, 0)

    acc = acc_sc[...]
    o_s = acc[:, :LANES] * (1.0 / acc[:, LANES:])

    head_out = [o_gw[h] + g_slc[h] * o_s[head_rows[h]] for h in range(NSA_HEADS)]
    for j in range(NSA_HPG):
        cols = slice(LANES * j, LANES * (j + 1))
        out_ref[0, :, cols] = (oc_ref[0, :, cols].astype(F32) + jnp.where(
            lane_q < NSA_D, head_out[j], head_out[NSA_HPG + j])).astype(out_ref.dtype)


def _nsa(safe, q, ks, vs, kw, vw, kc, vc, oh, gates, bcf, bs, bw, c2s, shift):
    bsz, s = q.shape[:2]
    flag = pl.BlockSpec(memory_space=pltpu.SMEM)
    T = NSA_TILE
    rows = NSA_HEADS * T
    grid = (bsz, s // T)
    params = pltpu.CompilerParams(dimension_semantics=("parallel", "arbitrary"), vmem_limit_bytes=VMEM_LIMIT)
    tile = lambda width: pl.BlockSpec((1, T, width), lambda b, i: (b, i, 0))
    per_b = lambda a: pl.BlockSpec((1,) + a.shape[1:], lambda b, i: (b, 0, 0), pipeline_mode=pl.Buffered(1))
    sel_tile = lambda width: pl.BlockSpec((1, SEL_TILE, width), lambda b, i: (b, i, 0))
    o_cmp, sb_near, sb_far = pl.pallas_call(
        _nsa_select_kernel,
        grid=(bsz, s // SEL_TILE),
        in_specs=[flag, sel_tile(NSA_HPG * LANES), per_b(kc), per_b(vc), sel_tile(LANES),
                  _const_spec(bcf.shape), _const_spec(c2s.shape), _const_spec(shift.shape)],
        out_specs=(sel_tile(NSA_HPG * LANES), sel_tile(NSA_GROUPS * LANES), sel_tile(NSA_GROUPS * LANES)),
        out_shape=(jax.ShapeDtypeStruct((bsz, s, NSA_HPG * LANES), BF16),
                   jax.ShapeDtypeStruct((bsz, s, NSA_GROUPS * LANES), BF16),
                   jax.ShapeDtypeStruct((bsz, s, NSA_GROUPS * LANES), BF16)),
        compiler_params=params,
        name="nsa_select",
    )(safe, q, kc, vc, gates, bcf, c2s, shift)
    scratch = [pltpu.VMEM((rows, 2 * LANES), F32),
               pltpu.VMEM((rows, 1), F32),
               pltpu.VMEM((rows, FAR_CHUNK), BF16),
               pltpu.VMEM((rows, FAR_CHUNK), BF16)]
    return pl.pallas_call(
        _nsa_attend_kernel,
        grid=grid,
        in_specs=[flag, tile(NSA_HPG * LANES), per_b(ks), per_b(vs), per_b(kw), per_b(vw), _const_spec(oh.shape),
                  tile(LANES), tile(NSA_HPG * LANES), tile(NSA_GROUPS * LANES), tile(NSA_GROUPS * LANES),
                  _const_spec(bs.shape), _const_spec(bw.shape)],
        out_specs=tile(NSA_HPG * LANES),
        out_shape=jax.ShapeDtypeStruct((bsz, s, NSA_HPG * LANES), BF16),
        scratch_shapes=scratch,
        compiler_params=params,
        name="nsa_attend",
    )(safe, q, ks, vs, kw, vw, oh, gates, o_cmp, sb_near, sb_far, bs, bw)


ML_ROWS = 256


def _mlstm_kernel(q_ref, k_ref, v_ref, o_ref, gif_ref, gcol_ref, gb_ref, gbrow_ref, tril_ref, triu_ref,
                  out_ref, c_sc, m_sc):
    j = pl.program_id(1)
    R = ML_ROWS
    L = ML_CHUNK

    @pl.when(j == 0)
    def _():
        c_sc[...] = jnp.zeros_like(c_sc)
        m_sc[...] = jnp.zeros_like(m_sc)

    og = o_ref[0]

    pre = gif_ref[...] + gb_ref[...]
    row8 = lax.broadcasted_iota(jnp.int32, (8, R), 0)
    logf = jnp.minimum(pre, 0.0) - jnp.log(1.0 + jnp.exp(-jnp.abs(pre)))
    g8 = jnp.where(row8 < ML_HEADS, pre, logf)
    pre_c = gcol_ref[0] + gbrow_ref[...]
    lane_c = lax.broadcasted_iota(jnp.int32, pre_c.shape, 1)
    logf_c = jnp.minimum(pre_c, 0.0) - jnp.log(1.0 + jnp.exp(-jnp.abs(pre_c)))
    g_c = jnp.where(lane_c < _IF_LANE + ML_HEADS, pre_c, logf_c)
    cum_c = _split3(g_c, lambda part: _dot(tril_ref[...], part))
    cum_row = _split3(g8, lambda part: _dot(part, triu_ref[...]))

    a_io = lax.broadcasted_iota(jnp.int32, (L, L), 0)
    b_io = lax.broadcasted_iota(jnp.int32, (L, L), 1)
    causal = (b_io <= a_io)[None]
    nc = R // L
    pairs = [(c, h) for c in range(nc) for h in range(ML_HEADS)]

    def blocks(a):
        return jnp.stack([a[c * L:(c + 1) * L, h * ML_D:(h + 1) * ML_D] for c, h in pairs])

    def cols(a, k0):
        return jnp.stack([jnp.broadcast_to(a[c * L:(c + 1) * L, k0 + h:k0 + h + 1], (L, LANES)) for c, h in pairs])

    def rows(a, k0):
        return jnp.stack([a[k0 + h:k0 + h + 1, c * L:(c + 1) * L] for c, h in pairs])

    bdot = lambda eq, x, y: jnp.einsum(eq, x, y, preferred_element_type=F32)
    qb, kb, vb = blocks(q_ref[0]), blocks(k_ref[0]), blocks(v_ref[0])
    b_col, li_col = cols(cum_c, _IF_LANE + ML_HEADS), cols(g_c, _IF_LANE)
    b_row, li_row = rows(cum_row, ML_HEADS), rows(g8, 0)
    gsum = b_row[:, :, L - 1:L]

    s_max = jnp.max(gsum - b_row + li_row, axis=2, keepdims=True)
    m_run = m_sc[:, 0:1, 0:1]
    m_ins, m_outs = [], []
    for c in range(nc):
        hs = slice(c * ML_HEADS, (c + 1) * ML_HEADS)
        m_ins.append(m_run)
        m_run = jnp.maximum(gsum[hs] + m_run, s_max[hs])
        m_outs.append(m_run)
    m_sc[...] = jnp.broadcast_to(m_run, m_sc.shape)
    m_in, m_out = jnp.concatenate(m_ins, axis=0), jnp.concatenate(m_outs, axis=0)

    log_d = jnp.where(causal, b_col[:, :, :L] - b_row + li_row, -jnp.inf)
    inter = b_col + m_in
    m_row = jnp.maximum(inter, jnp.broadcast_to(jnp.max(log_d, axis=2, keepdims=True), inter.shape))
    w = bdot('bik,bjk->bij', qb, kb) * jnp.exp(log_d - m_row[:, :, :L])
    v_aug = jnp.concatenate([vb, jnp.ones(vb.shape, BF16)], axis=2)
    wv = bdot('bij,bjd->bid', w.astype(BF16), v_aug)
    inter_scale = jnp.exp(inter - m_row)
    inter_scale = jnp.concatenate([inter_scale, inter_scale], axis=2)
    floor = jnp.exp(-m_row)
    k_src = (kb.astype(F32) * jnp.exp(gsum - b_col + li_col - m_out)).astype(BF16)
    upd = bdot('bjk,bjd->bkd', k_src, v_aug)
    decay = jnp.exp(gsum + m_in - m_out)

    c_aug = c_sc[...]
    for c in range(nc):
        hs = slice(c * ML_HEADS, (c + 1) * ML_HEADS)
        nd = inter_scale[hs] * bdot('hik,hkd->hid', qb[hs], c_aug.astype(BF16)) + wv[hs]
        hval = nd[:, :, :ML_D] / jnp.maximum(jnp.abs(nd[:, :, ML_D:]), floor[hs])
        for h in range(ML_HEADS):
            rs, cs = slice(c * L, (c + 1) * L), slice(h * ML_D, (h + 1) * ML_D)
            out_ref[0, rs, cs] = (og[rs, cs].astype(F32) * hval[h]).astype(out_ref.dtype)
        c_aug = decay[hs] * c_aug + upd[hs]
    c_sc[...] = c_aug


def _mlstm(mq, mk, mv, og, gif, gcol, gate_b, gate_b_row):
    bsz, s, width = mq.shape
    R = ML_ROWS
    nblk = s // R
    seq = lambda: pl.BlockSpec((1, R, width), lambda b, j: (b, j, 0))
    pos = np.arange(R)
    same_chunk = (pos[:, None] // ML_CHUNK) == (pos[None, :] // ML_CHUNK)
    tril = (same_chunk & (pos[None, :] <= pos[:, None])).astype(np.float32)
    return pl.pallas_call(
        _mlstm_kernel,
        grid=(bsz, nblk),
        in_specs=[seq(), seq(), seq(), seq(),
                  pl.BlockSpec((8, R), lambda b, j: (0, b * nblk + j)),
                  pl.BlockSpec((1, R, LANES), lambda b, j: (b, j, 0)),
                  _const_spec(gate_b.shape), _const_spec(gate_b_row.shape),
                  _const_spec((R, R)), _const_spec((R, R))],
        out_specs=seq(),
        out_shape=jax.ShapeDtypeStruct((bsz, s, width), BF16),
        scratch_shapes=[pltpu.VMEM((ML_HEADS, ML_D, 2 * ML_D), F32),
                        pltpu.VMEM((ML_HEADS, 8, LANES), F32)],
        compiler_params=pltpu.CompilerParams(dimension_semantics=("parallel", "arbitrary"),
                                             vmem_limit_bytes=VMEM_LIMIT),
        name="mlstm",
    )(mq, mk, mv, og, gif, gcol, gate_b, gate_b_row, jnp.asarray(tril, BF16), jnp.asarray(tril.T, BF16))


FF_CHUNK = 512


def _merge_ffn_kernel(x_ref, ya_ref, yb_ref, mg_ref, wa_ref, wb_ref, wo_ref, g2_ref, w1_ref, w2_ref,
                      out_ref):
    mg = mg_ref[...]
    mixed = (mg[:, :D_MODEL].astype(F32) * _dot(ya_ref[...], wa_ref[...])
             + mg[:, D_MODEL:].astype(F32) * _dot(yb_ref[...], wb_ref[...]))
    x1 = x_ref[...] + _dot(mixed.astype(BF16), wo_ref[...])
    h2 = x1 * lax.rsqrt(jnp.mean(x1 * x1, axis=-1, keepdims=True) + RMS_EPS) * g2_ref[...]
    h2 = h2.astype(BF16)
    acc = x1
    for c in range(D_FF // FF_CHUNK):
        a = jnp.maximum(_dot(h2, w1_ref[:, c * FF_CHUNK:(c + 1) * FF_CHUNK]), 0.0)
        acc = acc + _dot((a * a).astype(BF16), w2_ref[c * FF_CHUNK:(c + 1) * FF_CHUNK, :])
    out_ref[...] = acc


def _merge_ffn(x2d, ya, yb, mg, wa, wb, wo, g2, w1, w2, tm=512):
    n = x2d.shape[0]
    row = lambda w: pl.BlockSpec((tm, w), lambda i: (i, 0))
    return pl.pallas_call(
        _merge_ffn_kernel,
        grid=(n // tm,),
        in_specs=[row(D_MODEL), row(512), row(512), row(2048),
                  _const_spec(wa.shape), _const_spec(wb.shape), _const_spec(wo.shape),
                  _const_spec(g2.shape), _const_spec(w1.shape), _const_spec(w2.shape)],
        out_specs=row(D_MODEL),
        out_shape=jax.ShapeDtypeStruct((n, D_MODEL), F32),
        compiler_params=pltpu.CompilerParams(dimension_semantics=("parallel",),
                                             vmem_limit_bytes=VMEM_LIMIT),
        name="merge_ffn",
    )(x2d, ya, yb, mg, wa, wb, wo, g2, w1, w2)


def _proj_weights(w):
    widths = (512, 128, 128, 128, 128, 128, 128, 24, 512, 512, 512, 4, 4, 512, 2048)
    off = np.concatenate([[0], np.cumsum(widths)])
    (nq, nkc, nvc, nks, nvs, nkw, nvw, ngate, mq, mk, mv, mi, mf, mo, mgate) = (int(o) for o in off[:-1])
    col = lambda start, width: w[:, start:start + width]
    parts = [col(nq + NSA_D * (g * NSA_HPG + j), NSA_D) for j in range(NSA_HPG) for g in range(NSA_GROUPS)]
    parts += [col(nks, 128), col(nkw, 128), col(nkc, 128), col(nvc, 128), col(nvs, 128), col(nvw, 128)]
    gate = col(ngate, 24).reshape(-1, NSA_HEADS, 3).transpose(0, 2, 1).reshape(-1, 24)
    parts += [gate, col(mi, 4), col(mf, 4), jnp.zeros((w.shape[0], LANES - _IF_LANE - 8), w.dtype)]
    parts += [col(mq, 512), col(mk, 512), col(mv, 512), col(mo, 512), col(mgate, 2048)]
    w_all = jnp.concatenate(parts, axis=1).astype(BF16)
    w_if = jnp.concatenate([col(mi, 4), col(mf, 4)], axis=1).T.astype(BF16)
    return w_all, w_if


def _branch_a_weights(w):
    head = lambda h: w[NSA_D * h:NSA_D * (h + 1)]
    return jnp.concatenate([head(g * NSA_HPG + j) for j in range(NSA_HPG) for g in range(NSA_GROUPS)],
                           axis=0).astype(BF16)


def _compress_weights(pos, w1):
    r = w1.reshape(2, CMP_STRIDE, 1, NSA_D, CMP_HIDDEN)
    z = jnp.zeros_like(r)
    w1x = jnp.stack([jnp.concatenate([r, z], axis=2), jnp.concatenate([z, r], axis=2)])
    w1x = w1x.reshape(NSA_GROUPS, 2, CMP_STRIDE * NSA_GROUPS * NSA_D, CMP_HIDDEN).astype(BF16)
    posx = jnp.broadcast_to(pos.reshape(2, CMP_STRIDE, 1, NSA_D), (2, CMP_STRIDE, NSA_GROUPS, NSA_D))
    return w1x, posx.reshape(2, CMP_STRIDE * NSA_GROUPS * NSA_D).astype(BF16)


def _t5_bucket(dist):
    n = np.maximum(dist, 0)
    max_exact = REL_BUCKETS // 2
    nf = np.maximum(n, 1).astype(np.float32)
    large = max_exact + (np.log(nf / np.float32(max_exact)) / np.float32(math.log(REL_MAX_DIST / max_exact))
                         * np.float32(REL_BUCKETS - max_exact)).astype(np.int32)
    return np.where(n < max_exact, n, np.minimum(large, REL_BUCKETS - 1))


def _toeplitz(rel, n_rows, n_cols, stride, off, inner=1):
    a_rows = n_rows // inner
    lw = a_rows + n_cols
    k = np.arange(lw)[:, None]
    dist = stride * np.where(k < n_cols, -k, lw - k) + np.arange(inner)[None, :] + off
    live = (dist >= 0) & (k != n_cols)
    onehot = live[..., None] & (_t5_bucket(dist)[..., None] == np.arange(REL_BUCKETS))
    w = jnp.dot(jnp.asarray(onehot.reshape(lw * inner, REL_BUCKETS), F32), rel.T,
                precision=lax.Precision.HIGHEST)
    w = w.T.reshape(-1, lw, inner)
    flat = jnp.tile(w, (1, a_rows, 1))[:, :a_rows * (lw - 1)]
    out = flat.reshape(-1, a_rows, lw - 1, inner)[:, :, :n_cols]
    return out.transpose(0, 1, 3, 2).reshape(-1, n_rows, n_cols)


def _rel_bias(rel_table):
    return (rel_table - rel_table[REL_BUCKETS - 1][None, :]).T * LOG2E


def _bias_tables(rel, shifts):
    T = NSA_TILE
    sh_c, sh_s, sh_w = shifts
    r = np.arange(T)[:, None]

    c = np.arange(NEAR_KEYS)[None, :]
    ok = jnp.asarray((r - c + SLC_PAD) >= 0)[None]
    bs = jnp.where(ok, _toeplitz(rel, T, NEAR_KEYS, 1, SLC_PAD) - sh_s, MASK_F32)
    c = np.arange(WIN_KEYS)[None, :]
    d = r - c + WINDOW
    ok = jnp.asarray((d >= 0) & (d < WINDOW))[None]
    bw = jnp.where(ok, _toeplitz(rel, T, WIN_KEYS, 1, WINDOW) - sh_w, MASK_F32)
    r = np.arange(SEL_TILE)[:, None]
    m = np.arange(64)[None, :]
    d = r - CMP_STRIDE * m + BAND_DIST0
    band = (m >= 1) & (m < BAND_SLOTS - 1)
    vals = _toeplitz(rel, SEL_TILE, BAND_SLOTS, CMP_STRIDE, BAND_DIST0, inner=CMP_STRIDE)
    vals = jnp.pad(vals, ((0, 0), (0, 0), (0, 64 - BAND_SLOTS)))
    vals = jnp.where(jnp.asarray(band & (d >= 0))[None], vals, 0.0) - sh_c
    dead = (band & (d < 0)) | (m == BAND_SLOTS - 1)
    vals = jnp.where(jnp.asarray(dead)[None], MASK_BIG, vals)
    hi = vals.astype(BF16)
    lo = (vals - hi.astype(F32)).astype(BF16)
    bcf = jnp.concatenate([hi, lo], axis=-1)
    flat = lambda a: a.reshape(-1, a.shape[-1])
    return flat(bcf), flat(bs).astype(F32), flat(bw).astype(F32)


def _layer(l, x2d, bsz, s, consts, rel_table, norm1_g, w_in, nsa_q_gain, nsa_k_gain, cmp_k, cmp_v,
           ml_conv_w, ml_conv_b, ml_i_bias, ml_f_bias, w_branch_a, w_branch_b, w_out, norm2_g, w_ff1, w_ff2):
    n = bsz * s
    nseg = s // CMP_STRIDE
    oh, c2s = consts
    w_all, w_if = _proj_weights(w_in[l])
    qg_pad = jnp.concatenate([nsa_q_gain[l]] * 2)[None, :]
    kg_pad = jnp.stack([jnp.concatenate([nsa_k_gain[l, 1]] * 2), jnp.concatenate([nsa_k_gain[l, 2]] * 2)])
    (q, ks, kw, kc, vc, vs, vw, gates, gcol, mq, mk, mv, og, mg, gif) = _proj(
        x2d, s, norm1_g[l][None, :], w_all, w_if, qg_pad, kg_pad, ml_conv_w[l], ml_conv_b[l][None, :])

    def compress(a, params, gain, normalize):
        pos, w1, b1, w2, b2 = (p[l] for p in params)
        w1x, posx = _compress_weights(pos, w1)
        return _compress(a.reshape(bsz, nseg, CMP_STRIDE * LANES), w1x, posx, b1[None, :],
                         w2.astype(BF16), b2[None, :], gain[None, :], normalize)

    kcmp = compress(kc, cmp_k, nsa_k_gain[l, 0], True)
    vcmp = compress(vc, cmp_v, jnp.ones((NSA_D,), F32), False)

    qk_bound = lambda kg: 8.0 * LOG2E * jnp.max(jnp.abs(nsa_q_gain[l])) * jnp.max(jnp.abs(kg))
    tab = _rel_bias(rel_table)
    snap = lambda v: v.astype(BF16).astype(F32)
    shifts = [snap(qk_bound(nsa_k_gain[l, j]) + jnp.maximum(jnp.max(tab), 0.0)) for j in range(3)]
    safe = 2.0 * jnp.max(jnp.stack(shifts)) < SAFE_SHIFT_LOG2
    shifts = [jnp.where(safe, sh, 0.0) for sh in shifts]
    bcf, bs, bw = _bias_tables(tab, shifts)

    seq = lambda a: a.reshape(bsz, s, a.shape[-1])
    front = lambda a, p: jnp.pad(seq(a), ((0, 0), (p, 0), (0, 0)))
    operands = (seq(q), front(ks, SLC_PAD), front(vs, SLC_PAD), front(kw, WINDOW), front(vw, WINDOW),
                kcmp, vcmp, oh, seq(gates), bcf, bs, bw, c2s, jnp.full((1, LANES), -shifts[1], F32))
    y_a = _nsa(safe.astype(jnp.int32)[None], *operands)

    gate_b = jnp.concatenate([ml_i_bias[l], ml_f_bias[l]])
    gate_b_row = jnp.pad(gate_b, (_IF_LANE, LANES - _IF_LANE - 8))[None, :]
    y_b = _mlstm(seq(mq), seq(mk), seq(mv), seq(og), gif, seq(gcol), gate_b[:, None], gate_b_row)

    out = _merge_ffn(x2d, y_a.reshape(n, 512), y_b.reshape(n, 512), mg, _branch_a_weights(w_branch_a[l]),
                     w_branch_b[l].astype(BF16), w_out[l].astype(BF16), norm2_g[l][None, :],
                     w_ff1[l].astype(BF16), w_ff2[l].astype(BF16))
    return out, y_a, y_b


def _consts(s):
    nseg = s // CMP_STRIDE
    nsel = s // SLC_BLOCK
    blk_of_key = np.arange(s) // SLC_BLOCK
    oh = np.concatenate([np.ones((SLC_PAD, LANES), np.float32),
                         (blk_of_key[:, None] == np.arange(LANES)[None, :]).astype(np.float32)], axis=0)
    ci = np.arange(nseg)[:, None] * CMP_STRIDE
    sj = np.arange(LANES)[None, :] * SLC_BLOCK
    c2s = ((ci < sj + SLC_BLOCK) & (ci + CMP_LEN > sj) & (np.arange(LANES)[None, :] < nsel)
           & (np.arange(nseg)[:, None] < nseg - 1))
    return jnp.asarray(oh, BF16), jnp.asarray(c2s.astype(np.float32), BF16)


def kernel(x, norm1_g, w_in, nsa_q_gain, nsa_k_gain, cmp_k_pos, cmp_k_w1, cmp_k_b1, cmp_k_w2, cmp_k_b2, cmp_v_pos, cmp_v_w1, cmp_v_b1, cmp_v_w2, cmp_v_b2, rel_table, ml_conv_w, ml_conv_b, ml_i_bias, ml_f_bias, w_branch_a, w_branch_b, w_out, norm2_g, w_ff1, w_ff2):
    bsz, s, _ = x.shape
    consts = _consts(s)
    x2d = x.reshape(bsz * s, D_MODEL)
    for l in range(norm1_g.shape[0]):
        x2d, _, _ = _layer(l, x2d, bsz, s, consts, rel_table, norm1_g, w_in, nsa_q_gain, nsa_k_gain,
                           (cmp_k_pos, cmp_k_w1, cmp_k_b1, cmp_k_w2, cmp_k_b2),
                           (cmp_v_pos, cmp_v_w1, cmp_v_b1, cmp_v_w2, cmp_v_b2),
                           ml_conv_w, ml_conv_b, ml_i_bias, ml_f_bias,
                           w_branch_a, w_branch_b, w_out, norm2_g, w_ff1, w_ff2)
    return x2d.reshape(bsz, s, D_MODEL)
```

```python
import functools
import math

import numpy as np
import jax
import jax.numpy as jnp
from jax import lax
from jax.experimental import pallas as pl
from jax.experimental.pallas import tpu as pltpu

F32 = jnp.float32
BF16 = jnp.bfloat16

D_MODEL = 1024
NSA_HEADS = 8
NSA_GROUPS = 2
NSA_HPG = NSA_HEADS // NSA_GROUPS
NSA_D = 64
CMP_LEN = 32
CMP_STRIDE = 16
CMP_HIDDEN = 256
SLC_BLOCK = 64
SLC_TOPK = 16
WINDOW = 512
FORCED_SCORE = 1e4
ML_HEADS = 4
ML_D = 128
ML_CHUNK = 64
CONV_WIDTH = 4
D_FF = 4 * D_MODEL
REL_BUCKETS = 32
REL_MAX_DIST = 128
RMS_EPS = 1e-6

LANES = 128
NSA_TILE = 256
SEL_TILE = 512
SLC_PAD = 128
FAR_CHUNK = 512
NEAR_KEYS = SLC_PAD + NSA_TILE
WIN_KEYS = WINDOW + NSA_TILE
BAND_OFF = 10
BAND_SLOTS = 2 + (SEL_TILE + 128) // CMP_STRIDE
BAND_DIST0 = CMP_STRIDE * BAND_OFF - (CMP_LEN - 1)
MASK_BIG = -1e9
MASK_F32 = -1e30
LOG2E = math.log2(math.e)
SAFE_SHIFT_LOG2 = 50.0
VMEM_LIMIT = 56 * 1024 * 1024

_Q_OFF, _KS_OFF, _KW_OFF, _KC_OFF, _VC_OFF, _VS_OFF, _VW_OFF = 0, 512, 640, 768, 896, 1024, 1152
_GATE_OFF, _MQ_OFF, _MK_OFF, _MV_OFF, _MO_OFF, _MG_OFF, _W_COLS = 1280, 1408, 1920, 2432, 2944, 3456, 5504
_IF_LANE = 24


def _nt(a, b, precision=None):
    return lax.dot_general(a, b, (((1,), (1,)), ((), ())), precision=precision,
                           preferred_element_type=F32)


def _dot(a, b, precision=None):
    return jnp.dot(a, b, precision=precision, preferred_element_type=F32)


def _split3(x, dot_part):
    hi = x.astype(BF16)
    r1 = x - hi.astype(F32)
    mid = r1.astype(BF16)
    lo = (r1 - mid.astype(F32)).astype(BF16)
    return dot_part(hi) + dot_part(mid) + dot_part(lo)


def _const_spec(shape):
    nd = len(shape)
    return pl.BlockSpec(shape, lambda *_: (0,) * nd, pipeline_mode=pl.Buffered(1))


def _proj_kernel(x_ref, g1_ref, w_ref, wif_ref, qg_ref, kg_ref, cw_ref, cb_ref,
                 q_ref, ks_ref, kw_ref, kc_ref, vc_ref, vs_ref, vw_ref, gate_ref, gcol_ref,
                 mq_ref, mk_ref, mv_ref, mo_ref, mg_ref, gif_ref, ext_sc, *, tiles_per_seq):
    tm = x_ref.shape[0]
    x = x_ref[...]
    h = x * lax.rsqrt(jnp.mean(x * x, axis=-1, keepdims=True) + RMS_EPS) * g1_ref[...]
    hb = h.astype(BF16)

    def proj(off, width):
        return _dot(hb, w_ref[:, off:off + width])

    lane = lax.broadcasted_iota(jnp.int32, (1, LANES), 1)
    low = lane < NSA_D

    def half_norm(blk):
        sq = blk * blk
        ms0 = jnp.sum(jnp.where(low, sq, 0.0), axis=-1, keepdims=True) * (1.0 / NSA_D)
        ms1 = jnp.sum(jnp.where(low, 0.0, sq), axis=-1, keepdims=True) * (1.0 / NSA_D)
        return blk * jnp.where(low, lax.rsqrt(ms0 + RMS_EPS), lax.rsqrt(ms1 + RMS_EPS))

    for pair in range(NSA_HPG // 2):
        both = proj(_Q_OFF + 2 * LANES * pair, 2 * LANES)
        for hh in range(2):
            qn = half_norm(both[:, LANES * hh:LANES * (hh + 1)]) * qg_ref[...] * (NSA_D ** -0.5 * LOG2E)
            col = LANES * (2 * pair + hh)
            q_ref[:, col:col + LANES] = qn.astype(q_ref.dtype)

    both = proj(_KS_OFF, 2 * LANES)
    for hh, ref in enumerate((ks_ref, kw_ref)):
        ref[...] = (half_norm(both[:, LANES * hh:LANES * (hh + 1)]) * kg_ref[hh:hh + 1, :]).astype(ref.dtype)

    for off, refs in ((_KC_OFF, (kc_ref, vc_ref)), (_VS_OFF, (vs_ref, vw_ref))):
        both = proj(off, 2 * LANES)
        for hh, ref in enumerate(refs):
            ref[...] = both[:, LANES * hh:LANES * (hh + 1)].astype(ref.dtype)

    slab = proj(_GATE_OFF, LANES)
    gate_ref[...] = jax.nn.sigmoid(slab)
    gcol_ref[...] = slab

    @pl.when(pl.program_id(0) % tiles_per_seq == 0)
    def _():
        ext_sc[0:8, :] = jnp.zeros((8, ext_sc.shape[1]), F32)

    ext_sc[8:, 0:512] = proj(_MQ_OFF, 512)
    ext_sc[8:, 512:1024] = proj(_MK_OFF, 512)
    conv = cb_ref[...]
    for t in range(CONV_WIDTH):
        lo = 8 - (CONV_WIDTH - 1) + t
        conv = conv + ext_sc[lo:lo + tm, :] * cw_ref[t:t + 1, :]
    ext_sc[0:8, :] = ext_sc[tm:tm + 8, :]
    qk = conv * jax.nn.sigmoid(conv)
    mq_ref[...] = (qk[:, :512] * (ML_D ** -0.5)).astype(mq_ref.dtype)
    mk_ref[...] = qk[:, 512:].astype(mk_ref.dtype)

    mv_ref[...] = proj(_MV_OFF, 512).astype(mv_ref.dtype)
    mo_ref[...] = jax.nn.sigmoid(proj(_MO_OFF, 512)).astype(mo_ref.dtype)
    for c in range(4):
        mg_ref[:, 512 * c:512 * (c + 1)] = jax.nn.sigmoid(proj(_MG_OFF + 512 * c, 512)).astype(mg_ref.dtype)
    gif_ref[...] = _nt(wif_ref[...], hb)


def _proj(x2d, seq_len, g1, w_all, w_if, qg_pad, kg_pad, conv_w, conv_b, tm=512):
    n = x2d.shape[0]
    row = lambda w: pl.BlockSpec((tm, w), lambda i: (i, 0))
    out_shapes = (
        jax.ShapeDtypeStruct((n, 512), BF16),
        jax.ShapeDtypeStruct((n, LANES), BF16),
        jax.ShapeDtypeStruct((n, LANES), BF16),
        jax.ShapeDtypeStruct((n, LANES), BF16),
        jax.ShapeDtypeStruct((n, LANES), BF16),
        jax.ShapeDtypeStruct((n, LANES), BF16),
        jax.ShapeDtypeStruct((n, LANES), BF16),
        jax.ShapeDtypeStruct((n, LANES), F32),
        jax.ShapeDtypeStruct((n, LANES), F32),
        jax.ShapeDtypeStruct((n, 512), BF16),
        jax.ShapeDtypeStruct((n, 512), BF16),
        jax.ShapeDtypeStruct((n, 512), BF16),
        jax.ShapeDtypeStruct((n, 512), BF16),
        jax.ShapeDtypeStruct((n, 2048), BF16),
        jax.ShapeDtypeStruct((8, n), F32),
    )
    out_specs = (row(512),) + (row(LANES),) * 8 + (row(512),) * 4 + (row(2048),
                 pl.BlockSpec((8, tm), lambda i: (0, i)))
    return pl.pallas_call(
        functools.partial(_proj_kernel, tiles_per_seq=seq_len // tm),
        grid=(n // tm,),
        in_specs=[row(D_MODEL), _const_spec((1, D_MODEL)), _const_spec((D_MODEL, _W_COLS)),
                  _const_spec((8, D_MODEL)), _const_spec((1, LANES)), _const_spec((2, LANES)),
                  _const_spec(conv_w.shape), _const_spec(conv_b.shape)],
        out_specs=out_specs,
        out_shape=out_shapes,
        scratch_shapes=[pltpu.VMEM((tm + 8, 1024), F32)],
        compiler_params=pltpu.CompilerParams(dimension_semantics=("arbitrary",),
                                             vmem_limit_bytes=VMEM_LIMIT),
        name="proj",
    )(x2d, g1, w_all, w_if, qg_pad, kg_pad, conv_w, conv_b)


def _compress_kernel(seg_ref, w1_ref, pos_ref, b1_ref, w2_ref, b2_ref, gain_ref, out_ref, *, normalize):
    seg = seg_ref[0]
    nseg, width = seg.shape
    c = math.sqrt(2.0 / math.pi)
    pos_lo = jnp.broadcast_to(pos_ref[0:1, :], (8, width))
    pos_hi = jnp.broadcast_to(pos_ref[1:2, :], (8, width))
    outs = []
    for g in range(NSA_GROUPS):
        a = _dot(seg, w1_ref[g, 0])
        b = _dot(seg, w1_ref[g, 1])
        posb = (_dot(pos_lo, w1_ref[g, 0]) + _dot(pos_hi, w1_ref[g, 1]))[0:1] + b1_ref[...]
        pre = a + pltpu.roll(b, nseg - 1, 0) + posb
        hid = 0.5 * pre * (1.0 + jnp.tanh(c * (pre + 0.044715 * (pre * pre * pre))))
        out = _dot(hid.astype(BF16), w2_ref[...]) + b2_ref[...]
        if normalize:
            ms = jnp.mean(out * out, axis=-1, keepdims=True)
            out = out * lax.rsqrt(ms + RMS_EPS) * gain_ref[...]
        outs.append(out)
    out_ref[0] = jnp.concatenate(outs, axis=1).astype(out_ref.dtype)


def _compress(segs, w1x, posx, b1, w2, b2, gain, normalize):
    bsz, nseg, width = segs.shape
    return pl.pallas_call(
        functools.partial(_compress_kernel, normalize=normalize),
        grid=(bsz,),
        in_specs=[pl.BlockSpec((1, nseg, width), lambda b: (b, 0, 0)),
                  _const_spec(w1x.shape), _const_spec(posx.shape), _const_spec(b1.shape),
                  _const_spec(w2.shape), _const_spec(b2.shape), _const_spec(gain.shape)],
        out_specs=pl.BlockSpec((1, nseg, LANES), lambda b: (b, 0, 0)),
        out_shape=jax.ShapeDtypeStruct((bsz, nseg, LANES), BF16),
        compiler_params=pltpu.CompilerParams(dimension_semantics=("parallel",),
                                             vmem_limit_bytes=VMEM_LIMIT),
        name="compress",
    )(segs, w1x, posx, b1, w2, b2, gain)


def _stack_heads(q_ref, lane_q):
    q = q_ref[0]
    return jnp.concatenate(
        [jnp.where((lane_q >= NSA_D) == (g == 1), q[:, LANES * j:LANES * (j + 1)], jnp.zeros((), BF16))
         for g in range(NSA_GROUPS) for j in range(NSA_HPG)], axis=0)


def _softmax_pv(s, v, online):
    if online:
        m = jnp.max(s, axis=1, keepdims=True)
        s = s - jnp.where(m < 0.1 * MASK_BIG, 0.0, m)
    p = jnp.exp2(s)
    acc = _dot(p.astype(BF16), jnp.concatenate([v, jnp.ones(v.shape, BF16)], axis=1))
    return p, acc[:, :LANES], acc[:, LANES:]


def _nsa_select_kernel(safe_ref, q_ref, kc_ref, vc_ref, gate_ref, bcf_ref, c2s_ref, shift_ref,
                       oc_ref, sbn_ref, sbf_ref):
    i = pl.program_id(1)
    safe = safe_ref[0] != 0
    ncmp = kc_ref.shape[1]
    n_tiles = ncmp * CMP_STRIDE // SEL_TILE
    refs = (q_ref, kc_ref, vc_ref, gate_ref, bcf_ref, c2s_ref, shift_ref, oc_ref, sbn_ref, sbf_ref)
    n_cls = max(c for c in (1, 2, 4) if n_tiles % c == 0 and (ncmp // c) % LANES == 0)
    for c in range(n_cls):
        @pl.when(safe & (i // (n_tiles // n_cls) == c))
        def _():
            _nsa_select_tile(i, ncmp * (c + 1) // n_cls, LANES * (c + 1) // n_cls, *refs, False)

    @pl.when(jnp.logical_not(safe))
    def _():
        _nsa_select_tile(i, ncmp, LANES, *refs, True)


def _nsa_select_tile(i, ncmp, nblk, q_ref, kc_ref, vc_ref, gate_ref, bcf_ref, c2s_ref, shift_ref,
                     oc_ref, sbn_ref, sbf_ref, online):
    T = SEL_TILE
    grows = NSA_HPG * T
    lane_q = lax.broadcasted_iota(jnp.int32, (T, LANES), 1)
    qpad = _stack_heads(q_ref, lane_q)

    n_io = lax.broadcasted_iota(jnp.int32, (ncmp, LANES), 0)
    l_io = lax.broadcasted_iota(jnp.int32, (ncmp, LANES), 1)
    band = jnp.clip(n_io - i * (T // CMP_STRIDE) + BAND_OFF, 0, BAND_SLOTS - 1)
    oh_c = jnp.where(band == (l_io & 63), 1.0, 0.0).astype(BF16)
    kc_aug = jnp.concatenate([kc_ref[0, :ncmp, :], oh_c], axis=1)
    qc_aug = jnp.concatenate([qpad, bcf_ref[...]], axis=1)
    p_c, num_c, l_c = _softmax_pv(_nt(qc_aug, kc_aug), vc_ref[0, :ncmp, :], online)
    inv_c = 1.0 / jnp.maximum(l_c, 1e-30)
    o_c = num_c * inv_c
    p_c = p_c * jnp.concatenate([inv_c] * (ncmp // LANES), axis=1)

    gates = gate_ref[0]
    gated = [jnp.broadcast_to(gates[:, h:h + 1], (T, LANES)) * o_c[h * T:(h + 1) * T] for h in range(NSA_HEADS)]
    for j in range(NSA_HPG):
        oc_ref[0, :, LANES * j:LANES * (j + 1)] = jnp.where(
            lane_q < NSA_D, gated[j], gated[NSA_HPG + j]).astype(oc_ref.dtype)

    imp_t = []
    for g in range(NSA_GROUPS):
        ps = p_c[g * grows:g * grows + T]
        for hh in range(1, NSA_HPG):
            ps = ps + p_c[g * grows + hh * T:g * grows + (hh + 1) * T]
        ps_hi = ps.astype(BF16)
        ps_lo = (ps - ps_hi.astype(F32)).astype(BF16)
        c2s = c2s_ref[:ncmp, :]
        imp_t.append((_dot(ps_hi, c2s) + _dot(ps_lo, c2s)).T[:nblk])
    width = NSA_GROUPS * T
    j_io = lax.broadcasted_iota(jnp.int32, (nblk, width), 0)
    r_io = lax.broadcasted_iota(jnp.int32, (nblk, width), 1) & (T - 1)
    j_f = j_io.astype(F32)
    cur = i * (T // SLC_BLOCK) + jnp.right_shift(r_io, SLC_BLOCK.bit_length() - 1)
    forced = (j_io == 0) | (j_io == cur) | (j_io == cur - 1)
    score = jnp.where(forced, -jnp.inf, jnp.where(j_io <= cur, jnp.concatenate(imp_t, axis=1), -1.0))
    for _ in range(SLC_TOPK - 3):
        best = jnp.max(score, axis=0, keepdims=True)
        first = jnp.min(jnp.where(score == best, j_f, float(nblk)), axis=0, keepdims=True)
        score = jnp.where(j_f == first, -jnp.inf, score)
    if nblk < LANES:
        score = jnp.concatenate([score, jnp.zeros((LANES - nblk, width), F32)], axis=0)
    near_start = ((i * T + lax.broadcasted_iota(jnp.int32, (T, LANES), 0)) // NSA_TILE) * (NSA_TILE // SLC_BLOCK)
    near_blk = lane_q >= near_start - SLC_PAD // SLC_BLOCK
    far_pick = shift_ref[0:1, 0:1]
    for g in range(NSA_GROUPS):
        picked = score[:, g * T:(g + 1) * T].T == -jnp.inf
        sbn_ref[0, :, LANES * g:LANES * (g + 1)] = jnp.where(picked & near_blk, 0.0, MASK_BIG).astype(BF16)
        sbf_ref[0, :, LANES * g:LANES * (g + 1)] = jnp.where(
            picked & jnp.logical_not(near_blk), far_pick, MASK_BIG).astype(BF16)


def _nsa_attend_kernel(safe_ref, *refs):
    *io_refs, acc_sc, m_sc, pa_sc, pb_sc = refs

    @pl.when(safe_ref[0] != 0)
    def _():
        _nsa_attend_tile(*io_refs, acc_sc, pa_sc, pb_sc, online=False)

    @pl.when(safe_ref[0] == 0)
    def _():
        _nsa_attend_tile(*io_refs, acc_sc, m_sc, online=True)


def _nsa_attend_tile(q_ref, ks_ref, vs_ref, kw_ref, vw_ref, oh_ref, gate_ref, oc_ref, sbn_ref, sbf_ref,
                     bs_ref, bw_ref, out_ref, acc_sc, *extra_sc, online):
    i = pl.program_id(1)
    t0 = pl.multiple_of(i * NSA_TILE, NSA_TILE)
    T = NSA_TILE
    rows = NSA_HEADS * T
    grows = NSA_HPG * T
    lane_q = lax.broadcasted_iota(jnp.int32, (T, LANES), 1)
    qpad = _stack_heads(q_ref, lane_q)

    def with_features(feat_ref):
        feats = [jnp.concatenate([feat_ref[0, :, LANES * g:LANES * (g + 1)]] * NSA_HPG, axis=0)
                 for g in range(NSA_GROUPS)]
        return jnp.concatenate([qpad, jnp.concatenate(feats, axis=0)], axis=1)

    q_near, q_far = with_features(sbn_ref), with_features(sbf_ref)

    r_w = lax.broadcasted_iota(jnp.int32, (WIN_KEYS, LANES), 0)
    pad_flag = jnp.where(r_w + (t0 - WINDOW) < 0, 1.0, 0.0).astype(BF16)
    kw_aug = jnp.concatenate([kw_ref[0, pl.ds(t0, WIN_KEYS), :], pad_flag], axis=1)
    qw_aug = jnp.concatenate([qpad, jnp.full((rows, LANES), MASK_BIG / LANES, BF16)], axis=1)
    s_w = _nt(qw_aug, kw_aug) + bw_ref[...]
    _, num_w, l_w = _softmax_pv(s_w, vw_ref[0, pl.ds(t0, WIN_KEYS), :], online)
    o_w = num_w * (1.0 / l_w)

    gates = gate_ref[0]
    gate_tile = lambda col: jnp.broadcast_to(gates[:, col:col + 1], (T, LANES))
    head_rows = [slice(h * T, (h + 1) * T) for h in range(NSA_HEADS)]
    o_gw = [gate_tile(16 + h) * o_w[head_rows[h]] for h in range(NSA_HEADS)]
    g_slc = [gate_tile(8 + h) for h in range(NSA_HEADS)]

    ones_f = jnp.ones((FAR_CHUNK, LANES), BF16)
    k_near = jnp.concatenate([ks_ref[0, pl.ds(t0, NEAR_KEYS), :], oh_ref[pl.ds(t0, NEAR_KEYS), :]], axis=1)
    v_near = jnp.concatenate([vs_ref[0, pl.ds(t0, NEAR_KEYS), :], jnp.ones((NEAR_KEYS, LANES), BF16)], axis=1)
    s_n = _nt(q_near, k_near) + bs_ref[...]
    if online:
        m_sc, = extra_sc
        m_n = jnp.max(s_n, axis=1, keepdims=True)
        m_sc[...] = m_n
        s_n = s_n - m_n
    acc_sc[...] = _dot(jnp.exp2(s_n).astype(BF16), v_near)

    last_chunk = (ks_ref.shape[1] - SLC_PAD) // FAR_CHUNK - 1

    def far_start(c):
        return pl.multiple_of(SLC_PAD + FAR_CHUNK * jnp.minimum(c, last_chunk), LANES)

    def far_scores(c, by_group=False):
        start = far_start(c)
        k_f = jnp.concatenate([ks_ref[0, pl.ds(start, FAR_CHUNK), :], oh_ref[pl.ds(start, FAR_CHUNK), :]], axis=1)
        if by_group:
            return jnp.concatenate([_nt(q_far[:grows], k_f), _nt(q_far[grows:], k_f)], axis=0)
        return _nt(q_far, k_f)

    def far_values(c):
        return jnp.concatenate([vs_ref[0, pl.ds(far_start(c), FAR_CHUNK), :], ones_f], axis=1)

    n_far = (t0 + (FAR_CHUNK - 1 - SLC_PAD)) // FAR_CHUNK
    if online:
        def far_online(c, carry):
            s_f = far_scores(c)
            m_old = m_sc[...]
            m_new = jnp.maximum(m_old, jnp.max(s_f, axis=1, keepdims=True))
            m_sc[...] = m_new
            acc_sc[...] = (jnp.exp2(m_old - m_new) * acc_sc[...]
                           + _dot(jnp.exp2(s_f - m_new).astype(BF16), far_values(c)))
            return carry

        lax.fori_loop(0, n_far, far_online, 0)
    else:
        pa_sc, pb_sc = extra_sc
        pa_sc[...] = jnp.exp2(far_scores(0, by_group=True)).astype(BF16)

        def far_pair(cc, carry):
            c0 = 2 * cc
            pv0 = _dot(pa_sc[...], far_values(c0))
            pb_sc[...] = jnp.exp2(far_scores(c0 + 1)).astype(BF16)
            pv1 = _dot(pb_sc[...], far_values(c0 + 1))
            pa_sc[...] = jnp.exp2(far_scores(c0 + 2)).astype(BF16)
            acc_sc[...] += pv0 + pv1
            return carry

        lax.fori_loop(0, (n_far + 1) // 2, far_pair, 0)

    acc = acc_sc[...]
    o_s = acc[:, :LANES] * (1.0 / acc[:, LANES:])

    head_out = [o_gw[h] + g_slc[h] * o_s[head_rows[h]] for h in range(NSA_HEADS)]
    for j in range(NSA_HPG):
        cols = slice(LANES * j, LANES * (j + 1))
        out_ref[0, :, cols] = (oc_ref[0, :, cols].astype(F32) + jnp.where(
            lane_q < NSA_D, head_out[j], head_out[NSA_HPG + j])).astype(out_ref.dtype)


def _nsa(safe, q, ks, vs, kw, vw, kc, vc, oh, gates, bcf, bs, bw, c2s, shift):
    bsz, s = q.shape[:2]
    flag = pl.BlockSpec(memory_space=pltpu.SMEM)
    T = NSA_TILE
    rows = NSA_HEADS * T
    grid = (bsz, s // T)
    params = pltpu.CompilerParams(dimension_semantics=("parallel", "arbitrary"), vmem_limit_bytes=VMEM_LIMIT)
    tile = lambda width: pl.BlockSpec((1, T, width), lambda b, i: (b, i, 0))
    per_b = lambda a: pl.BlockSpec((1,) + a.shape[1:], lambda b, i: (b, 0, 0), pipeline_mode=pl.Buffered(1))
    sel_tile = lambda width: pl.BlockSpec((1, SEL_TILE, width), lambda b, i: (b, i, 0))
    o_cmp, sb_near, sb_far = pl.pallas_call(
        _nsa_select_kernel,
        grid=(bsz, s // SEL_TILE),
        in_specs=[flag, sel_tile(NSA_HPG * LANES), per_b(kc), per_b(vc), sel_tile(LANES),
                  _const_spec(bcf.shape), _const_spec(c2s.shape), _const_spec(shift.shape)],
        out_specs=(sel_tile(NSA_HPG * LANES), sel_tile(NSA_GROUPS * LANES), sel_tile(NSA_GROUPS * LANES)),
        out_shape=(jax.ShapeDtypeStruct((bsz, s, NSA_HPG * LANES), BF16),
                   jax.ShapeDtypeStruct((bsz, s, NSA_GROUPS * LANES), BF16),
                   jax.ShapeDtypeStruct((bsz, s, NSA_GROUPS * LANES), BF16)),
        compiler_params=params,
        name="nsa_select",
    )(safe, q, kc, vc, gates, bcf, c2s, shift)
    scratch = [pltpu.VMEM((rows, 2 * LANES), F32),
               pltpu.VMEM((rows, 1), F32),
               pltpu.VMEM((rows, FAR_CHUNK), BF16),
               pltpu.VMEM((rows, FAR_CHUNK), BF16)]
    return pl.pallas_call(
        _nsa_attend_kernel,
        grid=grid,
        in_specs=[flag, tile(NSA_HPG * LANES), per_b(ks), per_b(vs), per_b(kw), per_b(vw), _const_spec(oh.shape),
                  tile(LANES), tile(NSA_HPG * LANES), tile(NSA_GROUPS * LANES), tile(NSA_GROUPS * LANES),
                  _const_spec(bs.shape), _const_spec(bw.shape)],
        out_specs=tile(NSA_HPG * LANES),
        out_shape=jax.ShapeDtypeStruct((bsz, s, NSA_HPG * LANES), BF16),
        scratch_shapes=scratch,
        compiler_params=params,
        name="nsa_attend",
    )(safe, q, ks, vs, kw, vw, oh, gates, o_cmp, sb_near, sb_far, bs, bw)


ML_ROWS = 512


def _mlstm_kernel(q_ref, k_ref, v_ref, o_ref, gif_ref, gcol_ref, gb_ref, gbrow_ref, tril_ref, triu_ref,
                  out_ref, c_sc, m_sc):
    j = pl.program_id(1)
    R = ML_ROWS
    L = ML_CHUNK

    @pl.when(j == 0)
    def _():
        c_sc[...] = jnp.zeros_like(c_sc)
        m_sc[...] = jnp.zeros_like(m_sc)

    og = o_ref[0]

    pre = gif_ref[...] + gb_ref[...]
    row8 = lax.broadcasted_iota(jnp.int32, (8, R), 0)
    logf = jnp.minimum(pre, 0.0) - jnp.log(1.0 + jnp.exp(-jnp.abs(pre)))
    g8 = jnp.where(row8 < ML_HEADS, pre, logf)
    pre_c = gcol_ref[0] + gbrow_ref[...]
    lane_c = lax.broadcasted_iota(jnp.int32, pre_c.shape, 1)
    logf_c = jnp.minimum(pre_c, 0.0) - jnp.log(1.0 + jnp.exp(-jnp.abs(pre_c)))
    g_c = jnp.where(lane_c < _IF_LANE + ML_HEADS, pre_c, logf_c)
    cum_c = _split3(g_c, lambda part: _dot(tril_ref[...], part))
    cum_row = _split3(g8, lambda part: _dot(part, triu_ref[...]))

    a_io = lax.broadcasted_iota(jnp.int32, (L, L), 0)
    b_io = lax.broadcasted_iota(jnp.int32, (L, L), 1)
    causal = (b_io <= a_io)[None]
    nc = R // L
    pairs = [(c, h) for c in range(nc) for h in range(ML_HEADS)]

    def blocks(a):
        return jnp.stack([a[c * L:(c + 1) * L, h * ML_D:(h + 1) * ML_D] for c, h in pairs])

    def cols(a, k0):
        return jnp.stack([jnp.broadcast_to(a[c * L:(c + 1) * L, k0 + h:k0 + h + 1], (L, LANES)) for c, h in pairs])

    def rows(a, k0):
        return jnp.stack([a[k0 + h:k0 + h + 1, c * L:(c + 1) * L] for c, h in pairs])

    bdot = lambda eq, x, y: jnp.einsum(eq, x, y, preferred_element_type=F32)
    qb, kb, vb = blocks(q_ref[0]), blocks(k_ref[0]), blocks(v_ref[0])
    b_col, li_col = cols(cum_c, _IF_LANE + ML_HEADS), cols(g_c, _IF_LANE)
    b_row, li_row = rows(cum_row, ML_HEADS), rows(g8, 0)
    gsum = b_row[:, :, L - 1:L]

    s_max = jnp.max(gsum - b_row + li_row, axis=2, keepdims=True)
    m_run = m_sc[:, 0:1, 0:1]
    m_ins, m_outs = [], []
    for c in range(nc):
        hs = slice(c * ML_HEADS, (c + 1) * ML_HEADS)
        m_ins.append(m_run)
        m_run = jnp.maximum(gsum[hs] + m_run, s_max[hs])
        m_outs.append(m_run)
    m_sc[...] = jnp.broadcast_to(m_run, m_sc.shape)
    m_in, m_out = jnp.concatenate(m_ins, axis=0), jnp.concatenate(m_outs, axis=0)

    log_d = jnp.where(causal, b_col[:, :, :L] - b_row + li_row, -jnp.inf)
    inter = b_col + m_in
    m_row = jnp.maximum(inter, jnp.broadcast_to(jnp.max(log_d, axis=2, keepdims=True), inter.shape))
    w = bdot('bik,bjk->bij', qb, kb) * jnp.exp(log_d - m_row[:, :, :L])
    v_aug = jnp.concatenate([vb, jnp.ones(vb.shape, BF16)], axis=2)
    wv = bdot('bij,bjd->bid', w.astype(BF16), v_aug)
    inter_scale = jnp.exp(inter - m_row)
    inter_scale = jnp.concatenate([inter_scale, inter_scale], axis=2)
    floor = jnp.exp(-m_row)
    k_src = (kb.astype(F32) * jnp.exp(gsum - b_col + li_col - m_out)).astype(BF16)
    upd = bdot('bjk,bjd->bkd', k_src, v_aug)
    decay = jnp.exp(gsum + m_in - m_out)

    c_aug = c_sc[...]
    for c in range(nc):
        hs = slice(c * ML_HEADS, (c + 1) * ML_HEADS)
        nd = inter_scale[hs] * bdot('hik,hkd->hid', qb[hs], c_aug.astype(BF16)) + wv[hs]
        hval = nd[:, :, :ML_D] / jnp.maximum(jnp.abs(nd[:, :, ML_D:]), floor[hs])
        for h in range(ML_HEADS):
            rs, cs = slice(c * L, (c + 1) * L), slice(h * ML_D, (h + 1) * ML_D)
            out_ref[0, rs, cs] = (og[rs, cs].astype(F32) * hval[h]).astype(out_ref.dtype)
        c_aug = decay[hs] * c_aug + upd[hs]
    c_sc[...] = c_aug


def _mlstm(mq, mk, mv, og, gif, gcol, gate_b, gate_b_row):
    bsz, s, width = mq.shape
    R = ML_ROWS
    nblk = s // R
    seq = lambda: pl.BlockSpec((1, R, width), lambda b, j: (b, j, 0))
    pos = np.arange(R)
    same_chunk = (pos[:, None] // ML_CHUNK) == (pos[None, :] // ML_CHUNK)
    tril = (same_chunk & (pos[None, :] <= pos[:, None])).astype(np.float32)
    return pl.pallas_call(
        _mlstm_kernel,
        grid=(bsz, nblk),
        in_specs=[seq(), seq(), seq(), seq(),
                  pl.BlockSpec((8, R), lambda b, j: (0, b * nblk + j)),
                  pl.BlockSpec((1, R, LANES), lambda b, j: (b, j, 0)),
                  _const_spec(gate_b.shape), _const_spec(gate_b_row.shape),
                  _const_spec((R, R)), _const_spec((R, R))],
        out_specs=seq(),
        out_shape=jax.ShapeDtypeStruct((bsz, s, width), BF16),
        scratch_shapes=[pltpu.VMEM((ML_HEADS, ML_D, 2 * ML_D), F32),
                        pltpu.VMEM((ML_HEADS, 8, LANES), F32)],
        compiler_params=pltpu.CompilerParams(dimension_semantics=("parallel", "arbitrary"),
                                             vmem_limit_bytes=VMEM_LIMIT),
        name="mlstm",
    )(mq, mk, mv, og, gif, gcol, gate_b, gate_b_row, jnp.asarray(tril, BF16), jnp.asarray(tril.T, BF16))


FF_CHUNK = 512


def _merge_ffn_kernel(x_ref, ya_ref, yb_ref, mg_ref, wa_ref, wb_ref, wo_ref, g2_ref, w1_ref, w2_ref,
                      out_ref):
    mg = mg_ref[...]
    mixed = (mg[:, :D_MODEL].astype(F32) * _dot(ya_ref[...], wa_ref[...])
             + mg[:, D_MODEL:].astype(F32) * _dot(yb_ref[...], wb_ref[...]))
    x1 = x_ref[...] + _dot(mixed.astype(BF16), wo_ref[...])
    h2 = x1 * lax.rsqrt(jnp.mean(x1 * x1, axis=-1, keepdims=True) + RMS_EPS) * g2_ref[...]
    h2 = h2.astype(BF16)
    acc = x1
    for c in range(D_FF // FF_CHUNK):
        a = jnp.maximum(_dot(h2, w1_ref[:, c * FF_CHUNK:(c + 1) * FF_CHUNK]), 0.0)
        acc = acc + _dot((a * a).astype(BF16), w2_ref[c * FF_CHUNK:(c + 1) * FF_CHUNK, :])
    out_ref[...] = acc


def _merge_ffn(x2d, ya, yb, mg, wa, wb, wo, g2, w1, w2, tm=512):
    n = x2d.shape[0]
    row = lambda w: pl.BlockSpec((tm, w), lambda i: (i, 0))
    return pl.pallas_call(
        _merge_ffn_kernel,
        grid=(n // tm,),
        in_specs=[row(D_MODEL), row(512), row(512), row(2048),
                  _const_spec(wa.shape), _const_spec(wb.shape), _const_spec(wo.shape),
                  _const_spec(g2.shape), _const_spec(w1.shape), _const_spec(w2.shape)],
        out_specs=row(D_MODEL),
        out_shape=jax.ShapeDtypeStruct((n, D_MODEL), F32),
        compiler_params=pltpu.CompilerParams(dimension_semantics=("parallel",),
                                             vmem_limit_bytes=VMEM_LIMIT),
        name="merge_ffn",
    )(x2d, ya, yb, mg, wa, wb, wo, g2, w1, w2)


def _proj_weights(w):
    widths = (512, 128, 128, 128, 128, 128, 128, 24, 512, 512, 512, 4, 4, 512, 2048)
    off = np.concatenate([[0], np.cumsum(widths)])
    (nq, nkc, nvc, nks, nvs, nkw, nvw, ngate, mq, mk, mv, mi, mf, mo, mgate) = (int(o) for o in off[:-1])
    col = lambda start, width: w[:, start:start + width]
    parts = [col(nq + NSA_D * (g * NSA_HPG + j), NSA_D) for j in range(NSA_HPG) for g in range(NSA_GROUPS)]
    parts += [col(nks, 128), col(nkw, 128), col(nkc, 128), col(nvc, 128), col(nvs, 128), col(nvw, 128)]
    gate = col(ngate, 24).reshape(-1, NSA_HEADS, 3).transpose(0, 2, 1).reshape(-1, 24)
    parts += [gate, col(mi, 4), col(mf, 4), jnp.zeros((w.shape[0], LANES - _IF_LANE - 8), w.dtype)]
    parts += [col(mq, 512), col(mk, 512), col(mv, 512), col(mo, 512), col(mgate, 2048)]
    w_all = jnp.concatenate(parts, axis=1).astype(BF16)
    w_if = jnp.concatenate([col(mi, 4), col(mf, 4)], axis=1).T.astype(BF16)
    return w_all, w_if


def _branch_a_weights(w):
    head = lambda h: w[NSA_D * h:NSA_D * (h + 1)]
    return jnp.concatenate([head(g * NSA_HPG + j) for j in range(NSA_HPG) for g in range(NSA_GROUPS)],
                           axis=0).astype(BF16)


def _compress_weights(pos, w1):
    r = w1.reshape(2, CMP_STRIDE, 1, NSA_D, CMP_HIDDEN)
    z = jnp.zeros_like(r)
    w1x = jnp.stack([jnp.concatenate([r, z], axis=2), jnp.concatenate([z, r], axis=2)])
    w1x = w1x.reshape(NSA_GROUPS, 2, CMP_STRIDE * NSA_GROUPS * NSA_D, CMP_HIDDEN).astype(BF16)
    posx = jnp.broadcast_to(pos.reshape(2, CMP_STRIDE, 1, NSA_D), (2, CMP_STRIDE, NSA_GROUPS, NSA_D))
    return w1x, posx.reshape(2, CMP_STRIDE * NSA_GROUPS * NSA_D).astype(BF16)


def _t5_bucket(dist):
    n = np.maximum(dist, 0)
    max_exact = REL_BUCKETS // 2
    nf = np.maximum(n, 1).astype(np.float32)
    large = max_exact + (np.log(nf / np.float32(max_exact)) / np.float32(math.log(REL_MAX_DIST / max_exact))
                         * np.float32(REL_BUCKETS - max_exact)).astype(np.int32)
    return np.where(n < max_exact, n, np.minimum(large, REL_BUCKETS - 1))


def _toeplitz(rel, n_rows, n_cols, stride, off, inner=1):
    a_rows = n_rows // inner
    lw = a_rows + n_cols
    k = np.arange(lw)[:, None]
    dist = stride * np.where(k < n_cols, -k, lw - k) + np.arange(inner)[None, :] + off
    live = (dist >= 0) & (k != n_cols)
    onehot = live[..., None] & (_t5_bucket(dist)[..., None] == np.arange(REL_BUCKETS))
    w = jnp.dot(jnp.asarray(onehot.reshape(lw * inner, REL_BUCKETS), F32), rel.T,
                precision=lax.Precision.HIGHEST)
    w = w.T.reshape(-1, lw, inner)
    flat = jnp.tile(w, (1, a_rows, 1))[:, :a_rows * (lw - 1)]
    out = flat.reshape(-1, a_rows, lw - 1, inner)[:, :, :n_cols]
    return out.transpose(0, 1, 3, 2).reshape(-1, n_rows, n_cols)


def _rel_bias(rel_table):
    return (rel_table - rel_table[REL_BUCKETS - 1][None, :]).T * LOG2E


def _bias_tables(rel, shifts):
    T = NSA_TILE
    sh_c, sh_s, sh_w = shifts
    r = np.arange(T)[:, None]

    c = np.arange(NEAR_KEYS)[None, :]
    ok = jnp.asarray((r - c + SLC_PAD) >= 0)[None]
    bs = jnp.where(ok, _toeplitz(rel, T, NEAR_KEYS, 1, SLC_PAD) - sh_s, MASK_F32)
    c = np.arange(WIN_KEYS)[None, :]
    d = r - c + WINDOW
    ok = jnp.asarray((d >= 0) & (d < WINDOW))[None]
    bw = jnp.where(ok, _toeplitz(rel, T, WIN_KEYS, 1, WINDOW) - sh_w, MASK_F32)
    r = np.arange(SEL_TILE)[:, None]
    m = np.arange(64)[None, :]
    d = r - CMP_STRIDE * m + BAND_DIST0
    band = (m >= 1) & (m < BAND_SLOTS - 1)
    vals = _toeplitz(rel, SEL_TILE, BAND_SLOTS, CMP_STRIDE, BAND_DIST0, inner=CMP_STRIDE)
    vals = jnp.pad(vals, ((0, 0), (0, 0), (0, 64 - BAND_SLOTS)))
    vals = jnp.where(jnp.asarray(band & (d >= 0))[None], vals, 0.0) - sh_c
    dead = (band & (d < 0)) | (m == BAND_SLOTS - 1)
    vals = jnp.where(jnp.asarray(dead)[None], MASK_BIG, vals)
    hi = vals.astype(BF16)
    lo = (vals - hi.astype(F32)).astype(BF16)
    bcf = jnp.concatenate([hi, lo], axis=-1)
    flat = lambda a: a.reshape(-1, a.shape[-1])
    return flat(bcf), flat(bs).astype(F32), flat(bw).astype(F32)


def _layer(l, x2d, bsz, s, consts, rel_table, norm1_g, w_in, nsa_q_gain, nsa_k_gain, cmp_k, cmp_v,
           ml_conv_w, ml_conv_b, ml_i_bias, ml_f_bias, w_branch_a, w_branch_b, w_out, norm2_g, w_ff1, w_ff2):
    n = bsz * s
    nseg = s // CMP_STRIDE
    oh, c2s = consts
    w_all, w_if = _proj_weights(w_in[l])
    qg_pad = jnp.concatenate([nsa_q_gain[l]] * 2)[None, :]
    kg_pad = jnp.stack([jnp.concatenate([nsa_k_gain[l, 1]] * 2), jnp.concatenate([nsa_k_gain[l, 2]] * 2)])
    (q, ks, kw, kc, vc, vs, vw, gates, gcol, mq, mk, mv, og, mg, gif) = _proj(
        x2d, s, norm1_g[l][None, :], w_all, w_if, qg_pad, kg_pad, ml_conv_w[l], ml_conv_b[l][None, :])

    def compress(a, params, gain, normalize):
        pos, w1, b1, w2, b2 = (p[l] for p in params)
        w1x, posx = _compress_weights(pos, w1)
        return _compress(a.reshape(bsz, nseg, CMP_STRIDE * LANES), w1x, posx, b1[None, :],
                         w2.astype(BF16), b2[None, :], gain[None, :], normalize)

    kcmp = compress(kc, cmp_k, nsa_k_gain[l, 0], True)
    vcmp = compress(vc, cmp_v, jnp.ones((NSA_D,), F32), False)

    qk_bound = lambda kg: 8.0 * LOG2E * jnp.max(jnp.abs(nsa_q_gain[l])) * jnp.max(jnp.abs(kg))
    tab = _rel_bias(rel_table)
    snap = lambda v: v.astype(BF16).astype(F32)
    shifts = [snap(qk_bound(nsa_k_gain[l, j]) + jnp.maximum(jnp.max(tab), 0.0)) for j in range(3)]
    safe = 2.0 * jnp.max(jnp.stack(shifts)) < SAFE_SHIFT_LOG2
    shifts = [jnp.where(safe, sh, 0.0) for sh in shifts]
    bcf, bs, bw = _bias_tables(tab, shifts)

    seq = lambda a: a.reshape(bsz, s, a.shape[-1])
    front = lambda a, p: jnp.pad(seq(a), ((0, 0), (p, 0), (0, 0)))
    operands = (seq(q), front(ks, SLC_PAD), front(vs, SLC_PAD), front(kw, WINDOW), front(vw, WINDOW),
                kcmp, vcmp, oh, seq(gates), bcf, bs, bw, c2s, jnp.full((1, LANES), -shifts[1], F32))
    y_a = _nsa(safe.astype(jnp.int32)[None], *operands)

    gate_b = jnp.concatenate([ml_i_bias[l], ml_f_bias[l]])
    gate_b_row = jnp.pad(gate_b, (_IF_LANE, LANES - _IF_LANE - 8))[None, :]
    y_b = _mlstm(seq(mq), seq(mk), seq(mv), seq(og), gif, seq(gcol), gate_b[:, None], gate_b_row)

    out = _merge_ffn(x2d, y_a.reshape(n, 512), y_b.reshape(n, 512), mg, _branch_a_weights(w_branch_a[l]),
                     w_branch_b[l].astype(BF16), w_out[l].astype(BF16), norm2_g[l][None, :],
                     w_ff1[l].astype(BF16), w_ff2[l].astype(BF16))
    return out, y_a, y_b


def _consts(s):
    nseg = s // CMP_STRIDE
    nsel = s // SLC_BLOCK
    blk_of_key = np.arange(s) // SLC_BLOCK
    oh = np.concatenate([np.ones((SLC_PAD, LANES), np.float32),
                         (blk_of_key[:, None] == np.arange(LANES)[None, :]).astype(np.float32)], axis=0)
    ci = np.arange(nseg)[:, None] * CMP_STRIDE
    sj = np.arange(LANES)[None, :] * SLC_BLOCK
    c2s = ((ci < sj + SLC_BLOCK) & (ci + CMP_LEN > sj) & (np.arange(LANES)[None, :] < nsel)
           & (np.arange(nseg)[:, None] < nseg - 1))
    return jnp.asarray(oh, BF16), jnp.asarray(c2s.astype(np.float32), BF16)


def kernel(x, norm1_g, w_in, nsa_q_gain, nsa_k_gain, cmp_k_pos, cmp_k_w1, cmp_k_b1, cmp_k_w2, cmp_k_b2, cmp_v_pos, cmp_v_w1, cmp_v_b1, cmp_v_w2, cmp_v_b2, rel_table, ml_conv_w, ml_conv_b, ml_i_bias, ml_f_bias, w_branch_a, w_branch_b, w_out, norm2_g, w_ff1, w_ff2):
    bsz, s, _ = x.shape
    consts = _consts(s)
    x2d = x.reshape(bsz * s, D_MODEL)
    for l in range(norm1_g.shape[0]):
        x2d, _, _ = _layer(l, x2d, bsz, s, consts, rel_table, norm1_g, w_in, nsa_q_gain, nsa_k_gain,
                           (cmp_k_pos, cmp_k_w1, cmp_k_b1, cmp_k_w2, cmp_k_b2),
                           (cmp_v_pos, cmp_v_w1, cmp_v_b1, cmp_v_w2, cmp_v_b2),
                           ml_conv_w, ml_conv_b, ml_i_bias, ml_f_bias,
                           w_branch_a, w_branch_b, w_out, norm2_g, w_ff1, w_ff2)
    return x2d.reshape(bsz, s, D_MODEL)
```

```python
import functools
import math

import numpy as np
import jax
import jax.numpy as jnp
from jax import lax
from jax.experimental import pallas as pl
from jax.experimental.pallas import tpu as pltpu

F32 = jnp.float32
BF16 = jnp.bfloat16

D_MODEL = 1024
NSA_HEADS = 8
NSA_GROUPS = 2
NSA_HPG = NSA_HEADS // NSA_GROUPS
NSA_D = 64
CMP_LEN = 32
CMP_STRIDE = 16
CMP_HIDDEN = 256
SLC_BLOCK = 64
SLC_TOPK = 16
WINDOW = 512
FORCED_SCORE = 1e4
ML_HEADS = 4
ML_D = 128
ML_CHUNK = 64
CONV_WIDTH = 4
D_FF = 4 * D_MODEL
REL_BUCKETS = 32
REL_MAX_DIST = 128
RMS_EPS = 1e-6

LANES = 128
NSA_TILE = 256
SEL_TILE = 512
SLC_PAD = 128
FAR_CHUNK = 512
NEAR_KEYS = SLC_PAD + NSA_TILE
WIN_KEYS = WINDOW + NSA_TILE
BAND_OFF = 10
BAND_SLOTS = 2 + (SEL_TILE + 128) // CMP_STRIDE
BAND_DIST0 = CMP_STRIDE * BAND_OFF - (CMP_LEN - 1)
MASK_BIG = -1e9
MASK_F32 = -1e30
LOG2E = math.log2(math.e)
SAFE_SHIFT_LOG2 = 50.0
VMEM_LIMIT = 56 * 1024 * 1024

_Q_OFF, _KS_OFF, _KW_OFF, _KC_OFF, _VC_OFF, _VS_OFF, _VW_OFF = 0, 512, 640, 768, 896, 1024, 1152
_GATE_OFF, _MQ_OFF, _MK_OFF, _MV_OFF, _MO_OFF, _MG_OFF, _W_COLS = 1280, 1408, 1920, 2432, 2944, 3456, 5504
_IF_LANE = 24


def _nt(a, b, precision=None):
    return lax.dot_general(a, b, (((1,), (1,)), ((), ())), precision=precision,
                           preferred_element_type=F32)


def _dot(a, b, precision=None):
    return jnp.dot(a, b, precision=precision, preferred_element_type=F32)


def _split3(x, dot_part):
    hi = x.astype(BF16)
    r1 = x - hi.astype(F32)
    mid = r1.astype(BF16)
    lo = (r1 - mid.astype(F32)).astype(BF16)
    return dot_part(hi) + dot_part(mid) + dot_part(lo)


def _const_spec(shape):
    nd = len(shape)
    return pl.BlockSpec(shape, lambda *_: (0,) * nd, pipeline_mode=pl.Buffered(1))


def _proj_kernel(x_ref, g1_ref, w_ref, wif_ref, qg_ref, kg_ref, cw_ref, cb_ref,
                 q_ref, ks_ref, kw_ref, kc_ref, vc_ref, vs_ref, vw_ref, gate_ref, gcol_ref,
                 mq_ref, mk_ref, mv_ref, mo_ref, mg_ref, gif_ref, ext_sc, *, tiles_per_seq):
    tm = x_ref.shape[0]
    x = x_ref[...]
    h = x * lax.rsqrt(jnp.mean(x * x, axis=-1, keepdims=True) + RMS_EPS) * g1_ref[...]
    hb = h.astype(BF16)

    def proj(off, width):
        return _dot(hb, w_ref[:, off:off + width])

    lane = lax.broadcasted_iota(jnp.int32, (1, LANES), 1)
    low = lane < NSA_D

    def half_norm(blk):
        sq = blk * blk
        ms0 = jnp.sum(jnp.where(low, sq, 0.0), axis=-1, keepdims=True) * (1.0 / NSA_D)
        ms1 = jnp.sum(jnp.where(low, 0.0, sq), axis=-1, keepdims=True) * (1.0 / NSA_D)
        return blk * jnp.where(low, lax.rsqrt(ms0 + RMS_EPS), lax.rsqrt(ms1 + RMS_EPS))

    for pair in range(NSA_HPG // 2):
        both = proj(_Q_OFF + 2 * LANES * pair, 2 * LANES)
        for hh in range(2):
            qn = half_norm(both[:, LANES * hh:LANES * (hh + 1)]) * qg_ref[...] * (NSA_D ** -0.5 * LOG2E)
            col = LANES * (2 * pair + hh)
            q_ref[:, col:col + LANES] = qn.astype(q_ref.dtype)

    both = proj(_KS_OFF, 2 * LANES)
    for hh, ref in enumerate((ks_ref, kw_ref)):
        ref[...] = (half_norm(both[:, LANES * hh:LANES * (hh + 1)]) * kg_ref[hh:hh + 1, :]).astype(ref.dtype)

    for off, refs in ((_KC_OFF, (kc_ref, vc_ref)), (_VS_OFF, (vs_ref, vw_ref))):
        both = proj(off, 2 * LANES)
        for hh, ref in enumerate(refs):
            ref[...] = both[:, LANES * hh:LANES * (hh + 1)].astype(ref.dtype)

    slab = proj(_GATE_OFF, LANES)
    gate_ref[...] = jax.nn.sigmoid(slab)
    gcol_ref[...] = slab

    @pl.when(pl.program_id(0) % tiles_per_seq == 0)
    def _():
        ext_sc[0:8, :] = jnp.zeros((8, ext_sc.shape[1]), F32)

    ext_sc[8:, 0:512] = proj(_MQ_OFF, 512)
    ext_sc[8:, 512:1024] = proj(_MK_OFF, 512)
    conv = cb_ref[...]
    for t in range(CONV_WIDTH):
        lo = 8 - (CONV_WIDTH - 1) + t
        conv = conv + ext_sc[lo:lo + tm, :] * cw_ref[t:t + 1, :]
    ext_sc[0:8, :] = ext_sc[tm:tm + 8, :]
    qk = conv * jax.nn.sigmoid(conv)
    mq_ref[...] = (qk[:, :512] * (ML_D ** -0.5)).astype(mq_ref.dtype)
    mk_ref[...] = qk[:, 512:].astype(mk_ref.dtype)

    mv_ref[...] = proj(_MV_OFF, 512).astype(mv_ref.dtype)
    mo_ref[...] = jax.nn.sigmoid(proj(_MO_OFF, 512)).astype(mo_ref.dtype)
    for c in range(4):
        mg_ref[:, 512 * c:512 * (c + 1)] = jax.nn.sigmoid(proj(_MG_OFF + 512 * c, 512)).astype(mg_ref.dtype)
    gif_ref[...] = _nt(wif_ref[...], hb)


def _proj(x2d, seq_len, g1, w_all, w_if, qg_pad, kg_pad, conv_w, conv_b, tm=512):
    n = x2d.shape[0]
    row = lambda w: pl.BlockSpec((tm, w), lambda i: (i, 0))
    out_shapes = (
        jax.ShapeDtypeStruct((n, 512), BF16),
        jax.ShapeDtypeStruct((n, LANES), BF16),
        jax.ShapeDtypeStruct((n, LANES), BF16),
        jax.ShapeDtypeStruct((n, LANES), BF16),
        jax.ShapeDtypeStruct((n, LANES), BF16),
        jax.ShapeDtypeStruct((n, LANES), BF16),
        jax.ShapeDtypeStruct((n, LANES), BF16),
        jax.ShapeDtypeStruct((n, LANES), F32),
        jax.ShapeDtypeStruct((n, LANES), F32),
        jax.ShapeDtypeStruct((n, 512), BF16),
        jax.ShapeDtypeStruct((n, 512), BF16),
        jax.ShapeDtypeStruct((n, 512), BF16),
        jax.ShapeDtypeStruct((n, 512), BF16),
        jax.ShapeDtypeStruct((n, 2048), BF16),
        jax.ShapeDtypeStruct((8, n), F32),
    )
    out_specs = (row(512),) + (row(LANES),) * 8 + (row(512),) * 4 + (row(2048),
                 pl.BlockSpec((8, tm), lambda i: (0, i)))
    return pl.pallas_call(
        functools.partial(_proj_kernel, tiles_per_seq=seq_len // tm),
        grid=(n // tm,),
        in_specs=[row(D_MODEL), _const_spec((1, D_MODEL)), _const_spec((D_MODEL, _W_COLS)),
                  _const_spec((8, D_MODEL)), _const_spec((1, LANES)), _const_spec((2, LANES)),
                  _const_spec(conv_w.shape), _const_spec(conv_b.shape)],
        out_specs=out_specs,
        out_shape=out_shapes,
        scratch_shapes=[pltpu.VMEM((tm + 8, 1024), F32)],
        compiler_params=pltpu.CompilerParams(dimension_semantics=("arbitrary",),
                                             vmem_limit_bytes=VMEM_LIMIT),
        name="proj",
    )(x2d, g1, w_all, w_if, qg_pad, kg_pad, conv_w, conv_b)


def _compress_kernel(seg_ref, w1_ref, pos_ref, b1_ref, w2_ref, b2_ref, gain_ref, out_ref, *, normalize):
    seg = seg_ref[0]
    nseg, width = seg.shape
    c = math.sqrt(2.0 / math.pi)
    pos_lo = jnp.broadcast_to(pos_ref[0:1, :], (8, width))
    pos_hi = jnp.broadcast_to(pos_ref[1:2, :], (8, width))
    outs = []
    for g in range(NSA_GROUPS):
        a = _dot(seg, w1_ref[g, 0])
        b = _dot(seg, w1_ref[g, 1])
        posb = (_dot(pos_lo, w1_ref[g, 0]) + _dot(pos_hi, w1_ref[g, 1]))[0:1] + b1_ref[...]
        pre = a + pltpu.roll(b, nseg - 1, 0) + posb
        hid = 0.5 * pre * (1.0 + jnp.tanh(c * (pre + 0.044715 * (pre * pre * pre))))
        out = _dot(hid.astype(BF16), w2_ref[...]) + b2_ref[...]
        if normalize:
            ms = jnp.mean(out * out, axis=-1, keepdims=True)
            out = out * lax.rsqrt(ms + RMS_EPS) * gain_ref[...]
        outs.append(out)
    out_ref[0] = jnp.concatenate(outs, axis=1).astype(out_ref.dtype)


def _compress(segs, w1x, posx, b1, w2, b2, gain, normalize):
    bsz, nseg, width = segs.shape
    return pl.pallas_call(
        functools.partial(_compress_kernel, normalize=normalize),
        grid=(bsz,),
        in_specs=[pl.BlockSpec((1, nseg, width), lambda b: (b, 0, 0)),
                  _const_spec(w1x.shape), _const_spec(posx.shape), _const_spec(b1.shape),
                  _const_spec(w2.shape), _const_spec(b2.shape), _const_spec(gain.shape)],
        out_specs=pl.BlockSpec((1, nseg, LANES), lambda b: (b, 0, 0)),
        out_shape=jax.ShapeDtypeStruct((bsz, nseg, LANES), BF16),
        compiler_params=pltpu.CompilerParams(dimension_semantics=("parallel",),
                                             vmem_limit_bytes=VMEM_LIMIT),
        name="compress",
    )(segs, w1x, posx, b1, w2, b2, gain)


def _stack_heads(q_ref, lane_q):
    q = q_ref[0]
    return jnp.concatenate(
        [jnp.where((lane_q >= NSA_D) == (g == 1), q[:, LANES * j:LANES * (j + 1)], jnp.zeros((), BF16))
         for g in range(NSA_GROUPS) for j in range(NSA_HPG)], axis=0)


def _softmax_pv(s, v, online):
    if online:
        m = jnp.max(s, axis=1, keepdims=True)
        s = s - jnp.where(m < 0.1 * MASK_BIG, 0.0, m)
    p = jnp.exp2(s)
    acc = _dot(p.astype(BF16), jnp.concatenate([v, jnp.ones(v.shape, BF16)], axis=1))
    return p, acc[:, :LANES], acc[:, LANES:]


def _nsa_select_kernel(safe_ref, q_ref, kc_ref, vc_ref, gate_ref, bcf_ref, c2s_ref, shift_ref,
                       oc_ref, sbn_ref, sbf_ref):
    i = pl.program_id(1)
    safe = safe_ref[0] != 0
    ncmp = kc_ref.shape[1]
    n_tiles = ncmp * CMP_STRIDE // SEL_TILE
    refs = (q_ref, kc_ref, vc_ref, gate_ref, bcf_ref, c2s_ref, shift_ref, oc_ref, sbn_ref, sbf_ref)
    n_cls = max(c for c in (1, 2, 4) if n_tiles % c == 0 and (ncmp // c) % LANES == 0)
    for c in range(n_cls):
        @pl.when(safe & (i // (n_tiles // n_cls) == c))
        def _():
            _nsa_select_tile(i, ncmp * (c + 1) // n_cls, LANES * (c + 1) // n_cls, *refs, False)

    @pl.when(jnp.logical_not(safe))
    def _():
        _nsa_select_tile(i, ncmp, LANES, *refs, True)


def _nsa_select_tile(i, ncmp, nblk, q_ref, kc_ref, vc_ref, gate_ref, bcf_ref, c2s_ref, shift_ref,
                     oc_ref, sbn_ref, sbf_ref, online):
    T = SEL_TILE
    grows = NSA_HPG * T
    lane_q = lax.broadcasted_iota(jnp.int32, (T, LANES), 1)
    qpad = _stack_heads(q_ref, lane_q)

    n_io = lax.broadcasted_iota(jnp.int32, (ncmp, LANES), 0)
    l_io = lax.broadcasted_iota(jnp.int32, (ncmp, LANES), 1)
    band = jnp.clip(n_io - i * (T // CMP_STRIDE) + BAND_OFF, 0, BAND_SLOTS - 1)
    oh_c = jnp.where(band == (l_io & 63), 1.0, 0.0).astype(BF16)
    kc_aug = jnp.concatenate([kc_ref[0, :ncmp, :], oh_c], axis=1)
    qc_aug = jnp.concatenate([qpad, bcf_ref[...]], axis=1)
    p_c, num_c, l_c = _softmax_pv(_nt(qc_aug, kc_aug), vc_ref[0, :ncmp, :], online)
    inv_c = 1.0 / jnp.maximum(l_c, 1e-30)
    o_c = num_c * inv_c
    p_c = p_c * jnp.concatenate([inv_c] * (ncmp // LANES), axis=1)

    gates = gate_ref[0]
    gated = [jnp.broadcast_to(gates[:, h:h + 1], (T, LANES)) * o_c[h * T:(h + 1) * T] for h in range(NSA_HEADS)]
    for j in range(NSA_HPG):
        oc_ref[0, :, LANES * j:LANES * (j + 1)] = jnp.where(
            lane_q < NSA_D, gated[j], gated[NSA_HPG + j]).astype(oc_ref.dtype)

    imp_t = []
    for g in range(NSA_GROUPS):
        ps = p_c[g * grows:g * grows + T]
        for hh in range(1, NSA_HPG):
            ps = ps + p_c[g * grows + hh * T:g * grows + (hh + 1) * T]
        ps_hi = ps.astype(BF16)
        ps_lo = (ps - ps_hi.astype(F32)).astype(BF16)
        c2s = c2s_ref[:ncmp, :]
        imp_t.append((_dot(ps_hi, c2s) + _dot(ps_lo, c2s)).T[:nblk])
    width = NSA_GROUPS * T
    j_io = lax.broadcasted_iota(jnp.int32, (nblk, width), 0)
    r_io = lax.broadcasted_iota(jnp.int32, (nblk, width), 1) & (T - 1)
    j_f = j_io.astype(F32)
    cur = i * (T // SLC_BLOCK) + jnp.right_shift(r_io, SLC_BLOCK.bit_length() - 1)
    forced = (j_io == 0) | (j_io == cur) | (j_io == cur - 1)
    score = jnp.where(forced, -jnp.inf, jnp.where(j_io <= cur, jnp.concatenate(imp_t, axis=1), -1.0))
    for _ in range(SLC_TOPK - 3):
        best = jnp.max(score, axis=0, keepdims=True)
        first = jnp.min(jnp.where(score == best, j_f, float(nblk)), axis=0, keepdims=True)
        score = jnp.where(j_f == first, -jnp.inf, score)
    if nblk < LANES:
        score = jnp.concatenate([score, jnp.zeros((LANES - nblk, width), F32)], axis=0)
    near_start = ((i * T + lax.broadcasted_iota(jnp.int32, (T, LANES), 0)) // NSA_TILE) * (NSA_TILE // SLC_BLOCK)
    near_blk = lane_q >= near_start - SLC_PAD // SLC_BLOCK
    far_pick = shift_ref[0:1, 0:1]
    for g in range(NSA_GROUPS):
        picked = score[:, g * T:(g + 1) * T].T == -jnp.inf
        sbn_ref[0, :, LANES * g:LANES * (g + 1)] = jnp.where(picked & near_blk, 0.0, MASK_BIG).astype(BF16)
        sbf_ref[0, :, LANES * g:LANES * (g + 1)] = jnp.where(
            picked & jnp.logical_not(near_blk), far_pick, MASK_BIG).astype(BF16)


def _nsa_attend_kernel(safe_ref, *refs):
    *io_refs, acc_sc, m_sc, pa_sc, pb_sc = refs

    @pl.when(safe_ref[0] != 0)
    def _():
        _nsa_attend_tile(*io_refs, acc_sc, pa_sc, pb_sc, online=False)

    @pl.when(safe_ref[0] == 0)
    def _():
        _nsa_attend_tile(*io_refs, acc_sc, m_sc, online=True)


def _nsa_attend_tile(q_ref, ks_ref, vs_ref, kw_ref, vw_ref, oh_ref, gate_ref, oc_ref, sbn_ref, sbf_ref,
                     bs_ref, bw_ref, out_ref, acc_sc, *extra_sc, online):
    i = pl.program_id(1)
    t0 = pl.multiple_of(i * NSA_TILE, NSA_TILE)
    T = NSA_TILE
    rows = NSA_HEADS * T
    grows = NSA_HPG * T
    lane_q = lax.broadcasted_iota(jnp.int32, (T, LANES), 1)
    qpad = _stack_heads(q_ref, lane_q)

    def with_features(feat_ref):
        feats = [jnp.concatenate([feat_ref[0, :, LANES * g:LANES * (g + 1)]] * NSA_HPG, axis=0)
                 for g in range(NSA_GROUPS)]
        return jnp.concatenate([qpad, jnp.concatenate(feats, axis=0)], axis=1)

    q_near, q_far = with_features(sbn_ref), with_features(sbf_ref)

    r_w = lax.broadcasted_iota(jnp.int32, (WIN_KEYS, LANES), 0)
    pad_flag = jnp.where(r_w + (t0 - WINDOW) < 0, 1.0, 0.0).astype(BF16)
    kw_aug = jnp.concatenate([kw_ref[0, pl.ds(t0, WIN_KEYS), :], pad_flag], axis=1)
    qw_aug = jnp.concatenate([qpad, jnp.full((rows, LANES), MASK_BIG / LANES, BF16)], axis=1)
    s_w = _nt(qw_aug, kw_aug) + bw_ref[...]
    _, num_w, l_w = _softmax_pv(s_w, vw_ref[0, pl.ds(t0, WIN_KEYS), :], online)
    o_w = num_w * (1.0 / l_w)

    gates = gate_ref[0]
    gate_tile = lambda col: jnp.broadcast_to(gates[:, col:col + 1], (T, LANES))
    head_rows = [slice(h * T, (h + 1) * T) for h in range(NSA_HEADS)]
    o_gw = [gate_tile(16 + h) * o_w[head_rows[h]] for h in range(NSA_HEADS)]
    g_slc = [gate_tile(8 + h) for h in range(NSA_HEADS)]

    ones_f = jnp.ones((FAR_CHUNK, LANES), BF16)
    k_near = jnp.concatenate([ks_ref[0, pl.ds(t0, NEAR_KEYS), :], oh_ref[pl.ds(t0, NEAR_KEYS), :]], axis=1)
    v_near = jnp.concatenate([vs_ref[0, pl.ds(t0, NEAR_KEYS), :], jnp.ones((NEAR_KEYS, LANES), BF16)], axis=1)
    s_n = _nt(q_near, k_near) + bs_ref[...]
    if online:
        m_sc, = extra_sc
        m_n = jnp.max(s_n, axis=1, keepdims=True)
        m_sc[...] = m_n
        s_n = s_n - m_n
    acc_sc[...] = _dot(jnp.exp2(s_n).astype(BF16), v_near)

    last_chunk = (ks_ref.shape[1] - SLC_PAD) // FAR_CHUNK - 1

    def far_start(c):
        return pl.multiple_of(SLC_PAD + FAR_CHUNK * jnp.minimum(c, last_chunk), LANES)

    def far_scores(c, by_group=False):
        start = far_start(c)
        k_f = jnp.concatenate([ks_ref[0, pl.ds(start, FAR_CHUNK), :], oh_ref[pl.ds(start, FAR_CHUNK), :]], axis=1)
        if by_group:
            return jnp.concatenate([_nt(q_far[:grows], k_f), _nt(q_far[grows:], k_f)], axis=0)
        return _nt(q_far, k_f)

    def far_values(c):
        return jnp.concatenate([vs_ref[0, pl.ds(far_start(c), FAR_CHUNK), :], ones_f], axis=1)

    n_far = (t0 + (FAR_CHUNK - 1 - SLC_PAD)) // FAR_CHUNK
    if online:
        def far_online(c, carry):
            s_f = far_scores(c)
            m_old = m_sc[...]
            m_new = jnp.maximum(m_old, jnp.max(s_f, axis=1, keepdims=True))
            m_sc[...] = m_new
            acc_sc[...] = (jnp.exp2(m_old - m_new) * acc_sc[...]
                           + _dot(jnp.exp2(s_f - m_new).astype(BF16), far_values(c)))
            return carry

        lax.fori_loop(0, n_far, far_online, 0)
    else:
        pa_sc, pb_sc = extra_sc
        pa_sc[...] = jnp.exp2(far_scores(0, by_group=True)).astype(BF16)

        def far_pair(cc, carry):
            c0 = 2 * cc
            pv0 = _dot(pa_sc[...], far_values(c0))
            pb_sc[...] = jnp.exp2(far_scores(c0 + 1)).astype(BF16)
            pv1 = _dot(pb_sc[...], far_values(c0 + 1))
            pa_sc[...] = jnp.exp2(far_scores(c0 + 2)).astype(BF16)
            acc_sc[...] += pv0 + pv1
            return carry

        lax.fori_loop(0, n_far // 2, far_pair, 0)

        @pl.when(n_far % 2 == 1)
        def _():
            v_f = far_values(n_far - 1)
            acc_sc[0:grows, :] += _dot(pa_sc[0:grows, :], v_f)
            acc_sc[grows:, :] += _dot(pa_sc[grows:, :], v_f)

    acc = acc_sc[...]
    o_s = acc[:, :LANES] * (1.0 / acc[:, LANES:])

    head_out = [o_gw[h] + g_slc[h] * o_s[head_rows[h]] for h in range(NSA_HEADS)]
    for j in range(NSA_HPG):
        cols = slice(LANES * j, LANES * (j + 1))
        out_ref[0, :, cols] = (oc_ref[0, :, cols].astype(F32) + jnp.where(
            lane_q < NSA_D, head_out[j], head_out[NSA_HPG + j])).astype(out_ref.dtype)


def _nsa(safe, q, ks, vs, kw, vw, kc, vc, oh, gates, bcf, bs, bw, c2s, shift):
    bsz, s = q.shape[:2]
    flag = pl.BlockSpec(memory_space=pltpu.SMEM)
    T = NSA_TILE
    rows = NSA_HEADS * T
    grid = (bsz, s // T)
    params = pltpu.CompilerParams(dimension_semantics=("parallel", "arbitrary"), vmem_limit_bytes=VMEM_LIMIT)
    tile = lambda width: pl.BlockSpec((1, T, width), lambda b, i: (b, i, 0))
    per_b = lambda a: pl.BlockSpec((1,) + a.shape[1:], lambda b, i: (b, 0, 0), pipeline_mode=pl.Buffered(1))
    sel_tile = lambda width: pl.BlockSpec((1, SEL_TILE, width), lambda b, i: (b, i, 0))
    o_cmp, sb_near, sb_far = pl.pallas_call(
        _nsa_select_kernel,
        grid=(bsz, s // SEL_TILE),
        in_specs=[flag, sel_tile(NSA_HPG * LANES), per_b(kc), per_b(vc), sel_tile(LANES),
                  _const_spec(bcf.shape), _const_spec(c2s.shape), _const_spec(shift.shape)],
        out_specs=(sel_tile(NSA_HPG * LANES), sel_tile(NSA_GROUPS * LANES), sel_tile(NSA_GROUPS * LANES)),
        out_shape=(jax.ShapeDtypeStruct((bsz, s, NSA_HPG * LANES), BF16),
                   jax.ShapeDtypeStruct((bsz, s, NSA_GROUPS * LANES), BF16),
                   jax.ShapeDtypeStruct((bsz, s, NSA_GROUPS * LANES), BF16)),
        compiler_params=params,
        name="nsa_select",
    )(safe, q, kc, vc, gates, bcf, c2s, shift)
    scratch = [pltpu.VMEM((rows, 2 * LANES), F32),
               pltpu.VMEM((rows, 1), F32),
               pltpu.VMEM((rows, FAR_CHUNK), BF16),
               pltpu.VMEM((rows, FAR_CHUNK), BF16)]
    return pl.pallas_call(
        _nsa_attend_kernel,
        grid=grid,
        in_specs=[flag, tile(NSA_HPG * LANES), per_b(ks), per_b(vs), per_b(kw), per_b(vw), _const_spec(oh.shape),
                  tile(LANES), tile(NSA_HPG * LANES), tile(NSA_GROUPS * LANES), tile(NSA_GROUPS * LANES),
                  _const_spec(bs.shape), _const_spec(bw.shape)],
        out_specs=tile(NSA_HPG * LANES),
        out_shape=jax.ShapeDtypeStruct((bsz, s, NSA_HPG * LANES), BF16),
        scratch_shapes=scratch,
        compiler_params=params,
        name="nsa_attend",
    )(safe, q, ks, vs, kw, vw, oh, gates, o_cmp, sb_near, sb_far, bs, bw)


ML_ROWS = 512


def _mlstm_kernel(q_ref, k_ref, v_ref, o_ref, gif_ref, gcol_ref, gb_ref, gbrow_ref, tril_ref, triu_ref,
                  out_ref, c_sc, m_sc):
    j = pl.program_id(1)
    R = ML_ROWS
    L = ML_CHUNK

    @pl.when(j == 0)
    def _():
        c_sc[...] = jnp.zeros_like(c_sc)
        m_sc[...] = jnp.zeros_like(m_sc)

    og = o_ref[0]

    pre = gif_ref[...] + gb_ref[...]
    row8 = lax.broadcasted_iota(jnp.int32, (8, R), 0)
    logf = jnp.minimum(pre, 0.0) - jnp.log(1.0 + jnp.exp(-jnp.abs(pre)))
    g8 = jnp.where(row8 < ML_HEADS, pre, logf)
    pre_c = gcol_ref[0] + gbrow_ref[...]
    lane_c = lax.broadcasted_iota(jnp.int32, pre_c.shape, 1)
    logf_c = jnp.minimum(pre_c, 0.0) - jnp.log(1.0 + jnp.exp(-jnp.abs(pre_c)))
    g_c = jnp.where(lane_c < _IF_LANE + ML_HEADS, pre_c, logf_c)
    cum_c = _split3(g_c, lambda part: _dot(tril_ref[...], part))
    cum_row = _split3(g8, lambda part: _dot(part, triu_ref[...]))

    a_io = lax.broadcasted_iota(jnp.int32, (L, L), 0)
    b_io = lax.broadcasted_iota(jnp.int32, (L, L), 1)
    causal = (b_io <= a_io)[None]
    nc = R // L
    pairs = [(c, h) for c in range(nc) for h in range(ML_HEADS)]

    def blocks(a):
        return jnp.stack([a[c * L:(c + 1) * L, h * ML_D:(h + 1) * ML_D] for c, h in pairs])

    def cols(a, k0):
        return jnp.stack([jnp.broadcast_to(a[c * L:(c + 1) * L, k0 + h:k0 + h + 1], (L, LANES)) for c, h in pairs])

    def rows(a, k0):
        return jnp.stack([a[k0 + h:k0 + h + 1, c * L:(c + 1) * L] for c, h in pairs])

    bdot = lambda eq, x, y: jnp.einsum(eq, x, y, preferred_element_type=F32)
    qb, kb, vb = blocks(q_ref[0]), blocks(k_ref[0]), blocks(v_ref[0])
    b_col, li_col = cols(cum_c, _IF_LANE + ML_HEADS), cols(g_c, _IF_LANE)
    b_row, li_row = rows(cum_row, ML_HEADS), rows(g8, 0)
    gsum = b_row[:, :, L - 1:L]

    s_max = jnp.max(gsum - b_row + li_row, axis=2, keepdims=True)
    m_run = m_sc[:, 0:1, 0:1]
    m_ins, m_outs = [], []
    for c in range(nc):
        hs = slice(c * ML_HEADS, (c + 1) * ML_HEADS)
        m_ins.append(m_run)
        m_run = jnp.maximum(gsum[hs] + m_run, s_max[hs])
        m_outs.append(m_run)
    m_sc[...] = jnp.broadcast_to(m_run, m_sc.shape)
    m_in, m_out = jnp.concatenate(m_ins, axis=0), jnp.concatenate(m_outs, axis=0)

    log_d = jnp.where(causal, b_col[:, :, :L] - b_row + li_row, -jnp.inf)
    inter = b_col + m_in
    m_row = jnp.maximum(inter, jnp.broadcast_to(jnp.max(log_d, axis=2, keepdims=True), inter.shape))
    w = bdot('bik,bjk->bij', qb, kb) * jnp.exp(log_d - m_row[:, :, :L])
    v_aug = jnp.concatenate([vb, jnp.ones(vb.shape, BF16)], axis=2)
    wv = bdot('bij,bjd->bid', w.astype(BF16), v_aug)
    inter_scale = jnp.exp(inter - m_row)
    inter_scale = jnp.concatenate([inter_scale, inter_scale], axis=2)
    floor = jnp.exp(-m_row)
    k_src = (kb.astype(F32) * jnp.exp(gsum - b_col + li_col - m_out)).astype(BF16)
    upd = bdot('bjk,bjd->bkd', k_src, v_aug)
    decay = jnp.exp(gsum + m_in - m_out)

    c_aug = c_sc[...]
    for c in range(nc):
        hs = slice(c * ML_HEADS, (c + 1) * ML_HEADS)
        nd = inter_scale[hs] * bdot('hik,hkd->hid', qb[hs], c_aug.astype(BF16)) + wv[hs]
        hval = nd[:, :, :ML_D] / jnp.maximum(jnp.abs(nd[:, :, ML_D:]), floor[hs])
        for h in range(ML_HEADS):
            rs, cs = slice(c * L, (c + 1) * L), slice(h * ML_D, (h + 1) * ML_D)
            out_ref[0, rs, cs] = (og[rs, cs].astype(F32) * hval[h]).astype(out_ref.dtype)
        c_aug = decay[hs] * c_aug + upd[hs]
    c_sc[...] = c_aug


def _mlstm(mq, mk, mv, og, gif, gcol, gate_b, gate_b_row):
    bsz, s, width = mq.shape
    R = ML_ROWS
    nblk = s // R
    seq = lambda: pl.BlockSpec((1, R, width), lambda b, j: (b, j, 0))
    pos = np.arange(R)
    same_chunk = (pos[:, None] // ML_CHUNK) == (pos[None, :] // ML_CHUNK)
    tril = (same_chunk & (pos[None, :] <= pos[:, None])).astype(np.float32)
    return pl.pallas_call(
        _mlstm_kernel,
        grid=(bsz, nblk),
        in_specs=[seq(), seq(), seq(), seq(),
                  pl.BlockSpec((8, R), lambda b, j: (0, b * nblk + j)),
                  pl.BlockSpec((1, R, LANES), lambda b, j: (b, j, 0)),
                  _const_spec(gate_b.shape), _const_spec(gate_b_row.shape),
                  _const_spec((R, R)), _const_spec((R, R))],
        out_specs=seq(),
        out_shape=jax.ShapeDtypeStruct((bsz, s, width), BF16),
        scratch_shapes=[pltpu.VMEM((ML_HEADS, ML_D, 2 * ML_D), F32),
                        pltpu.VMEM((ML_HEADS, 8, LANES), F32)],
        compiler_params=pltpu.CompilerParams(dimension_semantics=("parallel", "arbitrary"),
                                             vmem_limit_bytes=VMEM_LIMIT),
        name="mlstm",
    )(mq, mk, mv, og, gif, gcol, gate_b, gate_b_row, jnp.asarray(tril, BF16), jnp.asarray(tril.T, BF16))


FF_CHUNK = 512


def _merge_ffn_kernel(x_ref, ya_ref, yb_ref, mg_ref, wa_ref, wb_ref, wo_ref, g2_ref, w1_ref, w2_ref,
                      out_ref):
    mg = mg_ref[...]
    mixed = (mg[:, :D_MODEL].astype(F32) * _dot(ya_ref[...], wa_ref[...])
             + mg[:, D_MODEL:].astype(F32) * _dot(yb_ref[...], wb_ref[...]))
    x1 = x_ref[...] + _dot(mixed.astype(BF16), wo_ref[...])
    h2 = x1 * lax.rsqrt(jnp.mean(x1 * x1, axis=-1, keepdims=True) + RMS_EPS) * g2_ref[...]
    h2 = h2.astype(BF16)
    acc = x1
    for c in range(D_FF // FF_CHUNK):
        a = jnp.maximum(_dot(h2, w1_ref[:, c * FF_CHUNK:(c + 1) * FF_CHUNK]), 0.0)
        acc = acc + _dot((a * a).astype(BF16), w2_ref[c * FF_CHUNK:(c + 1) * FF_CHUNK, :])
    out_ref[...] = acc


def _merge_ffn(x2d, ya, yb, mg, wa, wb, wo, g2, w1, w2, tm=512):
    n = x2d.shape[0]
    row = lambda w: pl.BlockSpec((tm, w), lambda i: (i, 0))
    return pl.pallas_call(
        _merge_ffn_kernel,
        grid=(n // tm,),
        in_specs=[row(D_MODEL), row(512), row(512), row(2048),
                  _const_spec(wa.shape), _const_spec(wb.shape), _const_spec(wo.shape),
                  _const_spec(g2.shape), _const_spec(w1.shape), _const_spec(w2.shape)],
        out_specs=row(D_MODEL),
        out_shape=jax.ShapeDtypeStruct((n, D_MODEL), F32),
        compiler_params=pltpu.CompilerParams(dimension_semantics=("parallel",),
                                             vmem_limit_bytes=VMEM_LIMIT),
        name="merge_ffn",
    )(x2d, ya, yb, mg, wa, wb, wo, g2, w1, w2)


def _proj_weights(w):
    widths = (512, 128, 128, 128, 128, 128, 128, 24, 512, 512, 512, 4, 4, 512, 2048)
    off = np.concatenate([[0], np.cumsum(widths)])
    (nq, nkc, nvc, nks, nvs, nkw, nvw, ngate, mq, mk, mv, mi, mf, mo, mgate) = (int(o) for o in off[:-1])
    col = lambda start, width: w[:, start:start + width]
    parts = [col(nq + NSA_D * (g * NSA_HPG + j), NSA_D) for j in range(NSA_HPG) for g in range(NSA_GROUPS)]
    parts += [col(nks, 128), col(nkw, 128), col(nkc, 128), col(nvc, 128), col(nvs, 128), col(nvw, 128)]
    gate = col(ngate, 24).reshape(-1, NSA_HEADS, 3).transpose(0, 2, 1).reshape(-1, 24)
    parts += [gate, col(mi, 4), col(mf, 4), jnp.zeros((w.shape[0], LANES - _IF_LANE - 8), w.dtype)]
    parts += [col(mq, 512), col(mk, 512), col(mv, 512), col(mo, 512), col(mgate, 2048)]
    w_all = jnp.concatenate(parts, axis=1).astype(BF16)
    w_if = jnp.concatenate([col(mi, 4), col(mf, 4)], axis=1).T.astype(BF16)
    return w_all, w_if


def _branch_a_weights(w):
    head = lambda h: w[NSA_D * h:NSA_D * (h + 1)]
    return jnp.concatenate([head(g * NSA_HPG + j) for j in range(NSA_HPG) for g in range(NSA_GROUPS)],
                           axis=0).astype(BF16)


def _compress_weights(pos, w1):
    r = w1.reshape(2, CMP_STRIDE, 1, NSA_D, CMP_HIDDEN)
    z = jnp.zeros_like(r)
    w1x = jnp.stack([jnp.concatenate([r, z], axis=2), jnp.concatenate([z, r], axis=2)])
    w1x = w1x.reshape(NSA_GROUPS, 2, CMP_STRIDE * NSA_GROUPS * NSA_D, CMP_HIDDEN).astype(BF16)
    posx = jnp.broadcast_to(pos.reshape(2, CMP_STRIDE, 1, NSA_D), (2, CMP_STRIDE, NSA_GROUPS, NSA_D))
    return w1x, posx.reshape(2, CMP_STRIDE * NSA_GROUPS * NSA_D).astype(BF16)


def _t5_bucket(dist):
    n = np.maximum(dist, 0)
    max_exact = REL_BUCKETS // 2
    nf = np.maximum(n, 1).astype(np.float32)
    large = max_exact + (np.log(nf / np.float32(max_exact)) / np.float32(math.log(REL_MAX_DIST / max_exact))
                         * np.float32(REL_BUCKETS - max_exact)).astype(np.int32)
    return np.where(n < max_exact, n, np.minimum(large, REL_BUCKETS - 1))


def _toeplitz(rel, n_rows, n_cols, stride, off, inner=1):
    a_rows = n_rows // inner
    lw = a_rows + n_cols
    k = np.arange(lw)[:, None]
    dist = stride * np.where(k < n_cols, -k, lw - k) + np.arange(inner)[None, :] + off
    live = (dist >= 0) & (k != n_cols)
    onehot = live[..., None] & (_t5_bucket(dist)[..., None] == np.arange(REL_BUCKETS))
    w = jnp.dot(jnp.asarray(onehot.reshape(lw * inner, REL_BUCKETS), F32), rel.T,
                precision=lax.Precision.HIGHEST)
    w = w.T.reshape(-1, lw, inner)
    flat = jnp.tile(w, (1, a_rows, 1))[:, :a_rows * (lw - 1)]
    out = flat.reshape(-1, a_rows, lw - 1, inner)[:, :, :n_cols]
    return out.transpose(0, 1, 3, 2).reshape(-1, n_rows, n_cols)


def _rel_bias(rel_table):
    return (rel_table - rel_table[REL_BUCKETS - 1][None, :]).T * LOG2E


def _bias_tables(rel, shifts):
    T = NSA_TILE
    sh_c, sh_s, sh_w = shifts
    r = np.arange(T)[:, None]

    c = np.arange(NEAR_KEYS)[None, :]
    ok = jnp.asarray((r - c + SLC_PAD) >= 0)[None]
    bs = jnp.where(ok, _toeplitz(rel, T, NEAR_KEYS, 1, SLC_PAD) - sh_s, MASK_F32)
    c = np.arange(WIN_KEYS)[None, :]
    d = r - c + WINDOW
    ok = jnp.asarray((d >= 0) & (d < WINDOW))[None]
    bw = jnp.where(ok, _toeplitz(rel, T, WIN_KEYS, 1, WINDOW) - sh_w, MASK_F32)
    r = np.arange(SEL_TILE)[:, None]
    m = np.arange(64)[None, :]
    d = r - CMP_STRIDE * m + BAND_DIST0
    band = (m >= 1) & (m < BAND_SLOTS - 1)
    vals = _toeplitz(rel, SEL_TILE, BAND_SLOTS, CMP_STRIDE, BAND_DIST0, inner=CMP_STRIDE)
    vals = jnp.pad(vals, ((0, 0), (0, 0), (0, 64 - BAND_SLOTS)))
    vals = jnp.where(jnp.asarray(band & (d >= 0))[None], vals, 0.0) - sh_c
    dead = (band & (d < 0)) | (m == BAND_SLOTS - 1)
    vals = jnp.where(jnp.asarray(dead)[None], MASK_BIG, vals)
    hi = vals.astype(BF16)
    lo = (vals - hi.astype(F32)).astype(BF16)
    bcf = jnp.concatenate([hi, lo], axis=-1)
    flat = lambda a: a.reshape(-1, a.shape[-1])
    return flat(bcf), flat(bs).astype(F32), flat(bw).astype(F32)


def _layer(l, x2d, bsz, s, consts, rel_table, norm1_g, w_in, nsa_q_gain, nsa_k_gain, cmp_k, cmp_v,
           ml_conv_w, ml_conv_b, ml_i_bias, ml_f_bias, w_branch_a, w_branch_b, w_out, norm2_g, w_ff1, w_ff2):
    n = bsz * s
    nseg = s // CMP_STRIDE
    oh, c2s = consts
    w_all, w_if = _proj_weights(w_in[l])
    qg_pad = jnp.concatenate([nsa_q_gain[l]] * 2)[None, :]
    kg_pad = jnp.stack([jnp.concatenate([nsa_k_gain[l, 1]] * 2), jnp.concatenate([nsa_k_gain[l, 2]] * 2)])
    (q, ks, kw, kc, vc, vs, vw, gates, gcol, mq, mk, mv, og, mg, gif) = _proj(
        x2d, s, norm1_g[l][None, :], w_all, w_if, qg_pad, kg_pad, ml_conv_w[l], ml_conv_b[l][None, :])

    def compress(a, params, gain, normalize):
        pos, w1, b1, w2, b2 = (p[l] for p in params)
        w1x, posx = _compress_weights(pos, w1)
        return _compress(a.reshape(bsz, nseg, CMP_STRIDE * LANES), w1x, posx, b1[None, :],
                         w2.astype(BF16), b2[None, :], gain[None, :], normalize)

    kcmp = compress(kc, cmp_k, nsa_k_gain[l, 0], True)
    vcmp = compress(vc, cmp_v, jnp.ones((NSA_D,), F32), False)

    qk_bound = lambda kg: 8.0 * LOG2E * jnp.max(jnp.abs(nsa_q_gain[l])) * jnp.max(jnp.abs(kg))
    tab = _rel_bias(rel_table)
    snap = lambda v: v.astype(BF16).astype(F32)
    shifts = [snap(qk_bound(nsa_k_gain[l, j]) + jnp.maximum(jnp.max(tab), 0.0)) for j in range(3)]
    safe = 2.0 * jnp.max(jnp.stack(shifts)) < SAFE_SHIFT_LOG2
    shifts = [jnp.where(safe, sh, 0.0) for sh in shifts]
    bcf, bs, bw = _bias_tables(tab, shifts)

    seq = lambda a: a.reshape(bsz, s, a.shape[-1])
    front = lambda a, p: jnp.pad(seq(a), ((0, 0), (p, 0), (0, 0)))
    operands = (seq(q), front(ks, SLC_PAD), front(vs, SLC_PAD), front(kw, WINDOW), front(vw, WINDOW),
                kcmp, vcmp, oh, seq(gates), bcf, bs, bw, c2s, jnp.full((1, LANES), -shifts[1], F32))
    y_a = _nsa(safe.astype(jnp.int32)[None], *operands)

    gate_b = jnp.concatenate([ml_i_bias[l], ml_f_bias[l]])
    gate_b_row = jnp.pad(gate_b, (_IF_LANE, LANES - _IF_LANE - 8))[None, :]
    y_b = _mlstm(seq(mq), seq(mk), seq(mv), seq(og), gif, seq(gcol), gate_b[:, None], gate_b_row)

    out = _merge_ffn(x2d, y_a.reshape(n, 512), y_b.reshape(n, 512), mg, _branch_a_weights(w_branch_a[l]),
                     w_branch_b[l].astype(BF16), w_out[l].astype(BF16), norm2_g[l][None, :],
                     w_ff1[l].astype(BF16), w_ff2[l].astype(BF16))
    return out, y_a, y_b


def _consts(s):
    nseg = s // CMP_STRIDE
    nsel = s // SLC_BLOCK
    blk_of_key = np.arange(s) // SLC_BLOCK
    oh = np.concatenate([np.ones((SLC_PAD, LANES), np.float32),
                         (blk_of_key[:, None] == np.arange(LANES)[None, :]).astype(np.float32)], axis=0)
    ci = np.arange(nseg)[:, None] * CMP_STRIDE
    sj = np.arange(LANES)[None, :] * SLC_BLOCK
    c2s = ((ci < sj + SLC_BLOCK) & (ci + CMP_LEN > sj) & (np.arange(LANES)[None, :] < nsel)
           & (np.arange(nseg)[:, None] < nseg - 1))
    return jnp.asarray(oh, BF16), jnp.asarray(c2s.astype(np.float32), BF16)


def kernel(x, norm1_g, w_in, nsa_q_gain, nsa_k_gain, cmp_k_pos, cmp_k_w1, cmp_k_b1, cmp_k_w2, cmp_k_b2, cmp_v_pos, cmp_v_w1, cmp_v_b1, cmp_v_w2, cmp_v_b2, rel_table, ml_conv_w, ml_conv_b, ml_i_bias, ml_f_bias, w_branch_a, w_branch_b, w_out, norm2_g, w_ff1, w_ff2):
    bsz, s, _ = x.shape
    consts = _consts(s)
    x2d = x.reshape(bsz * s, D_MODEL)
    for l in range(norm1_g.shape[0]):
        x2d, _, _ = _layer(l, x2d, bsz, s, consts, rel_table, norm1_g, w_in, nsa_q_gain, nsa_k_gain,
                           (cmp_k_pos, cmp_k_w1, cmp_k_b1, cmp_k_w2, cmp_k_b2),
                           (cmp_v_pos, cmp_v_w1, cmp_v_b1, cmp_v_w2, cmp_v_b2),
                           ml_conv_w, ml_conv_b, ml_i_bias, ml_f_bias,
                           w_branch_a, w_branch_b, w_out, norm2_g, w_ff1, w_ff2)
    return x2d.reshape(bsz, s, D_MODEL)
```

```python
import functools
import math

import numpy as np
import jax
import jax.numpy as jnp
from jax import lax
from jax.experimental import pallas as pl
from jax.experimental.pallas import tpu as pltpu

F32 = jnp.float32
BF16 = jnp.bfloat16

D_MODEL = 1024
NSA_HEADS = 8
NSA_GROUPS = 2
NSA_HPG = NSA_HEADS // NSA_GROUPS
NSA_D = 64
CMP_LEN = 32
CMP_STRIDE = 16
CMP_HIDDEN = 256
SLC_BLOCK = 64
SLC_TOPK = 16
WINDOW = 512
FORCED_SCORE = 1e4
ML_HEADS = 4
ML_D = 128
ML_CHUNK = 64
CONV_WIDTH = 4
D_FF = 4 * D_MODEL
REL_BUCKETS = 32
REL_MAX_DIST = 128
RMS_EPS = 1e-6

LANES = 128
NSA_TILE = 256
SEL_TILE = 512
SLC_PAD = 256
FAR_CHUNK = 512
NEAR_KEYS = SLC_PAD + NSA_TILE
WIN_KEYS = WINDOW + NSA_TILE
BAND_OFF = 10
BAND_SLOTS = 2 + (SEL_TILE + 128) // CMP_STRIDE
BAND_DIST0 = CMP_STRIDE * BAND_OFF - (CMP_LEN - 1)
MASK_BIG = -1e9
MASK_F32 = -1e30
LOG2E = math.log2(math.e)
SAFE_SHIFT_LOG2 = 50.0
VMEM_LIMIT = 56 * 1024 * 1024

_Q_OFF, _KS_OFF, _KW_OFF, _KC_OFF, _VC_OFF, _VS_OFF, _VW_OFF = 0, 512, 640, 768, 896, 1024, 1152
_GATE_OFF, _MQ_OFF, _MK_OFF, _MV_OFF, _MO_OFF, _MG_OFF, _W_COLS = 1280, 1408, 1920, 2432, 2944, 3456, 5504
_IF_LANE = 24


def _nt(a, b, precision=None):
    return lax.dot_general(a, b, (((1,), (1,)), ((), ())), precision=precision,
                           preferred_element_type=F32)


def _dot(a, b, precision=None):
    return jnp.dot(a, b, precision=precision, preferred_element_type=F32)


def _split3(x, dot_part):
    hi = x.astype(BF16)
    r1 = x - hi.astype(F32)
    mid = r1.astype(BF16)
    lo = (r1 - mid.astype(F32)).astype(BF16)
    return dot_part(hi) + dot_part(mid) + dot_part(lo)


def _const_spec(shape):
    nd = len(shape)
    return pl.BlockSpec(shape, lambda *_: (0,) * nd, pipeline_mode=pl.Buffered(1))


def _proj_kernel(x_ref, g1_ref, w_ref, wif_ref, qg_ref, kg_ref, cw_ref, cb_ref,
                 q_ref, ks_ref, kw_ref, kc_ref, vc_ref, vs_ref, vw_ref, gate_ref, gcol_ref,
                 mq_ref, mk_ref, mv_ref, mo_ref, mg_ref, gif_ref, ext_sc, *, tiles_per_seq):
    tm = x_ref.shape[0]
    x = x_ref[...]
    h = x * lax.rsqrt(jnp.mean(x * x, axis=-1, keepdims=True) + RMS_EPS) * g1_ref[...]
    hb = h.astype(BF16)

    def proj(off, width):
        return _dot(hb, w_ref[:, off:off + width])

    lane = lax.broadcasted_iota(jnp.int32, (1, LANES), 1)
    low = lane < NSA_D

    def half_norm(blk):
        sq = blk * blk
        ms0 = jnp.sum(jnp.where(low, sq, 0.0), axis=-1, keepdims=True) * (1.0 / NSA_D)
        ms1 = jnp.sum(jnp.where(low, 0.0, sq), axis=-1, keepdims=True) * (1.0 / NSA_D)
        return blk * jnp.where(low, lax.rsqrt(ms0 + RMS_EPS), lax.rsqrt(ms1 + RMS_EPS))

    for pair in range(NSA_HPG // 2):
        both = proj(_Q_OFF + 2 * LANES * pair, 2 * LANES)
        for hh in range(2):
            qn = half_norm(both[:, LANES * hh:LANES * (hh + 1)]) * qg_ref[...] * (NSA_D ** -0.5 * LOG2E)
            col = LANES * (2 * pair + hh)
            q_ref[:, col:col + LANES] = qn.astype(q_ref.dtype)

    both = proj(_KS_OFF, 2 * LANES)
    for hh, ref in enumerate((ks_ref, kw_ref)):
        ref[...] = (half_norm(both[:, LANES * hh:LANES * (hh + 1)]) * kg_ref[hh:hh + 1, :]).astype(ref.dtype)

    for off, refs in ((_KC_OFF, (kc_ref, vc_ref)), (_VS_OFF, (vs_ref, vw_ref))):
        both = proj(off, 2 * LANES)
        for hh, ref in enumerate(refs):
            ref[...] = both[:, LANES * hh:LANES * (hh + 1)].astype(ref.dtype)

    slab = proj(_GATE_OFF, LANES)
    gate_ref[...] = jax.nn.sigmoid(slab)
    gcol_ref[...] = slab

    @pl.when(pl.program_id(0) % tiles_per_seq == 0)
    def _():
        ext_sc[0:8, :] = jnp.zeros((8, ext_sc.shape[1]), F32)

    ext_sc[8:, 0:512] = proj(_MQ_OFF, 512)
    ext_sc[8:, 512:1024] = proj(_MK_OFF, 512)
    conv = cb_ref[...]
    for t in range(CONV_WIDTH):
        lo = 8 - (CONV_WIDTH - 1) + t
        conv = conv + ext_sc[lo:lo + tm, :] * cw_ref[t:t + 1, :]
    ext_sc[0:8, :] = ext_sc[tm:tm + 8, :]
    qk = conv * jax.nn.sigmoid(conv)
    mq_ref[...] = (qk[:, :512] * (ML_D ** -0.5)).astype(mq_ref.dtype)
    mk_ref[...] = qk[:, 512:].astype(mk_ref.dtype)

    mv_ref[...] = proj(_MV_OFF, 512).astype(mv_ref.dtype)
    mo_ref[...] = jax.nn.sigmoid(proj(_MO_OFF, 512)).astype(mo_ref.dtype)
    for c in range(4):
        mg_ref[:, 512 * c:512 * (c + 1)] = jax.nn.sigmoid(proj(_MG_OFF + 512 * c, 512)).astype(mg_ref.dtype)
    gif_ref[...] = _nt(wif_ref[...], hb)


def _proj(x2d, seq_len, g1, w_all, w_if, qg_pad, kg_pad, conv_w, conv_b, tm=512):
    n = x2d.shape[0]
    row = lambda w: pl.BlockSpec((tm, w), lambda i: (i, 0))
    out_shapes = (
        jax.ShapeDtypeStruct((n, 512), BF16),
        jax.ShapeDtypeStruct((n, LANES), BF16),
        jax.ShapeDtypeStruct((n, LANES), BF16),
        jax.ShapeDtypeStruct((n, LANES), BF16),
        jax.ShapeDtypeStruct((n, LANES), BF16),
        jax.ShapeDtypeStruct((n, LANES), BF16),
        jax.ShapeDtypeStruct((n, LANES), BF16),
        jax.ShapeDtypeStruct((n, LANES), F32),
        jax.ShapeDtypeStruct((n, LANES), F32),
        jax.ShapeDtypeStruct((n, 512), BF16),
        jax.ShapeDtypeStruct((n, 512), BF16),
        jax.ShapeDtypeStruct((n, 512), BF16),
        jax.ShapeDtypeStruct((n, 512), BF16),
        jax.ShapeDtypeStruct((n, 2048), BF16),
        jax.ShapeDtypeStruct((8, n), F32),
    )
    out_specs = (row(512),) + (row(LANES),) * 8 + (row(512),) * 4 + (row(2048),
                 pl.BlockSpec((8, tm), lambda i: (0, i)))
    return pl.pallas_call(
        functools.partial(_proj_kernel, tiles_per_seq=seq_len // tm),
        grid=(n // tm,),
        in_specs=[row(D_MODEL), _const_spec((1, D_MODEL)), _const_spec((D_MODEL, _W_COLS)),
                  _const_spec((8, D_MODEL)), _const_spec((1, LANES)), _const_spec((2, LANES)),
                  _const_spec(conv_w.shape), _const_spec(conv_b.shape)],
        out_specs=out_specs,
        out_shape=out_shapes,
        scratch_shapes=[pltpu.VMEM((tm + 8, 1024), F32)],
        compiler_params=pltpu.CompilerParams(dimension_semantics=("arbitrary",),
                                             vmem_limit_bytes=VMEM_LIMIT),
        name="proj",
    )(x2d, g1, w_all, w_if, qg_pad, kg_pad, conv_w, conv_b)


def _compress_kernel(seg_ref, w1_ref, pos_ref, b1_ref, w2_ref, b2_ref, gain_ref, out_ref, *, normalize):
    seg = seg_ref[0]
    nseg, width = seg.shape
    c = math.sqrt(2.0 / math.pi)
    pos_lo = jnp.broadcast_to(pos_ref[0:1, :], (8, width))
    pos_hi = jnp.broadcast_to(pos_ref[1:2, :], (8, width))
    outs = []
    for g in range(NSA_GROUPS):
        a = _dot(seg, w1_ref[g, 0])
        b = _dot(seg, w1_ref[g, 1])
        posb = (_dot(pos_lo, w1_ref[g, 0]) + _dot(pos_hi, w1_ref[g, 1]))[0:1] + b1_ref[...]
        pre = a + pltpu.roll(b, nseg - 1, 0) + posb
        hid = 0.5 * pre * (1.0 + jnp.tanh(c * (pre + 0.044715 * (pre * pre * pre))))
        out = _dot(hid.astype(BF16), w2_ref[...]) + b2_ref[...]
        if normalize:
            ms = jnp.mean(out * out, axis=-1, keepdims=True)
            out = out * lax.rsqrt(ms + RMS_EPS) * gain_ref[...]
        outs.append(out)
    out_ref[0] = jnp.concatenate(outs, axis=1).astype(out_ref.dtype)


def _compress(segs, w1x, posx, b1, w2, b2, gain, normalize):
    bsz, nseg, width = segs.shape
    return pl.pallas_call(
        functools.partial(_compress_kernel, normalize=normalize),
        grid=(bsz,),
        in_specs=[pl.BlockSpec((1, nseg, width), lambda b: (b, 0, 0)),
                  _const_spec(w1x.shape), _const_spec(posx.shape), _const_spec(b1.shape),
                  _const_spec(w2.shape), _const_spec(b2.shape), _const_spec(gain.shape)],
        out_specs=pl.BlockSpec((1, nseg, LANES), lambda b: (b, 0, 0)),
        out_shape=jax.ShapeDtypeStruct((bsz, nseg, LANES), BF16),
        compiler_params=pltpu.CompilerParams(dimension_semantics=("parallel",),
                                             vmem_limit_bytes=VMEM_LIMIT),
        name="compress",
    )(segs, w1x, posx, b1, w2, b2, gain)


def _stack_heads(q_ref, lane_q):
    q = q_ref[0]
    return jnp.concatenate(
        [jnp.where((lane_q >= NSA_D) == (g == 1), q[:, LANES * j:LANES * (j + 1)], jnp.zeros((), BF16))
         for g in range(NSA_GROUPS) for j in range(NSA_HPG)], axis=0)


def _softmax_pv(s, v, online):
    if online:
        m = jnp.max(s, axis=1, keepdims=True)
        s = s - jnp.where(m < 0.1 * MASK_BIG, 0.0, m)
    p = jnp.exp2(s)
    acc = _dot(p.astype(BF16), jnp.concatenate([v, jnp.ones(v.shape, BF16)], axis=1))
    return p, acc[:, :LANES], acc[:, LANES:]


def _nsa_select_kernel(safe_ref, q_ref, kc_ref, vc_ref, gate_ref, bcf_ref, c2s_ref, shift_ref,
                       oc_ref, sbn_ref, sbf_ref):
    i = pl.program_id(1)
    safe = safe_ref[0] != 0
    ncmp = kc_ref.shape[1]
    n_tiles = ncmp * CMP_STRIDE // SEL_TILE
    refs = (q_ref, kc_ref, vc_ref, gate_ref, bcf_ref, c2s_ref, shift_ref, oc_ref, sbn_ref, sbf_ref)
    n_cls = max(c for c in (1, 2, 4) if n_tiles % c == 0 and (ncmp // c) % LANES == 0)
    for c in range(n_cls):
        @pl.when(safe & (i // (n_tiles // n_cls) == c))
        def _():
            _nsa_select_tile(i, ncmp * (c + 1) // n_cls, LANES * (c + 1) // n_cls, *refs, False)

    @pl.when(jnp.logical_not(safe))
    def _():
        _nsa_select_tile(i, ncmp, LANES, *refs, True)


def _nsa_select_tile(i, ncmp, nblk, q_ref, kc_ref, vc_ref, gate_ref, bcf_ref, c2s_ref, shift_ref,
                     oc_ref, sbn_ref, sbf_ref, online):
    T = SEL_TILE
    grows = NSA_HPG * T
    lane_q = lax.broadcasted_iota(jnp.int32, (T, LANES), 1)
    qpad = _stack_heads(q_ref, lane_q)

    n_io = lax.broadcasted_iota(jnp.int32, (ncmp, LANES), 0)
    l_io = lax.broadcasted_iota(jnp.int32, (ncmp, LANES), 1)
    band = jnp.clip(n_io - i * (T // CMP_STRIDE) + BAND_OFF, 0, BAND_SLOTS - 1)
    oh_c = jnp.where(band == (l_io & 63), 1.0, 0.0).astype(BF16)
    kc_aug = jnp.concatenate([kc_ref[0, :ncmp, :], oh_c], axis=1)
    qc_aug = jnp.concatenate([qpad, bcf_ref[...]], axis=1)
    p_c, num_c, l_c = _softmax_pv(_nt(qc_aug, kc_aug), vc_ref[0, :ncmp, :], online)
    inv_c = 1.0 / jnp.maximum(l_c, 1e-30)
    o_c = num_c * inv_c
    p_c = p_c * jnp.concatenate([inv_c] * (ncmp // LANES), axis=1)

    gates = gate_ref[0]
    gated = [jnp.broadcast_to(gates[:, h:h + 1], (T, LANES)) * o_c[h * T:(h + 1) * T] for h in range(NSA_HEADS)]
    for j in range(NSA_HPG):
        oc_ref[0, :, LANES * j:LANES * (j + 1)] = jnp.where(
            lane_q < NSA_D, gated[j], gated[NSA_HPG + j]).astype(oc_ref.dtype)

    imp_t = []
    for g in range(NSA_GROUPS):
        ps = p_c[g * grows:g * grows + T]
        for hh in range(1, NSA_HPG):
            ps = ps + p_c[g * grows + hh * T:g * grows + (hh + 1) * T]
        ps_hi = ps.astype(BF16)
        ps_lo = (ps - ps_hi.astype(F32)).astype(BF16)
        c2s = c2s_ref[:ncmp, :]
        imp_t.append((_dot(ps_hi, c2s) + _dot(ps_lo, c2s)).T[:nblk])
    width = NSA_GROUPS * T
    j_io = lax.broadcasted_iota(jnp.int32, (nblk, width), 0)
    r_io = lax.broadcasted_iota(jnp.int32, (nblk, width), 1) & (T - 1)
    j_f = j_io.astype(F32)
    cur = i * (T // SLC_BLOCK) + jnp.right_shift(r_io, SLC_BLOCK.bit_length() - 1)
    forced = (j_io == 0) | (j_io == cur) | (j_io == cur - 1)
    score = jnp.where(forced, -jnp.inf, jnp.where(j_io <= cur, jnp.concatenate(imp_t, axis=1), -1.0))
    for _ in range(SLC_TOPK - 3):
        best = jnp.max(score, axis=0, keepdims=True)
        first = jnp.min(jnp.where(score == best, j_f, float(nblk)), axis=0, keepdims=True)
        score = jnp.where(j_f == first, -jnp.inf, score)
    if nblk < LANES:
        score = jnp.concatenate([score, jnp.zeros((LANES - nblk, width), F32)], axis=0)
    near_start = ((i * T + lax.broadcasted_iota(jnp.int32, (T, LANES), 0)) // NSA_TILE) * (NSA_TILE // SLC_BLOCK)
    near_blk = lane_q >= near_start - SLC_PAD // SLC_BLOCK
    far_pick = shift_ref[0:1, 0:1]
    for g in range(NSA_GROUPS):
        picked = score[:, g * T:(g + 1) * T].T == -jnp.inf
        sbn_ref[0, :, LANES * g:LANES * (g + 1)] = jnp.where(picked & near_blk, 0.0, MASK_BIG).astype(BF16)
        sbf_ref[0, :, LANES * g:LANES * (g + 1)] = jnp.where(
            picked & jnp.logical_not(near_blk), far_pick, MASK_BIG).astype(BF16)


def _nsa_attend_kernel(safe_ref, *refs):
    *io_refs, acc_sc, m_sc, pa_sc, pb_sc = refs

    @pl.when(safe_ref[0] != 0)
    def _():
        _nsa_attend_tile(*io_refs, acc_sc, pa_sc, pb_sc, online=False)

    @pl.when(safe_ref[0] == 0)
    def _():
        _nsa_attend_tile(*io_refs, acc_sc, m_sc, online=True)


def _nsa_attend_tile(q_ref, ks_ref, vs_ref, kw_ref, vw_ref, oh_ref, gate_ref, oc_ref, sbn_ref, sbf_ref,
                     bs_ref, bw_ref, out_ref, acc_sc, *extra_sc, online):
    i = pl.program_id(1)
    t0 = pl.multiple_of(i * NSA_TILE, NSA_TILE)
    T = NSA_TILE
    rows = NSA_HEADS * T
    grows = NSA_HPG * T
    lane_q = lax.broadcasted_iota(jnp.int32, (T, LANES), 1)
    qpad = _stack_heads(q_ref, lane_q)

    def with_features(feat_ref):
        feats = [jnp.concatenate([feat_ref[0, :, LANES * g:LANES * (g + 1)]] * NSA_HPG, axis=0)
                 for g in range(NSA_GROUPS)]
        return jnp.concatenate([qpad, jnp.concatenate(feats, axis=0)], axis=1)

    q_near, q_far = with_features(sbn_ref), with_features(sbf_ref)

    r_w = lax.broadcasted_iota(jnp.int32, (WIN_KEYS, LANES), 0)
    pad_flag = jnp.where(r_w + (t0 - WINDOW) < 0, 1.0, 0.0).astype(BF16)
    kw_aug = jnp.concatenate([kw_ref[0, pl.ds(t0, WIN_KEYS), :], pad_flag], axis=1)
    qw_aug = jnp.concatenate([qpad, jnp.full((rows, LANES), MASK_BIG / LANES, BF16)], axis=1)
    s_w = _nt(qw_aug, kw_aug) + bw_ref[...]
    _, num_w, l_w = _softmax_pv(s_w, vw_ref[0, pl.ds(t0, WIN_KEYS), :], online)
    o_w = num_w * (1.0 / l_w)

    gates = gate_ref[0]
    gate_tile = lambda col: jnp.broadcast_to(gates[:, col:col + 1], (T, LANES))
    head_rows = [slice(h * T, (h + 1) * T) for h in range(NSA_HEADS)]
    o_gw = [gate_tile(16 + h) * o_w[head_rows[h]] for h in range(NSA_HEADS)]
    g_slc = [gate_tile(8 + h) for h in range(NSA_HEADS)]

    ones_f = jnp.ones((FAR_CHUNK, LANES), BF16)
    k_near = jnp.concatenate([ks_ref[0, pl.ds(t0, NEAR_KEYS), :], oh_ref[pl.ds(t0, NEAR_KEYS), :]], axis=1)
    v_near = jnp.concatenate([vs_ref[0, pl.ds(t0, NEAR_KEYS), :], jnp.ones((NEAR_KEYS, LANES), BF16)], axis=1)
    s_n = _nt(q_near, k_near) + bs_ref[...]
    if online:
        m_sc, = extra_sc
        m_n = jnp.max(s_n, axis=1, keepdims=True)
        m_sc[...] = m_n
        s_n = s_n - m_n
    acc_sc[...] = _dot(jnp.exp2(s_n).astype(BF16), v_near)

    last_chunk = (ks_ref.shape[1] - SLC_PAD) // FAR_CHUNK - 1

    def far_start(c):
        return pl.multiple_of(SLC_PAD + FAR_CHUNK * jnp.minimum(c, last_chunk), LANES)

    def far_scores(c, by_group=False):
        start = far_start(c)
        k_f = jnp.concatenate([ks_ref[0, pl.ds(start, FAR_CHUNK), :], oh_ref[pl.ds(start, FAR_CHUNK), :]], axis=1)
        if by_group:
            return jnp.concatenate([_nt(q_far[:grows], k_f), _nt(q_far[grows:], k_f)], axis=0)
        return _nt(q_far, k_f)

    def far_values(c):
        return jnp.concatenate([vs_ref[0, pl.ds(far_start(c), FAR_CHUNK), :], ones_f], axis=1)

    n_far = (t0 + (FAR_CHUNK - 1 - SLC_PAD)) // FAR_CHUNK
    if online:
        def far_online(c, carry):
            s_f = far_scores(c)
            m_old = m_sc[...]
            m_new = jnp.maximum(m_old, jnp.max(s_f, axis=1, keepdims=True))
            m_sc[...] = m_new
            acc_sc[...] = (jnp.exp2(m_old - m_new) * acc_sc[...]
                           + _dot(jnp.exp2(s_f - m_new).astype(BF16), far_values(c)))
            return carry

        lax.fori_loop(0, n_far, far_online, 0)
    else:
        pa_sc, pb_sc = extra_sc
        pa_sc[...] = jnp.exp2(far_scores(0, by_group=True)).astype(BF16)

        def far_pair(cc, carry):
            c0 = 2 * cc
            pv0 = _dot(pa_sc[...], far_values(c0))
            pb_sc[...] = jnp.exp2(far_scores(c0 + 1)).astype(BF16)
            pv1 = _dot(pb_sc[...], far_values(c0 + 1))
            pa_sc[...] = jnp.exp2(far_scores(c0 + 2)).astype(BF16)
            acc_sc[...] += pv0 + pv1
            return carry

        lax.fori_loop(0, n_far // 2, far_pair, 0)

        @pl.when(n_far % 2 == 1)
        def _():
            v_f = far_values(n_far - 1)
            acc_sc[0:grows, :] += _dot(pa_sc[0:grows, :], v_f)
            acc_sc[grows:, :] += _dot(pa_sc[grows:, :], v_f)

    acc = acc_sc[...]
    o_s = acc[:, :LANES] * (1.0 / acc[:, LANES:])

    head_out = [o_gw[h] + g_slc[h] * o_s[head_rows[h]] for h in range(NSA_HEADS)]
    for j in range(NSA_HPG):
        cols = slice(LANES * j, LANES * (j + 1))
        out_ref[0, :, cols] = (oc_ref[0, :, cols].astype(F32) + jnp.where(
            lane_q < NSA_D, head_out[j], head_out[NSA_HPG + j])).astype(out_ref.dtype)


def _nsa(safe, q, ks, vs, kw, vw, kc, vc, oh, gates, bcf, bs, bw, c2s, shift):
    bsz, s = q.shape[:2]
    flag = pl.BlockSpec(memory_space=pltpu.SMEM)
    T = NSA_TILE
    rows = NSA_HEADS * T
    grid = (bsz, s // T)
    params = pltpu.CompilerParams(dimension_semantics=("parallel", "arbitrary"), vmem_limit_bytes=VMEM_LIMIT)
    tile = lambda width: pl.BlockSpec((1, T, width), lambda b, i: (b, i, 0))
    per_b = lambda a: pl.BlockSpec((1,) + a.shape[1:], lambda b, i: (b, 0, 0), pipeline_mode=pl.Buffered(1))
    sel_tile = lambda width: pl.BlockSpec((1, SEL_TILE, width), lambda b, i: (b, i, 0))
    o_cmp, sb_near, sb_far = pl.pallas_call(
        _nsa_select_kernel,
        grid=(bsz, s // SEL_TILE),
        in_specs=[flag, sel_tile(NSA_HPG * LANES), per_b(kc), per_b(vc), sel_tile(LANES),
                  _const_spec(bcf.shape), _const_spec(c2s.shape), _const_spec(shift.shape)],
        out_specs=(sel_tile(NSA_HPG * LANES), sel_tile(NSA_GROUPS * LANES), sel_tile(NSA_GROUPS * LANES)),
        out_shape=(jax.ShapeDtypeStruct((bsz, s, NSA_HPG * LANES), BF16),
                   jax.ShapeDtypeStruct((bsz, s, NSA_GROUPS * LANES), BF16),
                   jax.ShapeDtypeStruct((bsz, s, NSA_GROUPS * LANES), BF16)),
        compiler_params=params,
        name="nsa_select",
    )(safe, q, kc, vc, gates, bcf, c2s, shift)
    scratch = [pltpu.VMEM((rows, 2 * LANES), F32),
               pltpu.VMEM((rows, 1), F32),
               pltpu.VMEM((rows, FAR_CHUNK), BF16),
               pltpu.VMEM((rows, FAR_CHUNK), BF16)]
    return pl.pallas_call(
        _nsa_attend_kernel,
        grid=grid,
        in_specs=[flag, tile(NSA_HPG * LANES), per_b(ks), per_b(vs), per_b(kw), per_b(vw), _const_spec(oh.shape),
                  tile(LANES), tile(NSA_HPG * LANES), tile(NSA_GROUPS * LANES), tile(NSA_GROUPS * LANES),
                  _const_spec(bs.shape), _const_spec(bw.shape)],
        out_specs=tile(NSA_HPG * LANES),
        out_shape=jax.ShapeDtypeStruct((bsz, s, NSA_HPG * LANES), BF16),
        scratch_shapes=scratch,
        compiler_params=params,
        name="nsa_attend",
    )(safe, q, ks, vs, kw, vw, oh, gates, o_cmp, sb_near, sb_far, bs, bw)


ML_ROWS = 512


def _mlstm_kernel(q_ref, k_ref, v_ref, o_ref, gif_ref, gcol_ref, gb_ref, gbrow_ref, tril_ref, triu_ref,
                  out_ref, c_sc, m_sc):
    j = pl.program_id(1)
    R = ML_ROWS
    L = ML_CHUNK

    @pl.when(j == 0)
    def _():
        c_sc[...] = jnp.zeros_like(c_sc)
        m_sc[...] = jnp.zeros_like(m_sc)

    og = o_ref[0]

    pre = gif_ref[...] + gb_ref[...]
    row8 = lax.broadcasted_iota(jnp.int32, (8, R), 0)
    logf = jnp.minimum(pre, 0.0) - jnp.log(1.0 + jnp.exp(-jnp.abs(pre)))
    g8 = jnp.where(row8 < ML_HEADS, pre, logf)
    pre_c = gcol_ref[0] + gbrow_ref[...]
    lane_c = lax.broadcasted_iota(jnp.int32, pre_c.shape, 1)
    logf_c = jnp.minimum(pre_c, 0.0) - jnp.log(1.0 + jnp.exp(-jnp.abs(pre_c)))
    g_c = jnp.where(lane_c < _IF_LANE + ML_HEADS, pre_c, logf_c)
    cum_c = _split3(g_c, lambda part: _dot(tril_ref[...], part))
    cum_row = _split3(g8, lambda part: _dot(part, triu_ref[...]))

    a_io = lax.broadcasted_iota(jnp.int32, (L, L), 0)
    b_io = lax.broadcasted_iota(jnp.int32, (L, L), 1)
    causal = (b_io <= a_io)[None]
    nc = R // L
    pairs = [(c, h) for c in range(nc) for h in range(ML_HEADS)]

    def blocks(a):
        return jnp.stack([a[c * L:(c + 1) * L, h * ML_D:(h + 1) * ML_D] for c, h in pairs])

    def cols(a, k0):
        return jnp.stack([jnp.broadcast_to(a[c * L:(c + 1) * L, k0 + h:k0 + h + 1], (L, LANES)) for c, h in pairs])

    def rows(a, k0):
        return jnp.stack([a[k0 + h:k0 + h + 1, c * L:(c + 1) * L] for c, h in pairs])

    bdot = lambda eq, x, y: jnp.einsum(eq, x, y, preferred_element_type=F32)
    qb, kb, vb = blocks(q_ref[0]), blocks(k_ref[0]), blocks(v_ref[0])
    b_col, li_col = cols(cum_c, _IF_LANE + ML_HEADS), cols(g_c, _IF_LANE)
    b_row, li_row = rows(cum_row, ML_HEADS), rows(g8, 0)
    gsum = b_row[:, :, L - 1:L]

    s_max = jnp.max(gsum - b_row + li_row, axis=2, keepdims=True)
    m_run = m_sc[:, 0:1, 0:1]
    m_ins, m_outs = [], []
    for c in range(nc):
        hs = slice(c * ML_HEADS, (c + 1) * ML_HEADS)
        m_ins.append(m_run)
        m_run = jnp.maximum(gsum[hs] + m_run, s_max[hs])
        m_outs.append(m_run)
    m_sc[...] = jnp.broadcast_to(m_run, m_sc.shape)
    m_in, m_out = jnp.concatenate(m_ins, axis=0), jnp.concatenate(m_outs, axis=0)

    log_d = jnp.where(causal, b_col[:, :, :L] - b_row + li_row, -jnp.inf)
    inter = b_col + m_in
    m_row = jnp.maximum(inter, jnp.broadcast_to(jnp.max(log_d, axis=2, keepdims=True), inter.shape))
    w = bdot('bik,bjk->bij', qb, kb) * jnp.exp(log_d - m_row[:, :, :L])
    v_aug = jnp.concatenate([vb, jnp.ones(vb.shape, BF16)], axis=2)
    wv = bdot('bij,bjd->bid', w.astype(BF16), v_aug)
    inter_scale = jnp.exp(inter - m_row)
    inter_scale = jnp.concatenate([inter_scale, inter_scale], axis=2)
    floor = jnp.exp(-m_row)
    k_src = (kb.astype(F32) * jnp.exp(gsum - b_col + li_col - m_out)).astype(BF16)
    upd = bdot('bjk,bjd->bkd', k_src, v_aug)
    decay = jnp.exp(gsum + m_in - m_out)

    c_aug = c_sc[...]
    for c in range(nc):
        hs = slice(c * ML_HEADS, (c + 1) * ML_HEADS)
        nd = inter_scale[hs] * bdot('hik,hkd->hid', qb[hs], c_aug.astype(BF16)) + wv[hs]
        hval = nd[:, :, :ML_D] / jnp.maximum(jnp.abs(nd[:, :, ML_D:]), floor[hs])
        for h in range(ML_HEADS):
            rs, cs = slice(c * L, (c + 1) * L), slice(h * ML_D, (h + 1) * ML_D)
            out_ref[0, rs, cs] = (og[rs, cs].astype(F32) * hval[h]).astype(out_ref.dtype)
        c_aug = decay[hs] * c_aug + upd[hs]
    c_sc[...] = c_aug


def _mlstm(mq, mk, mv, og, gif, gcol, gate_b, gate_b_row):
    bsz, s, width = mq.shape
    R = ML_ROWS
    nblk = s // R
    seq = lambda: pl.BlockSpec((1, R, width), lambda b, j: (b, j, 0))
    pos = np.arange(R)
    same_chunk = (pos[:, None] // ML_CHUNK) == (pos[None, :] // ML_CHUNK)
    tril = (same_chunk & (pos[None, :] <= pos[:, None])).astype(np.float32)
    return pl.pallas_call(
        _mlstm_kernel,
        grid=(bsz, nblk),
        in_specs=[seq(), seq(), seq(), seq(),
                  pl.BlockSpec((8, R), lambda b, j: (0, b * nblk + j)),
                  pl.BlockSpec((1, R, LANES), lambda b, j: (b, j, 0)),
                  _const_spec(gate_b.shape), _const_spec(gate_b_row.shape),
                  _const_spec((R, R)), _const_spec((R, R))],
        out_specs=seq(),
        out_shape=jax.ShapeDtypeStruct((bsz, s, width), BF16),
        scratch_shapes=[pltpu.VMEM((ML_HEADS, ML_D, 2 * ML_D), F32),
                        pltpu.VMEM((ML_HEADS, 8, LANES), F32)],
        compiler_params=pltpu.CompilerParams(dimension_semantics=("parallel", "arbitrary"),
                                             vmem_limit_bytes=VMEM_LIMIT),
        name="mlstm",
    )(mq, mk, mv, og, gif, gcol, gate_b, gate_b_row, jnp.asarray(tril, BF16), jnp.asarray(tril.T, BF16))


FF_CHUNK = 512


def _merge_ffn_kernel(x_ref, ya_ref, yb_ref, mg_ref, wa_ref, wb_ref, wo_ref, g2_ref, w1_ref, w2_ref,
                      out_ref):
    mg = mg_ref[...]
    mixed = (mg[:, :D_MODEL].astype(F32) * _dot(ya_ref[...], wa_ref[...])
             + mg[:, D_MODEL:].astype(F32) * _dot(yb_ref[...], wb_ref[...]))
    x1 = x_ref[...] + _dot(mixed.astype(BF16), wo_ref[...])
    h2 = x1 * lax.rsqrt(jnp.mean(x1 * x1, axis=-1, keepdims=True) + RMS_EPS) * g2_ref[...]
    h2 = h2.astype(BF16)
    acc = x1
    for c in range(D_FF // FF_CHUNK):
        a = jnp.maximum(_dot(h2, w1_ref[:, c * FF_CHUNK:(c + 1) * FF_CHUNK]), 0.0)
        acc = acc + _dot((a * a).astype(BF16), w2_ref[c * FF_CHUNK:(c + 1) * FF_CHUNK, :])
    out_ref[...] = acc


def _merge_ffn(x2d, ya, yb, mg, wa, wb, wo, g2, w1, w2, tm=512):
    n = x2d.shape[0]
    row = lambda w: pl.BlockSpec((tm, w), lambda i: (i, 0))
    return pl.pallas_call(
        _merge_ffn_kernel,
        grid=(n // tm,),
        in_specs=[row(D_MODEL), row(512), row(512), row(2048),
                  _const_spec(wa.shape), _const_spec(wb.shape), _const_spec(wo.shape),
                  _const_spec(g2.shape), _const_spec(w1.shape), _const_spec(w2.shape)],
        out_specs=row(D_MODEL),
        out_shape=jax.ShapeDtypeStruct((n, D_MODEL), F32),
        compiler_params=pltpu.CompilerParams(dimension_semantics=("parallel",),
                                             vmem_limit_bytes=VMEM_LIMIT),
        name="merge_ffn",
    )(x2d, ya, yb, mg, wa, wb, wo, g2, w1, w2)


def _proj_weights(w):
    widths = (512, 128, 128, 128, 128, 128, 128, 24, 512, 512, 512, 4, 4, 512, 2048)
    off = np.concatenate([[0], np.cumsum(widths)])
    (nq, nkc, nvc, nks, nvs, nkw, nvw, ngate, mq, mk, mv, mi, mf, mo, mgate) = (int(o) for o in off[:-1])
    col = lambda start, width: w[:, start:start + width]
    parts = [col(nq + NSA_D * (g * NSA_HPG + j), NSA_D) for j in range(NSA_HPG) for g in range(NSA_GROUPS)]
    parts += [col(nks, 128), col(nkw, 128), col(nkc, 128), col(nvc, 128), col(nvs, 128), col(nvw, 128)]
    gate = col(ngate, 24).reshape(-1, NSA_HEADS, 3).transpose(0, 2, 1).reshape(-1, 24)
    parts += [gate, col(mi, 4), col(mf, 4), jnp.zeros((w.shape[0], LANES - _IF_LANE - 8), w.dtype)]
    parts += [col(mq, 512), col(mk, 512), col(mv, 512), col(mo, 512), col(mgate, 2048)]
    w_all = jnp.concatenate(parts, axis=1).astype(BF16)
    w_if = jnp.concatenate([col(mi, 4), col(mf, 4)], axis=1).T.astype(BF16)
    return w_all, w_if


def _branch_a_weights(w):
    head = lambda h: w[NSA_D * h:NSA_D * (h + 1)]
    return jnp.concatenate([head(g * NSA_HPG + j) for j in range(NSA_HPG) for g in range(NSA_GROUPS)],
                           axis=0).astype(BF16)


def _compress_weights(pos, w1):
    r = w1.reshape(2, CMP_STRIDE, 1, NSA_D, CMP_HIDDEN)
    z = jnp.zeros_like(r)
    w1x = jnp.stack([jnp.concatenate([r, z], axis=2), jnp.concatenate([z, r], axis=2)])
    w1x = w1x.reshape(NSA_GROUPS, 2, CMP_STRIDE * NSA_GROUPS * NSA_D, CMP_HIDDEN).astype(BF16)
    posx = jnp.broadcast_to(pos.reshape(2, CMP_STRIDE, 1, NSA_D), (2, CMP_STRIDE, NSA_GROUPS, NSA_D))
    return w1x, posx.reshape(2, CMP_STRIDE * NSA_GROUPS * NSA_D).astype(BF16)


def _t5_bucket(dist):
    n = np.maximum(dist, 0)
    max_exact = REL_BUCKETS // 2
    nf = np.maximum(n, 1).astype(np.float32)
    large = max_exact + (np.log(nf / np.float32(max_exact)) / np.float32(math.log(REL_MAX_DIST / max_exact))
                         * np.float32(REL_BUCKETS - max_exact)).astype(np.int32)
    return np.where(n < max_exact, n, np.minimum(large, REL_BUCKETS - 1))


def _toeplitz(rel, n_rows, n_cols, stride, off, inner=1):
    a_rows = n_rows // inner
    lw = a_rows + n_cols
    k = np.arange(lw)[:, None]
    dist = stride * np.where(k < n_cols, -k, lw - k) + np.arange(inner)[None, :] + off
    live = (dist >= 0) & (k != n_cols)
    onehot = live[..., None] & (_t5_bucket(dist)[..., None] == np.arange(REL_BUCKETS))
    w = jnp.dot(jnp.asarray(onehot.reshape(lw * inner, REL_BUCKETS), F32), rel.T,
                precision=lax.Precision.HIGHEST)
    w = w.T.reshape(-1, lw, inner)
    flat = jnp.tile(w, (1, a_rows, 1))[:, :a_rows * (lw - 1)]
    out = flat.reshape(-1, a_rows, lw - 1, inner)[:, :, :n_cols]
    return out.transpose(0, 1, 3, 2).reshape(-1, n_rows, n_cols)


def _rel_bias(rel_table):
    return (rel_table - rel_table[REL_BUCKETS - 1][None, :]).T * LOG2E


def _bias_tables(rel, shifts):
    T = NSA_TILE
    sh_c, sh_s, sh_w = shifts
    r = np.arange(T)[:, None]

    c = np.arange(NEAR_KEYS)[None, :]
    ok = jnp.asarray((r - c + SLC_PAD) >= 0)[None]
    bs = jnp.where(ok, _toeplitz(rel, T, NEAR_KEYS, 1, SLC_PAD) - sh_s, MASK_F32)
    c = np.arange(WIN_KEYS)[None, :]
    d = r - c + WINDOW
    ok = jnp.asarray((d >= 0) & (d < WINDOW))[None]
    bw = jnp.where(ok, _toeplitz(rel, T, WIN_KEYS, 1, WINDOW) - sh_w, MASK_F32)
    r = np.arange(SEL_TILE)[:, None]
    m = np.arange(64)[None, :]
    d = r - CMP_STRIDE * m + BAND_DIST0
    band = (m >= 1) & (m < BAND_SLOTS - 1)
    vals = _toeplitz(rel, SEL_TILE, BAND_SLOTS, CMP_STRIDE, BAND_DIST0, inner=CMP_STRIDE)
    vals = jnp.pad(vals, ((0, 0), (0, 0), (0, 64 - BAND_SLOTS)))
    vals = jnp.where(jnp.asarray(band & (d >= 0))[None], vals, 0.0) - sh_c
    dead = (band & (d < 0)) | (m == BAND_SLOTS - 1)
    vals = jnp.where(jnp.asarray(dead)[None], MASK_BIG, vals)
    hi = vals.astype(BF16)
    lo = (vals - hi.astype(F32)).astype(BF16)
    bcf = jnp.concatenate([hi, lo], axis=-1)
    flat = lambda a: a.reshape(-1, a.shape[-1])
    return flat(bcf), flat(bs).astype(F32), flat(bw).astype(F32)


def _layer(l, x2d, bsz, s, consts, rel_table, norm1_g, w_in, nsa_q_gain, nsa_k_gain, cmp_k, cmp_v,
           ml_conv_w, ml_conv_b, ml_i_bias, ml_f_bias, w_branch_a, w_branch_b, w_out, norm2_g, w_ff1, w_ff2):
    n = bsz * s
    nseg = s // CMP_STRIDE
    oh, c2s = consts
    w_all, w_if = _proj_weights(w_in[l])
    qg_pad = jnp.concatenate([nsa_q_gain[l]] * 2)[None, :]
    kg_pad = jnp.stack([jnp.concatenate([nsa_k_gain[l, 1]] * 2), jnp.concatenate([nsa_k_gain[l, 2]] * 2)])
    (q, ks, kw, kc, vc, vs, vw, gates, gcol, mq, mk, mv, og, mg, gif) = _proj(
        x2d, s, norm1_g[l][None, :], w_all, w_if, qg_pad, kg_pad, ml_conv_w[l], ml_conv_b[l][None, :])

    def compress(a, params, gain, normalize):
        pos, w1, b1, w2, b2 = (p[l] for p in params)
        w1x, posx = _compress_weights(pos, w1)
        return _compress(a.reshape(bsz, nseg, CMP_STRIDE * LANES), w1x, posx, b1[None, :],
                         w2.astype(BF16), b2[None, :], gain[None, :], normalize)

    kcmp = compress(kc, cmp_k, nsa_k_gain[l, 0], True)
    vcmp = compress(vc, cmp_v, jnp.ones((NSA_D,), F32), False)

    qk_bound = lambda kg: 8.0 * LOG2E * jnp.max(jnp.abs(nsa_q_gain[l])) * jnp.max(jnp.abs(kg))
    tab = _rel_bias(rel_table)
    snap = lambda v: v.astype(BF16).astype(F32)
    shifts = [snap(qk_bound(nsa_k_gain[l, j]) + jnp.maximum(jnp.max(tab), 0.0)) for j in range(3)]
    safe = 2.0 * jnp.max(jnp.stack(shifts)) < SAFE_SHIFT_LOG2
    shifts = [jnp.where(safe, sh, 0.0) for sh in shifts]
    bcf, bs, bw = _bias_tables(tab, shifts)

    seq = lambda a: a.reshape(bsz, s, a.shape[-1])
    front = lambda a, p: jnp.pad(seq(a), ((0, 0), (p, 0), (0, 0)))
    operands = (seq(q), front(ks, SLC_PAD), front(vs, SLC_PAD), front(kw, WINDOW), front(vw, WINDOW),
                kcmp, vcmp, oh, seq(gates), bcf, bs, bw, c2s, jnp.full((1, LANES), -shifts[1], F32))
    y_a = _nsa(safe.astype(jnp.int32)[None], *operands)

    gate_b = jnp.concatenate([ml_i_bias[l], ml_f_bias[l]])
    gate_b_row = jnp.pad(gate_b, (_IF_LANE, LANES - _IF_LANE - 8))[None, :]
    y_b = _mlstm(seq(mq), seq(mk), seq(mv), seq(og), gif, seq(gcol), gate_b[:, None], gate_b_row)

    out = _merge_ffn(x2d, y_a.reshape(n, 512), y_b.reshape(n, 512), mg, _branch_a_weights(w_branch_a[l]),
                     w_branch_b[l].astype(BF16), w_out[l].astype(BF16), norm2_g[l][None, :],
                     w_ff1[l].astype(BF16), w_ff2[l].astype(BF16))
    return out, y_a, y_b


def _consts(s):
    nseg = s // CMP_STRIDE
    nsel = s // SLC_BLOCK
    blk_of_key = np.arange(s) // SLC_BLOCK
    oh = np.concatenate([np.ones((SLC_PAD, LANES), np.float32),
                         (blk_of_key[:, None] == np.arange(LANES)[None, :]).astype(np.float32)], axis=0)
    ci = np.arange(nseg)[:, None] * CMP_STRIDE
    sj = np.arange(LANES)[None, :] * SLC_BLOCK
    c2s = ((ci < sj + SLC_BLOCK) & (ci + CMP_LEN > sj) & (np.arange(LANES)[None, :] < nsel)
           & (np.arange(nseg)[:, None] < nseg - 1))
    return jnp.asarray(oh, BF16), jnp.asarray(c2s.astype(np.float32), BF16)


def kernel(x, norm1_g, w_in, nsa_q_gain, nsa_k_gain, cmp_k_pos, cmp_k_w1, cmp_k_b1, cmp_k_w2, cmp_k_b2, cmp_v_pos, cmp_v_w1, cmp_v_b1, cmp_v_w2, cmp_v_b2, rel_table, ml_conv_w, ml_conv_b, ml_i_bias, ml_f_bias, w_branch_a, w_branch_b, w_out, norm2_g, w_ff1, w_ff2):
    bsz, s, _ = x.shape
    consts = _consts(s)
    x2d = x.reshape(bsz * s, D_MODEL)
    for l in range(norm1_g.shape[0]):
        x2d, _, _ = _layer(l, x2d, bsz, s, consts, rel_table, norm1_g, w_in, nsa_q_gain, nsa_k_gain,
                           (cmp_k_pos, cmp_k_w1, cmp_k_b1, cmp_k_w2, cmp_k_b2),
                           (cmp_v_pos, cmp_v_w1, cmp_v_b1, cmp_v_w2, cmp_v_b2),
                           ml_conv_w, ml_conv_b, ml_i_bias, ml_f_bias,
                           w_branch_a, w_branch_b, w_out, norm2_g, w_ff1, w_ff2)
    return x2d.reshape(bsz, s, D_MODEL)
```

```python
import functools
import math

import numpy as np
import jax
import jax.numpy as jnp
from jax import lax
from jax.experimental import pallas as pl
from jax.experimental.pallas import tpu as pltpu

F32 = jnp.float32
BF16 = jnp.bfloat16

D_MODEL = 1024
NSA_HEADS = 8
NSA_GROUPS = 2
NSA_HPG = NSA_HEADS // NSA_GROUPS
NSA_D = 64
CMP_LEN = 32
CMP_STRIDE = 16
CMP_HIDDEN = 256
SLC_BLOCK = 64
SLC_TOPK = 16
WINDOW = 512
FORCED_SCORE = 1e4
ML_HEADS = 4
ML_D = 128
ML_CHUNK = 64
CONV_WIDTH = 4
D_FF = 4 * D_MODEL
REL_BUCKETS = 32
REL_MAX_DIST = 128
RMS_EPS = 1e-6

LANES = 128
NSA_TILE = 256
SEL_TILE = 512
SLC_PAD = 256
FAR_CHUNK = 512
NEAR_KEYS = SLC_PAD + NSA_TILE
WIN_KEYS = WINDOW + NSA_TILE
BAND_OFF = 10
BAND_SLOTS = 2 + (SEL_TILE + 128) // CMP_STRIDE
BAND_DIST0 = CMP_STRIDE * BAND_OFF - (CMP_LEN - 1)
MASK_BIG = -1e9
MASK_F32 = -1e30
LOG2E = math.log2(math.e)
SAFE_SHIFT_LOG2 = 50.0
VMEM_LIMIT = 56 * 1024 * 1024

_Q_OFF, _KS_OFF, _KW_OFF, _KC_OFF, _VC_OFF, _VS_OFF, _VW_OFF = 0, 512, 640, 768, 896, 1024, 1152
_GATE_OFF, _MQ_OFF, _MK_OFF, _MV_OFF, _MO_OFF, _MG_OFF, _W_COLS = 1280, 1408, 1920, 2432, 2944, 3456, 5504
_IF_LANE = 24


def _nt(a, b, precision=None):
    return lax.dot_general(a, b, (((1,), (1,)), ((), ())), precision=precision,
                           preferred_element_type=F32)


def _dot(a, b, precision=None):
    return jnp.dot(a, b, precision=precision, preferred_element_type=F32)


def _split3(x, dot_part):
    hi = x.astype(BF16)
    r1 = x - hi.astype(F32)
    mid = r1.astype(BF16)
    lo = (r1 - mid.astype(F32)).astype(BF16)
    return dot_part(hi) + dot_part(mid) + dot_part(lo)


def _const_spec(shape):
    nd = len(shape)
    return pl.BlockSpec(shape, lambda *_: (0,) * nd, pipeline_mode=pl.Buffered(1))


def _proj_kernel(x_ref, g1_ref, w_ref, wif_ref, qg_ref, kg_ref, cw_ref, cb_ref,
                 q_ref, ks_ref, kw_ref, kc_ref, vc_ref, vs_ref, vw_ref, gate_ref, gcol_ref,
                 mq_ref, mk_ref, mv_ref, mo_ref, mg_ref, gif_ref, ext_sc, seg_sc, *, tiles_per_seq):
    tm = x_ref.shape[0]
    x = x_ref[...]
    h = x * lax.rsqrt(jnp.mean(x * x, axis=-1, keepdims=True) + RMS_EPS) * g1_ref[...]
    hb = h.astype(BF16)

    def proj(off, width):
        return _dot(hb, w_ref[:, off:off + width])

    lane = lax.broadcasted_iota(jnp.int32, (1, LANES), 1)
    low = lane < NSA_D

    def half_norm(blk):
        sq = blk * blk
        ms0 = jnp.sum(jnp.where(low, sq, 0.0), axis=-1, keepdims=True) * (1.0 / NSA_D)
        ms1 = jnp.sum(jnp.where(low, 0.0, sq), axis=-1, keepdims=True) * (1.0 / NSA_D)
        return blk * jnp.where(low, lax.rsqrt(ms0 + RMS_EPS), lax.rsqrt(ms1 + RMS_EPS))

    for pair in range(NSA_HPG // 2):
        both = proj(_Q_OFF + 2 * LANES * pair, 2 * LANES)
        for hh in range(2):
            qn = half_norm(both[:, LANES * hh:LANES * (hh + 1)]) * qg_ref[...] * (NSA_D ** -0.5 * LOG2E)
            col = LANES * (2 * pair + hh)
            q_ref[:, col:col + LANES] = qn.astype(q_ref.dtype)

    both = proj(_KS_OFF, 2 * LANES)
    for hh, ref in enumerate((ks_ref, kw_ref)):
        ref[...] = (half_norm(both[:, LANES * hh:LANES * (hh + 1)]) * kg_ref[hh:hh + 1, :]).astype(ref.dtype)

    both = proj(_VS_OFF, 2 * LANES)
    vs_ref[...] = both[:, :LANES].astype(vs_ref.dtype)
    vw_ref[...] = both[:, LANES:].astype(vw_ref.dtype)

    both = proj(_KC_OFF, 2 * LANES)
    for hh, ref in enumerate((kc_ref, vc_ref)):
        seg_sc[hh] = both[:, LANES * hh:LANES * (hh + 1)]
        for tok in range(CMP_STRIDE):
            ref[:, LANES * tok:LANES * (tok + 1)] = seg_sc[hh, pl.ds(tok, tm // CMP_STRIDE, stride=CMP_STRIDE),
                                                           :].astype(ref.dtype)

    slab = proj(_GATE_OFF, LANES)
    gate_ref[...] = jax.nn.sigmoid(slab)
    gcol_ref[...] = slab

    @pl.when(pl.program_id(0) % tiles_per_seq == 0)
    def _():
        ext_sc[0:8, :] = jnp.zeros((8, ext_sc.shape[1]), F32)

    ext_sc[8:, 0:512] = proj(_MQ_OFF, 512)
    ext_sc[8:, 512:1024] = proj(_MK_OFF, 512)
    conv = cb_ref[...]
    for t in range(CONV_WIDTH):
        lo = 8 - (CONV_WIDTH - 1) + t
        conv = conv + ext_sc[lo:lo + tm, :] * cw_ref[t:t + 1, :]
    ext_sc[0:8, :] = ext_sc[tm:tm + 8, :]
    qk = conv * jax.nn.sigmoid(conv)
    mq_ref[...] = (qk[:, :512] * (ML_D ** -0.5)).astype(mq_ref.dtype)
    mk_ref[...] = qk[:, 512:].astype(mk_ref.dtype)

    mv_ref[...] = proj(_MV_OFF, 512).astype(mv_ref.dtype)
    mo_ref[...] = jax.nn.sigmoid(proj(_MO_OFF, 512)).astype(mo_ref.dtype)
    for c in range(4):
        mg_ref[:, 512 * c:512 * (c + 1)] = jax.nn.sigmoid(proj(_MG_OFF + 512 * c, 512)).astype(mg_ref.dtype)
    gif_ref[...] = _nt(wif_ref[...], hb)


def _proj(x2d, seq_len, g1, w_all, w_if, qg_pad, kg_pad, conv_w, conv_b, tm=512):
    n = x2d.shape[0]
    row = lambda w: pl.BlockSpec((tm, w), lambda i: (i, 0))
    out_shapes = (
        jax.ShapeDtypeStruct((n, 512), BF16),
        jax.ShapeDtypeStruct((n, LANES), BF16),
        jax.ShapeDtypeStruct((n, LANES), BF16),
        jax.ShapeDtypeStruct((n // CMP_STRIDE, CMP_STRIDE * LANES), BF16),
        jax.ShapeDtypeStruct((n // CMP_STRIDE, CMP_STRIDE * LANES), BF16),
        jax.ShapeDtypeStruct((n, LANES), BF16),
        jax.ShapeDtypeStruct((n, LANES), BF16),
        jax.ShapeDtypeStruct((n, LANES), F32),
        jax.ShapeDtypeStruct((n, LANES), F32),
        jax.ShapeDtypeStruct((n, 512), BF16),
        jax.ShapeDtypeStruct((n, 512), BF16),
        jax.ShapeDtypeStruct((n, 512), BF16),
        jax.ShapeDtypeStruct((n, 512), BF16),
        jax.ShapeDtypeStruct((n, 2048), BF16),
        jax.ShapeDtypeStruct((8, n), F32),
    )
    seg = pl.BlockSpec((tm // CMP_STRIDE, CMP_STRIDE * LANES), lambda i: (i, 0))
    out_specs = (row(512), row(LANES), row(LANES), seg, seg) + (row(LANES),) * 4 + (row(512),) * 4 + (
        row(2048), pl.BlockSpec((8, tm), lambda i: (0, i)))
    return pl.pallas_call(
        functools.partial(_proj_kernel, tiles_per_seq=seq_len // tm),
        grid=(n // tm,),
        in_specs=[row(D_MODEL), _const_spec((1, D_MODEL)), _const_spec((D_MODEL, _W_COLS)),
                  _const_spec((8, D_MODEL)), _const_spec((1, LANES)), _const_spec((2, LANES)),
                  _const_spec(conv_w.shape), _const_spec(conv_b.shape)],
        out_specs=out_specs,
        out_shape=out_shapes,
        scratch_shapes=[pltpu.VMEM((tm + 8, 1024), F32), pltpu.VMEM((2, tm, LANES), F32)],
        compiler_params=pltpu.CompilerParams(dimension_semantics=("arbitrary",),
                                             vmem_limit_bytes=VMEM_LIMIT),
        name="proj",
    )(x2d, g1, w_all, w_if, qg_pad, kg_pad, conv_w, conv_b)


def _compress_kernel(seg_ref, w1_ref, pos_ref, b1_ref, w2_ref, b2_ref, gain_ref, out_ref, *, normalize):
    seg = seg_ref[0]
    nseg, width = seg.shape
    c = math.sqrt(2.0 / math.pi)
    pos_lo = jnp.broadcast_to(pos_ref[0:1, :], (8, width))
    pos_hi = jnp.broadcast_to(pos_ref[1:2, :], (8, width))
    outs = []
    for g in range(NSA_GROUPS):
        a = _dot(seg, w1_ref[g, 0])
        b = _dot(seg, w1_ref[g, 1])
        posb = (_dot(pos_lo, w1_ref[g, 0]) + _dot(pos_hi, w1_ref[g, 1]))[0:1] + b1_ref[...]
        pre = a + pltpu.roll(b, nseg - 1, 0) + posb
        hid = 0.5 * pre * (1.0 + jnp.tanh(c * (pre + 0.044715 * (pre * pre * pre))))
        out = _dot(hid.astype(BF16), w2_ref[...]) + b2_ref[...]
        if normalize:
            ms = jnp.mean(out * out, axis=-1, keepdims=True)
            out = out * lax.rsqrt(ms + RMS_EPS) * gain_ref[...]
        outs.append(out)
    out_ref[0] = jnp.concatenate(outs, axis=1).astype(out_ref.dtype)


def _compress(segs, w1x, posx, b1, w2, b2, gain, normalize):
    bsz, nseg, width = segs.shape
    return pl.pallas_call(
        functools.partial(_compress_kernel, normalize=normalize),
        grid=(bsz,),
        in_specs=[pl.BlockSpec((1, nseg, width), lambda b: (b, 0, 0)),
                  _const_spec(w1x.shape), _const_spec(posx.shape), _const_spec(b1.shape),
                  _const_spec(w2.shape), _const_spec(b2.shape), _const_spec(gain.shape)],
        out_specs=pl.BlockSpec((1, nseg, LANES), lambda b: (b, 0, 0)),
        out_shape=jax.ShapeDtypeStruct((bsz, nseg, LANES), BF16),
        compiler_params=pltpu.CompilerParams(dimension_semantics=("parallel",),
                                             vmem_limit_bytes=VMEM_LIMIT),
        name="compress",
    )(segs, w1x, posx, b1, w2, b2, gain)


def _stack_heads(q_ref, lane_q):
    q = q_ref[0]
    return jnp.concatenate(
        [jnp.where((lane_q >= NSA_D) == (g == 1), q[:, LANES * j:LANES * (j + 1)], jnp.zeros((), BF16))
         for g in range(NSA_GROUPS) for j in range(NSA_HPG)], axis=0)


def _softmax_pv(s, v, online):
    if online:
        m = jnp.max(s, axis=1, keepdims=True)
        s = s - jnp.where(m < 0.1 * MASK_BIG, 0.0, m)
    p = jnp.exp2(s)
    acc = _dot(p.astype(BF16), jnp.concatenate([v, jnp.ones(v.shape, BF16)], axis=1))
    return p, acc[:, :LANES], acc[:, LANES:]


def _nsa_select_kernel(safe_ref, q_ref, kc_ref, vc_ref, gate_ref, bcf_ref, c2s_ref, shift_ref,
                       oc_ref, sbn_ref, sbf_ref, score_sc):
    i = pl.program_id(1)
    safe = safe_ref[0] != 0
    ncmp = kc_ref.shape[1]
    n_tiles = ncmp * CMP_STRIDE // SEL_TILE
    refs = (q_ref, kc_ref, vc_ref, gate_ref, bcf_ref, c2s_ref, shift_ref, oc_ref, sbn_ref, sbf_ref, score_sc)
    n_cls = max(c for c in (1, 2, 4) if n_tiles % c == 0 and (ncmp // c) % LANES == 0)
    for c in range(n_cls):
        @pl.when(safe & (i // (n_tiles // n_cls) == c))
        def _():
            _nsa_select_tile(i, ncmp * (c + 1) // n_cls, LANES * (c + 1) // n_cls, *refs, False)

    @pl.when(jnp.logical_not(safe))
    def _():
        _nsa_select_tile(i, ncmp, LANES, *refs, True)


def _nsa_select_tile(i, ncmp, nblk, q_ref, kc_ref, vc_ref, gate_ref, bcf_ref, c2s_ref, shift_ref,
                     oc_ref, sbn_ref, sbf_ref, score_sc, online):
    T = SEL_TILE
    grows = NSA_HPG * T
    lane_q = lax.broadcasted_iota(jnp.int32, (T, LANES), 1)
    qpad = _stack_heads(q_ref, lane_q)

    n_io = lax.broadcasted_iota(jnp.int32, (ncmp, LANES), 0)
    l_io = lax.broadcasted_iota(jnp.int32, (ncmp, LANES), 1)
    band = jnp.clip(n_io - i * (T // CMP_STRIDE) + BAND_OFF, 0, BAND_SLOTS - 1)
    oh_c = jnp.where(band == (l_io & 63), 1.0, 0.0).astype(BF16)
    kc_aug = jnp.concatenate([kc_ref[0, :ncmp, :], oh_c], axis=1)
    qc_aug = jnp.concatenate([qpad, bcf_ref[...]], axis=1)
    p_c, num_c, l_c = _softmax_pv(_nt(qc_aug, kc_aug), vc_ref[0, :ncmp, :], online)
    inv_c = 1.0 / jnp.maximum(l_c, 1e-30)
    o_c = num_c * inv_c
    p_c = p_c * jnp.concatenate([inv_c] * (ncmp // LANES), axis=1)

    gates = gate_ref[0]
    gated = [jnp.broadcast_to(gates[:, h:h + 1], (T, LANES)) * o_c[h * T:(h + 1) * T] for h in range(NSA_HEADS)]
    for j in range(NSA_HPG):
        oc_ref[0, :, LANES * j:LANES * (j + 1)] = jnp.where(
            lane_q < NSA_D, gated[j], gated[NSA_HPG + j]).astype(oc_ref.dtype)

    imp_t = []
    for g in range(NSA_GROUPS):
        ps = p_c[g * grows:g * grows + T]
        for hh in range(1, NSA_HPG):
            ps = ps + p_c[g * grows + hh * T:g * grows + (hh + 1) * T]
        ps_hi = ps.astype(BF16)
        ps_lo = (ps - ps_hi.astype(F32)).astype(BF16)
        c2s = c2s_ref[:ncmp, :]
        imp_t.append((_dot(ps_hi, c2s) + _dot(ps_lo, c2s)).T[:nblk])
    width = NSA_GROUPS * T
    j_io = lax.broadcasted_iota(jnp.int32, (nblk, width), 0)
    r_io = lax.broadcasted_iota(jnp.int32, (nblk, width), 1) & (T - 1)
    j_f = j_io.astype(F32)
    cur = i * (T // SLC_BLOCK) + jnp.right_shift(r_io, SLC_BLOCK.bit_length() - 1)
    forced = (j_io == 0) | (j_io == cur) | (j_io == cur - 1)
    n_pick = SLC_TOPK - 3
    start = jnp.where(forced, -jnp.inf, jnp.where(j_io <= cur, jnp.concatenate(imp_t, axis=1), -1.0 - j_f))

    def padded(score):
        if nblk == LANES:
            return score
        return jnp.concatenate([score, jnp.zeros((LANES - nblk, width), F32)], axis=0)

    score = start
    for _ in range(n_pick):
        score = jnp.where(score == jnp.max(score, axis=0, keepdims=True), -jnp.inf, score)
    score_sc[...] = padded(score)
    taken = jnp.sum(jnp.where((score == -jnp.inf) & jnp.logical_not(forced), 1.0, 0.0), axis=0, keepdims=True)

    @pl.when(jnp.max(taken) > n_pick + 0.5)
    def _():
        exact = start
        for _ in range(n_pick):
            best = jnp.max(exact, axis=0, keepdims=True)
            first = jnp.min(jnp.where(exact == best, j_f, float(nblk)), axis=0, keepdims=True)
            exact = jnp.where(j_f == first, -jnp.inf, exact)
        score_sc[...] = padded(exact)

    score = score_sc[...]
    near_start = ((i * T + lax.broadcasted_iota(jnp.int32, (T, LANES), 0)) // NSA_TILE) * (NSA_TILE // SLC_BLOCK)
    near_blk = lane_q >= near_start - SLC_PAD // SLC_BLOCK
    far_pick = shift_ref[0:1, 0:1]
    for g in range(NSA_GROUPS):
        picked = score[:, g * T:(g + 1) * T].T == -jnp.inf
        sbn_ref[0, :, LANES * g:LANES * (g + 1)] = jnp.where(picked & near_blk, 0.0, MASK_BIG).astype(BF16)
        sbf_ref[0, :, LANES * g:LANES * (g + 1)] = jnp.where(
            picked & jnp.logical_not(near_blk), far_pick, MASK_BIG).astype(BF16)


def _nsa_attend_kernel(safe_ref, *refs):
    *io_refs, acc_sc, m_sc, pa_sc, pb_sc = refs

    @pl.when(safe_ref[0] != 0)
    def _():
        _nsa_attend_tile(*io_refs, acc_sc, pa_sc, pb_sc, online=False)

    @pl.when(safe_ref[0] == 0)
    def _():
        _nsa_attend_tile(*io_refs, acc_sc, m_sc, online=True)


def _nsa_attend_tile(q_ref, ks_ref, vs_ref, kw_ref, vw_ref, oh_ref, gate_ref, oc_ref, sbn_ref, sbf_ref,
                     bs_ref, bw_ref, out_ref, acc_sc, *extra_sc, online):
    i = pl.program_id(1)
    t0 = pl.multiple_of(i * NSA_TILE, NSA_TILE)
    T = NSA_TILE
    rows = NSA_HEADS * T
    grows = NSA_HPG * T
    lane_q = lax.broadcasted_iota(jnp.int32, (T, LANES), 1)
    qpad = _stack_heads(q_ref, lane_q)

    def with_features(feat_ref):
        feats = [jnp.concatenate([feat_ref[0, :, LANES * g:LANES * (g + 1)]] * NSA_HPG, axis=0)
                 for g in range(NSA_GROUPS)]
        return jnp.concatenate([qpad, jnp.concatenate(feats, axis=0)], axis=1)

    q_near, q_far = with_features(sbn_ref), with_features(sbf_ref)

    r_w = lax.broadcasted_iota(jnp.int32, (WIN_KEYS, LANES), 0)
    pad_flag = jnp.where(r_w + (t0 - WINDOW) < 0, 1.0, 0.0).astype(BF16)
    kw_aug = jnp.concatenate([kw_ref[0, pl.ds(t0, WIN_KEYS), :], pad_flag], axis=1)
    qw_aug = jnp.concatenate([qpad, jnp.full((rows, LANES), MASK_BIG / LANES, BF16)], axis=1)
    s_w = _nt(qw_aug, kw_aug) + bw_ref[...]
    _, num_w, l_w = _softmax_pv(s_w, vw_ref[0, pl.ds(t0, WIN_KEYS), :], online)
    o_w = num_w * (1.0 / l_w)

    gates = gate_ref[0]
    gate_tile = lambda col: jnp.broadcast_to(gates[:, col:col + 1], (T, LANES))
    head_rows = [slice(h * T, (h + 1) * T) for h in range(NSA_HEADS)]
    o_gw = [gate_tile(16 + h) * o_w[head_rows[h]] for h in range(NSA_HEADS)]
    g_slc = [gate_tile(8 + h) for h in range(NSA_HEADS)]

    ones_f = jnp.ones((FAR_CHUNK, LANES), BF16)
    k_near = jnp.concatenate([ks_ref[0, pl.ds(t0, NEAR_KEYS), :], oh_ref[pl.ds(t0, NEAR_KEYS), :]], axis=1)
    v_near = jnp.concatenate([vs_ref[0, pl.ds(t0, NEAR_KEYS), :], jnp.ones((NEAR_KEYS, LANES), BF16)], axis=1)
    s_n = _nt(q_near, k_near) + bs_ref[...]
    if online:
        m_sc, = extra_sc
        m_n = jnp.max(s_n, axis=1, keepdims=True)
        m_sc[...] = m_n
        s_n = s_n - m_n
    acc_sc[...] = _dot(jnp.exp2(s_n).astype(BF16), v_near)

    last_chunk = (ks_ref.shape[1] - SLC_PAD) // FAR_CHUNK - 1

    def far_start(c):
        return pl.multiple_of(SLC_PAD + FAR_CHUNK * jnp.minimum(c, last_chunk), LANES)

    def far_scores(c, by_group=False):
        start = far_start(c)
        k_f = jnp.concatenate([ks_ref[0, pl.ds(start, FAR_CHUNK), :], oh_ref[pl.ds(start, FAR_CHUNK), :]], axis=1)
        if by_group:
            return jnp.concatenate([_nt(q_far[:grows], k_f), _nt(q_far[grows:], k_f)], axis=0)
        return _nt(q_far, k_f)

    def far_values(c):
        return jnp.concatenate([vs_ref[0, pl.ds(far_start(c), FAR_CHUNK), :], ones_f], axis=1)

    n_far = (t0 + (FAR_CHUNK - 1 - SLC_PAD)) // FAR_CHUNK
    if online:
        def far_online(c, carry):
            s_f = far_scores(c)
            m_old = m_sc[...]
            m_new = jnp.maximum(m_old, jnp.max(s_f, axis=1, keepdims=True))
            m_sc[...] = m_new
            acc_sc[...] = (jnp.exp2(m_old - m_new) * acc_sc[...]
                           + _dot(jnp.exp2(s_f - m_new).astype(BF16), far_values(c)))
            return carry

        lax.fori_loop(0, n_far, far_online, 0)
    else:
        pa_sc, pb_sc = extra_sc
        pa_sc[...] = jnp.exp2(far_scores(0, by_group=True)).astype(BF16)

        def far_pair(cc, carry):
            c0 = 2 * cc
            pv0 = _dot(pa_sc[...], far_values(c0))
            pb_sc[...] = jnp.exp2(far_scores(c0 + 1)).astype(BF16)
            pv1 = _dot(pb_sc[...], far_values(c0 + 1))
            pa_sc[...] = jnp.exp2(far_scores(c0 + 2)).astype(BF16)
            acc_sc[...] += pv0 + pv1
            return carry

        lax.fori_loop(0, n_far // 2, far_pair, 0)

        @pl.when(n_far % 2 == 1)
        def _():
            v_f = far_values(n_far - 1)
            acc_sc[0:grows, :] += _dot(pa_sc[0:grows, :], v_f)
            acc_sc[grows:, :] += _dot(pa_sc[grows:, :], v_f)

    acc = acc_sc[...]
    o_s = acc[:, :LANES] * (1.0 / acc[:, LANES:])

    head_out = [o_gw[h] + g_slc[h] * o_s[head_rows[h]] for h in range(NSA_HEADS)]
    for j in range(NSA_HPG):
        cols = slice(LANES * j, LANES * (j + 1))
        out_ref[0, :, cols] = (oc_ref[0, :, cols].astype(F32) + jnp.where(
            lane_q < NSA_D, head_out[j], head_out[NSA_HPG + j])).astype(out_ref.dtype)


def _nsa(safe, q, ks, vs, kw, vw, kc, vc, oh, gates, bcf, bs, bw, c2s, shift):
    bsz, s = q.shape[:2]
    flag = pl.BlockSpec(memory_space=pltpu.SMEM)
    T = NSA_TILE
    rows = NSA_HEADS * T
    grid = (bsz, s // T)
    params = pltpu.CompilerParams(dimension_semantics=("parallel", "arbitrary"), vmem_limit_bytes=VMEM_LIMIT)
    tile = lambda width: pl.BlockSpec((1, T, width), lambda b, i: (b, i, 0))
    per_b = lambda a: pl.BlockSpec((1,) + a.shape[1:], lambda b, i: (b, 0, 0), pipeline_mode=pl.Buffered(1))
    sel_tile = lambda width: pl.BlockSpec((1, SEL_TILE, width), lambda b, i: (b, i, 0))
    o_cmp, sb_near, sb_far = pl.pallas_call(
        _nsa_select_kernel,
        grid=(bsz, s // SEL_TILE),
        in_specs=[flag, sel_tile(NSA_HPG * LANES), per_b(kc), per_b(vc), sel_tile(LANES),
                  _const_spec(bcf.shape), _const_spec(c2s.shape), _const_spec(shift.shape)],
        out_specs=(sel_tile(NSA_HPG * LANES), sel_tile(NSA_GROUPS * LANES), sel_tile(NSA_GROUPS * LANES)),
        out_shape=(jax.ShapeDtypeStruct((bsz, s, NSA_HPG * LANES), BF16),
                   jax.ShapeDtypeStruct((bsz, s, NSA_GROUPS * LANES), BF16),
                   jax.ShapeDtypeStruct((bsz, s, NSA_GROUPS * LANES), BF16)),
        scratch_shapes=[pltpu.VMEM((LANES, NSA_GROUPS * SEL_TILE), F32)],
        compiler_params=params,
        name="nsa_select",
    )(safe, q, kc, vc, gates, bcf, c2s, shift)
    scratch = [pltpu.VMEM((rows, 2 * LANES), F32),
               pltpu.VMEM((rows, 1), F32),
               pltpu.VMEM((rows, FAR_CHUNK), BF16),
               pltpu.VMEM((rows, FAR_CHUNK), BF16)]
    return pl.pallas_call(
        _nsa_attend_kernel,
        grid=grid,
        in_specs=[flag, tile(NSA_HPG * LANES), per_b(ks), per_b(vs), per_b(kw), per_b(vw), _const_spec(oh.shape),
                  tile(LANES), tile(NSA_HPG * LANES), tile(NSA_GROUPS * LANES), tile(NSA_GROUPS * LANES),
                  _const_spec(bs.shape), _const_spec(bw.shape)],
        out_specs=tile(NSA_HPG * LANES),
        out_shape=jax.ShapeDtypeStruct((bsz, s, NSA_HPG * LANES), BF16),
        scratch_shapes=scratch,
        compiler_params=params,
        name="nsa_attend",
    )(safe, q, ks, vs, kw, vw, oh, gates, o_cmp, sb_near, sb_far, bs, bw)


ML_ROWS = 512


def _mlstm_kernel(q_ref, k_ref, v_ref, o_ref, gif_ref, gcol_ref, gb_ref, gbrow_ref, tril_ref, triu_ref,
                  out_ref, c_sc, m_sc):
    j = pl.program_id(1)
    R = ML_ROWS
    L = ML_CHUNK

    @pl.when(j == 0)
    def _():
        c_sc[...] = jnp.zeros_like(c_sc)
        m_sc[...] = jnp.zeros_like(m_sc)

    og = o_ref[0]

    pre = gif_ref[...] + gb_ref[...]
    row8 = lax.broadcasted_iota(jnp.int32, (8, R), 0)
    logf = jnp.minimum(pre, 0.0) - jnp.log(1.0 + jnp.exp(-jnp.abs(pre)))
    g8 = jnp.where(row8 < ML_HEADS, pre, logf)
    pre_c = gcol_ref[0] + gbrow_ref[...]
    lane_c = lax.broadcasted_iota(jnp.int32, pre_c.shape, 1)
    logf_c = jnp.minimum(pre_c, 0.0) - jnp.log(1.0 + jnp.exp(-jnp.abs(pre_c)))
    g_c = jnp.where(lane_c < _IF_LANE + ML_HEADS, pre_c, logf_c)
    cum_c = _split3(g_c, lambda part: _dot(tril_ref[...], part))
    cum_row = _split3(g8, lambda part: _dot(part, triu_ref[...]))

    a_io = lax.broadcasted_iota(jnp.int32, (L, L), 0)
    b_io = lax.broadcasted_iota(jnp.int32, (L, L), 1)
    causal = (b_io <= a_io)[None]
    nc = R // L
    pairs = [(c, h) for c in range(nc) for h in range(ML_HEADS)]

    def blocks(a):
        return jnp.stack([a[c * L:(c + 1) * L, h * ML_D:(h + 1) * ML_D] for c, h in pairs])

    def cols(a, k0):
        return jnp.stack([jnp.broadcast_to(a[c * L:(c + 1) * L, k0 + h:k0 + h + 1], (L, LANES)) for c, h in pairs])

    def rows(a, k0):
        return jnp.stack([a[k0 + h:k0 + h + 1, c * L:(c + 1) * L] for c, h in pairs])

    bdot = lambda eq, x, y: jnp.einsum(eq, x, y, preferred_element_type=F32)
    qb, kb, vb = blocks(q_ref[0]), blocks(k_ref[0]), blocks(v_ref[0])
    b_col, li_col = cols(cum_c, _IF_LANE + ML_HEADS), cols(g_c, _IF_LANE)
    b_row, li_row = rows(cum_row, ML_HEADS), rows(g8, 0)
    gsum = b_row[:, :, L - 1:L]

    s_max = jnp.max(gsum - b_row + li_row, axis=2, keepdims=True)
    m_run = m_sc[:, 0:1, 0:1]
    m_ins, m_outs = [], []
    for c in range(nc):
        hs = slice(c * ML_HEADS, (c + 1) * ML_HEADS)
        m_ins.append(m_run)
        m_run = jnp.maximum(gsum[hs] + m_run, s_max[hs])
        m_outs.append(m_run)
    m_sc[...] = jnp.broadcast_to(m_run, m_sc.shape)
    m_in, m_out = jnp.concatenate(m_ins, axis=0), jnp.concatenate(m_outs, axis=0)

    log_d = jnp.where(causal, b_col[:, :, :L] - b_row + li_row, -jnp.inf)
    inter = b_col + m_in
    m_row = jnp.maximum(inter, jnp.broadcast_to(jnp.max(log_d, axis=2, keepdims=True), inter.shape))
    w = bdot('bik,bjk->bij', qb, kb) * jnp.exp(log_d - m_row[:, :, :L])
    v_aug = jnp.concatenate([vb, jnp.ones(vb.shape, BF16)], axis=2)
    wv = bdot('bij,bjd->bid', w.astype(BF16), v_aug)
    inter_scale = jnp.exp(inter - m_row)
    inter_scale = jnp.concatenate([inter_scale, inter_scale], axis=2)
    floor = jnp.exp(-m_row)
    k_src = (kb.astype(F32) * jnp.exp(gsum - b_col + li_col - m_out)).astype(BF16)
    upd = bdot('bjk,bjd->bkd', k_src, v_aug)
    decay = jnp.exp(gsum + m_in - m_out)

    c_aug = c_sc[...]
    for c in range(nc):
        hs = slice(c * ML_HEADS, (c + 1) * ML_HEADS)
        nd = inter_scale[hs] * bdot('hik,hkd->hid', qb[hs], c_aug.astype(BF16)) + wv[hs]
        hval = nd[:, :, :ML_D] / jnp.maximum(jnp.abs(nd[:, :, ML_D:]), floor[hs])
        for h in range(ML_HEADS):
            rs, cs = slice(c * L, (c + 1) * L), slice(h * ML_D, (h + 1) * ML_D)
            out_ref[0, rs, cs] = (og[rs, cs].astype(F32) * hval[h]).astype(out_ref.dtype)
        c_aug = decay[hs] * c_aug + upd[hs]
    c_sc[...] = c_aug


def _mlstm(mq, mk, mv, og, gif, gcol, gate_b, gate_b_row):
    bsz, s, width = mq.shape
    R = ML_ROWS
    nblk = s // R
    seq = lambda: pl.BlockSpec((1, R, width), lambda b, j: (b, j, 0))
    pos = np.arange(R)
    same_chunk = (pos[:, None] // ML_CHUNK) == (pos[None, :] // ML_CHUNK)
    tril = (same_chunk & (pos[None, :] <= pos[:, None])).astype(np.float32)
    return pl.pallas_call(
        _mlstm_kernel,
        grid=(bsz, nblk),
        in_specs=[seq(), seq(), seq(), seq(),
                  pl.BlockSpec((8, R), lambda b, j: (0, b * nblk + j)),
                  pl.BlockSpec((1, R, LANES), lambda b, j: (b, j, 0)),
                  _const_spec(gate_b.shape), _const_spec(gate_b_row.shape),
                  _const_spec((R, R)), _const_spec((R, R))],
        out_specs=seq(),
        out_shape=jax.ShapeDtypeStruct((bsz, s, width), BF16),
        scratch_shapes=[pltpu.VMEM((ML_HEADS, ML_D, 2 * ML_D), F32),
                        pltpu.VMEM((ML_HEADS, 8, LANES), F32)],
        compiler_params=pltpu.CompilerParams(dimension_semantics=("parallel", "arbitrary"),
                                             vmem_limit_bytes=VMEM_LIMIT),
        name="mlstm",
    )(mq, mk, mv, og, gif, gcol, gate_b, gate_b_row, jnp.asarray(tril, BF16), jnp.asarray(tril.T, BF16))


FF_CHUNK = 512


def _merge_ffn_kernel(x_ref, ya_ref, yb_ref, mg_ref, wa_ref, wb_ref, wo_ref, g2_ref, w1_ref, w2_ref,
                      out_ref):
    mg = mg_ref[...]
    mixed = (mg[:, :D_MODEL].astype(F32) * _dot(ya_ref[...], wa_ref[...])
             + mg[:, D_MODEL:].astype(F32) * _dot(yb_ref[...], wb_ref[...]))
    x1 = x_ref[...] + _dot(mixed.astype(BF16), wo_ref[...])
    h2 = x1 * lax.rsqrt(jnp.mean(x1 * x1, axis=-1, keepdims=True) + RMS_EPS) * g2_ref[...]
    h2 = h2.astype(BF16)
    acc = x1
    for c in range(D_FF // FF_CHUNK):
        a = jnp.maximum(_dot(h2, w1_ref[:, c * FF_CHUNK:(c + 1) * FF_CHUNK]), 0.0)
        acc = acc + _dot((a * a).astype(BF16), w2_ref[c * FF_CHUNK:(c + 1) * FF_CHUNK, :])
    out_ref[...] = acc


def _merge_ffn(x2d, ya, yb, mg, wa, wb, wo, g2, w1, w2, tm=512):
    n = x2d.shape[0]
    row = lambda w: pl.BlockSpec((tm, w), lambda i: (i, 0))
    return pl.pallas_call(
        _merge_ffn_kernel,
        grid=(n // tm,),
        in_specs=[row(D_MODEL), row(512), row(512), row(2048),
                  _const_spec(wa.shape), _const_spec(wb.shape), _const_spec(wo.shape),
                  _const_spec(g2.shape), _const_spec(w1.shape), _const_spec(w2.shape)],
        out_specs=row(D_MODEL),
        out_shape=jax.ShapeDtypeStruct((n, D_MODEL), F32),
        compiler_params=pltpu.CompilerParams(dimension_semantics=("parallel",),
                                             vmem_limit_bytes=VMEM_LIMIT),
        name="merge_ffn",
    )(x2d, ya, yb, mg, wa, wb, wo, g2, w1, w2)


def _proj_weights(w):
    widths = (512, 128, 128, 128, 128, 128, 128, 24, 512, 512, 512, 4, 4, 512, 2048)
    off = np.concatenate([[0], np.cumsum(widths)])
    (nq, nkc, nvc, nks, nvs, nkw, nvw, ngate, mq, mk, mv, mi, mf, mo, mgate) = (int(o) for o in off[:-1])
    col = lambda start, width: w[:, start:start + width]
    parts = [col(nq + NSA_D * (g * NSA_HPG + j), NSA_D) for j in range(NSA_HPG) for g in range(NSA_GROUPS)]
    parts += [col(nks, 128), col(nkw, 128), col(nkc, 128), col(nvc, 128), col(nvs, 128), col(nvw, 128)]
    gate = col(ngate, 24).reshape(-1, NSA_HEADS, 3).transpose(0, 2, 1).reshape(-1, 24)
    parts += [gate, col(mi, 4), col(mf, 4), jnp.zeros((w.shape[0], LANES - _IF_LANE - 8), w.dtype)]
    parts += [col(mq, 512), col(mk, 512), col(mv, 512), col(mo, 512), col(mgate, 2048)]
    w_all = jnp.concatenate(parts, axis=1).astype(BF16)
    w_if = jnp.concatenate([col(mi, 4), col(mf, 4)], axis=1).T.astype(BF16)
    return w_all, w_if


def _branch_a_weights(w):
    head = lambda h: w[NSA_D * h:NSA_D * (h + 1)]
    return jnp.concatenate([head(g * NSA_HPG + j) for j in range(NSA_HPG) for g in range(NSA_GROUPS)],
                           axis=0).astype(BF16)


def _compress_weights(pos, w1):
    r = w1.reshape(2, CMP_STRIDE, 1, NSA_D, CMP_HIDDEN)
    z = jnp.zeros_like(r)
    w1x = jnp.stack([jnp.concatenate([r, z], axis=2), jnp.concatenate([z, r], axis=2)])
    w1x = w1x.reshape(NSA_GROUPS, 2, CMP_STRIDE * NSA_GROUPS * NSA_D, CMP_HIDDEN).astype(BF16)
    posx = jnp.broadcast_to(pos.reshape(2, CMP_STRIDE, 1, NSA_D), (2, CMP_STRIDE, NSA_GROUPS, NSA_D))
    return w1x, posx.reshape(2, CMP_STRIDE * NSA_GROUPS * NSA_D).astype(BF16)


def _t5_bucket(dist):
    n = np.maximum(dist, 0)
    max_exact = REL_BUCKETS // 2
    nf = np.maximum(n, 1).astype(np.float32)
    large = max_exact + (np.log(nf / np.float32(max_exact)) / np.float32(math.log(REL_MAX_DIST / max_exact))
                         * np.float32(REL_BUCKETS - max_exact)).astype(np.int32)
    return np.where(n < max_exact, n, np.minimum(large, REL_BUCKETS - 1))


def _toeplitz(rel, n_rows, n_cols, stride, off, inner=1):
    a_rows = n_rows // inner
    lw = a_rows + n_cols
    k = np.arange(lw)[:, None]
    dist = stride * np.where(k < n_cols, -k, lw - k) + np.arange(inner)[None, :] + off
    live = (dist >= 0) & (k != n_cols)
    onehot = live[..., None] & (_t5_bucket(dist)[..., None] == np.arange(REL_BUCKETS))
    w = jnp.dot(jnp.asarray(onehot.reshape(lw * inner, REL_BUCKETS), F32), rel.T,
                precision=lax.Precision.HIGHEST)
    w = w.T.reshape(-1, lw, inner)
    flat = jnp.tile(w, (1, a_rows, 1))[:, :a_rows * (lw - 1)]
    out = flat.reshape(-1, a_rows, lw - 1, inner)[:, :, :n_cols]
    return out.transpose(0, 1, 3, 2).reshape(-1, n_rows, n_cols)


def _rel_bias(rel_table):
    return (rel_table - rel_table[REL_BUCKETS - 1][None, :]).T * LOG2E


def _bias_tables(rel, shifts):
    T = NSA_TILE
    sh_c, sh_s, sh_w = shifts
    r = np.arange(T)[:, None]

    c = np.arange(NEAR_KEYS)[None, :]
    ok = jnp.asarray((r - c + SLC_PAD) >= 0)[None]
    bs = jnp.where(ok, _toeplitz(rel, T, NEAR_KEYS, 1, SLC_PAD) - sh_s, MASK_F32)
    c = np.arange(WIN_KEYS)[None, :]
    d = r - c + WINDOW
    ok = jnp.asarray((d >= 0) & (d < WINDOW))[None]
    bw = jnp.where(ok, _toeplitz(rel, T, WIN_KEYS, 1, WINDOW) - sh_w, MASK_F32)
    r = np.arange(SEL_TILE)[:, None]
    m = np.arange(64)[None, :]
    d = r - CMP_STRIDE * m + BAND_DIST0
    band = (m >= 1) & (m < BAND_SLOTS - 1)
    vals = _toeplitz(rel, SEL_TILE, BAND_SLOTS, CMP_STRIDE, BAND_DIST0, inner=CMP_STRIDE)
    vals = jnp.pad(vals, ((0, 0), (0, 0), (0, 64 - BAND_SLOTS)))
    vals = jnp.where(jnp.asarray(band & (d >= 0))[None], vals, 0.0) - sh_c
    dead = (band & (d < 0)) | (m == BAND_SLOTS - 1)
    vals = jnp.where(jnp.asarray(dead)[None], MASK_BIG, vals)
    hi = vals.astype(BF16)
    lo = (vals - hi.astype(F32)).astype(BF16)
    bcf = jnp.concatenate([hi, lo], axis=-1)
    flat = lambda a: a.reshape(-1, a.shape[-1])
    return flat(bcf), flat(bs).astype(F32), flat(bw).astype(F32)


def _layer(l, x2d, bsz, s, consts, rel_table, norm1_g, w_in, nsa_q_gain, nsa_k_gain, cmp_k, cmp_v,
           ml_conv_w, ml_conv_b, ml_i_bias, ml_f_bias, w_branch_a, w_branch_b, w_out, norm2_g, w_ff1, w_ff2):
    n = bsz * s
    nseg = s // CMP_STRIDE
    oh, c2s = consts
    w_all, w_if = _proj_weights(w_in[l])
    qg_pad = jnp.concatenate([nsa_q_gain[l]] * 2)[None, :]
    kg_pad = jnp.stack([jnp.concatenate([nsa_k_gain[l, 1]] * 2), jnp.concatenate([nsa_k_gain[l, 2]] * 2)])
    (q, ks, kw, kc, vc, vs, vw, gates, gcol, mq, mk, mv, og, mg, gif) = _proj(
        x2d, s, norm1_g[l][None, :], w_all, w_if, qg_pad, kg_pad, ml_conv_w[l], ml_conv_b[l][None, :])

    def compress(a, params, gain, normalize):
        pos, w1, b1, w2, b2 = (p[l] for p in params)
        w1x, posx = _compress_weights(pos, w1)
        return _compress(a.reshape(bsz, nseg, CMP_STRIDE * LANES), w1x, posx, b1[None, :],
                         w2.astype(BF16), b2[None, :], gain[None, :], normalize)

    kcmp = compress(kc, cmp_k, nsa_k_gain[l, 0], True)
    vcmp = compress(vc, cmp_v, jnp.ones((NSA_D,), F32), False)

    qk_bound = lambda kg: 8.0 * LOG2E * jnp.max(jnp.abs(nsa_q_gain[l])) * jnp.max(jnp.abs(kg))
    tab = _rel_bias(rel_table)
    snap = lambda v: v.astype(BF16).astype(F32)
    shifts = [snap(qk_bound(nsa_k_gain[l, j]) + jnp.maximum(jnp.max(tab), 0.0)) for j in range(3)]
    safe = 2.0 * jnp.max(jnp.stack(shifts)) < SAFE_SHIFT_LOG2
    shifts = [jnp.where(safe, sh, 0.0) for sh in shifts]
    bcf, bs, bw = _bias_tables(tab, shifts)

    seq = lambda a: a.reshape(bsz, s, a.shape[-1])
    front = lambda a, p: jnp.pad(seq(a), ((0, 0), (p, 0), (0, 0)))
    operands = (seq(q), front(ks, SLC_PAD), front(vs, SLC_PAD), front(kw, WINDOW), front(vw, WINDOW),
                kcmp, vcmp, oh, seq(gates), bcf, bs, bw, c2s, jnp.full((1, LANES), -shifts[1], F32))
    y_a = _nsa(safe.astype(jnp.int32)[None], *operands)

    gate_b = jnp.concatenate([ml_i_bias[l], ml_f_bias[l]])
    gate_b_row = jnp.pad(gate_b, (_IF_LANE, LANES - _IF_LANE - 8))[None, :]
    y_b = _mlstm(seq(mq), seq(mk), seq(mv), seq(og), gif, seq(gcol), gate_b[:, None], gate_b_row)

    out = _merge_ffn(x2d, y_a.reshape(n, 512), y_b.reshape(n, 512), mg, _branch_a_weights(w_branch_a[l]),
                     w_branch_b[l].astype(BF16), w_out[l].astype(BF16), norm2_g[l][None, :],
                     w_ff1[l].astype(BF16), w_ff2[l].astype(BF16))
    return out, y_a, y_b


def _consts(s):
    nseg = s // CMP_STRIDE
    nsel = s // SLC_BLOCK
    blk_of_key = np.arange(s) // SLC_BLOCK
    oh = np.concatenate([np.ones((SLC_PAD, LANES), np.float32),
                         (blk_of_key[:, None] == np.arange(LANES)[None, :]).astype(np.float32)], axis=0)
    ci = np.arange(nseg)[:, None] * CMP_STRIDE
    sj = np.arange(LANES)[None, :] * SLC_BLOCK
    c2s = ((ci < sj + SLC_BLOCK) & (ci + CMP_LEN > sj) & (np.arange(LANES)[None, :] < nsel)
           & (np.arange(nseg)[:, None] < nseg - 1))
    return jnp.asarray(oh, BF16), jnp.asarray(c2s.astype(np.float32), BF16)


def kernel(x, norm1_g, w_in, nsa_q_gain, nsa_k_gain, cmp_k_pos, cmp_k_w1, cmp_k_b1, cmp_k_w2, cmp_k_b2, cmp_v_pos, cmp_v_w1, cmp_v_b1, cmp_v_w2, cmp_v_b2, rel_table, ml_conv_w, ml_conv_b, ml_i_bias, ml_f_bias, w_branch_a, w_branch_b, w_out, norm2_g, w_ff1, w_ff2):
    bsz, s, _ = x.shape
    consts = _consts(s)
    x2d = x.reshape(bsz * s, D_MODEL)
    for l in range(norm1_g.shape[0]):
        x2d, _, _ = _layer(l, x2d, bsz, s, consts, rel_table, norm1_g, w_in, nsa_q_gain, nsa_k_gain,
                           (cmp_k_pos, cmp_k_w1, cmp_k_b1, cmp_k_w2, cmp_k_b2),
                           (cmp_v_pos, cmp_v_w1, cmp_v_b1, cmp_v_w2, cmp_v_b2),
                           ml_conv_w, ml_conv_b, ml_i_bias, ml_f_bias,
                           w_branch_a, w_branch_b, w_out, norm2_g, w_ff1, w_ff2)
    return x2d.reshape(bsz, s, D_MODEL)
```

```python
import functools
import math

import numpy as np
import jax
import jax.numpy as jnp
from jax import lax
from jax.experimental import pallas as pl
from jax.experimental.pallas import tpu as pltpu

F32 = jnp.float32
BF16 = jnp.bfloat16

D_MODEL = 1024
NSA_HEADS = 8
NSA_GROUPS = 2
NSA_HPG = NSA_HEADS // NSA_GROUPS
NSA_D = 64
CMP_LEN = 32
CMP_STRIDE = 16
CMP_HIDDEN = 256
SLC_BLOCK = 64
SLC_TOPK = 16
WINDOW = 512
ML_HEADS = 4
ML_D = 128
ML_CHUNK = 64
CONV_WIDTH = 4
D_FF = 4 * D_MODEL
REL_BUCKETS = 32
REL_MAX_DIST = 128
RMS_EPS = 1e-6

LANES = 128
NSA_TILE = 256
SEL_TILE = 512
SLC_PAD = 256
FAR_CHUNK = 512
NEAR_KEYS = SLC_PAD + NSA_TILE
WIN_KEYS = WINDOW + NSA_TILE
BAND_OFF = 10
BAND_SLOTS = 2 + (SEL_TILE + 128) // CMP_STRIDE
BAND_DIST0 = CMP_STRIDE * BAND_OFF - (CMP_LEN - 1)
MASK_BIG = -1e9
MASK_F32 = -1e30
LOG2E = math.log2(math.e)
SAFE_SHIFT_LOG2 = 50.0
VMEM_LIMIT = 56 * 1024 * 1024

_Q_OFF, _KS_OFF, _KC_OFF, _VS_OFF = 0, 512, 768, 1024
_GATE_OFF, _MQ_OFF, _MK_OFF, _MV_OFF, _MO_OFF, _MG_OFF, _W_COLS = 1280, 1408, 1920, 2432, 2944, 3456, 5504
_IF_LANE = 24


def _nt(a, b):
    return lax.dot_general(a, b, (((1,), (1,)), ((), ())), preferred_element_type=F32)


def _dot(a, b):
    return jnp.dot(a, b, preferred_element_type=F32)


def _split3(x, dot_part):
    hi = x.astype(BF16)
    r1 = x - hi.astype(F32)
    mid = r1.astype(BF16)
    lo = (r1 - mid.astype(F32)).astype(BF16)
    return dot_part(hi) + dot_part(mid) + dot_part(lo)


def _const_spec(shape):
    nd = len(shape)
    return pl.BlockSpec(shape, lambda *_: (0,) * nd, pipeline_mode=pl.Buffered(1))


def _proj_kernel(x_ref, g1_ref, w_ref, wif_ref, qg_ref, kg_ref, cw_ref, cb_ref,
                 q_ref, ks_ref, kw_ref, kc_ref, vc_ref, vs_ref, vw_ref, gate_ref, gcol_ref,
                 mq_ref, mk_ref, mv_ref, mo_ref, mg_ref, gif_ref, ext_sc, seg_sc, *, tiles_per_seq):
    tm = x_ref.shape[0]
    x = x_ref[...]
    h = x * lax.rsqrt(jnp.mean(x * x, axis=-1, keepdims=True) + RMS_EPS) * g1_ref[...]
    hb = h.astype(BF16)

    def proj(off, width):
        return _dot(hb, w_ref[:, off:off + width])

    lane = lax.broadcasted_iota(jnp.int32, (1, LANES), 1)
    low = lane < NSA_D

    def half_norm(blk):
        sq = blk * blk
        ms0 = jnp.sum(jnp.where(low, sq, 0.0), axis=-1, keepdims=True) * (1.0 / NSA_D)
        ms1 = jnp.sum(jnp.where(low, 0.0, sq), axis=-1, keepdims=True) * (1.0 / NSA_D)
        return blk * jnp.where(low, lax.rsqrt(ms0 + RMS_EPS), lax.rsqrt(ms1 + RMS_EPS))

    for pair in range(NSA_HPG // 2):
        both = proj(_Q_OFF + 2 * LANES * pair, 2 * LANES)
        for hh in range(2):
            qn = half_norm(both[:, LANES * hh:LANES * (hh + 1)]) * qg_ref[...] * (NSA_D ** -0.5 * LOG2E)
            col = LANES * (2 * pair + hh)
            q_ref[:, col:col + LANES] = qn.astype(q_ref.dtype)

    both = proj(_KS_OFF, 2 * LANES)
    for hh, ref in enumerate((ks_ref, kw_ref)):
        ref[...] = (half_norm(both[:, LANES * hh:LANES * (hh + 1)]) * kg_ref[hh:hh + 1, :]).astype(ref.dtype)

    both = proj(_VS_OFF, 2 * LANES)
    vs_ref[...] = both[:, :LANES].astype(vs_ref.dtype)
    vw_ref[...] = both[:, LANES:].astype(vw_ref.dtype)

    both = proj(_KC_OFF, 2 * LANES)
    for hh, ref in enumerate((kc_ref, vc_ref)):
        seg_sc[hh] = both[:, LANES * hh:LANES * (hh + 1)]
        for tok in range(CMP_STRIDE):
            ref[:, LANES * tok:LANES * (tok + 1)] = seg_sc[hh, pl.ds(tok, tm // CMP_STRIDE, stride=CMP_STRIDE),
                                                           :].astype(ref.dtype)

    slab = proj(_GATE_OFF, LANES)
    gate_ref[...] = jax.nn.sigmoid(slab)
    gcol_ref[...] = slab

    @pl.when(pl.program_id(0) % tiles_per_seq == 0)
    def _():
        ext_sc[0:8, :] = jnp.zeros((8, ext_sc.shape[1]), F32)

    ext_sc[8:, 0:512] = proj(_MQ_OFF, 512)
    ext_sc[8:, 512:1024] = proj(_MK_OFF, 512)
    conv = cb_ref[...]
    for t in range(CONV_WIDTH):
        lo = 8 - (CONV_WIDTH - 1) + t
        conv = conv + ext_sc[lo:lo + tm, :] * cw_ref[t:t + 1, :]
    ext_sc[0:8, :] = ext_sc[tm:tm + 8, :]
    qk = conv * jax.nn.sigmoid(conv)
    mq_ref[...] = (qk[:, :512] * (ML_D ** -0.5)).astype(mq_ref.dtype)
    mk_ref[...] = qk[:, 512:].astype(mk_ref.dtype)

    mv_ref[...] = proj(_MV_OFF, 512).astype(mv_ref.dtype)
    mo_ref[...] = jax.nn.sigmoid(proj(_MO_OFF, 512)).astype(mo_ref.dtype)
    for c in range(4):
        mg_ref[:, 512 * c:512 * (c + 1)] = jax.nn.sigmoid(proj(_MG_OFF + 512 * c, 512)).astype(mg_ref.dtype)
    gif_ref[...] = _nt(wif_ref[...], hb)


def _proj(x2d, seq_len, g1, w_all, w_if, qg_pad, kg_pad, conv_w, conv_b, tm=512):
    n = x2d.shape[0]
    row = lambda w: pl.BlockSpec((tm, w), lambda i: (i, 0))
    out_shapes = (
        jax.ShapeDtypeStruct((n, 512), BF16),
        jax.ShapeDtypeStruct((n, LANES), BF16),
        jax.ShapeDtypeStruct((n, LANES), BF16),
        jax.ShapeDtypeStruct((n // CMP_STRIDE, CMP_STRIDE * LANES), BF16),
        jax.ShapeDtypeStruct((n // CMP_STRIDE, CMP_STRIDE * LANES), BF16),
        jax.ShapeDtypeStruct((n, LANES), BF16),
        jax.ShapeDtypeStruct((n, LANES), BF16),
        jax.ShapeDtypeStruct((n, LANES), F32),
        jax.ShapeDtypeStruct((n, LANES), F32),
        jax.ShapeDtypeStruct((n, 512), BF16),
        jax.ShapeDtypeStruct((n, 512), BF16),
        jax.ShapeDtypeStruct((n, 512), BF16),
        jax.ShapeDtypeStruct((n, 512), BF16),
        jax.ShapeDtypeStruct((n, 2048), BF16),
        jax.ShapeDtypeStruct((8, n), F32),
    )
    seg = pl.BlockSpec((tm // CMP_STRIDE, CMP_STRIDE * LANES), lambda i: (i, 0))
    out_specs = (row(512), row(LANES), row(LANES), seg, seg) + (row(LANES),) * 4 + (row(512),) * 4 + (
        row(2048), pl.BlockSpec((8, tm), lambda i: (0, i)))
    return pl.pallas_call(
        functools.partial(_proj_kernel, tiles_per_seq=seq_len // tm),
        grid=(n // tm,),
        in_specs=[row(D_MODEL), _const_spec((1, D_MODEL)), _const_spec((D_MODEL, _W_COLS)),
                  _const_spec((8, D_MODEL)), _const_spec((1, LANES)), _const_spec((2, LANES)),
                  _const_spec(conv_w.shape), _const_spec(conv_b.shape)],
        out_specs=out_specs,
        out_shape=out_shapes,
        scratch_shapes=[pltpu.VMEM((tm + 8, 1024), F32), pltpu.VMEM((2, tm, LANES), F32)],
        compiler_params=pltpu.CompilerParams(dimension_semantics=("arbitrary",),
                                             vmem_limit_bytes=VMEM_LIMIT),
        name="proj",
    )(x2d, g1, w_all, w_if, qg_pad, kg_pad, conv_w, conv_b)


def _compress_kernel(seg_ref, w1_ref, pos_ref, b1_ref, w2_ref, b2_ref, gain_ref, out_ref, *, normalize):
    seg = seg_ref[0]
    nseg, width = seg.shape
    c = math.sqrt(2.0 / math.pi)
    pos_lo = jnp.broadcast_to(pos_ref[0:1, :], (8, width))
    pos_hi = jnp.broadcast_to(pos_ref[1:2, :], (8, width))
    outs = []
    for g in range(NSA_GROUPS):
        a = _dot(seg, w1_ref[g, 0])
        b = _dot(seg, w1_ref[g, 1])
        posb = (_dot(pos_lo, w1_ref[g, 0]) + _dot(pos_hi, w1_ref[g, 1]))[0:1] + b1_ref[...]
        pre = a + pltpu.roll(b, nseg - 1, 0) + posb
        hid = 0.5 * pre * (1.0 + jnp.tanh(c * (pre + 0.044715 * (pre * pre * pre))))
        out = _dot(hid.astype(BF16), w2_ref[...]) + b2_ref[...]
        if normalize:
            ms = jnp.mean(out * out, axis=-1, keepdims=True)
            out = out * lax.rsqrt(ms + RMS_EPS) * gain_ref[...]
        outs.append(out)
    out_ref[0] = jnp.concatenate(outs, axis=1).astype(out_ref.dtype)


def _compress(segs, w1x, posx, b1, w2, b2, gain, normalize):
    bsz, nseg, width = segs.shape
    return pl.pallas_call(
        functools.partial(_compress_kernel, normalize=normalize),
        grid=(bsz,),
        in_specs=[pl.BlockSpec((1, nseg, width), lambda b: (b, 0, 0)),
                  _const_spec(w1x.shape), _const_spec(posx.shape), _const_spec(b1.shape),
                  _const_spec(w2.shape), _const_spec(b2.shape), _const_spec(gain.shape)],
        out_specs=pl.BlockSpec((1, nseg, LANES), lambda b: (b, 0, 0)),
        out_shape=jax.ShapeDtypeStruct((bsz, nseg, LANES), BF16),
        compiler_params=pltpu.CompilerParams(dimension_semantics=("parallel",),
                                             vmem_limit_bytes=VMEM_LIMIT),
        name="compress",
    )(segs, w1x, posx, b1, w2, b2, gain)


def _stack_heads(q_ref, lane_q):
    q = q_ref[0]
    return jnp.concatenate(
        [jnp.where((lane_q >= NSA_D) == (g == 1), q[:, LANES * j:LANES * (j + 1)], jnp.zeros((), BF16))
         for g in range(NSA_GROUPS) for j in range(NSA_HPG)], axis=0)


def _softmax_pv(s, v, online):
    if online:
        m = jnp.max(s, axis=1, keepdims=True)
        s = s - jnp.where(m < 0.1 * MASK_BIG, 0.0, m)
    p = jnp.exp2(s)
    acc = _dot(p.astype(BF16), jnp.concatenate([v, jnp.ones(v.shape, BF16)], axis=1))
    return p, acc[:, :LANES], acc[:, LANES:]


def _nsa_select_kernel(safe_ref, q_ref, kc_ref, vc_ref, gate_ref, bcf_ref, c2s_ref, shift_ref,
                       oc_ref, sbn_ref, sbf_ref, score_sc):
    i = pl.program_id(1)
    safe = safe_ref[0] != 0
    ncmp = kc_ref.shape[1]
    n_tiles = ncmp * CMP_STRIDE // SEL_TILE
    refs = (q_ref, kc_ref, vc_ref, gate_ref, bcf_ref, c2s_ref, shift_ref, oc_ref, sbn_ref, sbf_ref, score_sc)
    n_cls = max(c for c in (1, 2, 4) if n_tiles % c == 0 and (ncmp // c) % LANES == 0)
    for c in range(n_cls):
        @pl.when(safe & (i // (n_tiles // n_cls) == c))
        def _():
            _nsa_select_tile(i, ncmp * (c + 1) // n_cls, LANES * (c + 1) // n_cls, *refs, False)

    @pl.when(jnp.logical_not(safe))
    def _():
        _nsa_select_tile(i, ncmp, LANES, *refs, True)


def _nsa_select_tile(i, ncmp, nblk, q_ref, kc_ref, vc_ref, gate_ref, bcf_ref, c2s_ref, shift_ref,
                     oc_ref, sbn_ref, sbf_ref, score_sc, online):
    T = SEL_TILE
    grows = NSA_HPG * T
    lane_q = lax.broadcasted_iota(jnp.int32, (T, LANES), 1)
    qpad = _stack_heads(q_ref, lane_q)

    n_io = lax.broadcasted_iota(jnp.int32, (ncmp, LANES), 0)
    l_io = lax.broadcasted_iota(jnp.int32, (ncmp, LANES), 1)
    band = jnp.clip(n_io - i * (T // CMP_STRIDE) + BAND_OFF, 0, BAND_SLOTS - 1)
    oh_c = jnp.where(band == (l_io & 63), 1.0, 0.0).astype(BF16)
    kc_aug = jnp.concatenate([kc_ref[0, :ncmp, :], oh_c], axis=1)
    qc_aug = jnp.concatenate([qpad, bcf_ref[...]], axis=1)
    p_c, num_c, l_c = _softmax_pv(_nt(qc_aug, kc_aug), vc_ref[0, :ncmp, :], online)
    inv_c = 1.0 / jnp.maximum(l_c, 1e-30)
    o_c = num_c * inv_c
    p_c = p_c * jnp.concatenate([inv_c] * (ncmp // LANES), axis=1)

    gates = gate_ref[0]
    gated = [jnp.broadcast_to(gates[:, h:h + 1], (T, LANES)) * o_c[h * T:(h + 1) * T] for h in range(NSA_HEADS)]
    for j in range(NSA_HPG):
        oc_ref[0, :, LANES * j:LANES * (j + 1)] = jnp.where(
            lane_q < NSA_D, gated[j], gated[NSA_HPG + j]).astype(oc_ref.dtype)

    imp_t = []
    for g in range(NSA_GROUPS):
        ps = p_c[g * grows:g * grows + T]
        for hh in range(1, NSA_HPG):
            ps = ps + p_c[g * grows + hh * T:g * grows + (hh + 1) * T]
        ps_hi = ps.astype(BF16)
        ps_lo = (ps - ps_hi.astype(F32)).astype(BF16)
        c2s = c2s_ref[:ncmp, :]
        imp_t.append((_dot(ps_hi, c2s) + _dot(ps_lo, c2s)).T[:nblk])
    width = NSA_GROUPS * T
    j_io = lax.broadcasted_iota(jnp.int32, (nblk, width), 0)
    r_io = lax.broadcasted_iota(jnp.int32, (nblk, width), 1) & (T - 1)
    j_f = j_io.astype(F32)
    cur = i * (T // SLC_BLOCK) + jnp.right_shift(r_io, SLC_BLOCK.bit_length() - 1)
    forced = (j_io == 0) | (j_io == cur) | (j_io == cur - 1)
    n_pick = SLC_TOPK - 3
    start = jnp.where(forced, -jnp.inf, jnp.where(j_io <= cur, jnp.concatenate(imp_t, axis=1), -1.0 - j_f))

    def padded(score):
        if nblk == LANES:
            return score
        return jnp.concatenate([score, jnp.zeros((LANES - nblk, width), F32)], axis=0)

    score = start
    for _ in range(n_pick):
        score = jnp.where(score == jnp.max(score, axis=0, keepdims=True), -jnp.inf, score)
    score_sc[...] = padded(score)
    taken = jnp.sum(jnp.where((score == -jnp.inf) & jnp.logical_not(forced), 1.0, 0.0), axis=0, keepdims=True)

    @pl.when(jnp.max(taken) > n_pick + 0.5)
    def _():
        exact = start
        for _ in range(n_pick):
            best = jnp.max(exact, axis=0, keepdims=True)
            first = jnp.min(jnp.where(exact == best, j_f, float(nblk)), axis=0, keepdims=True)
            exact = jnp.where(j_f == first, -jnp.inf, exact)
        score_sc[...] = padded(exact)

    score = score_sc[...]
    near_start = ((i * T + lax.broadcasted_iota(jnp.int32, (T, LANES), 0)) // NSA_TILE) * (NSA_TILE // SLC_BLOCK)
    near_blk = lane_q >= near_start - SLC_PAD // SLC_BLOCK
    far_pick = shift_ref[0:1, 0:1]
    for g in range(NSA_GROUPS):
        picked = score[:, g * T:(g + 1) * T].T == -jnp.inf
        sbn_ref[0, :, LANES * g:LANES * (g + 1)] = jnp.where(picked & near_blk, 0.0, MASK_BIG).astype(BF16)
        sbf_ref[0, :, LANES * g:LANES * (g + 1)] = jnp.where(
            picked & jnp.logical_not(near_blk), far_pick, MASK_BIG).astype(BF16)


def _nsa_attend_kernel(safe_ref, *refs):
    *io_refs, acc_sc, m_sc, pa_sc, pb_sc = refs

    @pl.when(safe_ref[0] != 0)
    def _():
        _nsa_attend_tile(*io_refs, acc_sc, pa_sc, pb_sc, online=False)

    @pl.when(safe_ref[0] == 0)
    def _():
        _nsa_attend_tile(*io_refs, acc_sc, m_sc, online=True)


def _nsa_attend_tile(q_ref, ks_ref, vs_ref, kw_ref, vw_ref, oh_ref, gate_ref, oc_ref, sbn_ref, sbf_ref,
                     bs_ref, bw_ref, out_ref, acc_sc, *extra_sc, online):
    i = pl.program_id(1)
    t0 = pl.multiple_of(i * NSA_TILE, NSA_TILE)
    T = NSA_TILE
    rows = NSA_HEADS * T
    grows = NSA_HPG * T
    lane_q = lax.broadcasted_iota(jnp.int32, (T, LANES), 1)
    qpad = _stack_heads(q_ref, lane_q)

    def with_features(feat_ref):
        feats = [jnp.concatenate([feat_ref[0, :, LANES * g:LANES * (g + 1)]] * NSA_HPG, axis=0)
                 for g in range(NSA_GROUPS)]
        return jnp.concatenate([qpad, jnp.concatenate(feats, axis=0)], axis=1)

    q_near, q_far = with_features(sbn_ref), with_features(sbf_ref)

    r_w = lax.broadcasted_iota(jnp.int32, (WIN_KEYS, LANES), 0)
    pad_flag = jnp.where(r_w + (t0 - WINDOW) < 0, 1.0, 0.0).astype(BF16)
    kw_aug = jnp.concatenate([kw_ref[0, pl.ds(t0, WIN_KEYS), :], pad_flag], axis=1)
    qw_aug = jnp.concatenate([qpad, jnp.full((rows, LANES), MASK_BIG / LANES, BF16)], axis=1)
    s_w = _nt(qw_aug, kw_aug) + bw_ref[...]
    _, num_w, l_w = _softmax_pv(s_w, vw_ref[0, pl.ds(t0, WIN_KEYS), :], online)
    o_w = num_w * (1.0 / l_w)

    gates = gate_ref[0]
    gate_tile = lambda col: jnp.broadcast_to(gates[:, col:col + 1], (T, LANES))
    head_rows = [slice(h * T, (h + 1) * T) for h in range(NSA_HEADS)]
    o_gw = [gate_tile(16 + h) * o_w[head_rows[h]] for h in range(NSA_HEADS)]
    g_slc = [gate_tile(8 + h) for h in range(NSA_HEADS)]

    ones_f = jnp.ones((FAR_CHUNK, LANES), BF16)
    k_near = jnp.concatenate([ks_ref[0, pl.ds(t0, NEAR_KEYS), :], oh_ref[pl.ds(t0, NEAR_KEYS), :]], axis=1)
    v_near = jnp.concatenate([vs_ref[0, pl.ds(t0, NEAR_KEYS), :], jnp.ones((NEAR_KEYS, LANES), BF16)], axis=1)
    s_n = _nt(q_near, k_near) + bs_ref[...]
    if online:
        m_sc, = extra_sc
        m_n = jnp.max(s_n, axis=1, keepdims=True)
        m_sc[...] = m_n
        s_n = s_n - m_n
    acc_sc[...] = _dot(jnp.exp2(s_n).astype(BF16), v_near)

    last_chunk = (ks_ref.shape[1] - SLC_PAD) // FAR_CHUNK - 1

    def far_start(c):
        return pl.multiple_of(SLC_PAD + FAR_CHUNK * jnp.minimum(c, last_chunk), LANES)

    def far_scores(c, by_group=False):
        start = far_start(c)
        k_f = jnp.concatenate([ks_ref[0, pl.ds(start, FAR_CHUNK), :], oh_ref[pl.ds(start, FAR_CHUNK), :]], axis=1)
        if by_group:
            return jnp.concatenate([_nt(q_far[:grows], k_f), _nt(q_far[grows:], k_f)], axis=0)
        return _nt(q_far, k_f)

    def far_values(c):
        return jnp.concatenate([vs_ref[0, pl.ds(far_start(c), FAR_CHUNK), :], ones_f], axis=1)

    n_far = (t0 + (FAR_CHUNK - 1 - SLC_PAD)) // FAR_CHUNK
    if online:
        def far_online(c, carry):
            s_f = far_scores(c)
            m_old = m_sc[...]
            m_new = jnp.maximum(m_old, jnp.max(s_f, axis=1, keepdims=True))
            m_sc[...] = m_new
            acc_sc[...] = (jnp.exp2(m_old - m_new) * acc_sc[...]
                           + _dot(jnp.exp2(s_f - m_new).astype(BF16), far_values(c)))
            return carry

        lax.fori_loop(0, n_far, far_online, 0)
    else:
        pa_sc, pb_sc = extra_sc
        pa_sc[...] = jnp.exp2(far_scores(0, by_group=True)).astype(BF16)

        def far_pair(cc, carry):
            c0 = 2 * cc
            pv0 = _dot(pa_sc[...], far_values(c0))
            pb_sc[...] = jnp.exp2(far_scores(c0 + 1)).astype(BF16)
            pv1 = _dot(pb_sc[...], far_values(c0 + 1))
            pa_sc[...] = jnp.exp2(far_scores(c0 + 2)).astype(BF16)
            acc_sc[...] += pv0 + pv1
            return carry

        lax.fori_loop(0, n_far // 2, far_pair, 0)

        @pl.when(n_far % 2 == 1)
        def _():
            v_f = far_values(n_far - 1)
            acc_sc[0:grows, :] += _dot(pa_sc[0:grows, :], v_f)
            acc_sc[grows:, :] += _dot(pa_sc[grows:, :], v_f)

    acc = acc_sc[...]
    o_s = acc[:, :LANES] * (1.0 / acc[:, LANES:])

    head_out = [o_gw[h] + g_slc[h] * o_s[head_rows[h]] for h in range(NSA_HEADS)]
    for j in range(NSA_HPG):
        cols = slice(LANES * j, LANES * (j + 1))
        out_ref[0, :, cols] = (oc_ref[0, :, cols].astype(F32) + jnp.where(
            lane_q < NSA_D, head_out[j], head_out[NSA_HPG + j])).astype(out_ref.dtype)


def _nsa(safe, q, ks, vs, kw, vw, kc, vc, oh, gates, bcf, bs, bw, c2s, shift):
    bsz, s = q.shape[:2]
    flag = pl.BlockSpec(memory_space=pltpu.SMEM)
    T = NSA_TILE
    rows = NSA_HEADS * T
    grid = (bsz, s // T)
    params = pltpu.CompilerParams(dimension_semantics=("parallel", "arbitrary"), vmem_limit_bytes=VMEM_LIMIT)
    tile = lambda width: pl.BlockSpec((1, T, width), lambda b, i: (b, i, 0))
    per_b = lambda a: pl.BlockSpec((1,) + a.shape[1:], lambda b, i: (b, 0, 0), pipeline_mode=pl.Buffered(1))
    sel_tile = lambda width: pl.BlockSpec((1, SEL_TILE, width), lambda b, i: (b, i, 0))
    o_cmp, sb_near, sb_far = pl.pallas_call(
        _nsa_select_kernel,
        grid=(bsz, s // SEL_TILE),
        in_specs=[flag, sel_tile(NSA_HPG * LANES), per_b(kc), per_b(vc), sel_tile(LANES),
                  _const_spec(bcf.shape), _const_spec(c2s.shape), _const_spec(shift.shape)],
        out_specs=(sel_tile(NSA_HPG * LANES), sel_tile(NSA_GROUPS * LANES), sel_tile(NSA_GROUPS * LANES)),
        out_shape=(jax.ShapeDtypeStruct((bsz, s, NSA_HPG * LANES), BF16),
                   jax.ShapeDtypeStruct((bsz, s, NSA_GROUPS * LANES), BF16),
                   jax.ShapeDtypeStruct((bsz, s, NSA_GROUPS * LANES), BF16)),
        scratch_shapes=[pltpu.VMEM((LANES, NSA_GROUPS * SEL_TILE), F32)],
        compiler_params=params,
        name="nsa_select",
    )(safe, q, kc, vc, gates, bcf, c2s, shift)
    scratch = [pltpu.VMEM((rows, 2 * LANES), F32),
               pltpu.VMEM((rows, 1), F32),
               pltpu.VMEM((rows, FAR_CHUNK), BF16),
               pltpu.VMEM((rows, FAR_CHUNK), BF16)]
    return pl.pallas_call(
        _nsa_attend_kernel,
        grid=grid,
        in_specs=[flag, tile(NSA_HPG * LANES), per_b(ks), per_b(vs), per_b(kw), per_b(vw), _const_spec(oh.shape),
                  tile(LANES), tile(NSA_HPG * LANES), tile(NSA_GROUPS * LANES), tile(NSA_GROUPS * LANES),
                  _const_spec(bs.shape), _const_spec(bw.shape)],
        out_specs=tile(NSA_HPG * LANES),
        out_shape=jax.ShapeDtypeStruct((bsz, s, NSA_HPG * LANES), BF16),
        scratch_shapes=scratch,
        compiler_params=params,
        name="nsa_attend",
    )(safe, q, ks, vs, kw, vw, oh, gates, o_cmp, sb_near, sb_far, bs, bw)


ML_ROWS = 512


def _mlstm_kernel(q_ref, k_ref, v_ref, o_ref, gif_ref, gcol_ref, gb_ref, gbrow_ref, tril_ref, triu_ref,
                  out_ref, c_sc, m_sc):
    j = pl.program_id(1)
    R = ML_ROWS
    L = ML_CHUNK

    @pl.when(j == 0)
    def _():
        c_sc[...] = jnp.zeros_like(c_sc)
        m_sc[...] = jnp.zeros_like(m_sc)

    og = o_ref[0]

    pre = gif_ref[...] + gb_ref[...]
    row8 = lax.broadcasted_iota(jnp.int32, (8, R), 0)
    logf = jnp.minimum(pre, 0.0) - jnp.log(1.0 + jnp.exp(-jnp.abs(pre)))
    g8 = jnp.where(row8 < ML_HEADS, pre, logf)
    pre_c = gcol_ref[0] + gbrow_ref[...]
    lane_c = lax.broadcasted_iota(jnp.int32, pre_c.shape, 1)
    logf_c = jnp.minimum(pre_c, 0.0) - jnp.log(1.0 + jnp.exp(-jnp.abs(pre_c)))
    g_c = jnp.where(lane_c < _IF_LANE + ML_HEADS, pre_c, logf_c)
    cum_c = _split3(g_c, lambda part: _dot(tril_ref[...], part))
    cum_row = _split3(g8, lambda part: _dot(part, triu_ref[...]))

    a_io = lax.broadcasted_iota(jnp.int32, (L, L), 0)
    b_io = lax.broadcasted_iota(jnp.int32, (L, L), 1)
    causal = (b_io <= a_io)[None]
    nc = R // L
    pairs = [(c, h) for c in range(nc) for h in range(ML_HEADS)]

    def blocks(a):
        return jnp.stack([a[c * L:(c + 1) * L, h * ML_D:(h + 1) * ML_D] for c, h in pairs])

    def cols(a, k0):
        return jnp.stack([jnp.broadcast_to(a[c * L:(c + 1) * L, k0 + h:k0 + h + 1], (L, LANES)) for c, h in pairs])

    def rows(a, k0):
        return jnp.stack([a[k0 + h:k0 + h + 1, c * L:(c + 1) * L] for c, h in pairs])

    bdot = lambda eq, x, y: jnp.einsum(eq, x, y, preferred_element_type=F32)
    qb, kb, vb = blocks(q_ref[0]), blocks(k_ref[0]), blocks(v_ref[0])
    b_col, li_col = cols(cum_c, _IF_LANE + ML_HEADS), cols(g_c, _IF_LANE)
    b_row, li_row = rows(cum_row, ML_HEADS), rows(g8, 0)
    gsum = b_row[:, :, L - 1:L]

    s_max = jnp.max(gsum - b_row + li_row, axis=2, keepdims=True)
    m_run = m_sc[:, 0:1, 0:1]
    m_ins, m_outs = [], []
    for c in range(nc):
        hs = slice(c * ML_HEADS, (c + 1) * ML_HEADS)
        m_ins.append(m_run)
        m_run = jnp.maximum(gsum[hs] + m_run, s_max[hs])
        m_outs.append(m_run)
    m_sc[...] = jnp.broadcast_to(m_run, m_sc.shape)
    m_in, m_out = jnp.concatenate(m_ins, axis=0), jnp.concatenate(m_outs, axis=0)

    log_d = jnp.where(causal, b_col[:, :, :L] - b_row + li_row, -jnp.inf)
    inter = b_col + m_in
    m_row = jnp.maximum(inter, jnp.broadcast_to(jnp.max(log_d, axis=2, keepdims=True), inter.shape))
    w = bdot('bik,bjk->bij', qb, kb) * jnp.exp(log_d - m_row[:, :, :L])
    v_aug = jnp.concatenate([vb, jnp.ones(vb.shape, BF16)], axis=2)
    wv = bdot('bij,bjd->bid', w.astype(BF16), v_aug)
    inter_scale = jnp.exp(inter - m_row)
    inter_scale = jnp.concatenate([inter_scale, inter_scale], axis=2)
    floor = jnp.exp(-m_row)
    k_src = (kb.astype(F32) * jnp.exp(gsum - b_col + li_col - m_out)).astype(BF16)
    upd = bdot('bjk,bjd->bkd', k_src, v_aug)
    decay = jnp.exp(gsum + m_in - m_out)

    c_aug = c_sc[...]
    for c in range(nc):
        hs = slice(c * ML_HEADS, (c + 1) * ML_HEADS)
        nd = inter_scale[hs] * bdot('hik,hkd->hid', qb[hs], c_aug.astype(BF16)) + wv[hs]
        hval = nd[:, :, :ML_D] / jnp.maximum(jnp.abs(nd[:, :, ML_D:]), floor[hs])
        for h in range(ML_HEADS):
            rs, cs = slice(c * L, (c + 1) * L), slice(h * ML_D, (h + 1) * ML_D)
            out_ref[0, rs, cs] = (og[rs, cs].astype(F32) * hval[h]).astype(out_ref.dtype)
        c_aug = decay[hs] * c_aug + upd[hs]
    c_sc[...] = c_aug


def _mlstm(mq, mk, mv, og, gif, gcol, gate_b, gate_b_row):
    bsz, s, width = mq.shape
    R = ML_ROWS
    nblk = s // R
    seq = lambda: pl.BlockSpec((1, R, width), lambda b, j: (b, j, 0))
    pos = np.arange(R)
    same_chunk = (pos[:, None] // ML_CHUNK) == (pos[None, :] // ML_CHUNK)
    tril = (same_chunk & (pos[None, :] <= pos[:, None])).astype(np.float32)
    return pl.pallas_call(
        _mlstm_kernel,
        grid=(bsz, nblk),
        in_specs=[seq(), seq(), seq(), seq(),
                  pl.BlockSpec((8, R), lambda b, j: (0, b * nblk + j)),
                  pl.BlockSpec((1, R, LANES), lambda b, j: (b, j, 0)),
                  _const_spec(gate_b.shape), _const_spec(gate_b_row.shape),
                  _const_spec((R, R)), _const_spec((R, R))],
        out_specs=seq(),
        out_shape=jax.ShapeDtypeStruct((bsz, s, width), BF16),
        scratch_shapes=[pltpu.VMEM((ML_HEADS, ML_D, 2 * ML_D), F32),
                        pltpu.VMEM((ML_HEADS, 8, LANES), F32)],
        compiler_params=pltpu.CompilerParams(dimension_semantics=("parallel", "arbitrary"),
                                             vmem_limit_bytes=VMEM_LIMIT),
        name="mlstm",
    )(mq, mk, mv, og, gif, gcol, gate_b, gate_b_row, jnp.asarray(tril, BF16), jnp.asarray(tril.T, BF16))


FF_CHUNK = 512


def _merge_ffn_kernel(x_ref, ya_ref, yb_ref, mg_ref, wa_ref, wb_ref, wo_ref, g2_ref, w1_ref, w2_ref,
                      out_ref):
    mg = mg_ref[...]
    mixed = (mg[:, :D_MODEL].astype(F32) * _dot(ya_ref[...], wa_ref[...])
             + mg[:, D_MODEL:].astype(F32) * _dot(yb_ref[...], wb_ref[...]))
    x1 = x_ref[...] + _dot(mixed.astype(BF16), wo_ref[...])
    h2 = x1 * lax.rsqrt(jnp.mean(x1 * x1, axis=-1, keepdims=True) + RMS_EPS) * g2_ref[...]
    h2 = h2.astype(BF16)
    acc = x1
    for c in range(D_FF // FF_CHUNK):
        a = jnp.maximum(_dot(h2, w1_ref[:, c * FF_CHUNK:(c + 1) * FF_CHUNK]), 0.0)
        acc = acc + _dot((a * a).astype(BF16), w2_ref[c * FF_CHUNK:(c + 1) * FF_CHUNK, :])
    out_ref[...] = acc


def _merge_ffn(x2d, ya, yb, mg, wa, wb, wo, g2, w1, w2, tm=512):
    n = x2d.shape[0]
    row = lambda w: pl.BlockSpec((tm, w), lambda i: (i, 0))
    return pl.pallas_call(
        _merge_ffn_kernel,
        grid=(n // tm,),
        in_specs=[row(D_MODEL), row(512), row(512), row(2048),
                  _const_spec(wa.shape), _const_spec(wb.shape), _const_spec(wo.shape),
                  _const_spec(g2.shape), _const_spec(w1.shape), _const_spec(w2.shape)],
        out_specs=row(D_MODEL),
        out_shape=jax.ShapeDtypeStruct((n, D_MODEL), F32),
        compiler_params=pltpu.CompilerParams(dimension_semantics=("parallel",),
                                             vmem_limit_bytes=VMEM_LIMIT),
        name="merge_ffn",
    )(x2d, ya, yb, mg, wa, wb, wo, g2, w1, w2)


def _proj_weights(w):
    widths = (512, 128, 128, 128, 128, 128, 128, 24, 512, 512, 512, 4, 4, 512, 2048)
    off = np.concatenate([[0], np.cumsum(widths)])
    (nq, nkc, nvc, nks, nvs, nkw, nvw, ngate, mq, mk, mv, mi, mf, mo, mgate) = (int(o) for o in off[:-1])
    col = lambda start, width: w[:, start:start + width]
    parts = [col(nq + NSA_D * (g * NSA_HPG + j), NSA_D) for j in range(NSA_HPG) for g in range(NSA_GROUPS)]
    parts += [col(nks, 128), col(nkw, 128), col(nkc, 128), col(nvc, 128), col(nvs, 128), col(nvw, 128)]
    gate = col(ngate, 24).reshape(-1, NSA_HEADS, 3).transpose(0, 2, 1).reshape(-1, 24)
    parts += [gate, col(mi, 4), col(mf, 4), jnp.zeros((w.shape[0], LANES - _IF_LANE - 8), w.dtype)]
    parts += [col(mq, 512), col(mk, 512), col(mv, 512), col(mo, 512), col(mgate, 2048)]
    w_all = jnp.concatenate(parts, axis=1).astype(BF16)
    w_if = jnp.concatenate([col(mi, 4), col(mf, 4)], axis=1).T.astype(BF16)
    return w_all, w_if


def _branch_a_weights(w):
    head = lambda h: w[NSA_D * h:NSA_D * (h + 1)]
    return jnp.concatenate([head(g * NSA_HPG + j) for j in range(NSA_HPG) for g in range(NSA_GROUPS)],
                           axis=0).astype(BF16)


def _compress_weights(pos, w1):
    r = w1.reshape(2, CMP_STRIDE, 1, NSA_D, CMP_HIDDEN)
    z = jnp.zeros_like(r)
    w1x = jnp.stack([jnp.concatenate([r, z], axis=2), jnp.concatenate([z, r], axis=2)])
    w1x = w1x.reshape(NSA_GROUPS, 2, CMP_STRIDE * NSA_GROUPS * NSA_D, CMP_HIDDEN).astype(BF16)
    posx = jnp.broadcast_to(pos.reshape(2, CMP_STRIDE, 1, NSA_D), (2, CMP_STRIDE, NSA_GROUPS, NSA_D))
    return w1x, posx.reshape(2, CMP_STRIDE * NSA_GROUPS * NSA_D).astype(BF16)


def _t5_bucket(dist):
    n = np.maximum(dist, 0)
    max_exact = REL_BUCKETS // 2
    nf = np.maximum(n, 1).astype(np.float32)
    large = max_exact + (np.log(nf / np.float32(max_exact)) / np.float32(math.log(REL_MAX_DIST / max_exact))
                         * np.float32(REL_BUCKETS - max_exact)).astype(np.int32)
    return np.where(n < max_exact, n, np.minimum(large, REL_BUCKETS - 1))


def _toeplitz(rel, n_rows, n_cols, stride, off, inner=1):
    a_rows = n_rows // inner
    lw = a_rows + n_cols
    k = np.arange(lw)[:, None]
    dist = stride * np.where(k < n_cols, -k, lw - k) + np.arange(inner)[None, :] + off
    live = (dist >= 0) & (k != n_cols)
    onehot = live[..., None] & (_t5_bucket(dist)[..., None] == np.arange(REL_BUCKETS))
    w = jnp.dot(jnp.asarray(onehot.reshape(lw * inner, REL_BUCKETS), F32), rel.T,
                precision=lax.Precision.HIGHEST)
    w = w.T.reshape(-1, lw, inner)
    flat = jnp.tile(w, (1, a_rows, 1))[:, :a_rows * (lw - 1)]
    out = flat.reshape(-1, a_rows, lw - 1, inner)[:, :, :n_cols]
    return out.transpose(0, 1, 3, 2).reshape(-1, n_rows, n_cols)


def _rel_bias(rel_table):
    return (rel_table - rel_table[REL_BUCKETS - 1][None, :]).T * LOG2E


def _bias_tables(rel, shifts):
    T = NSA_TILE
    sh_c, sh_s, sh_w = shifts
    r = np.arange(T)[:, None]

    c = np.arange(NEAR_KEYS)[None, :]
    ok = jnp.asarray((r - c + SLC_PAD) >= 0)[None]
    bs = jnp.where(ok, _toeplitz(rel, T, NEAR_KEYS, 1, SLC_PAD) - sh_s, MASK_F32)
    c = np.arange(WIN_KEYS)[None, :]
    d = r - c + WINDOW
    ok = jnp.asarray((d >= 0) & (d < WINDOW))[None]
    bw = jnp.where(ok, _toeplitz(rel, T, WIN_KEYS, 1, WINDOW) - sh_w, MASK_F32)
    r = np.arange(SEL_TILE)[:, None]
    m = np.arange(64)[None, :]
    d = r - CMP_STRIDE * m + BAND_DIST0
    band = (m >= 1) & (m < BAND_SLOTS - 1)
    vals = _toeplitz(rel, SEL_TILE, BAND_SLOTS, CMP_STRIDE, BAND_DIST0, inner=CMP_STRIDE)
    vals = jnp.pad(vals, ((0, 0), (0, 0), (0, 64 - BAND_SLOTS)))
    vals = jnp.where(jnp.asarray(band & (d >= 0))[None], vals, 0.0) - sh_c
    dead = (band & (d < 0)) | (m == BAND_SLOTS - 1)
    vals = jnp.where(jnp.asarray(dead)[None], MASK_BIG, vals)
    hi = vals.astype(BF16)
    lo = (vals - hi.astype(F32)).astype(BF16)
    bcf = jnp.concatenate([hi, lo], axis=-1)
    flat = lambda a: a.reshape(-1, a.shape[-1])
    return flat(bcf), flat(bs).astype(F32), flat(bw).astype(F32)


def _layer(l, x2d, bsz, s, consts, rel_table, norm1_g, w_in, nsa_q_gain, nsa_k_gain, cmp_k, cmp_v,
           ml_conv_w, ml_conv_b, ml_i_bias, ml_f_bias, w_branch_a, w_branch_b, w_out, norm2_g, w_ff1, w_ff2):
    n = bsz * s
    nseg = s // CMP_STRIDE
    oh, c2s = consts
    w_all, w_if = _proj_weights(w_in[l])
    qg_pad = jnp.concatenate([nsa_q_gain[l]] * 2)[None, :]
    kg_pad = jnp.stack([jnp.concatenate([nsa_k_gain[l, 1]] * 2), jnp.concatenate([nsa_k_gain[l, 2]] * 2)])
    (q, ks, kw, kc, vc, vs, vw, gates, gcol, mq, mk, mv, og, mg, gif) = _proj(
        x2d, s, norm1_g[l][None, :], w_all, w_if, qg_pad, kg_pad, ml_conv_w[l], ml_conv_b[l][None, :])

    def compress(a, params, gain, normalize):
        pos, w1, b1, w2, b2 = (p[l] for p in params)
        w1x, posx = _compress_weights(pos, w1)
        return _compress(a.reshape(bsz, nseg, CMP_STRIDE * LANES), w1x, posx, b1[None, :],
                         w2.astype(BF16), b2[None, :], gain[None, :], normalize)

    kcmp = compress(kc, cmp_k, nsa_k_gain[l, 0], True)
    vcmp = compress(vc, cmp_v, jnp.ones((NSA_D,), F32), False)

    qk_bound = lambda kg: 8.0 * LOG2E * jnp.max(jnp.abs(nsa_q_gain[l])) * jnp.max(jnp.abs(kg))
    tab = _rel_bias(rel_table)
    snap = lambda v: v.astype(BF16).astype(F32)
    shifts = [snap(qk_bound(nsa_k_gain[l, j]) + jnp.maximum(jnp.max(tab), 0.0)) for j in range(3)]
    safe = 2.0 * jnp.max(jnp.stack(shifts)) < SAFE_SHIFT_LOG2
    shifts = [jnp.where(safe, sh, 0.0) for sh in shifts]
    bcf, bs, bw = _bias_tables(tab, shifts)

    seq = lambda a: a.reshape(bsz, s, a.shape[-1])
    front = lambda a, p: jnp.pad(seq(a), ((0, 0), (p, 0), (0, 0)))
    operands = (seq(q), front(ks, SLC_PAD), front(vs, SLC_PAD), front(kw, WINDOW), front(vw, WINDOW),
                kcmp, vcmp, oh, seq(gates), bcf, bs, bw, c2s, jnp.full((1, LANES), -shifts[1], F32))
    y_a = _nsa(safe.astype(jnp.int32)[None], *operands)

    gate_b = jnp.concatenate([ml_i_bias[l], ml_f_bias[l]])
    gate_b_row = jnp.pad(gate_b, (_IF_LANE, LANES - _IF_LANE - 8))[None, :]
    y_b = _mlstm(seq(mq), seq(mk), seq(mv), seq(og), gif, seq(gcol), gate_b[:, None], gate_b_row)

    out = _merge_ffn(x2d, y_a.reshape(n, 512), y_b.reshape(n, 512), mg, _branch_a_weights(w_branch_a[l]),
                     w_branch_b[l].astype(BF16), w_out[l].astype(BF16), norm2_g[l][None, :],
                     w_ff1[l].astype(BF16), w_ff2[l].astype(BF16))
    return out, y_a, y_b


def _consts(s):
    nseg = s // CMP_STRIDE
    nsel = s // SLC_BLOCK
    blk_of_key = np.arange(s) // SLC_BLOCK
    oh = np.concatenate([np.ones((SLC_PAD, LANES), np.float32),
                         (blk_of_key[:, None] == np.arange(LANES)[None, :]).astype(np.float32)], axis=0)
    ci = np.arange(nseg)[:, None] * CMP_STRIDE
    sj = np.arange(LANES)[None, :] * SLC_BLOCK
    c2s = ((ci < sj + SLC_BLOCK) & (ci + CMP_LEN > sj) & (np.arange(LANES)[None, :] < nsel)
           & (np.arange(nseg)[:, None] < nseg - 1))
    return jnp.asarray(oh, BF16), jnp.asarray(c2s.astype(np.float32), BF16)


def kernel(x, norm1_g, w_in, nsa_q_gain, nsa_k_gain, cmp_k_pos, cmp_k_w1, cmp_k_b1, cmp_k_w2, cmp_k_b2, cmp_v_pos, cmp_v_w1, cmp_v_b1, cmp_v_w2, cmp_v_b2, rel_table, ml_conv_w, ml_conv_b, ml_i_bias, ml_f_bias, w_branch_a, w_branch_b, w_out, norm2_g, w_ff1, w_ff2):
    bsz, s, _ = x.shape
    consts = _consts(s)
    x2d = x.reshape(bsz * s, D_MODEL)
    for l in range(norm1_g.shape[0]):
        x2d, _, _ = _layer(l, x2d, bsz, s, consts, rel_table, norm1_g, w_in, nsa_q_gain, nsa_k_gain,
                           (cmp_k_pos, cmp_k_w1, cmp_k_b1, cmp_k_w2, cmp_k_b2),
                           (cmp_v_pos, cmp_v_w1, cmp_v_b1, cmp_v_w2, cmp_v_b2),
                           ml_conv_w, ml_conv_b, ml_i_bias, ml_f_bias,
                           w_branch_a, w_branch_b, w_out, norm2_g, w_ff1, w_ff2)
    return x2d.reshape(bsz, s, D_MODEL)
```

```python
import functools
import math

import numpy as np
import jax
import jax.numpy as jnp
from jax import lax
from jax.experimental import pallas as pl
from jax.experimental.pallas import tpu as pltpu

F32 = jnp.float32
BF16 = jnp.bfloat16

D_MODEL = 1024
NSA_HEADS = 8
NSA_GROUPS = 2
NSA_HPG = NSA_HEADS // NSA_GROUPS
NSA_D = 64
CMP_LEN = 32
CMP_STRIDE = 16
CMP_HIDDEN = 256
SLC_BLOCK = 64
SLC_TOPK = 16
WINDOW = 512
ML_HEADS = 4
ML_D = 128
ML_CHUNK = 64
CONV_WIDTH = 4
D_FF = 4 * D_MODEL
REL_BUCKETS = 32
REL_MAX_DIST = 128
RMS_EPS = 1e-6

LANES = 128
NSA_TILE = 256
SEL_TILE = 512
SLC_PAD = 256
FAR_CHUNK = 512
NEAR_KEYS = SLC_PAD + NSA_TILE
WIN_KEYS = WINDOW + NSA_TILE
BAND_OFF = 10
BAND_SLOTS = 2 + (SEL_TILE + 128) // CMP_STRIDE
BAND_DIST0 = CMP_STRIDE * BAND_OFF - (CMP_LEN - 1)
MASK_BIG = -1e9
MASK_F32 = -1e30
LOG2E = math.log2(math.e)
SAFE_SHIFT_LOG2 = 50.0
VMEM_LIMIT = 56 * 1024 * 1024

_Q_OFF, _KS_OFF, _KC_OFF, _VS_OFF = 0, 512, 768, 1024
_GATE_OFF, _MQ_OFF, _MK_OFF, _MV_OFF, _MO_OFF, _MG_OFF, _W_COLS = 1280, 1408, 1920, 2432, 2944, 3456, 5504
_IF_LANE = 24


def _nt(a, b):
    return lax.dot_general(a, b, (((1,), (1,)), ((), ())), preferred_element_type=F32)


def _dot(a, b):
    return jnp.dot(a, b, preferred_element_type=F32)


def _split3(x, dot_part):
    hi = x.astype(BF16)
    r1 = x - hi.astype(F32)
    mid = r1.astype(BF16)
    lo = (r1 - mid.astype(F32)).astype(BF16)
    return dot_part(hi) + dot_part(mid) + dot_part(lo)


def _const_spec(shape):
    nd = len(shape)
    return pl.BlockSpec(shape, lambda *_: (0,) * nd, pipeline_mode=pl.Buffered(1))


def _proj_kernel(x_ref, g1_ref, w_ref, wif_ref, qg_ref, kg_ref, cw_ref, cb_ref,
                 q_ref, ks_ref, kw_ref, kc_ref, vc_ref, vs_ref, vw_ref, gate_ref, gcol_ref,
                 mq_ref, mk_ref, mv_ref, mo_ref, mg_ref, gif_ref, ext_sc, seg_sc, *, tiles_per_seq):
    tm = x_ref.shape[0]
    x = x_ref[...]
    h = x * lax.rsqrt(jnp.mean(x * x, axis=-1, keepdims=True) + RMS_EPS) * g1_ref[...]
    hb = h.astype(BF16)

    def proj(off, width):
        return _dot(hb, w_ref[:, off:off + width])

    lane = lax.broadcasted_iota(jnp.int32, (1, LANES), 1)
    low = lane < NSA_D

    def half_norm(blk):
        sq = blk * blk
        ms0 = jnp.sum(jnp.where(low, sq, 0.0), axis=-1, keepdims=True) * (1.0 / NSA_D)
        ms1 = jnp.sum(jnp.where(low, 0.0, sq), axis=-1, keepdims=True) * (1.0 / NSA_D)
        return blk * jnp.where(low, lax.rsqrt(ms0 + RMS_EPS), lax.rsqrt(ms1 + RMS_EPS))

    for pair in range(NSA_HPG // 2):
        both = proj(_Q_OFF + 2 * LANES * pair, 2 * LANES)
        for hh in range(2):
            qn = half_norm(both[:, LANES * hh:LANES * (hh + 1)]) * qg_ref[...] * (NSA_D ** -0.5 * LOG2E)
            col = LANES * (2 * pair + hh)
            q_ref[:, col:col + LANES] = qn.astype(q_ref.dtype)

    both = proj(_KS_OFF, 2 * LANES)
    for hh, ref in enumerate((ks_ref, kw_ref)):
        ref[...] = (half_norm(both[:, LANES * hh:LANES * (hh + 1)]) * kg_ref[hh:hh + 1, :]).astype(ref.dtype)

    both = proj(_VS_OFF, 2 * LANES)
    vs_ref[...] = both[:, :LANES].astype(vs_ref.dtype)
    vw_ref[...] = both[:, LANES:].astype(vw_ref.dtype)

    both = proj(_KC_OFF, 2 * LANES)
    for hh, ref in enumerate((kc_ref, vc_ref)):
        seg_sc[hh] = both[:, LANES * hh:LANES * (hh + 1)]
        for tok in range(CMP_STRIDE):
            ref[:, LANES * tok:LANES * (tok + 1)] = seg_sc[hh, pl.ds(tok, tm // CMP_STRIDE, stride=CMP_STRIDE),
                                                           :].astype(ref.dtype)

    slab = proj(_GATE_OFF, LANES)
    gate_ref[...] = jax.nn.sigmoid(slab)
    gcol_ref[...] = slab

    @pl.when(pl.program_id(0) % tiles_per_seq == 0)
    def _():
        ext_sc[0:8, :] = jnp.zeros((8, ext_sc.shape[1]), F32)

    ext_sc[8:, 0:512] = proj(_MQ_OFF, 512)
    ext_sc[8:, 512:1024] = proj(_MK_OFF, 512)
    conv = cb_ref[...]
    for t in range(CONV_WIDTH):
        lo = 8 - (CONV_WIDTH - 1) + t
        conv = conv + ext_sc[lo:lo + tm, :] * cw_ref[t:t + 1, :]
    ext_sc[0:8, :] = ext_sc[tm:tm + 8, :]
    qk = conv * jax.nn.sigmoid(conv)
    mq_ref[...] = (qk[:, :512] * (ML_D ** -0.5)).astype(mq_ref.dtype)
    mk_ref[...] = qk[:, 512:].astype(mk_ref.dtype)

    mv_ref[...] = proj(_MV_OFF, 512).astype(mv_ref.dtype)
    mo_ref[...] = jax.nn.sigmoid(proj(_MO_OFF, 512)).astype(mo_ref.dtype)
    for c in range(4):
        mg_ref[:, 512 * c:512 * (c + 1)] = jax.nn.sigmoid(proj(_MG_OFF + 512 * c, 512)).astype(mg_ref.dtype)
    gif_ref[...] = _nt(wif_ref[...], hb)


def _proj(x2d, seq_len, g1, w_all, w_if, qg_pad, kg_pad, conv_w, conv_b, tm=512):
    n = x2d.shape[0]
    row = lambda w: pl.BlockSpec((tm, w), lambda i: (i, 0))
    out_shapes = (
        jax.ShapeDtypeStruct((n, 512), BF16),
        jax.ShapeDtypeStruct((n, LANES), BF16),
        jax.ShapeDtypeStruct((n, LANES), BF16),
        jax.ShapeDtypeStruct((n // CMP_STRIDE, CMP_STRIDE * LANES), BF16),
        jax.ShapeDtypeStruct((n // CMP_STRIDE, CMP_STRIDE * LANES), BF16),
        jax.ShapeDtypeStruct((n, LANES), BF16),
        jax.ShapeDtypeStruct((n, LANES), BF16),
        jax.ShapeDtypeStruct((n, LANES), F32),
        jax.ShapeDtypeStruct((n, LANES), F32),
        jax.ShapeDtypeStruct((n, 512), BF16),
        jax.ShapeDtypeStruct((n, 512), BF16),
        jax.ShapeDtypeStruct((n, 512), BF16),
        jax.ShapeDtypeStruct((n, 512), BF16),
        jax.ShapeDtypeStruct((n, 2048), BF16),
        jax.ShapeDtypeStruct((8, n), F32),
    )
    seg = pl.BlockSpec((tm // CMP_STRIDE, CMP_STRIDE * LANES), lambda i: (i, 0))
    out_specs = (row(512), row(LANES), row(LANES), seg, seg) + (row(LANES),) * 4 + (row(512),) * 4 + (
        row(2048), pl.BlockSpec((8, tm), lambda i: (0, i)))
    return pl.pallas_call(
        functools.partial(_proj_kernel, tiles_per_seq=seq_len // tm),
        grid=(n // tm,),
        in_specs=[row(D_MODEL), _const_spec((1, D_MODEL)), _const_spec((D_MODEL, _W_COLS)),
                  _const_spec((8, D_MODEL)), _const_spec((1, LANES)), _const_spec((2, LANES)),
                  _const_spec(conv_w.shape), _const_spec(conv_b.shape)],
        out_specs=out_specs,
        out_shape=out_shapes,
        scratch_shapes=[pltpu.VMEM((tm + 8, 1024), F32), pltpu.VMEM((2, tm, LANES), F32)],
        compiler_params=pltpu.CompilerParams(dimension_semantics=("arbitrary",),
                                             vmem_limit_bytes=VMEM_LIMIT),
        name="proj",
    )(x2d, g1, w_all, w_if, qg_pad, kg_pad, conv_w, conv_b)


def _compress_kernel(seg_ref, w1_ref, pos_ref, b1_ref, w2_ref, b2_ref, gain_ref, out_ref, *, normalize):
    seg = seg_ref[0]
    nseg, width = seg.shape
    c = math.sqrt(2.0 / math.pi)
    pos_lo = jnp.broadcast_to(pos_ref[0:1, :], (8, width))
    pos_hi = jnp.broadcast_to(pos_ref[1:2, :], (8, width))
    outs = []
    for g in range(NSA_GROUPS):
        a = _dot(seg, w1_ref[g, 0])
        b = _dot(seg, w1_ref[g, 1])
        posb = (_dot(pos_lo, w1_ref[g, 0]) + _dot(pos_hi, w1_ref[g, 1]))[0:1] + b1_ref[...]
        pre = a + pltpu.roll(b, nseg - 1, 0) + posb
        hid = 0.5 * pre * (1.0 + jnp.tanh(c * (pre + 0.044715 * (pre * pre * pre))))
        out = _dot(hid.astype(BF16), w2_ref[...]) + b2_ref[...]
        if normalize:
            ms = jnp.mean(out * out, axis=-1, keepdims=True)
            out = out * lax.rsqrt(ms + RMS_EPS) * gain_ref[...]
        outs.append(out)
    out_ref[0] = jnp.concatenate(outs, axis=1).astype(out_ref.dtype)


def _compress(segs, w1x, posx, b1, w2, b2, gain, normalize):
    bsz, nseg, width = segs.shape
    return pl.pallas_call(
        functools.partial(_compress_kernel, normalize=normalize),
        grid=(bsz,),
        in_specs=[pl.BlockSpec((1, nseg, width), lambda b: (b, 0, 0)),
                  _const_spec(w1x.shape), _const_spec(posx.shape), _const_spec(b1.shape),
                  _const_spec(w2.shape), _const_spec(b2.shape), _const_spec(gain.shape)],
        out_specs=pl.BlockSpec((1, nseg, LANES), lambda b: (b, 0, 0)),
        out_shape=jax.ShapeDtypeStruct((bsz, nseg, LANES), BF16),
        compiler_params=pltpu.CompilerParams(dimension_semantics=("parallel",),
                                             vmem_limit_bytes=VMEM_LIMIT),
        name="compress",
    )(segs, w1x, posx, b1, w2, b2, gain)


def _stack_heads(q_ref, lane_q):
    q = q_ref[0]
    return jnp.concatenate(
        [jnp.where((lane_q >= NSA_D) == (g == 1), q[:, LANES * j:LANES * (j + 1)], jnp.zeros((), BF16))
         for g in range(NSA_GROUPS) for j in range(NSA_HPG)], axis=0)


def _softmax_pv(s, v, online):
    if online:
        m = jnp.max(s, axis=1, keepdims=True)
        s = s - jnp.where(m < 0.1 * MASK_BIG, 0.0, m)
    p = jnp.exp2(s)
    acc = _dot(p.astype(BF16), jnp.concatenate([v, jnp.ones(v.shape, BF16)], axis=1))
    return p, acc[:, :LANES], acc[:, LANES:]


def _nsa_select_kernel(safe_ref, q_ref, kc_ref, vc_ref, gate_ref, bcf_ref, c2s_ref, shift_ref,
                       oc_ref, sbn_ref, sbf_ref, score_sc):
    i = pl.program_id(1)
    safe = safe_ref[0] != 0
    ncmp = kc_ref.shape[1]
    n_tiles = ncmp * CMP_STRIDE // SEL_TILE
    refs = (q_ref, kc_ref, vc_ref, gate_ref, bcf_ref, c2s_ref, shift_ref, oc_ref, sbn_ref, sbf_ref, score_sc)
    n_cls = max(c for c in (1, 2, 4) if n_tiles % c == 0 and (ncmp // c) % LANES == 0)
    for c in range(n_cls):
        @pl.when(safe & (i // (n_tiles // n_cls) == c))
        def _():
            _nsa_select_tile(i, ncmp * (c + 1) // n_cls, LANES * (c + 1) // n_cls, *refs, False)

    @pl.when(jnp.logical_not(safe))
    def _():
        _nsa_select_tile(i, ncmp, LANES, *refs, True)


def _nsa_select_tile(i, ncmp, nblk, q_ref, kc_ref, vc_ref, gate_ref, bcf_ref, c2s_ref, shift_ref,
                     oc_ref, sbn_ref, sbf_ref, score_sc, online):
    T = SEL_TILE
    grows = NSA_HPG * T
    lane_q = lax.broadcasted_iota(jnp.int32, (T, LANES), 1)
    qpad = _stack_heads(q_ref, lane_q)

    n_io = lax.broadcasted_iota(jnp.int32, (ncmp, LANES), 0)
    l_io = lax.broadcasted_iota(jnp.int32, (ncmp, LANES), 1)
    band = jnp.clip(n_io - i * (T // CMP_STRIDE) + BAND_OFF, 0, BAND_SLOTS - 1)
    oh_c = jnp.where(band == (l_io & 63), 1.0, 0.0).astype(BF16)
    kc_aug = jnp.concatenate([kc_ref[0, :ncmp, :], oh_c], axis=1)
    qc_aug = jnp.concatenate([qpad, bcf_ref[...]], axis=1)
    p_c, num_c, l_c = _softmax_pv(_nt(qc_aug, kc_aug), vc_ref[0, :ncmp, :], online)
    inv_c = 1.0 / jnp.maximum(l_c, 1e-30)
    o_c = num_c * inv_c
    p_c = p_c * jnp.concatenate([inv_c] * (ncmp // LANES), axis=1)

    gates = gate_ref[0]
    gated = [jnp.broadcast_to(gates[:, h:h + 1], (T, LANES)) * o_c[h * T:(h + 1) * T] for h in range(NSA_HEADS)]
    for j in range(NSA_HPG):
        oc_ref[0, :, LANES * j:LANES * (j + 1)] = jnp.where(
            lane_q < NSA_D, gated[j], gated[NSA_HPG + j]).astype(oc_ref.dtype)

    imp_t = []
    for g in range(NSA_GROUPS):
        ps = p_c[g * grows:g * grows + T]
        for hh in range(1, NSA_HPG):
            ps = ps + p_c[g * grows + hh * T:g * grows + (hh + 1) * T]
        ps_hi = ps.astype(BF16)
        ps_lo = (ps - ps_hi.astype(F32)).astype(BF16)
        c2s = c2s_ref[:ncmp, :]
        imp_t.append((_dot(ps_hi, c2s) + _dot(ps_lo, c2s)).T[:nblk])
    width = NSA_GROUPS * T
    j_io = lax.broadcasted_iota(jnp.int32, (nblk, width), 0)
    r_io = lax.broadcasted_iota(jnp.int32, (nblk, width), 1) & (T - 1)
    j_f = j_io.astype(F32)
    cur = i * (T // SLC_BLOCK) + jnp.right_shift(r_io, SLC_BLOCK.bit_length() - 1)
    forced = (j_io == 0) | (j_io == cur) | (j_io == cur - 1)
    n_pick = SLC_TOPK - 3
    start = jnp.where(forced, -jnp.inf, jnp.where(j_io <= cur, jnp.concatenate(imp_t, axis=1), -1.0 - j_f))

    def padded(score):
        if nblk == LANES:
            return score
        return jnp.concatenate([score, jnp.zeros((LANES - nblk, width), F32)], axis=0)

    score = start
    for _ in range(n_pick):
        score = jnp.where(score == jnp.max(score, axis=0, keepdims=True), -jnp.inf, score)
    score_sc[...] = padded(score)
    taken = jnp.sum(jnp.where((score == -jnp.inf) & jnp.logical_not(forced), 1.0, 0.0), axis=0, keepdims=True)

    @pl.when(jnp.max(taken) > n_pick + 0.5)
    def _():
        exact = start
        for _ in range(n_pick):
            best = jnp.max(exact, axis=0, keepdims=True)
            first = jnp.min(jnp.where(exact == best, j_f, float(nblk)), axis=0, keepdims=True)
            exact = jnp.where(j_f == first, -jnp.inf, exact)
        score_sc[...] = padded(exact)

    score = score_sc[...]
    near_start = ((i * T + lax.broadcasted_iota(jnp.int32, (T, LANES), 0)) // NSA_TILE) * (NSA_TILE // SLC_BLOCK)
    near_blk = lane_q >= near_start - SLC_PAD // SLC_BLOCK
    far_pick = shift_ref[0:1, 0:1]
    for g in range(NSA_GROUPS):
        picked = score[:, g * T:(g + 1) * T].T == -jnp.inf
        sbn_ref[0, :, LANES * g:LANES * (g + 1)] = jnp.where(picked & near_blk, 0.0, MASK_BIG).astype(BF16)
        sbf_ref[0, :, LANES * g:LANES * (g + 1)] = jnp.where(
            picked & jnp.logical_not(near_blk), far_pick, MASK_BIG).astype(BF16)


def _nsa_attend_kernel(safe_ref, *refs):
    *io_refs, acc_sc, m_sc, pa_sc, pb_sc = refs

    @pl.when(safe_ref[0] != 0)
    def _():
        _nsa_attend_tile(*io_refs, acc_sc, pa_sc, pb_sc, online=False)

    @pl.when(safe_ref[0] == 0)
    def _():
        _nsa_attend_tile(*io_refs, acc_sc, m_sc, online=True)


def _nsa_attend_tile(q_ref, ks_ref, vs_ref, kw_ref, vw_ref, oh_ref, gate_ref, oc_ref, sbn_ref, sbf_ref,
                     bs_ref, bw_ref, out_ref, acc_sc, *extra_sc, online):
    i = pl.program_id(1)
    t0 = pl.multiple_of(i * NSA_TILE, NSA_TILE)
    T = NSA_TILE
    rows = NSA_HEADS * T
    grows = NSA_HPG * T
    lane_q = lax.broadcasted_iota(jnp.int32, (T, LANES), 1)
    qpad = _stack_heads(q_ref, lane_q)

    def with_features(feat_ref):
        feats = [jnp.concatenate([feat_ref[0, :, LANES * g:LANES * (g + 1)]] * NSA_HPG, axis=0)
                 for g in range(NSA_GROUPS)]
        return jnp.concatenate([qpad, jnp.concatenate(feats, axis=0)], axis=1)

    q_near, q_far = with_features(sbn_ref), with_features(sbf_ref)

    r_w = lax.broadcasted_iota(jnp.int32, (WIN_KEYS, LANES), 0)
    pad_flag = jnp.where(r_w + (t0 - WINDOW) < 0, 1.0, 0.0).astype(BF16)
    kw_aug = jnp.concatenate([kw_ref[0, pl.ds(t0, WIN_KEYS), :], pad_flag], axis=1)
    qw_aug = jnp.concatenate([qpad, jnp.full((rows, LANES), MASK_BIG / LANES, BF16)], axis=1)
    s_w = _nt(qw_aug, kw_aug) + bw_ref[...]
    _, num_w, l_w = _softmax_pv(s_w, vw_ref[0, pl.ds(t0, WIN_KEYS), :], online)
    o_w = num_w * (1.0 / l_w)

    gates = gate_ref[0]
    gate_tile = lambda col: jnp.broadcast_to(gates[:, col:col + 1], (T, LANES))
    head_rows = [slice(h * T, (h + 1) * T) for h in range(NSA_HEADS)]
    o_gw = [gate_tile(16 + h) * o_w[head_rows[h]] for h in range(NSA_HEADS)]
    g_slc = [gate_tile(8 + h) for h in range(NSA_HEADS)]

    ones_f = jnp.ones((FAR_CHUNK, LANES), BF16)
    k_near = jnp.concatenate([ks_ref[0, pl.ds(t0, NEAR_KEYS), :], oh_ref[pl.ds(t0, NEAR_KEYS), :]], axis=1)
    v_near = jnp.concatenate([vs_ref[0, pl.ds(t0, NEAR_KEYS), :], jnp.ones((NEAR_KEYS, LANES), BF16)], axis=1)
    s_n = _nt(q_near, k_near) + bs_ref[...]
    if online:
        m_sc, = extra_sc
        m_n = jnp.max(s_n, axis=1, keepdims=True)
        m_sc[...] = m_n
        s_n = s_n - m_n
    acc_sc[...] = _dot(jnp.exp2(s_n).astype(BF16), v_near)

    last_chunk = (ks_ref.shape[1] - SLC_PAD) // FAR_CHUNK - 1

    def far_start(c):
        return pl.multiple_of(SLC_PAD + FAR_CHUNK * jnp.minimum(c, last_chunk), LANES)

    def far_scores(c, by_group=False):
        start = far_start(c)
        k_f = jnp.concatenate([ks_ref[0, pl.ds(start, FAR_CHUNK), :], oh_ref[pl.ds(start, FAR_CHUNK), :]], axis=1)
        if by_group:
            return jnp.concatenate([_nt(q_far[:grows], k_f), _nt(q_far[grows:], k_f)], axis=0)
        return _nt(q_far, k_f)

    def far_values(c):
        return jnp.concatenate([vs_ref[0, pl.ds(far_start(c), FAR_CHUNK), :], ones_f], axis=1)

    n_far = (t0 + (FAR_CHUNK - 1 - SLC_PAD)) // FAR_CHUNK
    if online:
        def far_online(c, carry):
            s_f = far_scores(c)
            m_old = m_sc[...]
            m_new = jnp.maximum(m_old, jnp.max(s_f, axis=1, keepdims=True))
            m_sc[...] = m_new
            acc_sc[...] = (jnp.exp2(m_old - m_new) * acc_sc[...]
                           + _dot(jnp.exp2(s_f - m_new).astype(BF16), far_values(c)))
            return carry

        lax.fori_loop(0, n_far, far_online, 0)
    else:
        pa_sc, pb_sc = extra_sc
        pa_sc[...] = jnp.exp2(far_scores(0, by_group=True)).astype(BF16)

        def far_pair(cc, carry):
            c0 = 2 * cc
            pv0 = _dot(pa_sc[...], far_values(c0))
            pb_sc[...] = jnp.exp2(far_scores(c0 + 1)).astype(BF16)
            pv1 = _dot(pb_sc[...], far_values(c0 + 1))
            pa_sc[...] = jnp.exp2(far_scores(c0 + 2)).astype(BF16)
            acc_sc[...] += pv0 + pv1
            return carry

        lax.fori_loop(0, n_far // 2, far_pair, 0)

        @pl.when(n_far % 2 == 1)
        def _():
            v_f = far_values(n_far - 1)
            acc_sc[0:grows, :] += _dot(pa_sc[0:grows, :], v_f)
            acc_sc[grows:, :] += _dot(pa_sc[grows:, :], v_f)

    acc = acc_sc[...]
    o_s = acc[:, :LANES] * (1.0 / acc[:, LANES:])

    head_out = [o_gw[h] + g_slc[h] * o_s[head_rows[h]] for h in range(NSA_HEADS)]
    for j in range(NSA_HPG):
        cols = slice(LANES * j, LANES * (j + 1))
        out_ref[0, :, cols] = (oc_ref[0, :, cols].astype(F32) + jnp.where(
            lane_q < NSA_D, head_out[j], head_out[NSA_HPG + j])).astype(out_ref.dtype)


def _nsa(safe, q, ks, vs, kw, vw, kc, vc, oh, gates, bcf, bs, bw, c2s, shift):
    bsz, s = q.shape[:2]
    flag = pl.BlockSpec(memory_space=pltpu.SMEM)
    T = NSA_TILE
    rows = NSA_HEADS * T
    grid = (bsz, s // T)
    params = pltpu.CompilerParams(dimension_semantics=("parallel", "arbitrary"), vmem_limit_bytes=VMEM_LIMIT)
    tile = lambda width: pl.BlockSpec((1, T, width), lambda b, i: (b, i, 0))
    per_b = lambda a: pl.BlockSpec((1,) + a.shape[1:], lambda b, i: (b, 0, 0), pipeline_mode=pl.Buffered(1))
    sel_tile = lambda width: pl.BlockSpec((1, SEL_TILE, width), lambda b, i: (b, i, 0))
    o_cmp, sb_near, sb_far = pl.pallas_call(
        _nsa_select_kernel,
        grid=(bsz, s // SEL_TILE),
        in_specs=[flag, sel_tile(NSA_HPG * LANES), per_b(kc), per_b(vc), sel_tile(LANES),
                  _const_spec(bcf.shape), _const_spec(c2s.shape), _const_spec(shift.shape)],
        out_specs=(sel_tile(NSA_HPG * LANES), sel_tile(NSA_GROUPS * LANES), sel_tile(NSA_GROUPS * LANES)),
        out_shape=(jax.ShapeDtypeStruct((bsz, s, NSA_HPG * LANES), BF16),
                   jax.ShapeDtypeStruct((bsz, s, NSA_GROUPS * LANES), BF16),
                   jax.ShapeDtypeStruct((bsz, s, NSA_GROUPS * LANES), BF16)),
        scratch_shapes=[pltpu.VMEM((LANES, NSA_GROUPS * SEL_TILE), F32)],
        compiler_params=params,
        name="nsa_select",
    )(safe, q, kc, vc, gates, bcf, c2s, shift)
    scratch = [pltpu.VMEM((rows, 2 * LANES), F32),
               pltpu.VMEM((rows, 1), F32),
               pltpu.VMEM((rows, FAR_CHUNK), BF16),
               pltpu.VMEM((rows, FAR_CHUNK), BF16)]
    return pl.pallas_call(
        _nsa_attend_kernel,
        grid=grid,
        in_specs=[flag, tile(NSA_HPG * LANES), per_b(ks), per_b(vs), per_b(kw), per_b(vw), _const_spec(oh.shape),
                  tile(LANES), tile(NSA_HPG * LANES), tile(NSA_GROUPS * LANES), tile(NSA_GROUPS * LANES),
                  _const_spec(bs.shape), _const_spec(bw.shape)],
        out_specs=tile(NSA_HPG * LANES),
        out_shape=jax.ShapeDtypeStruct((bsz, s, NSA_HPG * LANES), BF16),
        scratch_shapes=scratch,
        compiler_params=params,
        name="nsa_attend",
    )(safe, q, ks, vs, kw, vw, oh, gates, o_cmp, sb_near, sb_far, bs, bw)


ML_ROWS = 512


def _mlstm_kernel(q_ref, k_ref, v_ref, o_ref, gif_ref, gcol_ref, gb_ref, gbrow_ref, tril_ref, triu_ref,
                  out_ref, c_sc, m_sc):
    j = pl.program_id(1)
    R = ML_ROWS
    L = ML_CHUNK

    @pl.when(j == 0)
    def _():
        c_sc[...] = jnp.zeros_like(c_sc)
        m_sc[...] = jnp.zeros_like(m_sc)

    og = o_ref[0]

    pre = gif_ref[...] + gb_ref[...]
    row8 = lax.broadcasted_iota(jnp.int32, (8, R), 0)
    logf = jnp.minimum(pre, 0.0) - jnp.log(1.0 + jnp.exp(-jnp.abs(pre)))
    g8 = jnp.where(row8 < ML_HEADS, pre, logf)
    pre_c = gcol_ref[0] + gbrow_ref[...]
    lane_c = lax.broadcasted_iota(jnp.int32, pre_c.shape, 1)
    logf_c = jnp.minimum(pre_c, 0.0) - jnp.log(1.0 + jnp.exp(-jnp.abs(pre_c)))
    g_c = jnp.where(lane_c < _IF_LANE + ML_HEADS, pre_c, logf_c)
    cum_c = _split3(g_c, lambda part: _dot(tril_ref[...], part))
    cum_row = _split3(g8, lambda part: _dot(part, triu_ref[...]))

    a_io = lax.broadcasted_iota(jnp.int32, (L, L), 0)
    b_io = lax.broadcasted_iota(jnp.int32, (L, L), 1)
    causal = (b_io <= a_io)[None]
    nc = R // L
    pairs = [(c, h) for c in range(nc) for h in range(ML_HEADS)]

    def blocks(a):
        return jnp.stack([a[c * L:(c + 1) * L, h * ML_D:(h + 1) * ML_D] for c, h in pairs])

    def cols(a, k0):
        return jnp.stack([jnp.broadcast_to(a[c * L:(c + 1) * L, k0 + h:k0 + h + 1], (L, LANES)) for c, h in pairs])

    def rows(a, k0):
        return jnp.stack([a[k0 + h:k0 + h + 1, c * L:(c + 1) * L] for c, h in pairs])

    bdot = lambda eq, x, y: jnp.einsum(eq, x, y, preferred_element_type=F32)
    qb, kb, vb = blocks(q_ref[0]), blocks(k_ref[0]), blocks(v_ref[0])
    b_col, li_col = cols(cum_c, _IF_LANE + ML_HEADS), cols(g_c, _IF_LANE)
    b_row, li_row = rows(cum_row, ML_HEADS), rows(g8, 0)
    gsum = b_row[:, :, L - 1:L]

    s_max = jnp.max(gsum - b_row + li_row, axis=2, keepdims=True)
    m_run = m_sc[:, 0:1, 0:1]
    m_ins, m_outs = [], []
    for c in range(nc):
        hs = slice(c * ML_HEADS, (c + 1) * ML_HEADS)
        m_ins.append(m_run)
        m_run = jnp.maximum(gsum[hs] + m_run, s_max[hs])
        m_outs.append(m_run)
    m_sc[...] = jnp.broadcast_to(m_run, m_sc.shape)
    m_in, m_out = jnp.concatenate(m_ins, axis=0), jnp.concatenate(m_outs, axis=0)

    log_d = jnp.where(causal, b_col[:, :, :L] - b_row + li_row, -jnp.inf)
    inter = b_col + m_in
    m_row = jnp.maximum(inter, jnp.broadcast_to(jnp.max(log_d, axis=2, keepdims=True), inter.shape))
    w = bdot('bik,bjk->bij', qb, kb) * jnp.exp(log_d - m_row[:, :, :L])
    v_aug = jnp.concatenate([vb, jnp.ones(vb.shape, BF16)], axis=2)
    wv = bdot('bij,bjd->bid', w.astype(BF16), v_aug)
    inter_scale = jnp.exp(inter - m_row)
    inter_scale = jnp.concatenate([inter_scale, inter_scale], axis=2)
    floor = jnp.exp(-m_row)
    k_src = (kb.astype(F32) * jnp.exp(gsum - b_col + li_col - m_out)).astype(BF16)
    upd = bdot('bjk,bjd->bkd', k_src, v_aug)
    decay = jnp.exp(gsum + m_in - m_out)

    c_aug = c_sc[...]
    for c in range(nc):
        hs = slice(c * ML_HEADS, (c + 1) * ML_HEADS)
        nd = inter_scale[hs] * bdot('hik,hkd->hid', qb[hs], c_aug.astype(BF16)) + wv[hs]
        hval = nd[:, :, :ML_D] / jnp.maximum(jnp.abs(nd[:, :, ML_D:]), floor[hs])
        for h in range(ML_HEADS):
            rs, cs = slice(c * L, (c + 1) * L), slice(h * ML_D, (h + 1) * ML_D)
            out_ref[0, rs, cs] = (og[rs, cs].astype(F32) * hval[h]).astype(out_ref.dtype)
        c_aug = decay[hs] * c_aug + upd[hs]
    c_sc[...] = c_aug


def _mlstm(mq, mk, mv, og, gif, gcol, gate_b, gate_b_row):
    bsz, s, width = mq.shape
    R = ML_ROWS
    nblk = s // R
    seq = lambda: pl.BlockSpec((1, R, width), lambda b, j: (b, j, 0))
    pos = np.arange(R)
    same_chunk = (pos[:, None] // ML_CHUNK) == (pos[None, :] // ML_CHUNK)
    tril = (same_chunk & (pos[None, :] <= pos[:, None])).astype(np.float32)
    return pl.pallas_call(
        _mlstm_kernel,
        grid=(bsz, nblk),
        in_specs=[seq(), seq(), seq(), seq(),
                  pl.BlockSpec((8, R), lambda b, j: (0, b * nblk + j)),
                  pl.BlockSpec((1, R, LANES), lambda b, j: (b, j, 0)),
                  _const_spec(gate_b.shape), _const_spec(gate_b_row.shape),
                  _const_spec((R, R)), _const_spec((R, R))],
        out_specs=seq(),
        out_shape=jax.ShapeDtypeStruct((bsz, s, width), BF16),
        scratch_shapes=[pltpu.VMEM((ML_HEADS, ML_D, 2 * ML_D), F32),
                        pltpu.VMEM((ML_HEADS, 8, LANES), F32)],
        compiler_params=pltpu.CompilerParams(dimension_semantics=("parallel", "arbitrary"),
                                             vmem_limit_bytes=VMEM_LIMIT),
        name="mlstm",
    )(mq, mk, mv, og, gif, gcol, gate_b, gate_b_row, jnp.asarray(tril, BF16), jnp.asarray(tril.T, BF16))


FF_CHUNK = 512


def _merge_ffn_kernel(x_ref, ya_ref, yb_ref, mg_ref, wa_ref, wb_ref, wo_ref, g2_ref, w1_ref, w2_ref,
                      out_ref):
    mg = mg_ref[...]
    mixed = (mg[:, :D_MODEL].astype(F32) * _dot(ya_ref[...], wa_ref[...])
             + mg[:, D_MODEL:].astype(F32) * _dot(yb_ref[...], wb_ref[...]))
    x1 = x_ref[...] + _dot(mixed.astype(BF16), wo_ref[...])
    h2 = x1 * lax.rsqrt(jnp.mean(x1 * x1, axis=-1, keepdims=True) + RMS_EPS) * g2_ref[...]
    h2 = h2.astype(BF16)
    acc = x1
    for c in range(D_FF // FF_CHUNK):
        a = jnp.maximum(_dot(h2, w1_ref[:, c * FF_CHUNK:(c + 1) * FF_CHUNK]), 0.0)
        acc = acc + _dot((a * a).astype(BF16), w2_ref[c * FF_CHUNK:(c + 1) * FF_CHUNK, :])
    out_ref[...] = acc


def _merge_ffn(x2d, ya, yb, mg, wa, wb, wo, g2, w1, w2, tm=512):
    n = x2d.shape[0]
    row = lambda w: pl.BlockSpec((tm, w), lambda i: (i, 0))
    return pl.pallas_call(
        _merge_ffn_kernel,
        grid=(n // tm,),
        in_specs=[row(D_MODEL), row(512), row(512), row(2048),
                  _const_spec(wa.shape), _const_spec(wb.shape), _const_spec(wo.shape),
                  _const_spec(g2.shape), _const_spec(w1.shape), _const_spec(w2.shape)],
        out_specs=row(D_MODEL),
        out_shape=jax.ShapeDtypeStruct((n, D_MODEL), F32),
        compiler_params=pltpu.CompilerParams(dimension_semantics=("parallel",),
                                             vmem_limit_bytes=VMEM_LIMIT),
        name="merge_ffn",
    )(x2d, ya, yb, mg, wa, wb, wo, g2, w1, w2)


def _proj_weights(w):
    widths = (512, 128, 128, 128, 128, 128, 128, 24, 512, 512, 512, 4, 4, 512, 2048)
    off = np.concatenate([[0], np.cumsum(widths)])
    (nq, nkc, nvc, nks, nvs, nkw, nvw, ngate, mq, mk, mv, mi, mf, mo, mgate) = (int(o) for o in off[:-1])
    col = lambda start, width: w[:, start:start + width]
    parts = [col(nq + NSA_D * (g * NSA_HPG + j), NSA_D) for j in range(NSA_HPG) for g in range(NSA_GROUPS)]
    parts += [col(nks, 128), col(nkw, 128), col(nkc, 128), col(nvc, 128), col(nvs, 128), col(nvw, 128)]
    gate = col(ngate, 24).reshape(-1, NSA_HEADS, 3).transpose(0, 2, 1).reshape(-1, 24)
    parts += [gate, col(mi, 4), col(mf, 4), jnp.zeros((w.shape[0], LANES - _IF_LANE - 8), w.dtype)]
    parts += [col(mq, 512), col(mk, 512), col(mv, 512), col(mo, 512), col(mgate, 2048)]
    w_all = jnp.concatenate(parts, axis=1).astype(BF16)
    w_if = jnp.concatenate([col(mi, 4), col(mf, 4)], axis=1).T.astype(BF16)
    return w_all, w_if


def _branch_a_weights(w):
    head = lambda h: w[NSA_D * h:NSA_D * (h + 1)]
    return jnp.concatenate([head(g * NSA_HPG + j) for j in range(NSA_HPG) for g in range(NSA_GROUPS)],
                           axis=0).astype(BF16)


def _compress_weights(pos, w1):
    r = w1.reshape(2, CMP_STRIDE, 1, NSA_D, CMP_HIDDEN)
    z = jnp.zeros_like(r)
    w1x = jnp.stack([jnp.concatenate([r, z], axis=2), jnp.concatenate([z, r], axis=2)])
    w1x = w1x.reshape(NSA_GROUPS, 2, CMP_STRIDE * NSA_GROUPS * NSA_D, CMP_HIDDEN).astype(BF16)
    posx = jnp.broadcast_to(pos.reshape(2, CMP_STRIDE, 1, NSA_D), (2, CMP_STRIDE, NSA_GROUPS, NSA_D))
    return w1x, posx.reshape(2, CMP_STRIDE * NSA_GROUPS * NSA_D).astype(BF16)


def _t5_bucket(dist):
    n = np.maximum(dist, 0)
    max_exact = REL_BUCKETS // 2
    nf = np.maximum(n, 1).astype(np.float32)
    large = max_exact + (np.log(nf / np.float32(max_exact)) / np.float32(math.log(REL_MAX_DIST / max_exact))
                         * np.float32(REL_BUCKETS - max_exact)).astype(np.int32)
    return np.where(n < max_exact, n, np.minimum(large, REL_BUCKETS - 1))


def _toeplitz(rel, n_rows, n_cols, stride, off, inner=1):
    a_rows = n_rows // inner
    lw = a_rows + n_cols
    k = np.arange(lw)[:, None]
    dist = stride * np.where(k < n_cols, -k, lw - k) + np.arange(inner)[None, :] + off
    live = (dist >= 0) & (k != n_cols)
    onehot = live[..., None] & (_t5_bucket(dist)[..., None] == np.arange(REL_BUCKETS))
    w = jnp.dot(jnp.asarray(onehot.reshape(lw * inner, REL_BUCKETS), F32), rel.T,
                precision=lax.Precision.HIGHEST)
    w = w.T.reshape(-1, lw, inner)
    flat = jnp.tile(w, (1, a_rows, 1))[:, :a_rows * (lw - 1)]
    out = flat.reshape(-1, a_rows, lw - 1, inner)[:, :, :n_cols]
    return out.transpose(0, 1, 3, 2).reshape(-1, n_rows, n_cols)


def _rel_bias(rel_table):
    return (rel_table - rel_table[REL_BUCKETS - 1][None, :]).T * LOG2E


def _bias_tables(rel, shifts):
    T = NSA_TILE
    sh_c, sh_s, sh_w = shifts
    r = np.arange(T)[:, None]

    c = np.arange(NEAR_KEYS)[None, :]
    ok = jnp.asarray((r - c + SLC_PAD) >= 0)[None]
    bs = jnp.where(ok, _toeplitz(rel, T, NEAR_KEYS, 1, SLC_PAD) - sh_s, MASK_F32)
    c = np.arange(WIN_KEYS)[None, :]
    d = r - c + WINDOW
    ok = jnp.asarray((d >= 0) & (d < WINDOW))[None]
    bw = jnp.where(ok, _toeplitz(rel, T, WIN_KEYS, 1, WINDOW) - sh_w, MASK_F32)
    r = np.arange(SEL_TILE)[:, None]
    m = np.arange(64)[None, :]
    d = r - CMP_STRIDE * m + BAND_DIST0
    band = (m >= 1) & (m < BAND_SLOTS - 1)
    vals = _toeplitz(rel, SEL_TILE, BAND_SLOTS, CMP_STRIDE, BAND_DIST0, inner=CMP_STRIDE)
    vals = jnp.pad(vals, ((0, 0), (0, 0), (0, 64 - BAND_SLOTS)))
    vals = jnp.where(jnp.asarray(band & (d >= 0))[None], vals, 0.0) - sh_c
    dead = (band & (d < 0)) | (m == BAND_SLOTS - 1)
    vals = jnp.where(jnp.asarray(dead)[None], MASK_BIG, vals)
    hi = vals.astype(BF16)
    lo = (vals - hi.astype(F32)).astype(BF16)
    bcf = jnp.concatenate([hi, lo], axis=-1)
    flat = lambda a: a.reshape(-1, a.shape[-1])
    return flat(bcf), flat(bs).astype(F32), flat(bw).astype(F32)


def _layer(l, x2d, bsz, s, consts, rel_table, norm1_g, w_in, nsa_q_gain, nsa_k_gain, cmp_k, cmp_v,
           ml_conv_w, ml_conv_b, ml_i_bias, ml_f_bias, w_branch_a, w_branch_b, w_out, norm2_g, w_ff1, w_ff2):
    n = bsz * s
    nseg = s // CMP_STRIDE
    oh, c2s = consts
    w_all, w_if = _proj_weights(w_in[l])
    qg_pad = jnp.concatenate([nsa_q_gain[l]] * 2)[None, :]
    kg_pad = jnp.stack([jnp.concatenate([nsa_k_gain[l, 1]] * 2), jnp.concatenate([nsa_k_gain[l, 2]] * 2)])
    (q, ks, kw, kc, vc, vs, vw, gates, gcol, mq, mk, mv, og, mg, gif) = _proj(
        x2d, s, norm1_g[l][None, :], w_all, w_if, qg_pad, kg_pad, ml_conv_w[l], ml_conv_b[l][None, :])

    def compress(a, params, gain, normalize):
        pos, w1, b1, w2, b2 = (p[l] for p in params)
        w1x, posx = _compress_weights(pos, w1)
        return _compress(a.reshape(bsz, nseg, CMP_STRIDE * LANES), w1x, posx, b1[None, :],
                         w2.astype(BF16), b2[None, :], gain[None, :], normalize)

    kcmp = compress(kc, cmp_k, nsa_k_gain[l, 0], True)
    vcmp = compress(vc, cmp_v, jnp.ones((NSA_D,), F32), False)

    qk_bound = lambda kg: 8.0 * LOG2E * jnp.max(jnp.abs(nsa_q_gain[l])) * jnp.max(jnp.abs(kg))
    tab = _rel_bias(rel_table)
    snap = lambda v: v.astype(BF16).astype(F32)
    shifts = [snap(qk_bound(nsa_k_gain[l, j]) + jnp.maximum(jnp.max(tab), 0.0)) for j in range(3)]
    bias_range = jnp.maximum(jnp.max(tab), 0.0) - jnp.minimum(jnp.min(tab), 0.0)
    safe = 2.0 * jnp.max(jnp.stack([qk_bound(nsa_k_gain[l, j]) for j in range(3)])) + bias_range < SAFE_SHIFT_LOG2
    shifts = [jnp.where(safe, sh, 0.0) for sh in shifts]
    bcf, bs, bw = _bias_tables(tab, shifts)

    seq = lambda a: a.reshape(bsz, s, a.shape[-1])
    front = lambda a, p: jnp.pad(seq(a), ((0, 0), (p, 0), (0, 0)))
    operands = (seq(q), front(ks, SLC_PAD), front(vs, SLC_PAD), front(kw, WINDOW), front(vw, WINDOW),
                kcmp, vcmp, oh, seq(gates), bcf, bs, bw, c2s, jnp.full((1, LANES), -shifts[1], F32))
    y_a = _nsa(safe.astype(jnp.int32)[None], *operands)

    gate_b = jnp.concatenate([ml_i_bias[l], ml_f_bias[l]])
    gate_b_row = jnp.pad(gate_b, (_IF_LANE, LANES - _IF_LANE - 8))[None, :]
    y_b = _mlstm(seq(mq), seq(mk), seq(mv), seq(og), gif, seq(gcol), gate_b[:, None], gate_b_row)

    out = _merge_ffn(x2d, y_a.reshape(n, 512), y_b.reshape(n, 512), mg, _branch_a_weights(w_branch_a[l]),
                     w_branch_b[l].astype(BF16), w_out[l].astype(BF16), norm2_g[l][None, :],
                     w_ff1[l].astype(BF16), w_ff2[l].astype(BF16))
    return out, y_a, y_b


def _consts(s):
    nseg = s // CMP_STRIDE
    nsel = s // SLC_BLOCK
    blk_of_key = np.arange(s) // SLC_BLOCK
    oh = np.concatenate([np.ones((SLC_PAD, LANES), np.float32),
                         (blk_of_key[:, None] == np.arange(LANES)[None, :]).astype(np.float32)], axis=0)
    ci = np.arange(nseg)[:, None] * CMP_STRIDE
    sj = np.arange(LANES)[None, :] * SLC_BLOCK
    c2s = ((ci < sj + SLC_BLOCK) & (ci + CMP_LEN > sj) & (np.arange(LANES)[None, :] < nsel)
           & (np.arange(nseg)[:, None] < nseg - 1))
    return jnp.asarray(oh, BF16), jnp.asarray(c2s.astype(np.float32), BF16)


def kernel(x, norm1_g, w_in, nsa_q_gain, nsa_k_gain, cmp_k_pos, cmp_k_w1, cmp_k_b1, cmp_k_w2, cmp_k_b2, cmp_v_pos, cmp_v_w1, cmp_v_b1, cmp_v_w2, cmp_v_b2, rel_table, ml_conv_w, ml_conv_b, ml_i_bias, ml_f_bias, w_branch_a, w_branch_b, w_out, norm2_g, w_ff1, w_ff2):
    bsz, s, _ = x.shape
    consts = _consts(s)
    x2d = x.reshape(bsz * s, D_MODEL)
    for l in range(norm1_g.shape[0]):
        x2d, _, _ = _layer(l, x2d, bsz, s, consts, rel_table, norm1_g, w_in, nsa_q_gain, nsa_k_gain,
                           (cmp_k_pos, cmp_k_w1, cmp_k_b1, cmp_k_w2, cmp_k_b2),
                           (cmp_v_pos, cmp_v_w1, cmp_v_b1, cmp_v_w2, cmp_v_b2),
                           ml_conv_w, ml_conv_b, ml_i_bias, ml_f_bias,
                           w_branch_a, w_branch_b, w_out, norm2_g, w_ff1, w_ff2)
    return x2d.reshape(bsz, s, D_MODEL)
```

```python
import functools
import math

import numpy as np
import jax
import jax.numpy as jnp
from jax import lax
from jax.experimental import pallas as pl
from jax.experimental.pallas import tpu as pltpu

F32 = jnp.float32
BF16 = jnp.bfloat16

D_MODEL = 1024
NSA_HEADS = 8
NSA_GROUPS = 2
NSA_HPG = NSA_HEADS // NSA_GROUPS
NSA_D = 64
CMP_LEN = 32
CMP_STRIDE = 16
CMP_HIDDEN = 256
SLC_BLOCK = 64
SLC_TOPK = 16
WINDOW = 512
ML_HEADS = 4
ML_D = 128
ML_CHUNK = 64
CONV_WIDTH = 4
D_FF = 4 * D_MODEL
REL_BUCKETS = 32
REL_MAX_DIST = 128
RMS_EPS = 1e-6

LANES = 128
NSA_TILE = 256
SEL_TILE = 512
SLC_PAD = 256
FAR_CHUNK = 512
NEAR_KEYS = SLC_PAD + NSA_TILE
WIN_KEYS = WINDOW + NSA_TILE
BAND_OFF = 10
BAND_SLOTS = 2 + (SEL_TILE + 128) // CMP_STRIDE
BAND_DIST0 = CMP_STRIDE * BAND_OFF - (CMP_LEN - 1)
MASK_BIG = -1e9
MASK_F32 = -1e30
LOG2E = math.log2(math.e)
SAFE_SHIFT_LOG2 = 50.0
VMEM_LIMIT = 56 * 1024 * 1024

_Q_OFF, _KS_OFF, _KC_OFF, _VS_OFF = 0, 512, 768, 1024
_GATE_OFF, _MQ_OFF, _MK_OFF, _MV_OFF, _MO_OFF, _MG_OFF, _W_COLS = 1280, 1408, 1920, 2432, 2944, 3456, 5504
_IF_LANE = 24


def _nt(a, b):
    return lax.dot_general(a, b, (((1,), (1,)), ((), ())), preferred_element_type=F32)


def _dot(a, b):
    return jnp.dot(a, b, preferred_element_type=F32)


def _split3(x, dot_part):
    hi = x.astype(BF16)
    r1 = x - hi.astype(F32)
    mid = r1.astype(BF16)
    lo = (r1 - mid.astype(F32)).astype(BF16)
    return dot_part(hi) + dot_part(mid) + dot_part(lo)


def _const_spec(shape):
    nd = len(shape)
    return pl.BlockSpec(shape, lambda *_: (0,) * nd, pipeline_mode=pl.Buffered(1))


def _proj_kernel(x_ref, g1_ref, w_ref, wif_ref, qg_ref, kg_ref, cw_ref, cb_ref,
                 q_ref, ks_ref, kw_ref, kc_ref, vc_ref, vs_ref, vw_ref, gate_ref, gcol_ref,
                 mq_ref, mk_ref, mv_ref, mo_ref, mg_ref, gif_ref, ext_sc, seg_sc, *, tiles_per_seq):
    tm = x_ref.shape[0]
    x = x_ref[...]
    h = x * lax.rsqrt(jnp.mean(x * x, axis=-1, keepdims=True) + RMS_EPS) * g1_ref[...]
    hb = h.astype(BF16)

    def proj(off, width):
        return _dot(hb, w_ref[:, off:off + width])

    lane = lax.broadcasted_iota(jnp.int32, (1, LANES), 1)
    low = lane < NSA_D

    def half_norm(blk):
        sq = blk * blk
        ms0 = jnp.sum(jnp.where(low, sq, 0.0), axis=-1, keepdims=True) * (1.0 / NSA_D)
        ms1 = jnp.sum(jnp.where(low, 0.0, sq), axis=-1, keepdims=True) * (1.0 / NSA_D)
        return blk * jnp.where(low, lax.rsqrt(ms0 + RMS_EPS), lax.rsqrt(ms1 + RMS_EPS))

    for pair in range(NSA_HPG // 2):
        both = proj(_Q_OFF + 2 * LANES * pair, 2 * LANES)
        for hh in range(2):
            qn = half_norm(both[:, LANES * hh:LANES * (hh + 1)]) * qg_ref[...] * (NSA_D ** -0.5 * LOG2E)
            col = LANES * (2 * pair + hh)
            q_ref[:, col:col + LANES] = qn.astype(q_ref.dtype)

    both = proj(_KS_OFF, 2 * LANES)
    for hh, ref in enumerate((ks_ref, kw_ref)):
        ref[...] = (half_norm(both[:, LANES * hh:LANES * (hh + 1)]) * kg_ref[hh:hh + 1, :]).astype(ref.dtype)

    both = proj(_VS_OFF, 2 * LANES)
    vs_ref[...] = both[:, :LANES].astype(vs_ref.dtype)
    vw_ref[...] = both[:, LANES:].astype(vw_ref.dtype)

    both = proj(_KC_OFF, 2 * LANES)
    for hh, ref in enumerate((kc_ref, vc_ref)):
        seg_sc[hh] = both[:, LANES * hh:LANES * (hh + 1)]
        for tok in range(CMP_STRIDE):
            ref[:, LANES * tok:LANES * (tok + 1)] = seg_sc[hh, pl.ds(tok, tm // CMP_STRIDE, stride=CMP_STRIDE),
                                                           :].astype(ref.dtype)

    slab = proj(_GATE_OFF, LANES)
    gate_ref[...] = jax.nn.sigmoid(slab)
    gcol_ref[...] = slab

    @pl.when(pl.program_id(0) % tiles_per_seq == 0)
    def _():
        ext_sc[0:8, :] = jnp.zeros((8, ext_sc.shape[1]), F32)

    ext_sc[8:, 0:512] = proj(_MQ_OFF, 512)
    ext_sc[8:, 512:1024] = proj(_MK_OFF, 512)
    conv = cb_ref[...]
    for t in range(CONV_WIDTH):
        lo = 8 - (CONV_WIDTH - 1) + t
        conv = conv + ext_sc[lo:lo + tm, :] * cw_ref[t:t + 1, :]
    ext_sc[0:8, :] = ext_sc[tm:tm + 8, :]
    qk = conv * jax.nn.sigmoid(conv)
    mq_ref[...] = (qk[:, :512] * (ML_D ** -0.5)).astype(mq_ref.dtype)
    mk_ref[...] = qk[:, 512:].astype(mk_ref.dtype)

    mv_ref[...] = proj(_MV_OFF, 512).astype(mv_ref.dtype)
    mo_ref[...] = jax.nn.sigmoid(proj(_MO_OFF, 512)).astype(mo_ref.dtype)
    for c in range(4):
        mg_ref[:, 512 * c:512 * (c + 1)] = jax.nn.sigmoid(proj(_MG_OFF + 512 * c, 512)).astype(mg_ref.dtype)
    gif_ref[...] = _nt(wif_ref[...], hb)


def _proj(x2d, seq_len, g1, w_all, w_if, qg_pad, kg_pad, conv_w, conv_b, tm=512):
    n = x2d.shape[0]
    row = lambda w: pl.BlockSpec((tm, w), lambda i: (i, 0))
    out_shapes = (
        jax.ShapeDtypeStruct((n, 512), BF16),
        jax.ShapeDtypeStruct((n, LANES), BF16),
        jax.ShapeDtypeStruct((n, LANES), BF16),
        jax.ShapeDtypeStruct((n // CMP_STRIDE, CMP_STRIDE * LANES), BF16),
        jax.ShapeDtypeStruct((n // CMP_STRIDE, CMP_STRIDE * LANES), BF16),
        jax.ShapeDtypeStruct((n, LANES), BF16),
        jax.ShapeDtypeStruct((n, LANES), BF16),
        jax.ShapeDtypeStruct((n, LANES), F32),
        jax.ShapeDtypeStruct((n, LANES), F32),
        jax.ShapeDtypeStruct((n, 512), BF16),
        jax.ShapeDtypeStruct((n, 512), BF16),
        jax.ShapeDtypeStruct((n, 512), BF16),
        jax.ShapeDtypeStruct((n, 512), BF16),
        jax.ShapeDtypeStruct((n, 2048), BF16),
        jax.ShapeDtypeStruct((8, n), F32),
    )
    seg = pl.BlockSpec((tm // CMP_STRIDE, CMP_STRIDE * LANES), lambda i: (i, 0))
    out_specs = (row(512), row(LANES), row(LANES), seg, seg) + (row(LANES),) * 4 + (row(512),) * 4 + (
        row(2048), pl.BlockSpec((8, tm), lambda i: (0, i)))
    return pl.pallas_call(
        functools.partial(_proj_kernel, tiles_per_seq=seq_len // tm),
        grid=(n // tm,),
        in_specs=[row(D_MODEL), _const_spec((1, D_MODEL)), _const_spec((D_MODEL, _W_COLS)),
                  _const_spec((8, D_MODEL)), _const_spec((1, LANES)), _const_spec((2, LANES)),
                  _const_spec(conv_w.shape), _const_spec(conv_b.shape)],
        out_specs=out_specs,
        out_shape=out_shapes,
        scratch_shapes=[pltpu.VMEM((tm + 8, 1024), F32), pltpu.VMEM((2, tm, LANES), F32)],
        compiler_params=pltpu.CompilerParams(dimension_semantics=("arbitrary",),
                                             vmem_limit_bytes=VMEM_LIMIT),
        name="proj",
    )(x2d, g1, w_all, w_if, qg_pad, kg_pad, conv_w, conv_b)


def _compress_kernel(seg_ref, w1_ref, pos_ref, b1_ref, w2_ref, b2_ref, gain_ref, out_ref, *, normalize):
    seg = seg_ref[0]
    nseg, width = seg.shape
    c = math.sqrt(2.0 / math.pi)
    pos_lo = jnp.broadcast_to(pos_ref[0:1, :], (8, width))
    pos_hi = jnp.broadcast_to(pos_ref[1:2, :], (8, width))
    outs = []
    for g in range(NSA_GROUPS):
        a = _dot(seg, w1_ref[g, 0])
        b = _dot(seg, w1_ref[g, 1])
        posb = (_dot(pos_lo, w1_ref[g, 0]) + _dot(pos_hi, w1_ref[g, 1]))[0:1] + b1_ref[...]
        pre = a + pltpu.roll(b, nseg - 1, 0) + posb
        hid = 0.5 * pre * (1.0 + jnp.tanh(c * (pre + 0.044715 * (pre * pre * pre))))
        out = _dot(hid.astype(BF16), w2_ref[...]) + b2_ref[...]
        if normalize:
            ms = jnp.mean(out * out, axis=-1, keepdims=True)
            out = out * lax.rsqrt(ms + RMS_EPS) * gain_ref[...]
        outs.append(out)
    out_ref[0] = jnp.concatenate(outs, axis=1).astype(out_ref.dtype)


def _compress(segs, w1x, posx, b1, w2, b2, gain, normalize):
    bsz, nseg, width = segs.shape
    return pl.pallas_call(
        functools.partial(_compress_kernel, normalize=normalize),
        grid=(bsz,),
        in_specs=[pl.BlockSpec((1, nseg, width), lambda b: (b, 0, 0)),
                  _const_spec(w1x.shape), _const_spec(posx.shape), _const_spec(b1.shape),
                  _const_spec(w2.shape), _const_spec(b2.shape), _const_spec(gain.shape)],
        out_specs=pl.BlockSpec((1, nseg, LANES), lambda b: (b, 0, 0)),
        out_shape=jax.ShapeDtypeStruct((bsz, nseg, LANES), BF16),
        compiler_params=pltpu.CompilerParams(dimension_semantics=("parallel",),
                                             vmem_limit_bytes=VMEM_LIMIT),
        name="compress",
    )(segs, w1x, posx, b1, w2, b2, gain)


def _stack_heads(q_ref, lane_q):
    q = q_ref[0]
    return jnp.concatenate(
        [jnp.where((lane_q >= NSA_D) == (g == 1), q[:, LANES * j:LANES * (j + 1)], jnp.zeros((), BF16))
         for g in range(NSA_GROUPS) for j in range(NSA_HPG)], axis=0)


def _softmax_pv(s, v, online):
    if online:
        m = jnp.max(s, axis=1, keepdims=True)
        s = s - jnp.where(m < 0.1 * MASK_BIG, 0.0, m)
    p = jnp.exp2(s)
    acc = _dot(p.astype(BF16), jnp.concatenate([v, jnp.ones(v.shape, BF16)], axis=1))
    return p, acc[:, :LANES], acc[:, LANES:]


def _nsa_select_kernel(safe_ref, q_ref, kc_ref, vc_ref, gate_ref, bcf_ref, c2s_ref, shift_ref,
                       oc_ref, sbn_ref, sbf_ref, score_sc):
    i = pl.program_id(1)
    safe = safe_ref[0] != 0
    ncmp = kc_ref.shape[1]
    n_tiles = ncmp * CMP_STRIDE // SEL_TILE
    refs = (q_ref, kc_ref, vc_ref, gate_ref, bcf_ref, c2s_ref, shift_ref, oc_ref, sbn_ref, sbf_ref, score_sc)
    n_cls = max(c for c in (1, 2, 4) if n_tiles % c == 0 and (ncmp // c) % LANES == 0)
    for c in range(n_cls):
        @pl.when(safe & (i // (n_tiles // n_cls) == c))
        def _():
            _nsa_select_tile(i, ncmp * (c + 1) // n_cls, LANES * (c + 1) // n_cls, *refs, False)

    @pl.when(jnp.logical_not(safe))
    def _():
        _nsa_select_tile(i, ncmp, LANES, *refs, True)


def _nsa_select_tile(i, ncmp, nblk, q_ref, kc_ref, vc_ref, gate_ref, bcf_ref, c2s_ref, shift_ref,
                     oc_ref, sbn_ref, sbf_ref, score_sc, online):
    T = SEL_TILE
    lane_q = lax.broadcasted_iota(jnp.int32, (T, LANES), 1)
    qpad = _stack_heads(q_ref, lane_q)

    n_io = lax.broadcasted_iota(jnp.int32, (ncmp, LANES), 0)
    l_io = lax.broadcasted_iota(jnp.int32, (ncmp, LANES), 1)
    band = jnp.clip(n_io - i * (T // CMP_STRIDE) + BAND_OFF, 0, BAND_SLOTS - 1)
    oh_c = jnp.where(band == (l_io & 63), 1.0, 0.0).astype(BF16)
    kc_aug = jnp.concatenate([kc_ref[0, :ncmp, :], oh_c], axis=1)
    qc_aug = jnp.concatenate([qpad, bcf_ref[...]], axis=1)
    gates = gate_ref[0]
    v_c = vc_ref[0, :ncmp, :]
    gated, p_sum = [], []
    for h in range(NSA_HEADS):
        p_h, num_h, l_h = _softmax_pv(_nt(qc_aug[h * T:(h + 1) * T], kc_aug), v_c, online)
        inv_h = 1.0 / jnp.maximum(l_h, 1e-30)
        gated.append(jnp.broadcast_to(gates[:, h:h + 1], (T, LANES)) * (num_h * inv_h))
        p_h = p_h * jnp.concatenate([inv_h] * (ncmp // LANES), axis=1)
        if h % NSA_HPG == 0:
            p_sum.append(p_h)
        else:
            p_sum[-1] = p_sum[-1] + p_h
    for j in range(NSA_HPG):
        oc_ref[0, :, LANES * j:LANES * (j + 1)] = jnp.where(
            lane_q < NSA_D, gated[j], gated[NSA_HPG + j]).astype(oc_ref.dtype)

    imp_t = []
    for ps in p_sum:
        ps_hi = ps.astype(BF16)
        ps_lo = (ps - ps_hi.astype(F32)).astype(BF16)
        c2s = c2s_ref[:ncmp, :]
        imp_t.append((_dot(ps_hi, c2s) + _dot(ps_lo, c2s)).T[:nblk])
    width = NSA_GROUPS * T
    j_io = lax.broadcasted_iota(jnp.int32, (nblk, width), 0)
    r_io = lax.broadcasted_iota(jnp.int32, (nblk, width), 1) & (T - 1)
    j_f = j_io.astype(F32)
    cur = i * (T // SLC_BLOCK) + jnp.right_shift(r_io, SLC_BLOCK.bit_length() - 1)
    forced = (j_io == 0) | (j_io == cur) | (j_io == cur - 1)
    n_pick = SLC_TOPK - 3
    start = jnp.where(forced, -jnp.inf, jnp.where(j_io <= cur, jnp.concatenate(imp_t, axis=1), -1.0 - j_f))

    def padded(score):
        if nblk == LANES:
            return score
        return jnp.concatenate([score, jnp.zeros((LANES - nblk, width), F32)], axis=0)

    score = start
    for _ in range(n_pick):
        score = jnp.where(score == jnp.max(score, axis=0, keepdims=True), -jnp.inf, score)
    score_sc[...] = padded(score)
    taken = jnp.sum(jnp.where((score == -jnp.inf) & jnp.logical_not(forced), 1.0, 0.0), axis=0, keepdims=True)

    @pl.when(jnp.max(taken) > n_pick + 0.5)
    def _():
        exact = start
        for _ in range(n_pick):
            best = jnp.max(exact, axis=0, keepdims=True)
            first = jnp.min(jnp.where(exact == best, j_f, float(nblk)), axis=0, keepdims=True)
            exact = jnp.where(j_f == first, -jnp.inf, exact)
        score_sc[...] = padded(exact)

    score = score_sc[...]
    near_start = ((i * T + lax.broadcasted_iota(jnp.int32, (T, LANES), 0)) // NSA_TILE) * (NSA_TILE // SLC_BLOCK)
    near_blk = lane_q >= near_start - SLC_PAD // SLC_BLOCK
    far_pick = shift_ref[0:1, 0:1]
    for g in range(NSA_GROUPS):
        picked = score[:, g * T:(g + 1) * T].T == -jnp.inf
        sbn_ref[0, :, LANES * g:LANES * (g + 1)] = jnp.where(picked & near_blk, 0.0, MASK_BIG).astype(BF16)
        sbf_ref[0, :, LANES * g:LANES * (g + 1)] = jnp.where(
            picked & jnp.logical_not(near_blk), far_pick, MASK_BIG).astype(BF16)


def _nsa_attend_kernel(safe_ref, *refs):
    *io_refs, acc_sc, m_sc, pa_sc, pb_sc = refs

    @pl.when(safe_ref[0] != 0)
    def _():
        _nsa_attend_tile(*io_refs, acc_sc, pa_sc, pb_sc, online=False)

    @pl.when(safe_ref[0] == 0)
    def _():
        _nsa_attend_tile(*io_refs, acc_sc, m_sc, online=True)


def _nsa_attend_tile(q_ref, ks_ref, vs_ref, kw_ref, vw_ref, oh_ref, gate_ref, oc_ref, sbn_ref, sbf_ref,
                     bs_ref, bw_ref, out_ref, acc_sc, *extra_sc, online):
    i = pl.program_id(1)
    t0 = pl.multiple_of(i * NSA_TILE, NSA_TILE)
    T = NSA_TILE
    rows = NSA_HEADS * T
    grows = NSA_HPG * T
    lane_q = lax.broadcasted_iota(jnp.int32, (T, LANES), 1)
    qpad = _stack_heads(q_ref, lane_q)

    def with_features(feat_ref):
        feats = [jnp.concatenate([feat_ref[0, :, LANES * g:LANES * (g + 1)]] * NSA_HPG, axis=0)
                 for g in range(NSA_GROUPS)]
        return jnp.concatenate([qpad, jnp.concatenate(feats, axis=0)], axis=1)

    q_near, q_far = with_features(sbn_ref), with_features(sbf_ref)

    r_w = lax.broadcasted_iota(jnp.int32, (WIN_KEYS, LANES), 0)
    pad_flag = jnp.where(r_w + (t0 - WINDOW) < 0, 1.0, 0.0).astype(BF16)
    kw_aug = jnp.concatenate([kw_ref[0, pl.ds(t0, WIN_KEYS), :], pad_flag], axis=1)
    qw_aug = jnp.concatenate([qpad, jnp.full((rows, LANES), MASK_BIG / LANES, BF16)], axis=1)
    s_w = _nt(qw_aug, kw_aug) + bw_ref[...]
    _, num_w, l_w = _softmax_pv(s_w, vw_ref[0, pl.ds(t0, WIN_KEYS), :], online)
    o_w = num_w * (1.0 / l_w)

    gates = gate_ref[0]
    gate_tile = lambda col: jnp.broadcast_to(gates[:, col:col + 1], (T, LANES))
    head_rows = [slice(h * T, (h + 1) * T) for h in range(NSA_HEADS)]
    o_gw = [gate_tile(16 + h) * o_w[head_rows[h]] for h in range(NSA_HEADS)]
    g_slc = [gate_tile(8 + h) for h in range(NSA_HEADS)]

    ones_f = jnp.ones((FAR_CHUNK, LANES), BF16)
    k_near = jnp.concatenate([ks_ref[0, pl.ds(t0, NEAR_KEYS), :], oh_ref[pl.ds(t0, NEAR_KEYS), :]], axis=1)
    v_near = jnp.concatenate([vs_ref[0, pl.ds(t0, NEAR_KEYS), :], jnp.ones((NEAR_KEYS, LANES), BF16)], axis=1)
    s_n = _nt(q_near, k_near) + bs_ref[...]
    if online:
        m_sc, = extra_sc
        m_n = jnp.max(s_n, axis=1, keepdims=True)
        m_sc[...] = m_n
        s_n = s_n - m_n
    acc_sc[...] = _dot(jnp.exp2(s_n).astype(BF16), v_near)

    last_chunk = (ks_ref.shape[1] - SLC_PAD) // FAR_CHUNK - 1

    def far_start(c):
        return pl.multiple_of(SLC_PAD + FAR_CHUNK * jnp.minimum(c, last_chunk), LANES)

    def far_scores(c, by_group=False):
        start = far_start(c)
        k_f = jnp.concatenate([ks_ref[0, pl.ds(start, FAR_CHUNK), :], oh_ref[pl.ds(start, FAR_CHUNK), :]], axis=1)
        if by_group:
            return jnp.concatenate([_nt(q_far[:grows], k_f), _nt(q_far[grows:], k_f)], axis=0)
        return _nt(q_far, k_f)

    def far_values(c):
        return jnp.concatenate([vs_ref[0, pl.ds(far_start(c), FAR_CHUNK), :], ones_f], axis=1)

    n_far = (t0 + (FAR_CHUNK - 1 - SLC_PAD)) // FAR_CHUNK
    if online:
        def far_online(c, carry):
            s_f = far_scores(c)
            m_old = m_sc[...]
            m_new = jnp.maximum(m_old, jnp.max(s_f, axis=1, keepdims=True))
            m_sc[...] = m_new
            acc_sc[...] = (jnp.exp2(m_old - m_new) * acc_sc[...]
                           + _dot(jnp.exp2(s_f - m_new).astype(BF16), far_values(c)))
            return carry

        lax.fori_loop(0, n_far, far_online, 0)
    else:
        pa_sc, pb_sc = extra_sc
        pa_sc[...] = jnp.exp2(far_scores(0, by_group=True)).astype(BF16)

        def far_pair(cc, carry):
            c0 = 2 * cc
            pv0 = _dot(pa_sc[...], far_values(c0))
            pb_sc[...] = jnp.exp2(far_scores(c0 + 1)).astype(BF16)
            pv1 = _dot(pb_sc[...], far_values(c0 + 1))
            pa_sc[...] = jnp.exp2(far_scores(c0 + 2)).astype(BF16)
            acc_sc[...] += pv0 + pv1
            return carry

        lax.fori_loop(0, n_far // 2, far_pair, 0)

        @pl.when(n_far % 2 == 1)
        def _():
            v_f = far_values(n_far - 1)
            acc_sc[0:grows, :] += _dot(pa_sc[0:grows, :], v_f)
            acc_sc[grows:, :] += _dot(pa_sc[grows:, :], v_f)

    acc = acc_sc[...]
    o_s = acc[:, :LANES] * (1.0 / acc[:, LANES:])

    head_out = [o_gw[h] + g_slc[h] * o_s[head_rows[h]] for h in range(NSA_HEADS)]
    for j in range(NSA_HPG):
        cols = slice(LANES * j, LANES * (j + 1))
        out_ref[0, :, cols] = (oc_ref[0, :, cols].astype(F32) + jnp.where(
            lane_q < NSA_D, head_out[j], head_out[NSA_HPG + j])).astype(out_ref.dtype)


def _nsa(safe, q, ks, vs, kw, vw, kc, vc, oh, gates, bcf, bs, bw, c2s, shift):
    bsz, s = q.shape[:2]
    flag = pl.BlockSpec(memory_space=pltpu.SMEM)
    T = NSA_TILE
    rows = NSA_HEADS * T
    grid = (bsz, s // T)
    params = pltpu.CompilerParams(dimension_semantics=("parallel", "arbitrary"), vmem_limit_bytes=VMEM_LIMIT)
    tile = lambda width: pl.BlockSpec((1, T, width), lambda b, i: (b, i, 0))
    per_b = lambda a: pl.BlockSpec((1,) + a.shape[1:], lambda b, i: (b, 0, 0), pipeline_mode=pl.Buffered(1))
    sel_tile = lambda width: pl.BlockSpec((1, SEL_TILE, width), lambda b, i: (b, i, 0))
    o_cmp, sb_near, sb_far = pl.pallas_call(
        _nsa_select_kernel,
        grid=(bsz, s // SEL_TILE),
        in_specs=[flag, sel_tile(NSA_HPG * LANES), per_b(kc), per_b(vc), sel_tile(LANES),
                  _const_spec(bcf.shape), _const_spec(c2s.shape), _const_spec(shift.shape)],
        out_specs=(sel_tile(NSA_HPG * LANES), sel_tile(NSA_GROUPS * LANES), sel_tile(NSA_GROUPS * LANES)),
        out_shape=(jax.ShapeDtypeStruct((bsz, s, NSA_HPG * LANES), BF16),
                   jax.ShapeDtypeStruct((bsz, s, NSA_GROUPS * LANES), BF16),
                   jax.ShapeDtypeStruct((bsz, s, NSA_GROUPS * LANES), BF16)),
        scratch_shapes=[pltpu.VMEM((LANES, NSA_GROUPS * SEL_TILE), F32)],
        compiler_params=params,
        name="nsa_select",
    )(safe, q, kc, vc, gates, bcf, c2s, shift)
    scratch = [pltpu.VMEM((rows, 2 * LANES), F32),
               pltpu.VMEM((rows, 1), F32),
               pltpu.VMEM((rows, FAR_CHUNK), BF16),
               pltpu.VMEM((rows, FAR_CHUNK), BF16)]
    return pl.pallas_call(
        _nsa_attend_kernel,
        grid=grid,
        in_specs=[flag, tile(NSA_HPG * LANES), per_b(ks), per_b(vs), per_b(kw), per_b(vw), _const_spec(oh.shape),
                  tile(LANES), tile(NSA_HPG * LANES), tile(NSA_GROUPS * LANES), tile(NSA_GROUPS * LANES),
                  _const_spec(bs.shape), _const_spec(bw.shape)],
        out_specs=tile(NSA_HPG * LANES),
        out_shape=jax.ShapeDtypeStruct((bsz, s, NSA_HPG * LANES), BF16),
        scratch_shapes=scratch,
        compiler_params=params,
        name="nsa_attend",
    )(safe, q, ks, vs, kw, vw, oh, gates, o_cmp, sb_near, sb_far, bs, bw)


ML_ROWS = 512


def _mlstm_kernel(q_ref, k_ref, v_ref, o_ref, gif_ref, gcol_ref, gb_ref, gbrow_ref, tril_ref, triu_ref,
                  out_ref, c_sc, m_sc):
    j = pl.program_id(1)
    R = ML_ROWS
    L = ML_CHUNK

    @pl.when(j == 0)
    def _():
        c_sc[...] = jnp.zeros_like(c_sc)
        m_sc[...] = jnp.zeros_like(m_sc)

    og = o_ref[0]

    pre = gif_ref[...] + gb_ref[...]
    row8 = lax.broadcasted_iota(jnp.int32, (8, R), 0)
    logf = jnp.minimum(pre, 0.0) - jnp.log(1.0 + jnp.exp(-jnp.abs(pre)))
    g8 = jnp.where(row8 < ML_HEADS, pre, logf)
    pre_c = gcol_ref[0] + gbrow_ref[...]
    lane_c = lax.broadcasted_iota(jnp.int32, pre_c.shape, 1)
    logf_c = jnp.minimum(pre_c, 0.0) - jnp.log(1.0 + jnp.exp(-jnp.abs(pre_c)))
    g_c = jnp.where(lane_c < _IF_LANE + ML_HEADS, pre_c, logf_c)
    cum_c = _split3(g_c, lambda part: _dot(tril_ref[...], part))
    cum_row = _split3(g8, lambda part: _dot(part, triu_ref[...]))

    a_io = lax.broadcasted_iota(jnp.int32, (L, L), 0)
    b_io = lax.broadcasted_iota(jnp.int32, (L, L), 1)
    causal = (b_io <= a_io)[None]
    nc = R // L
    pairs = [(c, h) for c in range(nc) for h in range(ML_HEADS)]

    def blocks(a):
        return jnp.stack([a[c * L:(c + 1) * L, h * ML_D:(h + 1) * ML_D] for c, h in pairs])

    def cols(a, k0):
        return jnp.stack([jnp.broadcast_to(a[c * L:(c + 1) * L, k0 + h:k0 + h + 1], (L, LANES)) for c, h in pairs])

    def rows(a, k0):
        return jnp.stack([a[k0 + h:k0 + h + 1, c * L:(c + 1) * L] for c, h in pairs])

    bdot = lambda eq, x, y: jnp.einsum(eq, x, y, preferred_element_type=F32)
    qb, kb, vb = blocks(q_ref[0]), blocks(k_ref[0]), blocks(v_ref[0])
    b_col, li_col = cols(cum_c, _IF_LANE + ML_HEADS), cols(g_c, _IF_LANE)
    b_row, li_row = rows(cum_row, ML_HEADS), rows(g8, 0)
    gsum = b_row[:, :, L - 1:L]

    s_max = jnp.max(gsum - b_row + li_row, axis=2, keepdims=True)
    m_run = m_sc[:, 0:1, 0:1]
    m_ins, m_outs = [], []
    for c in range(nc):
        hs = slice(c * ML_HEADS, (c + 1) * ML_HEADS)
        m_ins.append(m_run)
        m_run = jnp.maximum(gsum[hs] + m_run, s_max[hs])
        m_outs.append(m_run)
    m_sc[...] = jnp.broadcast_to(m_run, m_sc.shape)
    m_in, m_out = jnp.concatenate(m_ins, axis=0), jnp.concatenate(m_outs, axis=0)

    log_d = jnp.where(causal, b_col[:, :, :L] - b_row + li_row, -jnp.inf)
    inter = b_col + m_in
    m_row = jnp.maximum(inter, jnp.broadcast_to(jnp.max(log_d, axis=2, keepdims=True), inter.shape))
    w = bdot('bik,bjk->bij', qb, kb) * jnp.exp(log_d - m_row[:, :, :L])
    v_aug = jnp.concatenate([vb, jnp.ones(vb.shape, BF16)], axis=2)
    wv = bdot('bij,bjd->bid', w.astype(BF16), v_aug)
    inter_scale = jnp.exp(inter - m_row)
    inter_scale = jnp.concatenate([inter_scale, inter_scale], axis=2)
    floor = jnp.exp(-m_row)
    k_src = (kb.astype(F32) * jnp.exp(gsum - b_col + li_col - m_out)).astype(BF16)
    upd = bdot('bjk,bjd->bkd', k_src, v_aug)
    decay = jnp.exp(gsum + m_in - m_out)

    c_aug = c_sc[...]
    for c in range(nc):
        hs = slice(c * ML_HEADS, (c + 1) * ML_HEADS)
        nd = inter_scale[hs] * bdot('hik,hkd->hid', qb[hs], c_aug.astype(BF16)) + wv[hs]
        hval = nd[:, :, :ML_D] / jnp.maximum(jnp.abs(nd[:, :, ML_D:]), floor[hs])
        for h in range(ML_HEADS):
            rs, cs = slice(c * L, (c + 1) * L), slice(h * ML_D, (h + 1) * ML_D)
            out_ref[0, rs, cs] = (og[rs, cs].astype(F32) * hval[h]).astype(out_ref.dtype)
        c_aug = decay[hs] * c_aug + upd[hs]
    c_sc[...] = c_aug


def _mlstm(mq, mk, mv, og, gif, gcol, gate_b, gate_b_row):
    bsz, s, width = mq.shape
    R = ML_ROWS
    nblk = s // R
    seq = lambda: pl.BlockSpec((1, R, width), lambda b, j: (b, j, 0))
    pos = np.arange(R)
    same_chunk = (pos[:, None] // ML_CHUNK) == (pos[None, :] // ML_CHUNK)
    tril = (same_chunk & (pos[None, :] <= pos[:, None])).astype(np.float32)
    return pl.pallas_call(
        _mlstm_kernel,
        grid=(bsz, nblk),
        in_specs=[seq(), seq(), seq(), seq(),
                  pl.BlockSpec((8, R), lambda b, j: (0, b * nblk + j)),
                  pl.BlockSpec((1, R, LANES), lambda b, j: (b, j, 0)),
                  _const_spec(gate_b.shape), _const_spec(gate_b_row.shape),
                  _const_spec((R, R)), _const_spec((R, R))],
        out_specs=seq(),
        out_shape=jax.ShapeDtypeStruct((bsz, s, width), BF16),
        scratch_shapes=[pltpu.VMEM((ML_HEADS, ML_D, 2 * ML_D), F32),
                        pltpu.VMEM((ML_HEADS, 8, LANES), F32)],
        compiler_params=pltpu.CompilerParams(dimension_semantics=("parallel", "arbitrary"),
                                             vmem_limit_bytes=VMEM_LIMIT),
        name="mlstm",
    )(mq, mk, mv, og, gif, gcol, gate_b, gate_b_row, jnp.asarray(tril, BF16), jnp.asarray(tril.T, BF16))


FF_CHUNK = 512


def _merge_ffn_kernel(x_ref, ya_ref, yb_ref, mg_ref, wa_ref, wb_ref, wo_ref, g2_ref, w1_ref, w2_ref,
                      out_ref):
    mg = mg_ref[...]
    mixed = (mg[:, :D_MODEL].astype(F32) * _dot(ya_ref[...], wa_ref[...])
             + mg[:, D_MODEL:].astype(F32) * _dot(yb_ref[...], wb_ref[...]))
    x1 = x_ref[...] + _dot(mixed.astype(BF16), wo_ref[...])
    h2 = x1 * lax.rsqrt(jnp.mean(x1 * x1, axis=-1, keepdims=True) + RMS_EPS) * g2_ref[...]
    h2 = h2.astype(BF16)
    acc = x1
    for c in range(D_FF // FF_CHUNK):
        a = jnp.maximum(_dot(h2, w1_ref[:, c * FF_CHUNK:(c + 1) * FF_CHUNK]), 0.0)
        acc = acc + _dot((a * a).astype(BF16), w2_ref[c * FF_CHUNK:(c + 1) * FF_CHUNK, :])
    out_ref[...] = acc


def _merge_ffn(x2d, ya, yb, mg, wa, wb, wo, g2, w1, w2, tm=512):
    n = x2d.shape[0]
    row = lambda w: pl.BlockSpec((tm, w), lambda i: (i, 0))
    return pl.pallas_call(
        _merge_ffn_kernel,
        grid=(n // tm,),
        in_specs=[row(D_MODEL), row(512), row(512), row(2048),
                  _const_spec(wa.shape), _const_spec(wb.shape), _const_spec(wo.shape),
                  _const_spec(g2.shape), _const_spec(w1.shape), _const_spec(w2.shape)],
        out_specs=row(D_MODEL),
        out_shape=jax.ShapeDtypeStruct((n, D_MODEL), F32),
        compiler_params=pltpu.CompilerParams(dimension_semantics=("parallel",),
                                             vmem_limit_bytes=VMEM_LIMIT),
        name="merge_ffn",
    )(x2d, ya, yb, mg, wa, wb, wo, g2, w1, w2)


def _proj_weights(w):
    widths = (512, 128, 128, 128, 128, 128, 128, 24, 512, 512, 512, 4, 4, 512, 2048)
    off = np.concatenate([[0], np.cumsum(widths)])
    (nq, nkc, nvc, nks, nvs, nkw, nvw, ngate, mq, mk, mv, mi, mf, mo, mgate) = (int(o) for o in off[:-1])
    col = lambda start, width: w[:, start:start + width]
    parts = [col(nq + NSA_D * (g * NSA_HPG + j), NSA_D) for j in range(NSA_HPG) for g in range(NSA_GROUPS)]
    parts += [col(nks, 128), col(nkw, 128), col(nkc, 128), col(nvc, 128), col(nvs, 128), col(nvw, 128)]
    gate = col(ngate, 24).reshape(-1, NSA_HEADS, 3).transpose(0, 2, 1).reshape(-1, 24)
    parts += [gate, col(mi, 4), col(mf, 4), jnp.zeros((w.shape[0], LANES - _IF_LANE - 8), w.dtype)]
    parts += [col(mq, 512), col(mk, 512), col(mv, 512), col(mo, 512), col(mgate, 2048)]
    w_all = jnp.concatenate(parts, axis=1).astype(BF16)
    w_if = jnp.concatenate([col(mi, 4), col(mf, 4)], axis=1).T.astype(BF16)
    return w_all, w_if


def _branch_a_weights(w):
    head = lambda h: w[NSA_D * h:NSA_D * (h + 1)]
    return jnp.concatenate([head(g * NSA_HPG + j) for j in range(NSA_HPG) for g in range(NSA_GROUPS)],
                           axis=0).astype(BF16)


def _compress_weights(pos, w1):
    r = w1.reshape(2, CMP_STRIDE, 1, NSA_D, CMP_HIDDEN)
    z = jnp.zeros_like(r)
    w1x = jnp.stack([jnp.concatenate([r, z], axis=2), jnp.concatenate([z, r], axis=2)])
    w1x = w1x.reshape(NSA_GROUPS, 2, CMP_STRIDE * NSA_GROUPS * NSA_D, CMP_HIDDEN).astype(BF16)
    posx = jnp.broadcast_to(pos.reshape(2, CMP_STRIDE, 1, NSA_D), (2, CMP_STRIDE, NSA_GROUPS, NSA_D))
    return w1x, posx.reshape(2, CMP_STRIDE * NSA_GROUPS * NSA_D).astype(BF16)


def _t5_bucket(dist):
    n = np.maximum(dist, 0)
    max_exact = REL_BUCKETS // 2
    nf = np.maximum(n, 1).astype(np.float32)
    large = max_exact + (np.log(nf / np.float32(max_exact)) / np.float32(math.log(REL_MAX_DIST / max_exact))
                         * np.float32(REL_BUCKETS - max_exact)).astype(np.int32)
    return np.where(n < max_exact, n, np.minimum(large, REL_BUCKETS - 1))


def _toeplitz(rel, n_rows, n_cols, stride, off, inner=1):
    a_rows = n_rows // inner
    lw = a_rows + n_cols
    k = np.arange(lw)[:, None]
    dist = stride * np.where(k < n_cols, -k, lw - k) + np.arange(inner)[None, :] + off
    live = (dist >= 0) & (k != n_cols)
    onehot = live[..., None] & (_t5_bucket(dist)[..., None] == np.arange(REL_BUCKETS))
    w = jnp.dot(jnp.asarray(onehot.reshape(lw * inner, REL_BUCKETS), F32), rel.T,
                precision=lax.Precision.HIGHEST)
    w = w.T.reshape(-1, lw, inner)
    flat = jnp.tile(w, (1, a_rows, 1))[:, :a_rows * (lw - 1)]
    out = flat.reshape(-1, a_rows, lw - 1, inner)[:, :, :n_cols]
    return out.transpose(0, 1, 3, 2).reshape(-1, n_rows, n_cols)


def _rel_bias(rel_table):
    return (rel_table - rel_table[REL_BUCKETS - 1][None, :]).T * LOG2E


def _bias_tables(rel, shifts):
    T = NSA_TILE
    sh_c, sh_s, sh_w = shifts
    r = np.arange(T)[:, None]

    c = np.arange(NEAR_KEYS)[None, :]
    ok = jnp.asarray((r - c + SLC_PAD) >= 0)[None]
    bs = jnp.where(ok, _toeplitz(rel, T, NEAR_KEYS, 1, SLC_PAD) - sh_s, MASK_F32)
    c = np.arange(WIN_KEYS)[None, :]
    d = r - c + WINDOW
    ok = jnp.asarray((d >= 0) & (d < WINDOW))[None]
    bw = jnp.where(ok, _toeplitz(rel, T, WIN_KEYS, 1, WINDOW) - sh_w, MASK_F32)
    r = np.arange(SEL_TILE)[:, None]
    m = np.arange(64)[None, :]
    d = r - CMP_STRIDE * m + BAND_DIST0
    band = (m >= 1) & (m < BAND_SLOTS - 1)
    vals = _toeplitz(rel, SEL_TILE, BAND_SLOTS, CMP_STRIDE, BAND_DIST0, inner=CMP_STRIDE)
    vals = jnp.pad(vals, ((0, 0), (0, 0), (0, 64 - BAND_SLOTS)))
    vals = jnp.where(jnp.asarray(band & (d >= 0))[None], vals, 0.0) - sh_c
    dead = (band & (d < 0)) | (m == BAND_SLOTS - 1)
    vals = jnp.where(jnp.asarray(dead)[None], MASK_BIG, vals)
    hi = vals.astype(BF16)
    lo = (vals - hi.astype(F32)).astype(BF16)
    bcf = jnp.concatenate([hi, lo], axis=-1)
    flat = lambda a: a.reshape(-1, a.shape[-1])
    return flat(bcf), flat(bs).astype(F32), flat(bw).astype(F32)


def _layer(l, x2d, bsz, s, consts, rel_table, norm1_g, w_in, nsa_q_gain, nsa_k_gain, cmp_k, cmp_v,
           ml_conv_w, ml_conv_b, ml_i_bias, ml_f_bias, w_branch_a, w_branch_b, w_out, norm2_g, w_ff1, w_ff2):
    n = bsz * s
    nseg = s // CMP_STRIDE
    oh, c2s = consts
    w_all, w_if = _proj_weights(w_in[l])
    qg_pad = jnp.concatenate([nsa_q_gain[l]] * 2)[None, :]
    kg_pad = jnp.stack([jnp.concatenate([nsa_k_gain[l, 1]] * 2), jnp.concatenate([nsa_k_gain[l, 2]] * 2)])
    (q, ks, kw, kc, vc, vs, vw, gates, gcol, mq, mk, mv, og, mg, gif) = _proj(
        x2d, s, norm1_g[l][None, :], w_all, w_if, qg_pad, kg_pad, ml_conv_w[l], ml_conv_b[l][None, :])

    def compress(a, params, gain, normalize):
        pos, w1, b1, w2, b2 = (p[l] for p in params)
        w1x, posx = _compress_weights(pos, w1)
        return _compress(a.reshape(bsz, nseg, CMP_STRIDE * LANES), w1x, posx, b1[None, :],
                         w2.astype(BF16), b2[None, :], gain[None, :], normalize)

    kcmp = compress(kc, cmp_k, nsa_k_gain[l, 0], True)
    vcmp = compress(vc, cmp_v, jnp.ones((NSA_D,), F32), False)

    qk_bound = lambda kg: 8.0 * LOG2E * jnp.max(jnp.abs(nsa_q_gain[l])) * jnp.max(jnp.abs(kg))
    tab = _rel_bias(rel_table)
    snap = lambda v: v.astype(BF16).astype(F32)
    shifts = [snap(qk_bound(nsa_k_gain[l, j]) + jnp.maximum(jnp.max(tab), 0.0)) for j in range(3)]
    bias_range = jnp.maximum(jnp.max(tab), 0.0) - jnp.minimum(jnp.min(tab), 0.0)
    safe = 2.0 * jnp.max(jnp.stack([qk_bound(nsa_k_gain[l, j]) for j in range(3)])) + bias_range < SAFE_SHIFT_LOG2
    shifts = [jnp.where(safe, sh, 0.0) for sh in shifts]
    bcf, bs, bw = _bias_tables(tab, shifts)

    seq = lambda a: a.reshape(bsz, s, a.shape[-1])
    front = lambda a, p: jnp.pad(seq(a), ((0, 0), (p, 0), (0, 0)))
    operands = (seq(q), front(ks, SLC_PAD), front(vs, SLC_PAD), front(kw, WINDOW), front(vw, WINDOW),
                kcmp, vcmp, oh, seq(gates), bcf, bs, bw, c2s, jnp.full((1, LANES), -shifts[1], F32))
    y_a = _nsa(safe.astype(jnp.int32)[None], *operands)

    gate_b = jnp.concatenate([ml_i_bias[l], ml_f_bias[l]])
    gate_b_row = jnp.pad(gate_b, (_IF_LANE, LANES - _IF_LANE - 8))[None, :]
    y_b = _mlstm(seq(mq), seq(mk), seq(mv), seq(og), gif, seq(gcol), gate_b[:, None], gate_b_row)

    out = _merge_ffn(x2d, y_a.reshape(n, 512), y_b.reshape(n, 512), mg, _branch_a_weights(w_branch_a[l]),
                     w_branch_b[l].astype(BF16), w_out[l].astype(BF16), norm2_g[l][None, :],
                     w_ff1[l].astype(BF16), w_ff2[l].astype(BF16))
    return out, y_a, y_b


def _consts(s):
    nseg = s // CMP_STRIDE
    nsel = s // SLC_BLOCK
    blk_of_key = np.arange(s) // SLC_BLOCK
    oh = np.concatenate([np.ones((SLC_PAD, LANES), np.float32),
                         (blk_of_key[:, None] == np.arange(LANES)[None, :]).astype(np.float32)], axis=0)
    ci = np.arange(nseg)[:, None] * CMP_STRIDE
    sj = np.arange(LANES)[None, :] * SLC_BLOCK
    c2s = ((ci < sj + SLC_BLOCK) & (ci + CMP_LEN > sj) & (np.arange(LANES)[None, :] < nsel)
           & (np.arange(nseg)[:, None] < nseg - 1))
    return jnp.asarray(oh, BF16), jnp.asarray(c2s.astype(np.float32), BF16)


def kernel(x, norm1_g, w_in, nsa_q_gain, nsa_k_gain, cmp_k_pos, cmp_k_w1, cmp_k_b1, cmp_k_w2, cmp_k_b2, cmp_v_pos, cmp_v_w1, cmp_v_b1, cmp_v_w2, cmp_v_b2, rel_table, ml_conv_w, ml_conv_b, ml_i_bias, ml_f_bias, w_branch_a, w_branch_b, w_out, norm2_g, w_ff1, w_ff2):
    bsz, s, _ = x.shape
    consts = _consts(s)
    x2d = x.reshape(bsz * s, D_MODEL)
    for l in range(norm1_g.shape[0]):
        x2d, _, _ = _layer(l, x2d, bsz, s, consts, rel_table, norm1_g, w_in, nsa_q_gain, nsa_k_gain,
                           (cmp_k_pos, cmp_k_w1, cmp_k_b1, cmp_k_w2, cmp_k_b2),
                           (cmp_v_pos, cmp_v_w1, cmp_v_b1, cmp_v_w2, cmp_v_b2),
                           ml_conv_w, ml_conv_b, ml_i_bias, ml_f_bias,
                           w_branch_a, w_branch_b, w_out, norm2_g, w_ff1, w_ff2)
    return x2d.reshape(bsz, s, D_MODEL)
```

```python
import functools
import math

import numpy as np
import jax
import jax.numpy as jnp
from jax import lax
from jax.experimental import pallas as pl
from jax.experimental.pallas import tpu as pltpu

F32 = jnp.float32
BF16 = jnp.bfloat16

D_MODEL = 1024
NSA_HEADS = 8
NSA_GROUPS = 2
NSA_HPG = NSA_HEADS // NSA_GROUPS
NSA_D = 64
CMP_LEN = 32
CMP_STRIDE = 16
CMP_HIDDEN = 256
SLC_BLOCK = 64
SLC_TOPK = 16
WINDOW = 512
ML_HEADS = 4
ML_D = 128
ML_CHUNK = 64
CONV_WIDTH = 4
D_FF = 4 * D_MODEL
REL_BUCKETS = 32
REL_MAX_DIST = 128
RMS_EPS = 1e-6

LANES = 128
NSA_TILE = 256
SEL_TILE = 512
SLC_PAD = 256
FAR_CHUNK = 512
NEAR_KEYS = SLC_PAD + NSA_TILE
WIN_KEYS = WINDOW + NSA_TILE
BAND_OFF = 10
BAND_SLOTS = 2 + (SEL_TILE + 128) // CMP_STRIDE
BAND_DIST0 = CMP_STRIDE * BAND_OFF - (CMP_LEN - 1)
MASK_BIG = -1e9
MASK_F32 = -1e30
LOG2E = math.log2(math.e)
SAFE_SHIFT_LOG2 = 50.0
VMEM_LIMIT = 56 * 1024 * 1024

_Q_OFF, _KS_OFF, _KC_OFF, _VS_OFF = 0, 512, 768, 1024
_GATE_OFF, _MQ_OFF, _MK_OFF, _MV_OFF, _MO_OFF, _MG_OFF, _W_COLS = 1280, 1408, 1920, 2432, 2944, 3456, 5504
_IF_LANE = 24


def _nt(a, b):
    return lax.dot_general(a, b, (((1,), (1,)), ((), ())), preferred_element_type=F32)


def _dot(a, b):
    return jnp.dot(a, b, preferred_element_type=F32)


def _split3(x, dot_part):
    hi = x.astype(BF16)
    r1 = x - hi.astype(F32)
    mid = r1.astype(BF16)
    lo = (r1 - mid.astype(F32)).astype(BF16)
    return dot_part(hi) + dot_part(mid) + dot_part(lo)


def _const_spec(shape):
    nd = len(shape)
    return pl.BlockSpec(shape, lambda *_: (0,) * nd, pipeline_mode=pl.Buffered(1))


def _proj_kernel(x_ref, g1_ref, w_ref, wif_ref, qg_ref, kg_ref, cw_ref, cb_ref,
                 q_ref, ks_ref, kw_ref, kc_ref, vc_ref, vs_ref, vw_ref, gate_ref, gcol_ref,
                 mq_ref, mk_ref, mv_ref, mo_ref, mg_ref, gif_ref, ext_sc, seg_sc, *, tiles_per_seq):
    tm = x_ref.shape[0]
    x = x_ref[...]
    h = x * lax.rsqrt(jnp.mean(x * x, axis=-1, keepdims=True) + RMS_EPS) * g1_ref[...]
    hb = h.astype(BF16)

    def proj(off, width):
        return _dot(hb, w_ref[:, off:off + width])

    lane = lax.broadcasted_iota(jnp.int32, (1, LANES), 1)
    low = lane < NSA_D

    def half_norm(blk):
        sq = blk * blk
        ms0 = jnp.sum(jnp.where(low, sq, 0.0), axis=-1, keepdims=True) * (1.0 / NSA_D)
        ms1 = jnp.sum(jnp.where(low, 0.0, sq), axis=-1, keepdims=True) * (1.0 / NSA_D)
        return blk * jnp.where(low, lax.rsqrt(ms0 + RMS_EPS), lax.rsqrt(ms1 + RMS_EPS))

    for pair in range(NSA_HPG // 2):
        both = proj(_Q_OFF + 2 * LANES * pair, 2 * LANES)
        for hh in range(2):
            qn = half_norm(both[:, LANES * hh:LANES * (hh + 1)]) * qg_ref[...] * (NSA_D ** -0.5 * LOG2E)
            col = LANES * (2 * pair + hh)
            q_ref[:, col:col + LANES] = qn.astype(q_ref.dtype)

    both = proj(_KS_OFF, 2 * LANES)
    for hh, ref in enumerate((ks_ref, kw_ref)):
        ref[...] = (half_norm(both[:, LANES * hh:LANES * (hh + 1)]) * kg_ref[hh:hh + 1, :]).astype(ref.dtype)

    both = proj(_VS_OFF, 2 * LANES)
    vs_ref[...] = both[:, :LANES].astype(vs_ref.dtype)
    vw_ref[...] = both[:, LANES:].astype(vw_ref.dtype)

    both = proj(_KC_OFF, 2 * LANES)
    for hh, ref in enumerate((kc_ref, vc_ref)):
        seg_sc[hh] = both[:, LANES * hh:LANES * (hh + 1)]
        for tok in range(CMP_STRIDE):
            ref[:, LANES * tok:LANES * (tok + 1)] = seg_sc[hh, pl.ds(tok, tm // CMP_STRIDE, stride=CMP_STRIDE),
                                                           :].astype(ref.dtype)

    slab = proj(_GATE_OFF, LANES)
    gate_ref[...] = jax.nn.sigmoid(slab)
    gcol_ref[...] = slab

    @pl.when(pl.program_id(0) % tiles_per_seq == 0)
    def _():
        ext_sc[0:8, :] = jnp.zeros((8, ext_sc.shape[1]), F32)

    ext_sc[8:, 0:512] = proj(_MQ_OFF, 512)
    ext_sc[8:, 512:1024] = proj(_MK_OFF, 512)
    conv = cb_ref[...]
    for t in range(CONV_WIDTH):
        lo = 8 - (CONV_WIDTH - 1) + t
        conv = conv + ext_sc[lo:lo + tm, :] * cw_ref[t:t + 1, :]
    ext_sc[0:8, :] = ext_sc[tm:tm + 8, :]
    qk = conv * jax.nn.sigmoid(conv)
    mq_ref[...] = (qk[:, :512] * (ML_D ** -0.5)).astype(mq_ref.dtype)
    mk_ref[...] = qk[:, 512:].astype(mk_ref.dtype)

    mv_ref[...] = proj(_MV_OFF, 512).astype(mv_ref.dtype)
    mo_ref[...] = jax.nn.sigmoid(proj(_MO_OFF, 512)).astype(mo_ref.dtype)
    for c in range(4):
        mg_ref[:, 512 * c:512 * (c + 1)] = proj(_MG_OFF + 512 * c, 512).astype(mg_ref.dtype)
    gif_ref[...] = _nt(wif_ref[...], hb)


def _proj(x2d, seq_len, g1, w_all, w_if, qg_pad, kg_pad, conv_w, conv_b, tm=512):
    n = x2d.shape[0]
    row = lambda w: pl.BlockSpec((tm, w), lambda i: (i, 0))
    out_shapes = (
        jax.ShapeDtypeStruct((n, 512), BF16),
        jax.ShapeDtypeStruct((n, LANES), BF16),
        jax.ShapeDtypeStruct((n, LANES), BF16),
        jax.ShapeDtypeStruct((n // CMP_STRIDE, CMP_STRIDE * LANES), BF16),
        jax.ShapeDtypeStruct((n // CMP_STRIDE, CMP_STRIDE * LANES), BF16),
        jax.ShapeDtypeStruct((n, LANES), BF16),
        jax.ShapeDtypeStruct((n, LANES), BF16),
        jax.ShapeDtypeStruct((n, LANES), F32),
        jax.ShapeDtypeStruct((n, LANES), F32),
        jax.ShapeDtypeStruct((n, 512), BF16),
        jax.ShapeDtypeStruct((n, 512), BF16),
        jax.ShapeDtypeStruct((n, 512), BF16),
        jax.ShapeDtypeStruct((n, 512), BF16),
        jax.ShapeDtypeStruct((n, 2048), BF16),
        jax.ShapeDtypeStruct((8, n), F32),
    )
    seg = pl.BlockSpec((tm // CMP_STRIDE, CMP_STRIDE * LANES), lambda i: (i, 0))
    out_specs = (row(512), row(LANES), row(LANES), seg, seg) + (row(LANES),) * 4 + (row(512),) * 4 + (
        row(2048), pl.BlockSpec((8, tm), lambda i: (0, i)))
    return pl.pallas_call(
        functools.partial(_proj_kernel, tiles_per_seq=seq_len // tm),
        grid=(n // tm,),
        in_specs=[row(D_MODEL), _const_spec((1, D_MODEL)), _const_spec((D_MODEL, _W_COLS)),
                  _const_spec((8, D_MODEL)), _const_spec((1, LANES)), _const_spec((2, LANES)),
                  _const_spec(conv_w.shape), _const_spec(conv_b.shape)],
        out_specs=out_specs,
        out_shape=out_shapes,
        scratch_shapes=[pltpu.VMEM((tm + 8, 1024), F32), pltpu.VMEM((2, tm, LANES), F32)],
        compiler_params=pltpu.CompilerParams(dimension_semantics=("arbitrary",),
                                             vmem_limit_bytes=VMEM_LIMIT),
        name="proj",
    )(x2d, g1, w_all, w_if, qg_pad, kg_pad, conv_w, conv_b)


def _compress_kernel(seg_ref, w1_ref, pos_ref, b1_ref, w2_ref, b2_ref, gain_ref, out_ref, *, normalize):
    seg = seg_ref[0]
    nseg, width = seg.shape
    c = math.sqrt(2.0 / math.pi)
    pos_lo = jnp.broadcast_to(pos_ref[0:1, :], (8, width))
    pos_hi = jnp.broadcast_to(pos_ref[1:2, :], (8, width))
    outs = []
    for g in range(NSA_GROUPS):
        a = _dot(seg, w1_ref[g, 0])
        b = _dot(seg, w1_ref[g, 1])
        posb = (_dot(pos_lo, w1_ref[g, 0]) + _dot(pos_hi, w1_ref[g, 1]))[0:1] + b1_ref[...]
        pre = a + pltpu.roll(b, nseg - 1, 0) + posb
        hid = 0.5 * pre * (1.0 + jnp.tanh(c * (pre + 0.044715 * (pre * pre * pre))))
        out = _dot(hid.astype(BF16), w2_ref[...]) + b2_ref[...]
        if normalize:
            ms = jnp.mean(out * out, axis=-1, keepdims=True)
            out = out * lax.rsqrt(ms + RMS_EPS) * gain_ref[...]
        outs.append(out)
    out_ref[0] = jnp.concatenate(outs, axis=1).astype(out_ref.dtype)


def _compress(segs, w1x, posx, b1, w2, b2, gain, normalize):
    bsz, nseg, width = segs.shape
    return pl.pallas_call(
        functools.partial(_compress_kernel, normalize=normalize),
        grid=(bsz,),
        in_specs=[pl.BlockSpec((1, nseg, width), lambda b: (b, 0, 0)),
                  _const_spec(w1x.shape), _const_spec(posx.shape), _const_spec(b1.shape),
                  _const_spec(w2.shape), _const_spec(b2.shape), _const_spec(gain.shape)],
        out_specs=pl.BlockSpec((1, nseg, LANES), lambda b: (b, 0, 0)),
        out_shape=jax.ShapeDtypeStruct((bsz, nseg, LANES), BF16),
        compiler_params=pltpu.CompilerParams(dimension_semantics=("parallel",),
                                             vmem_limit_bytes=VMEM_LIMIT),
        name="compress",
    )(segs, w1x, posx, b1, w2, b2, gain)


def _stack_heads(q_ref, lane_q):
    q = q_ref[0]
    return jnp.concatenate(
        [jnp.where((lane_q >= NSA_D) == (g == 1), q[:, LANES * j:LANES * (j + 1)], jnp.zeros((), BF16))
         for g in range(NSA_GROUPS) for j in range(NSA_HPG)], axis=0)


def _softmax_pv(s, v, online):
    if online:
        m = jnp.max(s, axis=1, keepdims=True)
        s = s - jnp.where(m < 0.1 * MASK_BIG, 0.0, m)
    p = jnp.exp2(s)
    acc = _dot(p.astype(BF16), jnp.concatenate([v, jnp.ones(v.shape, BF16)], axis=1))
    return p, acc[:, :LANES], acc[:, LANES:]


def _nsa_select_kernel(safe_ref, q_ref, kc_ref, vc_ref, gate_ref, bcf_ref, c2s_ref, shift_ref,
                       oc_ref, sbn_ref, sbf_ref, score_sc):
    i = pl.program_id(1)
    safe = safe_ref[0] != 0
    ncmp = kc_ref.shape[1]
    n_tiles = ncmp * CMP_STRIDE // SEL_TILE
    refs = (q_ref, kc_ref, vc_ref, gate_ref, bcf_ref, c2s_ref, shift_ref, oc_ref, sbn_ref, sbf_ref, score_sc)
    n_cls = max(c for c in (1, 2, 4) if n_tiles % c == 0 and (ncmp // c) % LANES == 0)
    for c in range(n_cls):
        @pl.when(safe & (i // (n_tiles // n_cls) == c))
        def _():
            _nsa_select_tile(i, ncmp * (c + 1) // n_cls, LANES * (c + 1) // n_cls, *refs, False)

    @pl.when(jnp.logical_not(safe))
    def _():
        _nsa_select_tile(i, ncmp, LANES, *refs, True)


def _nsa_select_tile(i, ncmp, nblk, q_ref, kc_ref, vc_ref, gate_ref, bcf_ref, c2s_ref, shift_ref,
                     oc_ref, sbn_ref, sbf_ref, score_sc, online):
    T = SEL_TILE
    lane_q = lax.broadcasted_iota(jnp.int32, (T, LANES), 1)
    qpad = _stack_heads(q_ref, lane_q)

    n_io = lax.broadcasted_iota(jnp.int32, (ncmp, LANES), 0)
    l_io = lax.broadcasted_iota(jnp.int32, (ncmp, LANES), 1)
    band = jnp.clip(n_io - i * (T // CMP_STRIDE) + BAND_OFF, 0, BAND_SLOTS - 1)
    oh_c = jnp.where(band == (l_io & 63), 1.0, 0.0).astype(BF16)
    kc_aug = jnp.concatenate([kc_ref[0, :ncmp, :], oh_c], axis=1)
    qc_aug = jnp.concatenate([qpad, bcf_ref[...]], axis=1)
    gates = gate_ref[0]
    v_c = vc_ref[0, :ncmp, :]
    gated, p_sum = [], []
    for h in range(NSA_HEADS):
        p_h, num_h, l_h = _softmax_pv(_nt(qc_aug[h * T:(h + 1) * T], kc_aug), v_c, online)
        inv_h = 1.0 / jnp.maximum(l_h, 1e-30)
        gated.append(jnp.broadcast_to(gates[:, h:h + 1], (T, LANES)) * (num_h * inv_h))
        p_h = p_h * jnp.concatenate([inv_h] * (ncmp // LANES), axis=1)
        if h % NSA_HPG == 0:
            p_sum.append(p_h)
        else:
            p_sum[-1] = p_sum[-1] + p_h
    for j in range(NSA_HPG):
        oc_ref[0, :, LANES * j:LANES * (j + 1)] = jnp.where(
            lane_q < NSA_D, gated[j], gated[NSA_HPG + j]).astype(oc_ref.dtype)

    imp_t = []
    for ps in p_sum:
        ps_hi = ps.astype(BF16)
        ps_lo = (ps - ps_hi.astype(F32)).astype(BF16)
        c2s = c2s_ref[:ncmp, :]
        imp_t.append((_dot(ps_hi, c2s) + _dot(ps_lo, c2s)).T[:nblk])
    width = NSA_GROUPS * T
    j_io = lax.broadcasted_iota(jnp.int32, (nblk, width), 0)
    r_io = lax.broadcasted_iota(jnp.int32, (nblk, width), 1) & (T - 1)
    j_f = j_io.astype(F32)
    cur = i * (T // SLC_BLOCK) + jnp.right_shift(r_io, SLC_BLOCK.bit_length() - 1)
    forced = (j_io == 0) | (j_io == cur) | (j_io == cur - 1)
    n_pick = SLC_TOPK - 3
    start = jnp.where(forced, -jnp.inf, jnp.where(j_io <= cur, jnp.concatenate(imp_t, axis=1), -1.0 - j_f))

    def padded(score):
        if nblk == LANES:
            return score
        return jnp.concatenate([score, jnp.zeros((LANES - nblk, width), F32)], axis=0)

    score = start
    for _ in range(n_pick):
        score = jnp.where(score == jnp.max(score, axis=0, keepdims=True), -jnp.inf, score)
    score_sc[...] = padded(score)
    taken = jnp.sum(jnp.where((score == -jnp.inf) & jnp.logical_not(forced), 1.0, 0.0), axis=0, keepdims=True)

    @pl.when(jnp.max(taken) > n_pick + 0.5)
    def _():
        exact = start
        for _ in range(n_pick):
            best = jnp.max(exact, axis=0, keepdims=True)
            first = jnp.min(jnp.where(exact == best, j_f, float(nblk)), axis=0, keepdims=True)
            exact = jnp.where(j_f == first, -jnp.inf, exact)
        score_sc[...] = padded(exact)

    score = score_sc[...]
    near_start = ((i * T + lax.broadcasted_iota(jnp.int32, (T, LANES), 0)) // NSA_TILE) * (NSA_TILE // SLC_BLOCK)
    near_blk = lane_q >= near_start - SLC_PAD // SLC_BLOCK
    far_pick = shift_ref[0:1, 0:1]
    for g in range(NSA_GROUPS):
        picked = score[:, g * T:(g + 1) * T].T == -jnp.inf
        sbn_ref[0, :, LANES * g:LANES * (g + 1)] = jnp.where(picked & near_blk, 0.0, MASK_BIG).astype(BF16)
        sbf_ref[0, :, LANES * g:LANES * (g + 1)] = jnp.where(
            picked & jnp.logical_not(near_blk), far_pick, MASK_BIG).astype(BF16)


def _nsa_attend_kernel(safe_ref, *refs):
    *io_refs, acc_sc, m_sc, pa_sc, pb_sc = refs

    @pl.when(safe_ref[0] != 0)
    def _():
        _nsa_attend_tile(*io_refs, acc_sc, pa_sc, pb_sc, online=False)

    @pl.when(safe_ref[0] == 0)
    def _():
        _nsa_attend_tile(*io_refs, acc_sc, m_sc, online=True)


def _nsa_attend_tile(q_ref, ks_ref, vs_ref, kw_ref, vw_ref, oh_ref, gate_ref, oc_ref, sbn_ref, sbf_ref,
                     bs_ref, bw_ref, out_ref, acc_sc, *extra_sc, online):
    i = pl.program_id(1)
    t0 = pl.multiple_of(i * NSA_TILE, NSA_TILE)
    T = NSA_TILE
    rows = NSA_HEADS * T
    grows = NSA_HPG * T
    lane_q = lax.broadcasted_iota(jnp.int32, (T, LANES), 1)
    qpad = _stack_heads(q_ref, lane_q)

    def with_features(feat_ref):
        feats = [jnp.concatenate([feat_ref[0, :, LANES * g:LANES * (g + 1)]] * NSA_HPG, axis=0)
                 for g in range(NSA_GROUPS)]
        return jnp.concatenate([qpad, jnp.concatenate(feats, axis=0)], axis=1)

    q_near, q_far = with_features(sbn_ref), with_features(sbf_ref)

    r_w = lax.broadcasted_iota(jnp.int32, (WIN_KEYS, LANES), 0)
    pad_flag = jnp.where(r_w + (t0 - WINDOW) < 0, 1.0, 0.0).astype(BF16)
    kw_aug = jnp.concatenate([kw_ref[0, pl.ds(t0, WIN_KEYS), :], pad_flag], axis=1)
    qw_aug = jnp.concatenate([qpad, jnp.full((rows, LANES), MASK_BIG / LANES, BF16)], axis=1)
    s_w = _nt(qw_aug, kw_aug) + bw_ref[...]
    _, num_w, l_w = _softmax_pv(s_w, vw_ref[0, pl.ds(t0, WIN_KEYS), :], online)
    o_w = num_w * (1.0 / l_w)

    gates = gate_ref[0]
    gate_tile = lambda col: jnp.broadcast_to(gates[:, col:col + 1], (T, LANES))
    head_rows = [slice(h * T, (h + 1) * T) for h in range(NSA_HEADS)]
    o_gw = [gate_tile(16 + h) * o_w[head_rows[h]] for h in range(NSA_HEADS)]
    g_slc = [gate_tile(8 + h) for h in range(NSA_HEADS)]

    ones_f = jnp.ones((FAR_CHUNK, LANES), BF16)
    k_near = jnp.concatenate([ks_ref[0, pl.ds(t0, NEAR_KEYS), :], oh_ref[pl.ds(t0, NEAR_KEYS), :]], axis=1)
    v_near = jnp.concatenate([vs_ref[0, pl.ds(t0, NEAR_KEYS), :], jnp.ones((NEAR_KEYS, LANES), BF16)], axis=1)
    s_n = _nt(q_near, k_near) + bs_ref[...]
    if online:
        m_sc, = extra_sc
        m_n = jnp.max(s_n, axis=1, keepdims=True)
        m_sc[...] = m_n
        s_n = s_n - m_n
    acc_sc[...] = _dot(jnp.exp2(s_n).astype(BF16), v_near)

    last_chunk = (ks_ref.shape[1] - SLC_PAD) // FAR_CHUNK - 1

    def far_start(c):
        return pl.multiple_of(SLC_PAD + FAR_CHUNK * jnp.minimum(c, last_chunk), LANES)

    def far_scores(c, by_group=False):
        start = far_start(c)
        k_f = jnp.concatenate([ks_ref[0, pl.ds(start, FAR_CHUNK), :], oh_ref[pl.ds(start, FAR_CHUNK), :]], axis=1)
        if by_group:
            return jnp.concatenate([_nt(q_far[:grows], k_f), _nt(q_far[grows:], k_f)], axis=0)
        return _nt(q_far, k_f)

    def far_values(c):
        return jnp.concatenate([vs_ref[0, pl.ds(far_start(c), FAR_CHUNK), :], ones_f], axis=1)

    n_far = (t0 + (FAR_CHUNK - 1 - SLC_PAD)) // FAR_CHUNK
    if online:
        def far_online(c, carry):
            s_f = far_scores(c)
            m_old = m_sc[...]
            m_new = jnp.maximum(m_old, jnp.max(s_f, axis=1, keepdims=True))
            m_sc[...] = m_new
            acc_sc[...] = (jnp.exp2(m_old - m_new) * acc_sc[...]
                           + _dot(jnp.exp2(s_f - m_new).astype(BF16), far_values(c)))
            return carry

        lax.fori_loop(0, n_far, far_online, 0)
    else:
        pa_sc, pb_sc = extra_sc
        pa_sc[...] = jnp.exp2(far_scores(0, by_group=True)).astype(BF16)

        def far_pair(cc, carry):
            c0 = 2 * cc
            pv0 = _dot(pa_sc[...], far_values(c0))
            pb_sc[...] = jnp.exp2(far_scores(c0 + 1)).astype(BF16)
            pv1 = _dot(pb_sc[...], far_values(c0 + 1))
            pa_sc[...] = jnp.exp2(far_scores(c0 + 2)).astype(BF16)
            acc_sc[...] += pv0 + pv1
            return carry

        lax.fori_loop(0, n_far // 2, far_pair, 0)

        @pl.when(n_far % 2 == 1)
        def _():
            v_f = far_values(n_far - 1)
            acc_sc[0:grows, :] += _dot(pa_sc[0:grows, :], v_f)
            acc_sc[grows:, :] += _dot(pa_sc[grows:, :], v_f)

    acc = acc_sc[...]
    o_s = acc[:, :LANES] * (1.0 / acc[:, LANES:])

    head_out = [o_gw[h] + g_slc[h] * o_s[head_rows[h]] for h in range(NSA_HEADS)]
    for j in range(NSA_HPG):
        cols = slice(LANES * j, LANES * (j + 1))
        out_ref[0, :, cols] = (oc_ref[0, :, cols].astype(F32) + jnp.where(
            lane_q < NSA_D, head_out[j], head_out[NSA_HPG + j])).astype(out_ref.dtype)


def _nsa(safe, q, ks, vs, kw, vw, kc, vc, oh, gates, bcf, bs, bw, c2s, shift):
    bsz, s = q.shape[:2]
    flag = pl.BlockSpec(memory_space=pltpu.SMEM)
    T = NSA_TILE
    rows = NSA_HEADS * T
    grid = (bsz, s // T)
    params = pltpu.CompilerParams(dimension_semantics=("parallel", "arbitrary"), vmem_limit_bytes=VMEM_LIMIT)
    tile = lambda width: pl.BlockSpec((1, T, width), lambda b, i: (b, i, 0))
    per_b = lambda a: pl.BlockSpec((1,) + a.shape[1:], lambda b, i: (b, 0, 0), pipeline_mode=pl.Buffered(1))
    sel_tile = lambda width: pl.BlockSpec((1, SEL_TILE, width), lambda b, i: (b, i, 0))
    o_cmp, sb_near, sb_far = pl.pallas_call(
        _nsa_select_kernel,
        grid=(bsz, s // SEL_TILE),
        in_specs=[flag, sel_tile(NSA_HPG * LANES), per_b(kc), per_b(vc), sel_tile(LANES),
                  _const_spec(bcf.shape), _const_spec(c2s.shape), _const_spec(shift.shape)],
        out_specs=(sel_tile(NSA_HPG * LANES), sel_tile(NSA_GROUPS * LANES), sel_tile(NSA_GROUPS * LANES)),
        out_shape=(jax.ShapeDtypeStruct((bsz, s, NSA_HPG * LANES), BF16),
                   jax.ShapeDtypeStruct((bsz, s, NSA_GROUPS * LANES), BF16),
                   jax.ShapeDtypeStruct((bsz, s, NSA_GROUPS * LANES), BF16)),
        scratch_shapes=[pltpu.VMEM((LANES, NSA_GROUPS * SEL_TILE), F32)],
        compiler_params=params,
        name="nsa_select",
    )(safe, q, kc, vc, gates, bcf, c2s, shift)
    scratch = [pltpu.VMEM((rows, 2 * LANES), F32),
               pltpu.VMEM((rows, 1), F32),
               pltpu.VMEM((rows, FAR_CHUNK), BF16),
               pltpu.VMEM((rows, FAR_CHUNK), BF16)]
    return pl.pallas_call(
        _nsa_attend_kernel,
        grid=grid,
        in_specs=[flag, tile(NSA_HPG * LANES), per_b(ks), per_b(vs), per_b(kw), per_b(vw), _const_spec(oh.shape),
                  tile(LANES), tile(NSA_HPG * LANES), tile(NSA_GROUPS * LANES), tile(NSA_GROUPS * LANES),
                  _const_spec(bs.shape), _const_spec(bw.shape)],
        out_specs=tile(NSA_HPG * LANES),
        out_shape=jax.ShapeDtypeStruct((bsz, s, NSA_HPG * LANES), BF16),
        scratch_shapes=scratch,
        compiler_params=params,
        name="nsa_attend",
    )(safe, q, ks, vs, kw, vw, oh, gates, o_cmp, sb_near, sb_far, bs, bw)


ML_ROWS = 512


def _mlstm_kernel(q_ref, k_ref, v_ref, o_ref, gif_ref, gcol_ref, gb_ref, gbrow_ref, tril_ref, triu_ref,
                  out_ref, c_sc, m_sc):
    j = pl.program_id(1)
    R = ML_ROWS
    L = ML_CHUNK

    @pl.when(j == 0)
    def _():
        c_sc[...] = jnp.zeros_like(c_sc)
        m_sc[...] = jnp.zeros_like(m_sc)

    og = o_ref[0]

    pre = gif_ref[...] + gb_ref[...]
    row8 = lax.broadcasted_iota(jnp.int32, (8, R), 0)
    logf = jnp.minimum(pre, 0.0) - jnp.log(1.0 + jnp.exp(-jnp.abs(pre)))
    g8 = jnp.where(row8 < ML_HEADS, pre, logf)
    pre_c = gcol_ref[0] + gbrow_ref[...]
    lane_c = lax.broadcasted_iota(jnp.int32, pre_c.shape, 1)
    logf_c = jnp.minimum(pre_c, 0.0) - jnp.log(1.0 + jnp.exp(-jnp.abs(pre_c)))
    g_c = jnp.where(lane_c < _IF_LANE + ML_HEADS, pre_c, logf_c)
    cum_c = _split3(g_c, lambda part: _dot(tril_ref[...], part))
    cum_row = _split3(g8, lambda part: _dot(part, triu_ref[...]))

    a_io = lax.broadcasted_iota(jnp.int32, (L, L), 0)
    b_io = lax.broadcasted_iota(jnp.int32, (L, L), 1)
    causal = (b_io <= a_io)[None]
    nc = R // L
    pairs = [(c, h) for c in range(nc) for h in range(ML_HEADS)]

    def blocks(a):
        return jnp.stack([a[c * L:(c + 1) * L, h * ML_D:(h + 1) * ML_D] for c, h in pairs])

    def cols(a, k0):
        return jnp.stack([jnp.broadcast_to(a[c * L:(c + 1) * L, k0 + h:k0 + h + 1], (L, LANES)) for c, h in pairs])

    def rows(a, k0):
        return jnp.stack([a[k0 + h:k0 + h + 1, c * L:(c + 1) * L] for c, h in pairs])

    bdot = lambda eq, x, y: jnp.einsum(eq, x, y, preferred_element_type=F32)
    qb, kb, vb = blocks(q_ref[0]), blocks(k_ref[0]), blocks(v_ref[0])
    b_col, li_col = cols(cum_c, _IF_LANE + ML_HEADS), cols(g_c, _IF_LANE)
    b_row, li_row = rows(cum_row, ML_HEADS), rows(g8, 0)
    gsum = b_row[:, :, L - 1:L]

    s_max = jnp.max(gsum - b_row + li_row, axis=2, keepdims=True)
    m_run = m_sc[:, 0:1, 0:1]
    m_ins, m_outs = [], []
    for c in range(nc):
        hs = slice(c * ML_HEADS, (c + 1) * ML_HEADS)
        m_ins.append(m_run)
        m_run = jnp.maximum(gsum[hs] + m_run, s_max[hs])
        m_outs.append(m_run)
    m_sc[...] = jnp.broadcast_to(m_run, m_sc.shape)
    m_in, m_out = jnp.concatenate(m_ins, axis=0), jnp.concatenate(m_outs, axis=0)

    log_d = jnp.where(causal, b_col[:, :, :L] - b_row + li_row, -jnp.inf)
    inter = b_col + m_in
    m_row = jnp.maximum(inter, jnp.broadcast_to(jnp.max(log_d, axis=2, keepdims=True), inter.shape))
    w = bdot('bik,bjk->bij', qb, kb) * jnp.exp(log_d - m_row[:, :, :L])
    v_aug = jnp.concatenate([vb, jnp.ones(vb.shape, BF16)], axis=2)
    wv = bdot('bij,bjd->bid', w.astype(BF16), v_aug)
    inter_scale = jnp.exp(inter - m_row)
    inter_scale = jnp.concatenate([inter_scale, inter_scale], axis=2)
    floor = jnp.exp(-m_row)
    k_src = (kb.astype(F32) * jnp.exp(gsum - b_col + li_col - m_out)).astype(BF16)
    upd = bdot('bjk,bjd->bkd', k_src, v_aug)
    decay = jnp.exp(gsum + m_in - m_out)

    c_aug = c_sc[...]
    for c in range(nc):
        hs = slice(c * ML_HEADS, (c + 1) * ML_HEADS)
        nd = inter_scale[hs] * bdot('hik,hkd->hid', qb[hs], c_aug.astype(BF16)) + wv[hs]
        hval = nd[:, :, :ML_D] / jnp.maximum(jnp.abs(nd[:, :, ML_D:]), floor[hs])
        for h in range(ML_HEADS):
            rs, cs = slice(c * L, (c + 1) * L), slice(h * ML_D, (h + 1) * ML_D)
            out_ref[0, rs, cs] = (og[rs, cs].astype(F32) * hval[h]).astype(out_ref.dtype)
        c_aug = decay[hs] * c_aug + upd[hs]
    c_sc[...] = c_aug


def _mlstm(mq, mk, mv, og, gif, gcol, gate_b, gate_b_row):
    bsz, s, width = mq.shape
    R = ML_ROWS
    nblk = s // R
    seq = lambda: pl.BlockSpec((1, R, width), lambda b, j: (b, j, 0))
    pos = np.arange(R)
    same_chunk = (pos[:, None] // ML_CHUNK) == (pos[None, :] // ML_CHUNK)
    tril = (same_chunk & (pos[None, :] <= pos[:, None])).astype(np.float32)
    return pl.pallas_call(
        _mlstm_kernel,
        grid=(bsz, nblk),
        in_specs=[seq(), seq(), seq(), seq(),
                  pl.BlockSpec((8, R), lambda b, j: (0, b * nblk + j)),
                  pl.BlockSpec((1, R, LANES), lambda b, j: (b, j, 0)),
                  _const_spec(gate_b.shape), _const_spec(gate_b_row.shape),
                  _const_spec((R, R)), _const_spec((R, R))],
        out_specs=seq(),
        out_shape=jax.ShapeDtypeStruct((bsz, s, width), BF16),
        scratch_shapes=[pltpu.VMEM((ML_HEADS, ML_D, 2 * ML_D), F32),
                        pltpu.VMEM((ML_HEADS, 8, LANES), F32)],
        compiler_params=pltpu.CompilerParams(dimension_semantics=("parallel", "arbitrary"),
                                             vmem_limit_bytes=VMEM_LIMIT),
        name="mlstm",
    )(mq, mk, mv, og, gif, gcol, gate_b, gate_b_row, jnp.asarray(tril, BF16), jnp.asarray(tril.T, BF16))


FF_CHUNK = 512


def _merge_ffn_kernel(x_ref, ya_ref, yb_ref, mg_ref, wa_ref, wb_ref, wo_ref, g2_ref, w1_ref, w2_ref,
                      out_ref):
    mg = mg_ref[...]
    mixed = (jax.nn.sigmoid(mg[:, :D_MODEL].astype(F32)) * _dot(ya_ref[...], wa_ref[...])
             + jax.nn.sigmoid(mg[:, D_MODEL:].astype(F32)) * _dot(yb_ref[...], wb_ref[...]))
    x1 = x_ref[...] + _dot(mixed.astype(BF16), wo_ref[...])
    h2 = x1 * lax.rsqrt(jnp.mean(x1 * x1, axis=-1, keepdims=True) + RMS_EPS) * g2_ref[...]
    h2 = h2.astype(BF16)
    acc = x1
    for c in range(D_FF // FF_CHUNK):
        a = jnp.maximum(_dot(h2, w1_ref[:, c * FF_CHUNK:(c + 1) * FF_CHUNK]), 0.0)
        acc = acc + _dot((a * a).astype(BF16), w2_ref[c * FF_CHUNK:(c + 1) * FF_CHUNK, :])
    out_ref[...] = acc


def _merge_ffn(x2d, ya, yb, mg, wa, wb, wo, g2, w1, w2, tm=512):
    n = x2d.shape[0]
    row = lambda w: pl.BlockSpec((tm, w), lambda i: (i, 0))
    return pl.pallas_call(
        _merge_ffn_kernel,
        grid=(n // tm,),
        in_specs=[row(D_MODEL), row(512), row(512), row(2048),
                  _const_spec(wa.shape), _const_spec(wb.shape), _const_spec(wo.shape),
                  _const_spec(g2.shape), _const_spec(w1.shape), _const_spec(w2.shape)],
        out_specs=row(D_MODEL),
        out_shape=jax.ShapeDtypeStruct((n, D_MODEL), F32),
        compiler_params=pltpu.CompilerParams(dimension_semantics=("parallel",),
                                             vmem_limit_bytes=VMEM_LIMIT),
        name="merge_ffn",
    )(x2d, ya, yb, mg, wa, wb, wo, g2, w1, w2)


def _proj_weights(w):
    widths = (512, 128, 128, 128, 128, 128, 128, 24, 512, 512, 512, 4, 4, 512, 2048)
    off = np.concatenate([[0], np.cumsum(widths)])
    (nq, nkc, nvc, nks, nvs, nkw, nvw, ngate, mq, mk, mv, mi, mf, mo, mgate) = (int(o) for o in off[:-1])
    col = lambda start, width: w[:, start:start + width]
    parts = [col(nq + NSA_D * (g * NSA_HPG + j), NSA_D) for j in range(NSA_HPG) for g in range(NSA_GROUPS)]
    parts += [col(nks, 128), col(nkw, 128), col(nkc, 128), col(nvc, 128), col(nvs, 128), col(nvw, 128)]
    gate = col(ngate, 24).reshape(-1, NSA_HEADS, 3).transpose(0, 2, 1).reshape(-1, 24)
    parts += [gate, col(mi, 4), col(mf, 4), jnp.zeros((w.shape[0], LANES - _IF_LANE - 8), w.dtype)]
    parts += [col(mq, 512), col(mk, 512), col(mv, 512), col(mo, 512), col(mgate, 2048)]
    w_all = jnp.concatenate(parts, axis=1).astype(BF16)
    w_if = jnp.concatenate([col(mi, 4), col(mf, 4)], axis=1).T.astype(BF16)
    return w_all, w_if


def _branch_a_weights(w):
    head = lambda h: w[NSA_D * h:NSA_D * (h + 1)]
    return jnp.concatenate([head(g * NSA_HPG + j) for j in range(NSA_HPG) for g in range(NSA_GROUPS)],
                           axis=0).astype(BF16)


def _compress_weights(pos, w1):
    r = w1.reshape(2, CMP_STRIDE, 1, NSA_D, CMP_HIDDEN)
    z = jnp.zeros_like(r)
    w1x = jnp.stack([jnp.concatenate([r, z], axis=2), jnp.concatenate([z, r], axis=2)])
    w1x = w1x.reshape(NSA_GROUPS, 2, CMP_STRIDE * NSA_GROUPS * NSA_D, CMP_HIDDEN).astype(BF16)
    posx = jnp.broadcast_to(pos.reshape(2, CMP_STRIDE, 1, NSA_D), (2, CMP_STRIDE, NSA_GROUPS, NSA_D))
    return w1x, posx.reshape(2, CMP_STRIDE * NSA_GROUPS * NSA_D).astype(BF16)


def _t5_bucket(dist):
    n = np.maximum(dist, 0)
    max_exact = REL_BUCKETS // 2
    nf = np.maximum(n, 1).astype(np.float32)
    large = max_exact + (np.log(nf / np.float32(max_exact)) / np.float32(math.log(REL_MAX_DIST / max_exact))
                         * np.float32(REL_BUCKETS - max_exact)).astype(np.int32)
    return np.where(n < max_exact, n, np.minimum(large, REL_BUCKETS - 1))


def _toeplitz(rel, n_rows, n_cols, stride, off, inner=1):
    a_rows = n_rows // inner
    lw = a_rows + n_cols
    k = np.arange(lw)[:, None]
    dist = stride * np.where(k < n_cols, -k, lw - k) + np.arange(inner)[None, :] + off
    live = (dist >= 0) & (k != n_cols)
    onehot = live[..., None] & (_t5_bucket(dist)[..., None] == np.arange(REL_BUCKETS))
    w = jnp.dot(jnp.asarray(onehot.reshape(lw * inner, REL_BUCKETS), F32), rel.T,
                precision=lax.Precision.HIGHEST)
    w = w.T.reshape(-1, lw, inner)
    flat = jnp.tile(w, (1, a_rows, 1))[:, :a_rows * (lw - 1)]
    out = flat.reshape(-1, a_rows, lw - 1, inner)[:, :, :n_cols]
    return out.transpose(0, 1, 3, 2).reshape(-1, n_rows, n_cols)


def _rel_bias(rel_table):
    return (rel_table - rel_table[REL_BUCKETS - 1][None, :]).T * LOG2E


def _bias_tables(rel, shifts):
    T = NSA_TILE
    sh_c, sh_s, sh_w = shifts
    r = np.arange(T)[:, None]

    c = np.arange(NEAR_KEYS)[None, :]
    ok = jnp.asarray((r - c + SLC_PAD) >= 0)[None]
    bs = jnp.where(ok, _toeplitz(rel, T, NEAR_KEYS, 1, SLC_PAD) - sh_s, MASK_F32)
    c = np.arange(WIN_KEYS)[None, :]
    d = r - c + WINDOW
    ok = jnp.asarray((d >= 0) & (d < WINDOW))[None]
    bw = jnp.where(ok, _toeplitz(rel, T, WIN_KEYS, 1, WINDOW) - sh_w, MASK_F32)
    r = np.arange(SEL_TILE)[:, None]
    m = np.arange(64)[None, :]
    d = r - CMP_STRIDE * m + BAND_DIST0
    band = (m >= 1) & (m < BAND_SLOTS - 1)
    vals = _toeplitz(rel, SEL_TILE, BAND_SLOTS, CMP_STRIDE, BAND_DIST0, inner=CMP_STRIDE)
    vals = jnp.pad(vals, ((0, 0), (0, 0), (0, 64 - BAND_SLOTS)))
    vals = jnp.where(jnp.asarray(band & (d >= 0))[None], vals, 0.0) - sh_c
    dead = (band & (d < 0)) | (m == BAND_SLOTS - 1)
    vals = jnp.where(jnp.asarray(dead)[None], MASK_BIG, vals)
    hi = vals.astype(BF16)
    lo = (vals - hi.astype(F32)).astype(BF16)
    bcf = jnp.concatenate([hi, lo], axis=-1)
    flat = lambda a: a.reshape(-1, a.shape[-1])
    return flat(bcf), flat(bs).astype(F32), flat(bw).astype(F32)


def _layer(l, x2d, bsz, s, consts, rel_table, norm1_g, w_in, nsa_q_gain, nsa_k_gain, cmp_k, cmp_v,
           ml_conv_w, ml_conv_b, ml_i_bias, ml_f_bias, w_branch_a, w_branch_b, w_out, norm2_g, w_ff1, w_ff2):
    n = bsz * s
    nseg = s // CMP_STRIDE
    oh, c2s = consts
    w_all, w_if = _proj_weights(w_in[l])
    qg_pad = jnp.concatenate([nsa_q_gain[l]] * 2)[None, :]
    kg_pad = jnp.stack([jnp.concatenate([nsa_k_gain[l, 1]] * 2), jnp.concatenate([nsa_k_gain[l, 2]] * 2)])
    (q, ks, kw, kc, vc, vs, vw, gates, gcol, mq, mk, mv, og, mg, gif) = _proj(
        x2d, s, norm1_g[l][None, :], w_all, w_if, qg_pad, kg_pad, ml_conv_w[l], ml_conv_b[l][None, :])

    def compress(a, params, gain, normalize):
        pos, w1, b1, w2, b2 = (p[l] for p in params)
        w1x, posx = _compress_weights(pos, w1)
        return _compress(a.reshape(bsz, nseg, CMP_STRIDE * LANES), w1x, posx, b1[None, :],
                         w2.astype(BF16), b2[None, :], gain[None, :], normalize)

    kcmp = compress(kc, cmp_k, nsa_k_gain[l, 0], True)
    vcmp = compress(vc, cmp_v, jnp.ones((NSA_D,), F32), False)

    qk_bound = lambda kg: 8.0 * LOG2E * jnp.max(jnp.abs(nsa_q_gain[l])) * jnp.max(jnp.abs(kg))
    tab = _rel_bias(rel_table)
    snap = lambda v: v.astype(BF16).astype(F32)
    shifts = [snap(qk_bound(nsa_k_gain[l, j]) + jnp.maximum(jnp.max(tab), 0.0)) for j in range(3)]
    bias_range = jnp.maximum(jnp.max(tab), 0.0) - jnp.minimum(jnp.min(tab), 0.0)
    safe = 2.0 * jnp.max(jnp.stack([qk_bound(nsa_k_gain[l, j]) for j in range(3)])) + bias_range < SAFE_SHIFT_LOG2
    shifts = [jnp.where(safe, sh, 0.0) for sh in shifts]
    bcf, bs, bw = _bias_tables(tab, shifts)

    seq = lambda a: a.reshape(bsz, s, a.shape[-1])
    front = lambda a, p: jnp.pad(seq(a), ((0, 0), (p, 0), (0, 0)))
    operands = (seq(q), front(ks, SLC_PAD), front(vs, SLC_PAD), front(kw, WINDOW), front(vw, WINDOW),
                kcmp, vcmp, oh, seq(gates), bcf, bs, bw, c2s, jnp.full((1, LANES), -shifts[1], F32))
    y_a = _nsa(safe.astype(jnp.int32)[None], *operands)

    gate_b = jnp.concatenate([ml_i_bias[l], ml_f_bias[l]])
    gate_b_row = jnp.pad(gate_b, (_IF_LANE, LANES - _IF_LANE - 8))[None, :]
    y_b = _mlstm(seq(mq), seq(mk), seq(mv), seq(og), gif, seq(gcol), gate_b[:, None], gate_b_row)

    out = _merge_ffn(x2d, y_a.reshape(n, 512), y_b.reshape(n, 512), mg, _branch_a_weights(w_branch_a[l]),
                     w_branch_b[l].astype(BF16), w_out[l].astype(BF16), norm2_g[l][None, :],
                     w_ff1[l].astype(BF16), w_ff2[l].astype(BF16))
    return out, y_a, y_b


def _consts(s):
    nseg = s // CMP_STRIDE
    nsel = s // SLC_BLOCK
    blk_of_key = np.arange(s) // SLC_BLOCK
    oh = np.concatenate([np.ones((SLC_PAD, LANES), np.float32),
                         (blk_of_key[:, None] == np.arange(LANES)[None, :]).astype(np.float32)], axis=0)
    ci = np.arange(nseg)[:, None] * CMP_STRIDE
    sj = np.arange(LANES)[None, :] * SLC_BLOCK
    c2s = ((ci < sj + SLC_BLOCK) & (ci + CMP_LEN > sj) & (np.arange(LANES)[None, :] < nsel)
           & (np.arange(nseg)[:, None] < nseg - 1))
    return jnp.asarray(oh, BF16), jnp.asarray(c2s.astype(np.float32), BF16)


def kernel(x, norm1_g, w_in, nsa_q_gain, nsa_k_gain, cmp_k_pos, cmp_k_w1, cmp_k_b1, cmp_k_w2, cmp_k_b2, cmp_v_pos, cmp_v_w1, cmp_v_b1, cmp_v_w2, cmp_v_b2, rel_table, ml_conv_w, ml_conv_b, ml_i_bias, ml_f_bias, w_branch_a, w_branch_b, w_out, norm2_g, w_ff1, w_ff2):
    bsz, s, _ = x.shape
    consts = _consts(s)
    x2d = x.reshape(bsz * s, D_MODEL)
    for l in range(norm1_g.shape[0]):
        x2d, _, _ = _layer(l, x2d, bsz, s, consts, rel_table, norm1_g, w_in, nsa_q_gain, nsa_k_gain,
                           (cmp_k_pos, cmp_k_w1, cmp_k_b1, cmp_k_w2, cmp_k_b2),
                           (cmp_v_pos, cmp_v_w1, cmp_v_b1, cmp_v_w2, cmp_v_b2),
                           ml_conv_w, ml_conv_b, ml_i_bias, ml_f_bias,
                           w_branch_a, w_branch_b, w_out, norm2_g, w_ff1, w_ff2)
    return x2d.reshape(bsz, s, D_MODEL)
```
